```python
import math
import jax
import jax.numpy as jnp
from jax import lax
import numpy as np

D_MODEL = 1024
BATCH = 2
SEQ = 8192
DEPTH = 4

N_META = 16
S5_WIDTH = 512
S5_GROUP = 16
S5_GROUPS = S5_WIDTH // S5_GROUP
S5_STATE = 64
DT_MIN = 1e-3
DT_MAX = 1e-1
LRU_WIDTH = 512
LRU_HEADS = 8
LRU_HEAD_DIM = LRU_WIDTH // LRU_HEADS
CONV_WIDTH = 4
LRU_C = 8.0
N_BRANCH = 2
D_IN = S5_WIDTH + 2 * LRU_WIDTH + N_BRANCH * D_MODEL
DENSE_FF = 3 * D_MODEL
N_EXPERTS = 8
TOP_K = 2
EXPERT_FF = 3 * D_MODEL // 2
MOE_BLOCK = 256
N_DENSE = (DEPTH + 1) // 2
N_MOE = DEPTH // 2
EPS = 1e-6

kernel_name = "hybrid_s5_rglru_gated_moe_trunk"


def rmsnorm(x, g):
    x32 = x.astype(jnp.float32)
    y = x32 * lax.rsqrt(jnp.mean(x32 * x32, axis=-1, keepdims=True) + EPS)
    return (y * g.astype(jnp.float32)).astype(x.dtype)


def _complex_linear_combine(left, right):
    a1r, a1i, b1r, b1i = left
    a2r, a2i, b2r, b2i = right
    return (a2r * a1r - a2i * a1i,
            a2r * a1i + a2i * a1r,
            a2r * b1r - a2i * b1i + b2r,
            a2r * b1i + a2i * b1r + b2i)


def _real_linear_combine(left, right):
    a1, b1 = left
    a2, b2 = right
    return (a1 * a2, a2 * b1 + b2)


def s5_branch(u, lam_re, lam_im, log_dt, b_re, b_im, c_re, c_im, d_skip, w_glu, b_glu):
    f32 = jnp.float32
    bsz, t, _ = u.shape
    u32 = u.astype(f32)
    ug = u32.reshape(bsz, t, S5_GROUPS, S5_GROUP)
    lr = lam_re.astype(f32)
    li = lam_im.astype(f32)
    dt = jnp.exp(log_dt.astype(f32))[:, None]
    mag = jnp.exp(lr * dt)
    ab_re = mag * jnp.cos(li * dt)
    ab_im = mag * jnp.sin(li * dt)
    den = lr * lr + li * li
    num_re = ab_re - 1.0
    coef_re = (num_re * lr + ab_im * li) / den
    coef_im = (ab_im * lr - num_re * li) / den
    br = b_re.astype(f32)
    bi = b_im.astype(f32)
    bb_re = coef_re[..., None] * br - coef_im[..., None] * bi
    bb_im = coef_re[..., None] * bi + coef_im[..., None] * br
    bu_re = jnp.einsum('btgh,gph->btgp', ug, bb_re)
    bu_im = jnp.einsum('btgh,gph->btgp', ug, bb_im)
    a_re = jnp.broadcast_to(ab_re, bu_re.shape)
    a_im = jnp.broadcast_to(ab_im, bu_re.shape)
    _, _, s_re, s_im = lax.associative_scan(
        _complex_linear_combine, (a_re, a_im, bu_re, bu_im), axis=1)
    y = (jnp.einsum('btgp,ghp->btgh', s_re, c_re.astype(f32))
         - jnp.einsum('btgp,ghp->btgh', s_im, c_im.astype(f32)))
    y = y.reshape(bsz, t, S5_WIDTH) + d_skip.astype(f32) * u32
    y = jax.nn.gelu(y).astype(u.dtype)
    return y * jax.nn.sigmoid(y @ w_glu + b_glu)


def rglru_branch(xb, gate_in, conv_w, conv_b, w_r, b_r, w_i, b_i, lam):
    f32 = jnp.float32
    bsz, t, c = xb.shape
    xc = lax.conv_general_dilated(
        xb, conv_w[:, None, :], window_strides=(1,), padding=[(CONV_WIDTH - 1, 0)],
        dimension_numbers=('NWC', 'WIO', 'NWC'), feature_group_count=c) + conv_b
    x32 = xc.astype(f32)
    xh = x32.reshape(bsz, t, LRU_HEADS, LRU_HEAD_DIM)
    r = jax.nn.sigmoid(jnp.einsum('btnh,nhk->btnk', xh, w_r.astype(f32)).reshape(bsz, t, c)
                       + b_r.astype(f32))
    i = jax.nn.sigmoid(jnp.einsum('btnh,nhk->btnk', xh, w_i.astype(f32)).reshape(bsz, t, c)
                       + b_i.astype(f32))
    log_a = -LRU_C * r * jax.nn.softplus(-lam.astype(f32))
    a = jnp.exp(log_a)
    mult = jnp.sqrt(-jnp.expm1(2.0 * log_a))
    b = mult * (i * x32)
    _, h = lax.associative_scan(_real_linear_combine, (a, b), axis=1)
    return (h * jax.nn.gelu(gate_in.astype(f32))).astype(xb.dtype)


def swiglu(h, w_gate, w_up, w_down):
    return (jax.nn.silu(h @ w_gate) * (h @ w_up)) @ w_down


def moe_swiglu(x2d, router_w, router_b, w_gate, w_up, w_down):
    n, d = x2d.shape
    logits = (x2d @ router_w).astype(jnp.float32) + router_b.astype(jnp.float32)
    top_val, top_idx = lax.top_k(logits, TOP_K)
    gates = jax.nn.softmax(top_val, axis=-1)
    nk = n * TOP_K
    flat_e = top_idx.reshape(nk)
    flat_tok = jnp.arange(nk, dtype=jnp.int32) // TOP_K
    flat_g = gates.reshape(nk)
    order = jnp.argsort(flat_e, stable=True)
    sorted_e = flat_e[order]
    counts = jnp.bincount(flat_e, length=N_EXPERTS)
    padded = ((counts + MOE_BLOCK - 1) // MOE_BLOCK) * MOE_BLOCK
    cum_pad = jnp.cumsum(padded)
    pad_start = cum_pad - padded
    start = jnp.cumsum(counts) - counts
    rank = jnp.arange(nk, dtype=jnp.int32) - start[sorted_e]
    dest = pad_start[sorted_e] + rank
    n_blocks = -(-nk // MOE_BLOCK) + N_EXPERTS
    n_slots = n_blocks * MOE_BLOCK
    slot_tok = jnp.full((n_slots,), n, dtype=jnp.int32).at[dest].set(flat_tok[order])
    slot_gate = jnp.zeros((n_slots,), jnp.float32).at[dest].set(flat_g[order])
    block_start = jnp.arange(n_blocks, dtype=jnp.int32) * MOE_BLOCK
    block_expert = jnp.minimum(jnp.searchsorted(cum_pad, block_start, side='right'),
                               N_EXPERTS - 1).astype(jnp.int32)
    x_pad = jnp.concatenate([x2d, jnp.zeros((1, d), x2d.dtype)], axis=0)
    xs = x_pad[slot_tok].reshape(n_blocks, MOE_BLOCK, d)

    def expert_block(args):
        xb, e = args
        return swiglu(xb, w_gate[e], w_up[e], w_down[e])

    ys = lax.map(expert_block, (xs, block_expert)).reshape(n_slots, d)
    ys = ys * slot_gate[:, None].astype(ys.dtype)
    out = jnp.zeros((n + 1, d), ys.dtype).at[slot_tok].add(ys)
    return out[:n]


def setup_inputs(seed: int = 0) -> dict:
    key = jax.random.key(seed)
    ks = iter(jax.random.split(key, 48))
    f32 = jnp.float32

    def nrm(shape, scale):
        return jax.random.normal(next(ks), shape, f32) * scale

    x = nrm((BATCH, SEQ, D_MODEL), 1.0)
    meta_tokens = nrm((N_META, D_MODEL), 1.0)
    mix_norm = 1.0 + nrm((DEPTH, D_MODEL), 0.01)
    w_in = nrm((DEPTH, D_MODEL, D_IN), D_MODEL ** -0.5)
    merge_bias = nrm((DEPTH, N_BRANCH * D_MODEL), 0.01)
    n_idx = jnp.arange(S5_STATE, dtype=f32)
    s5_lambda_re = -0.5 + nrm((DEPTH, S5_GROUPS, S5_STATE), 0.01)
    s5_lambda_im = math.pi * n_idx + nrm((DEPTH, S5_GROUPS, S5_STATE), 0.01)
    s5_log_dt = jax.random.uniform(next(ks), (DEPTH, S5_GROUPS), f32,
                                   math.log(DT_MIN), math.log(DT_MAX))
    s5_b_re = nrm((DEPTH, S5_GROUPS, S5_STATE, S5_GROUP), (2 * S5_GROUP) ** -0.5)
    s5_b_im = nrm((DEPTH, S5_GROUPS, S5_STATE, S5_GROUP), (2 * S5_GROUP) ** -0.5)
    s5_c_re = nrm((DEPTH, S5_GROUPS, S5_GROUP, S5_STATE), (2 * S5_STATE) ** -0.5)
    s5_c_im = nrm((DEPTH, S5_GROUPS, S5_GROUP, S5_STATE), (2 * S5_STATE) ** -0.5)
    s5_d = nrm((DEPTH, S5_WIDTH), 1.0)
    s5_w_glu = nrm((DEPTH, S5_WIDTH, S5_WIDTH), S5_WIDTH ** -0.5)
    s5_b_glu = nrm((DEPTH, S5_WIDTH), 0.01)
    s5_w_proj = nrm((DEPTH, S5_WIDTH, D_MODEL), S5_WIDTH ** -0.5)
    lru_conv_w = nrm((DEPTH, CONV_WIDTH, LRU_WIDTH), CONV_WIDTH ** -0.5)
    lru_conv_b = nrm((DEPTH, LRU_WIDTH), 0.01)
    lru_w_rgate = nrm((DEPTH, LRU_HEADS, LRU_HEAD_DIM, LRU_HEAD_DIM), LRU_HEAD_DIM ** -0.5)
    lru_b_rgate = nrm((DEPTH, LRU_WIDTH), 0.01)
    lru_w_igate = nrm((DEPTH, LRU_HEADS, LRU_HEAD_DIM, LRU_HEAD_DIM), LRU_HEAD_DIM ** -0.5)
    lru_b_igate = nrm((DEPTH, LRU_WIDTH), 0.01)
    a_c = jax.random.uniform(next(ks), (DEPTH, LRU_WIDTH), f32, 0.9, 0.999)
    s = a_c ** (1.0 / LRU_C)
    lru_lambda = jnp.log(s) - jnp.log1p(-s)
    lru_w_proj = nrm((DEPTH, LRU_WIDTH, D_MODEL), LRU_WIDTH ** -0.5)
    w_out = nrm((DEPTH, D_MODEL, D_MODEL), D_MODEL ** -0.5)
    ffn_norm = 1.0 + nrm((DEPTH, D_MODEL), 0.01)
    dense_w_gate = nrm((N_DENSE, D_MODEL, DENSE_FF), D_MODEL ** -0.5)
    dense_w_up = nrm((N_DENSE, D_MODEL, DENSE_FF), D_MODEL ** -0.5)
    dense_w_down = nrm((N_DENSE, DENSE_FF, D_MODEL), DENSE_FF ** -0.5)
    router_w = nrm((N_MOE, D_MODEL, N_EXPERTS), D_MODEL ** -0.5)
    router_b = nrm((N_MOE, N_EXPERTS), 0.01)
    moe_w_gate = nrm((N_MOE, N_EXPERTS, D_MODEL, EXPERT_FF), D_MODEL ** -0.5)
    moe_w_up = nrm((N_MOE, N_EXPERTS, D_MODEL, EXPERT_FF), D_MODEL ** -0.5)
    moe_w_down = nrm((N_MOE, N_EXPERTS, EXPERT_FF, D_MODEL), EXPERT_FF ** -0.5)
    final_norm = 1.0 + nrm((D_MODEL,), 0.01)
    return {
        "x": x, "meta_tokens": meta_tokens, "mix_norm": mix_norm, "w_in": w_in,
        "merge_bias": merge_bias, "s5_lambda_re": s5_lambda_re, "s5_lambda_im": s5_lambda_im,
        "s5_log_dt": s5_log_dt, "s5_b_re": s5_b_re, "s5_b_im": s5_b_im,
        "s5_c_re": s5_c_re, "s5_c_im": s5_c_im, "s5_d": s5_d, "s5_w_glu": s5_w_glu,
        "s5_b_glu": s5_b_glu, "s5_w_proj": s5_w_proj, "lru_conv_w": lru_conv_w,
        "lru_conv_b": lru_conv_b, "lru_w_rgate": lru_w_rgate, "lru_b_rgate": lru_b_rgate,
        "lru_w_igate": lru_w_igate, "lru_b_igate": lru_b_igate, "lru_lambda": lru_lambda,
        "lru_w_proj": lru_w_proj, "w_out": w_out, "ffn_norm": ffn_norm,
        "dense_w_gate": dense_w_gate, "dense_w_up": dense_w_up, "dense_w_down": dense_w_down,
        "router_w": router_w, "router_b": router_b, "moe_w_gate": moe_w_gate,
        "moe_w_up": moe_w_up, "moe_w_down": moe_w_down, "final_norm": final_norm,
    }


def reference(x, meta_tokens, mix_norm, w_in, merge_bias, s5_lambda_re, s5_lambda_im,
              s5_log_dt, s5_b_re, s5_b_im, s5_c_re, s5_c_im, s5_d, s5_w_glu, s5_b_glu,
              s5_w_proj, lru_conv_w, lru_conv_b, lru_w_rgate, lru_b_rgate, lru_w_igate,
              lru_b_igate, lru_lambda, lru_w_proj, w_out, ffn_norm, dense_w_gate,
              dense_w_up, dense_w_down, router_w, router_b, moe_w_gate, moe_w_up,
              moe_w_down, final_norm):
    bsz = x.shape[0]
    meta = jnp.broadcast_to(meta_tokens[None].astype(x.dtype), (bsz, N_META, D_MODEL))
    hs = jnp.concatenate([meta, x], axis=1)
    t = hs.shape[1]
    o_x = S5_WIDTH
    o_g = S5_WIDTH + LRU_WIDTH
    o_m = S5_WIDTH + 2 * LRU_WIDTH
    for layer in range(DEPTH):
        hn = rmsnorm(hs, mix_norm[layer])
        z = hn @ w_in[layer]
        u_s5 = z[..., :o_x]
        x_lru = z[..., o_x:o_g]
        g_lru = z[..., o_g:o_m]
        gates = jax.nn.sigmoid(z[..., o_m:] + merge_bias[layer]).reshape(
            bsz, t, N_BRANCH, D_MODEL)
        y_a = s5_branch(u_s5, s5_lambda_re[layer], s5_lambda_im[layer], s5_log_dt[layer],
                        s5_b_re[layer], s5_b_im[layer], s5_c_re[layer], s5_c_im[layer],
                        s5_d[layer], s5_w_glu[layer], s5_b_glu[layer]) @ s5_w_proj[layer]
        y_b = rglru_branch(x_lru, g_lru, lru_conv_w[layer], lru_conv_b[layer],
                           lru_w_rgate[layer], lru_b_rgate[layer], lru_w_igate[layer],
                           lru_b_igate[layer], lru_lambda[layer]) @ lru_w_proj[layer]
        y = gates[:, :, 0, :] * y_a + gates[:, :, 1, :] * y_b
        hs = hs + y @ w_out[layer]
        hn = rmsnorm(hs, ffn_norm[layer])
        j = layer // 2
        if layer % 2 == 0:
            hs = hs + swiglu(hn, dense_w_gate[j], dense_w_up[j], dense_w_down[j])
        else:
            f = moe_swiglu(hn.reshape(bsz * t, D_MODEL), router_w[j], router_b[j],
                           moe_w_gate[j], moe_w_up[j], moe_w_down[j])
            hs = hs + f.reshape(bsz, t, D_MODEL)
    return rmsnorm(hs[:, N_META:], final_norm)
```

```python
import functools
import math

import jax
import jax.numpy as jnp
from jax import lax
from jax.experimental import pallas as pl
from jax.experimental.pallas import tpu as pltpu

F32 = jnp.float32
BF16 = jnp.bfloat16

D_MODEL = 1024
N_META = 16
S5_WIDTH = 512
S5_GROUP = 16
S5_GROUPS = 32
S5_STATE = 64
LRU_WIDTH = 512
LRU_HEADS = 8
LRU_HEAD_DIM = 64
CONV_WIDTH = 4
LRU_C = 8.0
N_EXPERTS = 8
EPS = 1e-6

FOLD = 8
S5_PARTS = 4
PART_W = S5_WIDTH // S5_PARTS
PART_GROUPS = PART_W // S5_GROUP
PART_STATE = PART_GROUPS * S5_STATE
FOLD_W = FOLD * PART_W

T_PAD = 8256
ROWS = T_PAD // FOLD
TM = 688
TM_FFN = 1376
FF_CHUNK = 512
LRU_CHUNK = 1032
VMEM_LIMIT = 56 * 1024 * 1024
LANES = 128
SUBLANES = 8
MASKED_LOGIT = float("-inf")


def _dot(a, b):
    return jnp.dot(a, b, preferred_element_type=F32)


def _const_spec(block_shape, index_map):
    return pl.BlockSpec(block_shape, index_map, pipeline_mode=pl.Buffered(1))


def _rms(x, g):
    ms = jnp.mean(x * x, axis=-1, keepdims=True)
    return x * lax.rsqrt(ms + EPS) * g


def _in_proj_kernel(hs_ref, g_ref, w_ref, mb_ref, u_ref, xl_ref, gl_ref, gt_ref):
    hn = _rms(hs_ref[...], g_ref[0]).astype(BF16)
    u = _dot(hn, w_ref[0, :, 0:S5_WIDTH]).astype(BF16)
    for q in range(S5_PARTS):
        u_ref[q] = u[:, q * PART_W:(q + 1) * PART_W]
    o_x = S5_WIDTH
    o_g = o_x + LRU_WIDTH
    o_m = o_g + LRU_WIDTH
    xl_ref[...] = _dot(hn, w_ref[0, :, o_x:o_g]).astype(BF16)
    gl_ref[...] = _dot(hn, w_ref[0, :, o_g:o_m]).astype(BF16)
    z = _dot(hn, w_ref[0, :, o_m:]) + mb_ref[0]
    gt_ref[...] = jax.nn.sigmoid(z).astype(BF16)


def _in_proj(hs, mix_norm, w_in, merge_bias, layer):
    n = hs.shape[0]
    d_in = w_in.shape[-1]
    lay = lambda i: (layer, 0, 0)
    return pl.pallas_call(
        _in_proj_kernel,
        grid=(n // TM,),
        in_specs=[
            pl.BlockSpec((TM, D_MODEL), lambda i: (i, 0)),
            _const_spec((1, 1, D_MODEL), lay),
            _const_spec((1, D_MODEL, d_in), lay),
            _const_spec((1, 1, 2 * D_MODEL), lay),
        ],
        out_specs=[
            pl.BlockSpec((S5_PARTS, TM, PART_W), lambda i: (0, i, 0)),
            pl.BlockSpec((TM, LRU_WIDTH), lambda i: (i, 0)),
            pl.BlockSpec((TM, LRU_WIDTH), lambda i: (i, 0)),
            pl.BlockSpec((TM, 2 * D_MODEL), lambda i: (i, 0)),
        ],
        out_shape=[
            jax.ShapeDtypeStruct((S5_PARTS, n, PART_W), BF16),
            jax.ShapeDtypeStruct((n, LRU_WIDTH), BF16),
            jax.ShapeDtypeStruct((n, LRU_WIDTH), BF16),
            jax.ShapeDtypeStruct((n, 2 * D_MODEL), BF16),
        ],
        compiler_params=pltpu.CompilerParams(
            dimension_semantics=("arbitrary",), vmem_limit_bytes=VMEM_LIMIT),
        name="in_proj",
    )(hs, mix_norm, w_in, merge_bias)


def _s5_prep(lam_re, lam_im, log_dt, b_re, b_im, c_re, c_im, d_skip):
    hi = lax.Precision.HIGHEST
    dt = jnp.exp(log_dt)[:, None]
    mag = jnp.exp(lam_re * dt)
    a_re = mag * jnp.cos(lam_im * dt)
    a_im = mag * jnp.sin(lam_im * dt)
    den = lam_re * lam_re + lam_im * lam_im
    num_re = a_re - 1.0
    coef_re = (num_re * lam_re + a_im * lam_im) / den
    coef_im = (a_im * lam_re - num_re * lam_im) / den
    bb_re = coef_re[..., None] * b_re - coef_im[..., None] * b_im
    bb_im = coef_re[..., None] * b_im + coef_im[..., None] * b_re

    def cmul(xr, xi, yr, yi):
        return xr * yr - xi * yi, xr * yi + xi * yr

    def powers(br, bi, n):
        pr, pi = [jnp.ones_like(br)], [jnp.zeros_like(bi)]
        for _ in range(n):
            r, i = cmul(pr[-1], pi[-1], br, bi)
            pr.append(r)
            pi.append(i)
        return jnp.stack(pr), jnp.stack(pi)

    p_re, p_im = powers(a_re, a_im, FOLD)
    q_re, q_im = powers(p_re[FOLD], p_im[FOLD], FOLD)
    eye = jnp.eye(PART_GROUPS, dtype=F32)

    def part_blockdiag(x, rows, cols):
        lead = x.shape[:-3]
        xp = x.reshape(lead + (S5_PARTS, PART_GROUPS, rows, cols))
        out = xp[..., :, :, None, :] * eye[:, None, :, None]
        return jnp.moveaxis(out, -5, 0)

    wr, wi = cmul(p_re[:FOLD][::-1][..., None], p_im[:FOLD][::-1][..., None],
                  bb_re[None], bb_im[None])
    w_ri = jnp.stack([wr, wi], axis=1)
    w_ri = jnp.swapaxes(w_ri, -1, -2)
    w1 = part_blockdiag(w_ri, S5_GROUP, S5_STATE)
    w1 = jnp.transpose(w1, (0, 1, 3, 4, 2, 5, 6)).reshape(S5_PARTS, FOLD_W, 2 * PART_STATE)

    ca_re, ca_im = cmul(c_re[None], c_im[None], p_re[:, :, None, :], p_im[:, :, None, :])
    taps = (jnp.einsum('kghs,gsi->kgih', ca_re[:FOLD], bb_re, precision=hi)
            - jnp.einsum('kghs,gsi->kgih', ca_im[:FOLD], bb_im, precision=hi))
    skip = d_skip.reshape(S5_GROUPS, S5_GROUP)
    taps = taps.at[0].add(skip[:, :, None] * jnp.eye(S5_GROUP, dtype=F32)[None])
    taps = jnp.concatenate([taps, jnp.zeros_like(taps[:1])], axis=0)
    jj = jnp.arange(FOLD)[:, None]
    tt = jnp.arange(FOLD)[None, :]
    lag = jnp.where(tt >= jj, tt - jj, FOLD)
    toep = taps[lag]
    toep = part_blockdiag(toep, S5_GROUP, S5_GROUP)
    toep = jnp.transpose(toep, (0, 1, 3, 4, 2, 5, 6)).reshape(S5_PARTS, FOLD_W, FOLD_W)

    v_ri = jnp.stack([ca_re[1:], -ca_im[1:]], axis=0)
    v_ri = jnp.swapaxes(v_ri, -1, -2)
    v = part_blockdiag(v_ri, S5_STATE, S5_GROUP)
    v = jnp.transpose(v, (0, 1, 3, 4, 2, 5, 6)).reshape(S5_PARTS, 2 * PART_STATE, FOLD_W)
    tv = jnp.concatenate([toep, v], axis=1)

    def part_vec(x):
        lead = x.shape[:-2]
        xp = x.reshape(lead + (S5_PARTS, PART_STATE))
        return jnp.moveaxis(xp, -2, 0)

    lvl = jnp.stack([jnp.stack([part_vec(q_re[k]), part_vec(q_im[k])], axis=1) for k in (1, 2, 4)], axis=1)
    lvl = lvl[:, :, :, None, :]
    rowpow = jnp.stack([part_vec(q_re[1:]), part_vec(q_im[1:])], axis=1)
    return w1.astype(BF16), tv.astype(BF16), lvl, rowpow


def _s5_kernel(u_ref, w1_ref, tv_ref, lvl_ref, rp_ref, y_ref, st_ref):
    ps = PART_STATE
    u = u_ref[0, 0]
    f = _dot(u, w1_ref[0])
    fr = f[:, :ps]
    fi = f[:, ps:]
    row = lax.broadcasted_iota(jnp.int32, (ROWS, ps), 0) & (SUBLANES - 1)
    for lv, k in enumerate((1, 2, 4)):
        ar = lvl_ref[0, lv, 0]
        ai = lvl_ref[0, lv, 1]
        sr = pltpu.roll(fr, k, axis=0)
        si = pltpu.roll(fi, k, axis=0)
        m = row >= k
        fr, fi = (fr + jnp.where(m, ar * sr - ai * si, 0.0),
                  fi + jnp.where(m, ar * si + ai * sr, 0.0))
    st_ref[0:SUBLANES, :] = jnp.zeros((SUBLANES, 2 * ps), F32)
    st_ref[SUBLANES:, :ps] = fr
    st_ref[SUBLANES:, ps:] = fi
    pr = rp_ref[0, 0]
    pi = rp_ref[0, 1]

    def body(i, carry):
        cr, ci = carry
        r = pl.multiple_of(SUBLANES + i * SUBLANES, SUBLANES)
        xr = st_ref[pl.ds(r, SUBLANES), :ps]
        xi = st_ref[pl.ds(r, SUBLANES), ps:]
        hr = xr + pr * cr - pi * ci
        hi = xi + pr * ci + pi * cr
        st_ref[pl.ds(r, SUBLANES), :ps] = hr
        st_ref[pl.ds(r, SUBLANES), ps:] = hi
        return hr[SUBLANES - 1:SUBLANES], hi[SUBLANES - 1:SUBLANES]

    zero = jnp.zeros((1, ps), F32)
    lax.fori_loop(0, ROWS // SUBLANES, body, (zero, zero))
    h_prev = st_ref[pl.ds(SUBLANES - 1, ROWS), :].astype(BF16)
    y = _dot(u, tv_ref[0, :FOLD_W]) + _dot(h_prev, tv_ref[0, FOLD_W:])
    y_ref[0, 0] = jax.nn.gelu(y).astype(BF16)


def _s5_scan(u_parts, w1, tv, lvl, rowpow, bsz):
    n = u_parts.shape[1]
    u2 = u_parts.reshape(S5_PARTS, bsz, ROWS, FOLD_W)
    y2 = pl.pallas_call(
        _s5_kernel,
        grid=(S5_PARTS, bsz),
        in_specs=[
            pl.BlockSpec((1, 1, ROWS, FOLD_W), lambda q, b: (q, b, 0, 0)),
            pl.BlockSpec((1, FOLD_W, 2 * PART_STATE), lambda q, b: (q, 0, 0)),
            pl.BlockSpec((1, FOLD_W + 2 * PART_STATE, FOLD_W), lambda q, b: (q, 0, 0)),
            pl.BlockSpec((1, 3, 2, 1, PART_STATE), lambda q, b: (q, 0, 0, 0, 0)),
            pl.BlockSpec((1, 2, FOLD, PART_STATE), lambda q, b: (q, 0, 0, 0)),
        ],
        out_specs=pl.BlockSpec((1, 1, ROWS, FOLD_W), lambda q, b: (q, b, 0, 0)),
        out_shape=jax.ShapeDtypeStruct((S5_PARTS, bsz, ROWS, FOLD_W), BF16),
        scratch_shapes=[pltpu.VMEM((ROWS + SUBLANES, 2 * PART_STATE), F32)],
        compiler_params=pltpu.CompilerParams(
            dimension_semantics=("arbitrary", "arbitrary"), vmem_limit_bytes=VMEM_LIMIT),
        name="s5_scan",
    )(u2, w1, tv, lvl, rowpow)
    return y2.reshape(S5_PARTS, n, PART_W)


def _lru_kernel(x_ref, g_ref, cw_ref, cb_ref, wri_ref, bri_ref, nsp_ref, o_ref, xp_ref, a_ref, b_ref, h_ref):
    tc = LRU_CHUNK
    c = LRU_WIDTH

    @pl.when(pl.program_id(1) == 0)
    def _():
        xp_ref[0:SUBLANES, :] = jnp.zeros((SUBLANES, c), F32)
        h_ref[...] = jnp.zeros((1, c), F32)

    xp_ref[SUBLANES:, :] = x_ref[0].astype(F32)
    xc = cb_ref[0]
    for k in range(CONV_WIDTH):
        off = SUBLANES - (CONV_WIDTH - 1) + k
        xc = xc + cw_ref[0, k:k + 1, :] * xp_ref[pl.ds(off, tc), :]
    xp_ref[0:SUBLANES, :] = xp_ref[tc:tc + SUBLANES, :]
    ri = jax.nn.sigmoid(_dot(xc.astype(BF16), wri_ref[0]) + bri_ref[0])
    r = ri[:, :c]
    ig = ri[:, c:]
    log_a = r * nsp_ref[0]
    a = jnp.exp(log_a)
    b = jnp.sqrt(1.0 - a * a) * (ig * xc)
    row = lax.broadcasted_iota(jnp.int32, (tc, c), 0) & (SUBLANES - 1)
    for k in (1, 2, 4):
        a_s = pltpu.roll(a, k, axis=0)
        b_s = pltpu.roll(b, k, axis=0)
        m = row >= k
        b = b + jnp.where(m, a * b_s, 0.0)
        a = jnp.where(m, a * a_s, a)
    a_ref[...] = a
    b_ref[...] = b

    def body(i, carry):
        r0 = pl.multiple_of(i * SUBLANES, SUBLANES)
        h = b_ref[pl.ds(r0, SUBLANES), :] + a_ref[pl.ds(r0, SUBLANES), :] * carry
        b_ref[pl.ds(r0, SUBLANES), :] = h
        return h[SUBLANES - 1:SUBLANES]

    h_ref[...] = lax.fori_loop(0, tc // SUBLANES, body, h_ref[...])
    o_ref[0] = (b_ref[...] * jax.nn.gelu(g_ref[0].astype(F32))).astype(BF16)


def _lru(x_lru, g_lru, conv_w, conv_b, w_ri, b_ri, neg_sp, layer, bsz):
    n = x_lru.shape[0]
    c = LRU_WIDTH
    x3 = x_lru.reshape(bsz, T_PAD, c)
    g3 = g_lru.reshape(bsz, T_PAD, c)
    lay = lambda b, t: (layer, 0, 0)
    out = pl.pallas_call(
        _lru_kernel,
        grid=(bsz, T_PAD // LRU_CHUNK),
        in_specs=[
            pl.BlockSpec((1, LRU_CHUNK, c), lambda b, t: (b, t, 0)),
            pl.BlockSpec((1, LRU_CHUNK, c), lambda b, t: (b, t, 0)),
            _const_spec((1, CONV_WIDTH, c), lay),
            _const_spec((1, 1, c), lay),
            _const_spec((1, c, 2 * c), lay),
            _const_spec((1, 1, 2 * c), lay),
            _const_spec((1, 1, c), lay),
        ],
        out_specs=pl.BlockSpec((1, LRU_CHUNK, c), lambda b, t: (b, t, 0)),
        out_shape=jax.ShapeDtypeStruct((bsz, T_PAD, c), BF16),
        scratch_shapes=[
            pltpu.VMEM((LRU_CHUNK + SUBLANES, c), F32),
            pltpu.VMEM((LRU_CHUNK, c), F32),
            pltpu.VMEM((LRU_CHUNK, c), F32),
            pltpu.VMEM((1, c), F32),
        ],
        compiler_params=pltpu.CompilerParams(
            dimension_semantics=("arbitrary", "arbitrary"), vmem_limit_bytes=VMEM_LIMIT),
        name="rglru",
    )(x3, g3, conv_w, conv_b, w_ri, b_ri, neg_sp)
    return out.reshape(n, c)


def _merge_kernel(hs_ref, ys_ref, yl_ref, gt_ref, wglu_ref, bglu_ref, wsp_ref, wlp_ref, wout_ref, g_ref,
                  *rest, with_router):
    if with_router:
        rw_ref, rb_ref, hs_out_ref, hn_ref, cw_ref = rest
    else:
        hs_out_ref, hn_ref = rest
    ys = jnp.concatenate([ys_ref[q] for q in range(S5_PARTS)], axis=-1)
    glu = ys.astype(F32) * jax.nn.sigmoid(_dot(ys, wglu_ref[0]) + bglu_ref[0])
    y_a = _dot(glu.astype(BF16), wsp_ref[0])
    y_b = _dot(yl_ref[...], wlp_ref[0])
    y = gt_ref[:, :D_MODEL].astype(F32) * y_a + gt_ref[:, D_MODEL:].astype(F32) * y_b
    hs = hs_ref[...] + _dot(y.astype(BF16), wout_ref[0])
    hs_out_ref[...] = hs
    hn = _rms(hs, g_ref[0])
    hn_ref[...] = hn.astype(BF16)
    if with_router:
        logits = jnp.dot(hn, rw_ref[0], preferred_element_type=F32,
                         precision=lax.Precision.HIGHEST) + rb_ref[0]
        lane = lax.broadcasted_iota(jnp.int32, logits.shape, 1).astype(F32)
        m1 = jnp.max(logits, axis=-1, keepdims=True)
        i1 = jnp.min(jnp.where(logits == m1, lane, float(LANES)), axis=-1, keepdims=True)
        rest_l = jnp.where(lane == i1, MASKED_LOGIT, logits)
        m2 = jnp.max(rest_l, axis=-1, keepdims=True)
        i2 = jnp.min(jnp.where(rest_l == m2, lane, float(LANES)), axis=-1, keepdims=True)
        e2 = jnp.exp(m2 - m1)
        g1 = 1.0 / (1.0 + e2)
        g2 = e2 / (1.0 + e2)
        cw_ref[...] = jnp.where(lane == i1, g1, 0.0) + jnp.where(lane == i2, g2, 0.0)


def _merge(hs, ys_parts, y_lru, gates, w_glu, b_glu, w_sp, w_lp, w_out, ffn_norm, layer, router=None):
    n = hs.shape[0]
    lay = lambda i: (layer, 0, 0)
    in_specs = [
        pl.BlockSpec((TM, D_MODEL), lambda i: (i, 0)),
        pl.BlockSpec((S5_PARTS, TM, PART_W), lambda i: (0, i, 0)),
        pl.BlockSpec((TM, LRU_WIDTH), lambda i: (i, 0)),
        pl.BlockSpec((TM, 2 * D_MODEL), lambda i: (i, 0)),
        _const_spec((1, S5_WIDTH, S5_WIDTH), lay),
        _const_spec((1, 1, S5_WIDTH), lay),
        _const_spec((1, S5_WIDTH, D_MODEL), lay),
        _const_spec((1, LRU_WIDTH, D_MODEL), lay),
        _const_spec((1, D_MODEL, D_MODEL), lay),
        _const_spec((1, 1, D_MODEL), lay),
    ]
    out_specs = [pl.BlockSpec((TM, D_MODEL), lambda i: (i, 0)),
                 pl.BlockSpec((TM, D_MODEL), lambda i: (i, 0))]
    out_shape = [jax.ShapeDtypeStruct((n, D_MODEL), F32),
                 jax.ShapeDtypeStruct((n, D_MODEL), BF16)]
    args = [hs, ys_parts, y_lru, gates, w_glu, b_glu, w_sp, w_lp, w_out, ffn_norm]
    if router is not None:
        rw, rb, j = router
        in_specs += [_const_spec((1, D_MODEL, LANES), lambda i: (j, 0, 0)),
                     _const_spec((1, 1, LANES), lambda i: (j, 0, 0))]
        out_specs.append(pl.BlockSpec((TM, LANES), lambda i: (i, 0)))
        out_shape.append(jax.ShapeDtypeStruct((n, LANES), F32))
        args += [rw, rb]
    return pl.pallas_call(
        functools.partial(_merge_kernel, with_router=router is not None),
        grid=(n // TM,),
        in_specs=in_specs,
        out_specs=out_specs,
        out_shape=out_shape,
        compiler_params=pltpu.CompilerParams(
            dimension_semantics=("arbitrary",), vmem_limit_bytes=VMEM_LIMIT),
        name="merge_router" if router is not None else "merge",
    )(*args)


def _ffn_kernel(*refs, chunks_per_expert, with_cw):
    if with_cw:
        x_ref, hs_ref, cw_ref, wg_ref, wu_ref, wd_ref, o_ref, acc_ref = refs
    else:
        x_ref, hs_ref, wg_ref, wu_ref, wd_ref, o_ref, acc_ref = refs
    c = pl.program_id(1)

    @pl.when(c == 0)
    def _():
        acc_ref[...] = jnp.zeros_like(acc_ref)

    x = x_ref[...]
    g = _dot(x, wg_ref[0])
    h = g * jax.nn.sigmoid(g) * _dot(x, wu_ref[0])
    if with_cw:
        e = c // chunks_per_expert
        cw = cw_ref[...]
        lane = lax.broadcasted_iota(jnp.int32, cw.shape, 1)
        h = h * jnp.sum(jnp.where(lane == e, cw, 0.0), axis=-1, keepdims=True)
    acc_ref[...] += _dot(h.astype(BF16), wd_ref[0])

    @pl.when(c == pl.num_programs(1) - 1)
    def _():
        o_ref[...] = hs_ref[...] + acc_ref[...]


def _ffn(hn, hs, w_gate, w_up, w_down, first, n_sets, cw=None):
    n = hn.shape[0]
    ff = w_gate.shape[-1]
    cpe = ff // FF_CHUNK
    with_cw = cw is not None
    in_specs = [pl.BlockSpec((TM_FFN, D_MODEL), lambda i, c: (i, 0)),
                pl.BlockSpec((TM_FFN, D_MODEL), lambda i, c: (i, 0))]
    args = [hn, hs]
    if with_cw:
        in_specs.append(pl.BlockSpec((TM_FFN, LANES), lambda i, c: (i, 0)))
        args.append(cw)
    in_specs += [
        pl.BlockSpec((1, D_MODEL, FF_CHUNK), lambda i, c: (first + c // cpe, 0, c % cpe)),
        pl.BlockSpec((1, D_MODEL, FF_CHUNK), lambda i, c: (first + c // cpe, 0, c % cpe)),
        pl.BlockSpec((1, FF_CHUNK, D_MODEL), lambda i, c: (first + c // cpe, c % cpe, 0)),
    ]
    args += [w_gate, w_up, w_down]
    return pl.pallas_call(
        functools.partial(_ffn_kernel, chunks_per_expert=cpe, with_cw=with_cw),
        grid=(n // TM_FFN, n_sets * cpe),
        in_specs=in_specs,
        out_specs=pl.BlockSpec((TM_FFN, D_MODEL), lambda i, c: (i, 0)),
        out_shape=jax.ShapeDtypeStruct((n, D_MODEL), F32),
        scratch_shapes=[pltpu.VMEM((TM_FFN, D_MODEL), F32)],
        compiler_params=pltpu.CompilerParams(
            dimension_semantics=("arbitrary", "arbitrary"), vmem_limit_bytes=VMEM_LIMIT),
        name="moe_ffn" if with_cw else "dense_ffn",
    )(*args)


def _final_kernel(hs_ref, g_ref, o_ref):
    o_ref[...] = _rms(hs_ref[...], g_ref[...])


def _final_norm(hs, g):
    n = hs.shape[0]
    return pl.pallas_call(
        _final_kernel,
        grid=(n // TM,),
        in_specs=[pl.BlockSpec((TM, D_MODEL), lambda i: (i, 0)),
                  _const_spec((1, D_MODEL), lambda i: (0, 0))],
        out_specs=pl.BlockSpec((TM, D_MODEL), lambda i: (i, 0)),
        out_shape=jax.ShapeDtypeStruct((n, D_MODEL), F32),
        compiler_params=pltpu.CompilerParams(
            dimension_semantics=("arbitrary",), vmem_limit_bytes=VMEM_LIMIT),
        name="final_norm",
    )(hs, g)


def _head_blockdiag(w):
    eye = jnp.eye(LRU_HEADS, dtype=w.dtype)
    out = jnp.einsum('lnhk,nm->lnhmk', w, eye)
    return out.reshape(w.shape[0], LRU_WIDTH, LRU_WIDTH)


def kernel(x, meta_tokens, mix_norm, w_in, merge_bias, s5_lambda_re, s5_lambda_im, s5_log_dt, s5_b_re, s5_b_im, s5_c_re, s5_c_im, s5_d, s5_w_glu, s5_b_glu, s5_w_proj, lru_conv_w, lru_conv_b, lru_w_rgate, lru_b_rgate, lru_w_igate, lru_b_igate, lru_lambda, lru_w_proj, w_out, ffn_norm, dense_w_gate, dense_w_up, dense_w_down, router_w, router_b, moe_w_gate, moe_w_up, moe_w_down, final_norm):
    bsz, seq, d = x.shape
    depth = w_in.shape[0]
    assert d == D_MODEL and N_META + seq <= T_PAD
    n = bsz * T_PAD
    assert n % TM == 0 and n % TM_FFN == 0

    meta = jnp.broadcast_to(meta_tokens[None].astype(x.dtype), (bsz, N_META, d))
    pad = jnp.zeros((bsz, T_PAD - N_META - seq, d), x.dtype)
    hs = jnp.concatenate([meta, x, pad], axis=1).reshape(n, d)

    row3 = lambda a: a[:, None, :]
    w_in_b = w_in.astype(BF16)
    w_glu_b = s5_w_glu.astype(BF16)
    w_sp_b = s5_w_proj.astype(BF16)
    w_lp_b = lru_w_proj.astype(BF16)
    w_out_b = w_out.astype(BF16)
    w_ri = jnp.concatenate([_head_blockdiag(lru_w_rgate), _head_blockdiag(lru_w_igate)], axis=-1).astype(BF16)
    b_ri = jnp.concatenate([lru_b_rgate, lru_b_igate], axis=-1)
    neg_sp = -LRU_C * jax.nn.softplus(-lru_lambda)
    dense = [w.astype(BF16) for w in (dense_w_gate, dense_w_up, dense_w_down)]
    n_moe = router_w.shape[0]
    moe = [w.astype(BF16).reshape((n_moe * N_EXPERTS,) + w.shape[2:]) for w in (moe_w_gate, moe_w_up, moe_w_down)]
    rw_pad = jnp.pad(router_w, ((0, 0), (0, 0), (0, LANES - N_EXPERTS)))
    rb_pad = jnp.pad(router_b, ((0, 0), (0, LANES - N_EXPERTS)), constant_values=MASKED_LOGIT)

    for layer in range(depth):
        u_parts, x_lru, g_lru, gates = _in_proj(hs, row3(mix_norm), w_in_b, row3(merge_bias), layer)
        w1, tv, lvl, rowpow = _s5_prep(s5_lambda_re[layer], s5_lambda_im[layer], s5_log_dt[layer],
                                       s5_b_re[layer], s5_b_im[layer], s5_c_re[layer], s5_c_im[layer],
                                       s5_d[layer])
        ys_parts = _s5_scan(u_parts, w1, tv, lvl, rowpow, bsz)
        y_lru = _lru(x_lru, g_lru, lru_conv_w, row3(lru_conv_b), w_ri, row3(b_ri), row3(neg_sp), layer, bsz)
        j = layer // 2
        router = (rw_pad, row3(rb_pad), j) if layer % 2 == 1 else None
        res = _merge(hs, ys_parts, y_lru, gates, w_glu_b, row3(s5_b_glu), w_sp_b, w_lp_b, w_out_b,
                     row3(ffn_norm), layer, router)
        if layer % 2 == 0:
            hs, hn = res
            hs = _ffn(hn, hs, *dense, first=j, n_sets=1)
        else:
            hs, hn, cw = res
            hs = _ffn(hn, hs, *moe, first=j * N_EXPERTS, n_sets=N_EXPERTS, cw=cw)

    out = _final_norm(hs, final_norm[None, :]).reshape(bsz, T_PAD, d)
    return out[:, N_META:N_META + seq]
```

```python
import functools

import jax
import jax.numpy as jnp
from jax import lax
from jax.experimental import pallas as pl
from jax.experimental.pallas import tpu as pltpu

F32 = jnp.float32
BF16 = jnp.bfloat16

D_MODEL = 1024
N_META = 16
S5_WIDTH = 512
S5_GROUP = 16
S5_GROUPS = 32
S5_STATE = 64
LRU_WIDTH = 512
LRU_HEADS = 8
LRU_HEAD_DIM = 64
CONV_WIDTH = 4
LRU_C = 8.0
N_EXPERTS = 8
EPS = 1e-6

FOLD = 8
S5_PARTS = 4
PART_W = S5_WIDTH // S5_PARTS
PART_GROUPS = PART_W // S5_GROUP
PART_STATE = PART_GROUPS * S5_STATE
FOLD_W = FOLD * PART_W

T_PAD = 8256
ROWS = T_PAD // FOLD
TM = 688
TM_FFN = 1376
FF_CHUNK = 512
LRU_CHUNK = 1032
TB_FINAL = 512
VMEM_LIMIT = 56 * 1024 * 1024
LANES = 128
SUBLANES = 8
MASKED_LOGIT = float("-inf")


def _dot(a, b):
    return jnp.dot(a, b, preferred_element_type=F32)


def _const_spec(block_shape, index_map):
    return pl.BlockSpec(block_shape, index_map, pipeline_mode=pl.Buffered(1))


def _rms(x, g):
    ms = jnp.mean(x * x, axis=-1, keepdims=True)
    return x * lax.rsqrt(ms + EPS) * g


def _in_proj_kernel(hs_ref, g_ref, w_ref, mb_ref, u_ref, xl_ref, gl_ref, gt_ref):
    hn = _rms(hs_ref[...], g_ref[0]).astype(BF16)
    u = _dot(hn, w_ref[0, :, 0:S5_WIDTH])
    for q in range(S5_PARTS):
        u_ref[q] = u[:, q * PART_W:(q + 1) * PART_W]
    o_x = S5_WIDTH
    o_g = o_x + LRU_WIDTH
    o_m = o_g + LRU_WIDTH
    xl_ref[...] = _dot(hn, w_ref[0, :, o_x:o_g]).astype(BF16)
    gl_ref[...] = _dot(hn, w_ref[0, :, o_g:o_m]).astype(BF16)
    z = _dot(hn, w_ref[0, :, o_m:]) + mb_ref[0]
    gt_ref[...] = jax.nn.sigmoid(z).astype(BF16)


def _in_proj(hs, mix_norm, w_in, merge_bias, layer):
    n = hs.shape[0]
    d_in = w_in.shape[-1]
    lay = lambda i: (layer, 0, 0)
    return pl.pallas_call(
        _in_proj_kernel,
        grid=(n // TM,),
        in_specs=[
            pl.BlockSpec((TM, D_MODEL), lambda i: (i, 0)),
            _const_spec((1, 1, D_MODEL), lay),
            _const_spec((1, D_MODEL, d_in), lay),
            _const_spec((1, 1, 2 * D_MODEL), lay),
        ],
        out_specs=[
            pl.BlockSpec((S5_PARTS, TM, PART_W), lambda i: (0, i, 0)),
            pl.BlockSpec((TM, LRU_WIDTH), lambda i: (i, 0)),
            pl.BlockSpec((TM, LRU_WIDTH), lambda i: (i, 0)),
            pl.BlockSpec((TM, 2 * D_MODEL), lambda i: (i, 0)),
        ],
        out_shape=[
            jax.ShapeDtypeStruct((S5_PARTS, n, PART_W), F32),
            jax.ShapeDtypeStruct((n, LRU_WIDTH), BF16),
            jax.ShapeDtypeStruct((n, LRU_WIDTH), BF16),
            jax.ShapeDtypeStruct((n, 2 * D_MODEL), BF16),
        ],
        compiler_params=pltpu.CompilerParams(
            dimension_semantics=("arbitrary",), vmem_limit_bytes=VMEM_LIMIT),
        name="in_proj",
    )(hs, mix_norm, w_in, merge_bias)


def _s5_prep(lam_re, lam_im, log_dt, b_re, b_im, c_re, c_im, d_skip):
    hi = lax.Precision.HIGHEST
    dt = jnp.exp(log_dt)[:, None]
    mag = jnp.exp(lam_re * dt)
    a_re = mag * jnp.cos(lam_im * dt)
    a_im = mag * jnp.sin(lam_im * dt)
    den = lam_re * lam_re + lam_im * lam_im
    num_re = a_re - 1.0
    coef_re = (num_re * lam_re + a_im * lam_im) / den
    coef_im = (a_im * lam_re - num_re * lam_im) / den
    bb_re = coef_re[..., None] * b_re - coef_im[..., None] * b_im
    bb_im = coef_re[..., None] * b_im + coef_im[..., None] * b_re

    def cmul(xr, xi, yr, yi):
        return xr * yr - xi * yi, xr * yi + xi * yr

    def powers(br, bi, n):
        pr, pi = [jnp.ones_like(br)], [jnp.zeros_like(bi)]
        for _ in range(n):
            r, i = cmul(pr[-1], pi[-1], br, bi)
            pr.append(r)
            pi.append(i)
        return jnp.stack(pr), jnp.stack(pi)

    p_re, p_im = powers(a_re, a_im, FOLD)
    q_re, q_im = powers(p_re[FOLD], p_im[FOLD], FOLD)

    def per_part(x):
        lead = x.shape[:-3]
        xp = x.reshape(lead + (S5_PARTS, PART_GROUPS) + x.shape[-2:])
        return jnp.moveaxis(xp, len(lead), 0)

    rev_re = jnp.stack([p_re[FOLD - 1 - j] for j in range(FOLD)])
    rev_im = jnp.stack([p_im[FOLD - 1 - j] for j in range(FOLD)])
    wr, wi = cmul(rev_re[..., None], rev_im[..., None], bb_re[None], bb_im[None])
    w_ri = jnp.swapaxes(jnp.stack([wr, wi], axis=1), -1, -2)
    xq = jnp.transpose(per_part(w_ri), (0, 1, 3, 4, 2, 5)).reshape(S5_PARTS, FOLD_W, 2 * S5_STATE)

    ca_re, ca_im = cmul(c_re[None], c_im[None], p_re[:, :, None, :], p_im[:, :, None, :])
    taps = (jnp.einsum('kghs,gsi->kgih', ca_re[:FOLD], bb_re, precision=hi)
            - jnp.einsum('kghs,gsi->kgih', ca_im[:FOLD], bb_im, precision=hi))
    skip = d_skip.reshape(S5_GROUPS, S5_GROUP)
    taps = taps.at[0].add(skip[:, :, None] * jnp.eye(S5_GROUP, dtype=F32)[None])
    rc = jnp.transpose(per_part(taps), (0, 2, 3, 1, 4)).reshape(S5_PARTS, PART_W, FOLD * S5_GROUP)

    v_ri = jnp.swapaxes(jnp.stack([ca_re[1:], -ca_im[1:]], axis=0), -1, -2)
    vc = jnp.transpose(per_part(v_ri), (0, 1, 3, 4, 2, 5)).reshape(S5_PARTS, 2 * PART_STATE, FOLD * S5_GROUP)

    def part_vec(x):
        lead = x.shape[:-2]
        xp = x.reshape(lead + (S5_PARTS, PART_STATE))
        return jnp.moveaxis(xp, -2, 0)

    lvl = jnp.stack([jnp.stack([part_vec(q_re[k]), part_vec(q_im[k])], axis=1) for k in (1, 2, 4)], axis=1)
    lvl = lvl[:, :, :, None, :]
    rowpow = jnp.stack([part_vec(q_re[1:]), part_vec(q_im[1:])], axis=1)
    return xq, rc, vc, lvl, rowpow


def _iota2(shape):
    return (lax.broadcasted_iota(jnp.int32, shape, 0), lax.broadcasted_iota(jnp.int32, shape, 1))


def _s5_expand(xq, rc, vc, w1_s, tv_s):
    ps = PART_STATE
    lg_state, lg_group, lg_part = (v.bit_length() - 1 for v in (S5_STATE, S5_GROUP, PART_W))
    lg_pg = PART_GROUPS.bit_length() - 1
    grp = PART_GROUPS - 1
    one_hot = lambda m: jnp.where(m, 1.0, 0.0).astype(BF16)
    r, c = _iota2((2 * S5_STATE, 2 * ps))
    e1 = one_hot(((r >> lg_state) == (c >> (lg_state + lg_pg))) & ((r & (S5_STATE - 1)) == (c & (S5_STATE - 1))))
    r, c = _iota2((FOLD * S5_GROUP, FOLD_W))
    e2 = one_hot(((r >> lg_group) == (c >> lg_part)) & ((r & (S5_GROUP - 1)) == (c & (S5_GROUP - 1))))
    r, c = _iota2((FOLD_W, 2 * ps))
    m1 = ((r >> lg_group) & grp) == ((c >> lg_state) & grp)
    w1_s[...] = jnp.where(m1, _dot(xq.astype(BF16), e1), 0.0).astype(BF16)
    r, c = _iota2((PART_W, FOLD_W))
    m2 = (r >> lg_group) == ((c >> lg_group) & grp)
    r0 = jnp.where(m2, _dot(rc.astype(BF16), e2), 0.0).astype(BF16)
    for j in range(FOLD):
        if j == 0:
            blk = r0
        else:
            blk = jnp.concatenate([jnp.zeros((PART_W, j * PART_W), BF16), r0[:, :FOLD_W - j * PART_W]], axis=1)
        tv_s[j * PART_W:(j + 1) * PART_W, :] = blk
    r, c = _iota2((2 * ps, FOLD_W))
    m3 = ((r >> lg_state) & grp) == ((c >> lg_group) & grp)
    tv_s[FOLD_W:, :] = jnp.where(m3, _dot(vc.astype(BF16), e2), 0.0).astype(BF16)


def _s5_kernel(u_ref, xq_ref, rc_ref, vc_ref, lvl_ref, rp_ref, y_ref, w1_s, tv_s, st_ref):
    ps = PART_STATE

    @pl.when(pl.program_id(1) == 0)
    def _():
        _s5_expand(xq_ref[0, 0], rc_ref[0, 0], vc_ref[0, 0], w1_s, tv_s)

    u = jnp.concatenate([u_ref[0, 0, pl.ds(j, ROWS, stride=FOLD), :] for j in range(FOLD)],
                        axis=-1).astype(BF16)
    f = _dot(u, w1_s[...])
    fr = f[:, :ps]
    fi = f[:, ps:]
    row = lax.broadcasted_iota(jnp.int32, (ROWS, ps), 0) & (SUBLANES - 1)
    for lv, k in enumerate((1, 2, 4)):
        ar = lvl_ref[0, 0, lv, 0]
        ai = lvl_ref[0, 0, lv, 1]
        sr = pltpu.roll(fr, k, axis=0)
        si = pltpu.roll(fi, k, axis=0)
        m = row >= k
        fr, fi = (fr + jnp.where(m, ar * sr - ai * si, 0.0),
                  fi + jnp.where(m, ar * si + ai * sr, 0.0))
    st_ref[0:SUBLANES, :] = jnp.zeros((SUBLANES, 2 * ps), F32)
    st_ref[SUBLANES:, :ps] = fr
    st_ref[SUBLANES:, ps:] = fi
    pr = rp_ref[0, 0, 0]
    pi = rp_ref[0, 0, 1]

    def body(i, carry):
        cr, ci = carry
        r = pl.multiple_of(SUBLANES + i * SUBLANES, SUBLANES)
        xr = st_ref[pl.ds(r, SUBLANES), :ps]
        xi = st_ref[pl.ds(r, SUBLANES), ps:]
        hr = xr + pr * cr - pi * ci
        hi = xi + pr * ci + pi * cr
        st_ref[pl.ds(r, SUBLANES), :ps] = hr
        st_ref[pl.ds(r, SUBLANES), ps:] = hi
        return hr[SUBLANES - 1:SUBLANES], hi[SUBLANES - 1:SUBLANES]

    zero = jnp.zeros((1, ps), F32)
    lax.fori_loop(0, ROWS // SUBLANES, body, (zero, zero))
    h_prev = st_ref[pl.ds(SUBLANES - 1, ROWS), :].astype(BF16)
    y = jax.nn.gelu(_dot(u, tv_s[:FOLD_W, :]) + _dot(h_prev, tv_s[FOLD_W:, :]))
    for j in range(FOLD):
        y_ref[0, 0, pl.ds(j, ROWS, stride=FOLD), :] = y[:, j * PART_W:(j + 1) * PART_W]


def _s5_scan(u_parts, ops, layer, bsz):
    xq, rc, vc, lvl, rowpow = ops
    n = u_parts.shape[1]
    u4 = u_parts.reshape(S5_PARTS, bsz, T_PAD, PART_W)
    lay4 = lambda q, b: (layer, q, 0, 0)
    y4 = pl.pallas_call(
        _s5_kernel,
        grid=(S5_PARTS, bsz),
        in_specs=[
            pl.BlockSpec((1, 1, T_PAD, PART_W), lambda q, b: (q, b, 0, 0)),
            pl.BlockSpec((1, 1, FOLD_W, 2 * S5_STATE), lay4),
            pl.BlockSpec((1, 1, PART_W, FOLD * S5_GROUP), lay4),
            pl.BlockSpec((1, 1, 2 * PART_STATE, FOLD * S5_GROUP), lay4),
            pl.BlockSpec((1, 1, 3, 2, 1, PART_STATE), lambda q, b: (layer, q, 0, 0, 0, 0)),
            pl.BlockSpec((1, 1, 2, FOLD, PART_STATE), lambda q, b: (layer, q, 0, 0, 0)),
        ],
        out_specs=pl.BlockSpec((1, 1, T_PAD, PART_W), lambda q, b: (q, b, 0, 0)),
        out_shape=jax.ShapeDtypeStruct((S5_PARTS, bsz, T_PAD, PART_W), F32),
        scratch_shapes=[
            pltpu.VMEM((FOLD_W, 2 * PART_STATE), BF16),
            pltpu.VMEM((FOLD_W + 2 * PART_STATE, FOLD_W), BF16),
            pltpu.VMEM((ROWS + SUBLANES, 2 * PART_STATE), F32),
        ],
        compiler_params=pltpu.CompilerParams(
            dimension_semantics=("arbitrary", "arbitrary"), vmem_limit_bytes=VMEM_LIMIT),
        name="s5_scan",
    )(u4, xq, rc, vc, lvl, rowpow)
    return y4.reshape(S5_PARTS, n, PART_W)


def _lru_kernel(x_ref, g_ref, cw_ref, cb_ref, wri_ref, bri_ref, nsp_ref, o_ref, xp_ref, a_ref, b_ref, h_ref):
    tc = LRU_CHUNK
    c = LRU_WIDTH

    @pl.when(pl.program_id(1) == 0)
    def _():
        xp_ref[0:SUBLANES, :] = jnp.zeros((SUBLANES, c), F32)
        h_ref[...] = jnp.zeros((1, c), F32)

    xp_ref[SUBLANES:, :] = x_ref[0].astype(F32)
    xc = cb_ref[0]
    for k in range(CONV_WIDTH):
        off = SUBLANES - (CONV_WIDTH - 1) + k
        xc = xc + cw_ref[0, k:k + 1, :] * xp_ref[pl.ds(off, tc), :]
    xp_ref[0:SUBLANES, :] = xp_ref[tc:tc + SUBLANES, :]
    ri = jax.nn.sigmoid(_dot(xc.astype(BF16), wri_ref[0]) + bri_ref[0])
    r = ri[:, :c]
    ig = ri[:, c:]
    log_a = r * nsp_ref[0]
    a = jnp.exp(log_a)
    b = jnp.sqrt(1.0 - a * a) * (ig * xc)
    row = lax.broadcasted_iota(jnp.int32, (tc, c), 0) & (SUBLANES - 1)
    for k in (1, 2, 4):
        a_s = pltpu.roll(a, k, axis=0)
        b_s = pltpu.roll(b, k, axis=0)
        m = row >= k
        b = b + jnp.where(m, a * b_s, 0.0)
        a = jnp.where(m, a * a_s, a)
    a_ref[...] = a
    b_ref[...] = b

    def body(i, carry):
        r0 = pl.multiple_of(i * SUBLANES, SUBLANES)
        h = b_ref[pl.ds(r0, SUBLANES), :] + a_ref[pl.ds(r0, SUBLANES), :] * carry
        b_ref[pl.ds(r0, SUBLANES), :] = h
        return h[SUBLANES - 1:SUBLANES]

    h_ref[...] = lax.fori_loop(0, tc // SUBLANES, body, h_ref[...])
    o_ref[0] = (b_ref[...] * jax.nn.gelu(g_ref[0].astype(F32))).astype(BF16)


def _lru(x_lru, g_lru, conv_w, conv_b, w_ri, b_ri, neg_sp, layer, bsz):
    n = x_lru.shape[0]
    c = LRU_WIDTH
    x3 = x_lru.reshape(bsz, T_PAD, c)
    g3 = g_lru.reshape(bsz, T_PAD, c)
    lay = lambda b, t: (layer, 0, 0)
    out = pl.pallas_call(
        _lru_kernel,
        grid=(bsz, T_PAD // LRU_CHUNK),
        in_specs=[
            pl.BlockSpec((1, LRU_CHUNK, c), lambda b, t: (b, t, 0)),
            pl.BlockSpec((1, LRU_CHUNK, c), lambda b, t: (b, t, 0)),
            _const_spec((1, CONV_WIDTH, c), lay),
            _const_spec((1, 1, c), lay),
            _const_spec((1, c, 2 * c), lay),
            _const_spec((1, 1, 2 * c), lay),
            _const_spec((1, 1, c), lay),
        ],
        out_specs=pl.BlockSpec((1, LRU_CHUNK, c), lambda b, t: (b, t, 0)),
        out_shape=jax.ShapeDtypeStruct((bsz, T_PAD, c), BF16),
        scratch_shapes=[
            pltpu.VMEM((LRU_CHUNK + SUBLANES, c), F32),
            pltpu.VMEM((LRU_CHUNK, c), F32),
            pltpu.VMEM((LRU_CHUNK, c), F32),
            pltpu.VMEM((1, c), F32),
        ],
        compiler_params=pltpu.CompilerParams(
            dimension_semantics=("arbitrary", "arbitrary"), vmem_limit_bytes=VMEM_LIMIT),
        name="rglru",
    )(x3, g3, conv_w, conv_b, w_ri, b_ri, neg_sp)
    return out.reshape(n, c)


def _merge_kernel(hs_ref, ys_ref, yl_ref, gt_ref, wglu_ref, bglu_ref, wsp_ref, wlp_ref, wout_ref, g_ref,
                  *rest, with_router):
    if with_router:
        rw_ref, rb_ref, hs_out_ref, hn_ref, cw_ref = rest
    else:
        hs_out_ref, hn_ref = rest
    ys = jnp.concatenate([ys_ref[q] for q in range(S5_PARTS)], axis=-1)
    glu = ys * jax.nn.sigmoid(_dot(ys.astype(BF16), wglu_ref[0]) + bglu_ref[0])
    y_a = _dot(glu.astype(BF16), wsp_ref[0])
    y_b = _dot(yl_ref[...], wlp_ref[0])
    y = gt_ref[:, :D_MODEL].astype(F32) * y_a + gt_ref[:, D_MODEL:].astype(F32) * y_b
    hs = hs_ref[...] + _dot(y.astype(BF16), wout_ref[0])
    hs_out_ref[...] = hs
    hn = _rms(hs, g_ref[0])
    hn_ref[...] = hn.astype(BF16)
    if with_router:
        logits = jnp.dot(hn, rw_ref[0], preferred_element_type=F32,
                         precision=lax.Precision.HIGHEST) + rb_ref[0]
        lane = lax.broadcasted_iota(jnp.int32, logits.shape, 1).astype(F32)
        m1 = jnp.max(logits, axis=-1, keepdims=True)
        i1 = jnp.min(jnp.where(logits == m1, lane, float(LANES)), axis=-1, keepdims=True)
        rest_l = jnp.where(lane == i1, MASKED_LOGIT, logits)
        m2 = jnp.max(rest_l, axis=-1, keepdims=True)
        i2 = jnp.min(jnp.where(rest_l == m2, lane, float(LANES)), axis=-1, keepdims=True)
        e2 = jnp.exp(m2 - m1)
        g1 = 1.0 / (1.0 + e2)
        g2 = e2 / (1.0 + e2)
        cw_ref[...] = jnp.where(lane == i1, g1, 0.0) + jnp.where(lane == i2, g2, 0.0)


def _merge(hs, ys_parts, y_lru, gates, w_glu, b_glu, w_sp, w_lp, w_out, ffn_norm, layer, router=None):
    n = hs.shape[0]
    lay = lambda i: (layer, 0, 0)
    in_specs = [
        pl.BlockSpec((TM, D_MODEL), lambda i: (i, 0)),
        pl.BlockSpec((S5_PARTS, TM, PART_W), lambda i: (0, i, 0)),
        pl.BlockSpec((TM, LRU_WIDTH), lambda i: (i, 0)),
        pl.BlockSpec((TM, 2 * D_MODEL), lambda i: (i, 0)),
        _const_spec((1, S5_WIDTH, S5_WIDTH), lay),
        _const_spec((1, 1, S5_WIDTH), lay),
        _const_spec((1, S5_WIDTH, D_MODEL), lay),
        _const_spec((1, LRU_WIDTH, D_MODEL), lay),
        _const_spec((1, D_MODEL, D_MODEL), lay),
        _const_spec((1, 1, D_MODEL), lay),
    ]
    out_specs = [pl.BlockSpec((TM, D_MODEL), lambda i: (i, 0)),
                 pl.BlockSpec((TM, D_MODEL), lambda i: (i, 0))]
    out_shape = [jax.ShapeDtypeStruct((n, D_MODEL), F32),
                 jax.ShapeDtypeStruct((n, D_MODEL), BF16)]
    args = [hs, ys_parts, y_lru, gates, w_glu, b_glu, w_sp, w_lp, w_out, ffn_norm]
    if router is not None:
        rw, rb, j = router
        in_specs += [_const_spec((1, D_MODEL, LANES), lambda i: (j, 0, 0)),
                     _const_spec((1, 1, LANES), lambda i: (j, 0, 0))]
        out_specs.append(pl.BlockSpec((TM, LANES), lambda i: (i, 0)))
        out_shape.append(jax.ShapeDtypeStruct((n, LANES), F32))
        args += [rw, rb]
    return pl.pallas_call(
        functools.partial(_merge_kernel, with_router=router is not None),
        grid=(n // TM,),
        in_specs=in_specs,
        out_specs=out_specs,
        out_shape=out_shape,
        compiler_params=pltpu.CompilerParams(
            dimension_semantics=("arbitrary",), vmem_limit_bytes=VMEM_LIMIT),
        name="merge_router" if router is not None else "merge",
    )(*args)


def _ffn_kernel(*refs, chunks_per_expert, with_cw):
    if with_cw:
        x_ref, hs_ref, cw_ref, wg_ref, wu_ref, wd_ref, o_ref, acc_ref = refs
    else:
        x_ref, hs_ref, wg_ref, wu_ref, wd_ref, o_ref, acc_ref = refs
    c = pl.program_id(1)

    @pl.when(c == 0)
    def _():
        acc_ref[...] = jnp.zeros_like(acc_ref)

    x = x_ref[...]
    g = _dot(x, wg_ref[0].astype(BF16))
    h = g * jax.nn.sigmoid(g) * _dot(x, wu_ref[0].astype(BF16))
    if with_cw:
        e = c // chunks_per_expert
        cw = cw_ref[...]
        lane = lax.broadcasted_iota(jnp.int32, cw.shape, 1)
        h = h * jnp.sum(jnp.where(lane == e, cw, 0.0), axis=-1, keepdims=True)
    acc_ref[...] += _dot(h.astype(BF16), wd_ref[0].astype(BF16))

    @pl.when(c == pl.num_programs(1) - 1)
    def _():
        o_ref[...] = hs_ref[...] + acc_ref[...]


def _ffn(hn, hs, w_gate, w_up, w_down, first, n_sets, cw=None):
    n = hn.shape[0]
    ff = w_gate.shape[-1]
    cpe = ff // FF_CHUNK
    with_cw = cw is not None
    in_specs = [pl.BlockSpec((TM_FFN, D_MODEL), lambda i, c: (i, 0)),
                pl.BlockSpec((TM_FFN, D_MODEL), lambda i, c: (i, 0))]
    args = [hn, hs]
    if with_cw:
        in_specs.append(pl.BlockSpec((TM_FFN, LANES), lambda i, c: (i, 0)))
        args.append(cw)
    in_specs += [
        pl.BlockSpec((1, D_MODEL, FF_CHUNK), lambda i, c: (first + c // cpe, 0, c % cpe)),
        pl.BlockSpec((1, D_MODEL, FF_CHUNK), lambda i, c: (first + c // cpe, 0, c % cpe)),
        pl.BlockSpec((1, FF_CHUNK, D_MODEL), lambda i, c: (first + c // cpe, c % cpe, 0)),
    ]
    args += [w_gate, w_up, w_down]
    return pl.pallas_call(
        functools.partial(_ffn_kernel, chunks_per_expert=cpe, with_cw=with_cw),
        grid=(n // TM_FFN, n_sets * cpe),
        in_specs=in_specs,
        out_specs=pl.BlockSpec((TM_FFN, D_MODEL), lambda i, c: (i, 0)),
        out_shape=jax.ShapeDtypeStruct((n, D_MODEL), F32),
        scratch_shapes=[pltpu.VMEM((TM_FFN, D_MODEL), F32)],
        compiler_params=pltpu.CompilerParams(
            dimension_semantics=("arbitrary", "arbitrary"), vmem_limit_bytes=VMEM_LIMIT),
        name="moe_ffn" if with_cw else "dense_ffn",
    )(*args)


def _final_kernel(a_ref, b_ref, g_ref, o_ref):
    tb = a_ref.shape[1]
    o_ref[0, :tb - N_META] = _rms(a_ref[0, N_META:], g_ref[...])
    o_ref[0, tb - N_META:] = _rms(b_ref[0], g_ref[...])


def _final_norm(hs, g, bsz, seq):
    hs3 = hs.reshape(bsz, T_PAD, D_MODEL)
    return pl.pallas_call(
        _final_kernel,
        grid=(bsz, seq // TB_FINAL),
        in_specs=[pl.BlockSpec((1, TB_FINAL, D_MODEL), lambda b, i: (b, i, 0)),
                  pl.BlockSpec((1, N_META, D_MODEL), lambda b, i: (b, (i + 1) * (TB_FINAL // N_META), 0)),
                  _const_spec((1, D_MODEL), lambda b, i: (0, 0))],
        out_specs=pl.BlockSpec((1, TB_FINAL, D_MODEL), lambda b, i: (b, i, 0)),
        out_shape=jax.ShapeDtypeStruct((bsz, seq, D_MODEL), F32),
        compiler_params=pltpu.CompilerParams(
            dimension_semantics=("arbitrary", "arbitrary"), vmem_limit_bytes=VMEM_LIMIT),
        name="final_norm",
    )(hs3, hs3, g)


def _head_blockdiag(w):
    eye = jnp.eye(LRU_HEADS, dtype=w.dtype)
    out = jnp.einsum('lnhk,nm->lnhmk', w, eye)
    return out.reshape(w.shape[0], LRU_WIDTH, LRU_WIDTH)


def kernel(x, meta_tokens, mix_norm, w_in, merge_bias, s5_lambda_re, s5_lambda_im, s5_log_dt, s5_b_re, s5_b_im, s5_c_re, s5_c_im, s5_d, s5_w_glu, s5_b_glu, s5_w_proj, lru_conv_w, lru_conv_b, lru_w_rgate, lru_b_rgate, lru_w_igate, lru_b_igate, lru_lambda, lru_w_proj, w_out, ffn_norm, dense_w_gate, dense_w_up, dense_w_down, router_w, router_b, moe_w_gate, moe_w_up, moe_w_down, final_norm):
    bsz, seq, d = x.shape
    depth = w_in.shape[0]
    assert d == D_MODEL and N_META + seq <= T_PAD
    n = bsz * T_PAD
    assert n % TM == 0 and n % TM_FFN == 0 and seq % TB_FINAL == 0 and TB_FINAL % N_META == 0

    meta = jnp.broadcast_to(meta_tokens[None].astype(x.dtype), (bsz, N_META, d))
    pad = jnp.zeros((bsz, T_PAD - N_META - seq, d), x.dtype)
    hs = jnp.concatenate([meta, x, pad], axis=1).reshape(n, d)

    row3 = lambda a: a[:, None, :]
    w_in_b = w_in.astype(BF16)
    w_glu_b = s5_w_glu.astype(BF16)
    w_sp_b = s5_w_proj.astype(BF16)
    w_lp_b = lru_w_proj.astype(BF16)
    w_out_b = w_out.astype(BF16)
    w_ri = jnp.concatenate([_head_blockdiag(lru_w_rgate), _head_blockdiag(lru_w_igate)], axis=-1).astype(BF16)
    b_ri = jnp.concatenate([lru_b_rgate, lru_b_igate], axis=-1)
    neg_sp = -LRU_C * jax.nn.softplus(-lru_lambda)
    dense = (dense_w_gate, dense_w_up, dense_w_down)
    n_moe = router_w.shape[0]
    moe = [w.reshape((n_moe * N_EXPERTS,) + w.shape[2:]) for w in (moe_w_gate, moe_w_up, moe_w_down)]
    s5_ops = jax.vmap(_s5_prep)(s5_lambda_re, s5_lambda_im, s5_log_dt, s5_b_re, s5_b_im, s5_c_re, s5_c_im, s5_d)
    rw_pad = jnp.pad(router_w, ((0, 0), (0, 0), (0, LANES - N_EXPERTS)))
    rb_pad = jnp.pad(router_b, ((0, 0), (0, LANES - N_EXPERTS)), constant_values=MASKED_LOGIT)

    for layer in range(depth):
        u_parts, x_lru, g_lru, gates = _in_proj(hs, row3(mix_norm), w_in_b, row3(merge_bias), layer)
        ys_parts = _s5_scan(u_parts, s5_ops, layer, bsz)
        y_lru = _lru(x_lru, g_lru, lru_conv_w, row3(lru_conv_b), w_ri, row3(b_ri), row3(neg_sp), layer, bsz)
        j = layer // 2
        router = (rw_pad, row3(rb_pad), j) if layer % 2 == 1 else None
        res = _merge(hs, ys_parts, y_lru, gates, w_glu_b, row3(s5_b_glu), w_sp_b, w_lp_b, w_out_b,
                     row3(ffn_norm), layer, router)
        if layer % 2 == 0:
            hs, hn = res
            hs = _ffn(hn, hs, *dense, first=j, n_sets=1)
        else:
            hs, hn, cw = res
            hs = _ffn(hn, hs, *moe, first=j * N_EXPERTS, n_sets=N_EXPERTS, cw=cw)

    return _final_norm(hs, final_norm[None, :], bsz, seq)
```

```python
import functools

import jax
import jax.numpy as jnp
from jax import lax
from jax.experimental import pallas as pl
from jax.experimental.pallas import tpu as pltpu

F32 = jnp.float32
BF16 = jnp.bfloat16

D_MODEL = 1024
N_META = 16
S5_WIDTH = 512
S5_GROUP = 16
S5_GROUPS = 32
S5_STATE = 64
LRU_WIDTH = 512
LRU_HEADS = 8
LRU_HEAD_DIM = 64
CONV_WIDTH = 4
LRU_C = 8.0
N_EXPERTS = 8
EPS = 1e-6

FOLD = 8
S5_PARTS = 4
PART_W = S5_WIDTH // S5_PARTS
PART_GROUPS = PART_W // S5_GROUP
PART_STATE = PART_GROUPS * S5_STATE
FOLD_W = FOLD * PART_W

T_PAD = 8256
ROWS = T_PAD // FOLD
TM = 688
TM_FFN = 1376
FF_CHUNK = 512
MOE_BLOCK = 512
LRU_CHUNK = 1032
TB_FINAL = 512
VMEM_LIMIT = 56 * 1024 * 1024
LANES = 128
SUBLANES = 8
MASKED_LOGIT = float("-inf")


def _dot(a, b):
    return jnp.dot(a, b, preferred_element_type=F32)


def _const_spec(block_shape, index_map):
    return pl.BlockSpec(block_shape, index_map, pipeline_mode=pl.Buffered(1))


def _rms(x, g):
    ms = jnp.mean(x * x, axis=-1, keepdims=True)
    return x * lax.rsqrt(ms + EPS) * g


def _in_proj_kernel(hs_ref, g_ref, w_ref, mb_ref, u_ref, xl_ref, gl_ref, gt_ref):
    hn = _rms(hs_ref[...], g_ref[0]).astype(BF16)
    u = _dot(hn, w_ref[0, :, 0:S5_WIDTH])
    for q in range(S5_PARTS):
        u_ref[q] = u[:, q * PART_W:(q + 1) * PART_W]
    o_x = S5_WIDTH
    o_g = o_x + LRU_WIDTH
    o_m = o_g + LRU_WIDTH
    xl_ref[...] = _dot(hn, w_ref[0, :, o_x:o_g]).astype(BF16)
    gl_ref[...] = _dot(hn, w_ref[0, :, o_g:o_m]).astype(BF16)
    z = _dot(hn, w_ref[0, :, o_m:]) + mb_ref[0]
    gt_ref[...] = jax.nn.sigmoid(z).astype(BF16)


def _in_proj(hs, mix_norm, w_in, merge_bias, layer):
    n = hs.shape[0]
    d_in = w_in.shape[-1]
    lay = lambda i: (layer, 0, 0)
    return pl.pallas_call(
        _in_proj_kernel,
        grid=(n // TM,),
        in_specs=[
            pl.BlockSpec((TM, D_MODEL), lambda i: (i, 0)),
            _const_spec((1, 1, D_MODEL), lay),
            _const_spec((1, D_MODEL, d_in), lay),
            _const_spec((1, 1, 2 * D_MODEL), lay),
        ],
        out_specs=[
            pl.BlockSpec((S5_PARTS, TM, PART_W), lambda i: (0, i, 0)),
            pl.BlockSpec((TM, LRU_WIDTH), lambda i: (i, 0)),
            pl.BlockSpec((TM, LRU_WIDTH), lambda i: (i, 0)),
            pl.BlockSpec((TM, 2 * D_MODEL), lambda i: (i, 0)),
        ],
        out_shape=[
            jax.ShapeDtypeStruct((S5_PARTS, n, PART_W), F32),
            jax.ShapeDtypeStruct((n, LRU_WIDTH), BF16),
            jax.ShapeDtypeStruct((n, LRU_WIDTH), BF16),
            jax.ShapeDtypeStruct((n, 2 * D_MODEL), BF16),
        ],
        compiler_params=pltpu.CompilerParams(
            dimension_semantics=("arbitrary",), vmem_limit_bytes=VMEM_LIMIT),
        name="in_proj",
    )(hs, mix_norm, w_in, merge_bias)


def _s5_prep(lam_re, lam_im, log_dt, b_re, b_im, c_re, c_im, d_skip):
    hi = lax.Precision.HIGHEST
    dt = jnp.exp(log_dt)[:, None]
    mag = jnp.exp(lam_re * dt)
    a_re = mag * jnp.cos(lam_im * dt)
    a_im = mag * jnp.sin(lam_im * dt)
    den = lam_re * lam_re + lam_im * lam_im
    num_re = a_re - 1.0
    coef_re = (num_re * lam_re + a_im * lam_im) / den
    coef_im = (a_im * lam_re - num_re * lam_im) / den
    bb_re = coef_re[..., None] * b_re - coef_im[..., None] * b_im
    bb_im = coef_re[..., None] * b_im + coef_im[..., None] * b_re

    def cmul(xr, xi, yr, yi):
        return xr * yr - xi * yi, xr * yi + xi * yr

    def powers(br, bi, n):
        pr, pi = [jnp.ones_like(br)], [jnp.zeros_like(bi)]
        for _ in range(n):
            r, i = cmul(pr[-1], pi[-1], br, bi)
            pr.append(r)
            pi.append(i)
        return jnp.stack(pr), jnp.stack(pi)

    p_re, p_im = powers(a_re, a_im, FOLD)
    q_re, q_im = powers(p_re[FOLD], p_im[FOLD], FOLD)

    def per_part(x):
        lead = x.shape[:-3]
        xp = x.reshape(lead + (S5_PARTS, PART_GROUPS) + x.shape[-2:])
        return jnp.moveaxis(xp, len(lead), 0)

    rev_re = jnp.stack([p_re[FOLD - 1 - j] for j in range(FOLD)])
    rev_im = jnp.stack([p_im[FOLD - 1 - j] for j in range(FOLD)])
    wr, wi = cmul(rev_re[..., None], rev_im[..., None], bb_re[None], bb_im[None])
    w_ri = jnp.swapaxes(jnp.stack([wr, wi], axis=1), -1, -2)
    xq = jnp.transpose(per_part(w_ri), (0, 1, 3, 4, 2, 5)).reshape(S5_PARTS, FOLD_W, 2 * S5_STATE)

    ca_re, ca_im = cmul(c_re[None], c_im[None], p_re[:, :, None, :], p_im[:, :, None, :])
    taps = (jnp.einsum('kghs,gsi->kgih', ca_re[:FOLD], bb_re, precision=hi)
            - jnp.einsum('kghs,gsi->kgih', ca_im[:FOLD], bb_im, precision=hi))
    skip = d_skip.reshape(S5_GROUPS, S5_GROUP)
    taps = taps.at[0].add(skip[:, :, None] * jnp.eye(S5_GROUP, dtype=F32)[None])
    rc = jnp.transpose(per_part(taps), (0, 2, 3, 1, 4)).reshape(S5_PARTS, PART_W, FOLD * S5_GROUP)

    v_ri = jnp.swapaxes(jnp.stack([ca_re[1:], -ca_im[1:]], axis=0), -1, -2)
    vc = jnp.transpose(per_part(v_ri), (0, 1, 3, 4, 2, 5)).reshape(S5_PARTS, 2 * PART_STATE, FOLD * S5_GROUP)

    def part_vec(x):
        lead = x.shape[:-2]
        xp = x.reshape(lead + (S5_PARTS, PART_STATE))
        return jnp.moveaxis(xp, -2, 0)

    lvl = jnp.stack([jnp.stack([part_vec(q_re[k]), part_vec(q_im[k])], axis=1) for k in (1, 2, 4)], axis=1)
    lvl = lvl[:, :, :, None, :]
    rowpow = jnp.stack([part_vec(q_re[1:]), part_vec(q_im[1:])], axis=1)
    return xq, rc, vc, lvl, rowpow


def _iota2(shape):
    return (lax.broadcasted_iota(jnp.int32, shape, 0), lax.broadcasted_iota(jnp.int32, shape, 1))


def _s5_expand(xq, rc, vc, w1_s, tv_s):
    ps = PART_STATE
    lg_state, lg_group, lg_part = (v.bit_length() - 1 for v in (S5_STATE, S5_GROUP, PART_W))
    lg_pg = PART_GROUPS.bit_length() - 1
    grp = PART_GROUPS - 1
    one_hot = lambda m: jnp.where(m, 1.0, 0.0).astype(BF16)
    r, c = _iota2((2 * S5_STATE, 2 * ps))
    e1 = one_hot(((r >> lg_state) == (c >> (lg_state + lg_pg))) & ((r & (S5_STATE - 1)) == (c & (S5_STATE - 1))))
    r, c = _iota2((FOLD * S5_GROUP, FOLD_W))
    e2 = one_hot(((r >> lg_group) == (c >> lg_part)) & ((r & (S5_GROUP - 1)) == (c & (S5_GROUP - 1))))
    r, c = _iota2((FOLD_W, 2 * ps))
    m1 = ((r >> lg_group) & grp) == ((c >> lg_state) & grp)
    w1_s[...] = jnp.where(m1, _dot(xq.astype(BF16), e1), 0.0).astype(BF16)
    r, c = _iota2((PART_W, FOLD_W))
    m2 = (r >> lg_group) == ((c >> lg_group) & grp)
    r0 = jnp.where(m2, _dot(rc.astype(BF16), e2), 0.0).astype(BF16)
    for j in range(FOLD):
        if j == 0:
            blk = r0
        else:
            blk = jnp.concatenate([jnp.zeros((PART_W, j * PART_W), BF16), r0[:, :FOLD_W - j * PART_W]], axis=1)
        tv_s[j * PART_W:(j + 1) * PART_W, :] = blk
    r, c = _iota2((2 * ps, FOLD_W))
    m3 = ((r >> lg_state) & grp) == ((c >> lg_group) & grp)
    tv_s[FOLD_W:, :] = jnp.where(m3, _dot(vc.astype(BF16), e2), 0.0).astype(BF16)


def _s5_kernel(u_ref, xq_ref, rc_ref, vc_ref, lvl_ref, rp_ref, y_ref, w1_s, tv_s, st_ref):
    ps = PART_STATE

    @pl.when(pl.program_id(1) == 0)
    def _():
        _s5_expand(xq_ref[0, 0], rc_ref[0, 0], vc_ref[0, 0], w1_s, tv_s)

    u = jnp.concatenate([u_ref[0, 0, pl.ds(j, ROWS, stride=FOLD), :] for j in range(FOLD)],
                        axis=-1).astype(BF16)
    f = _dot(u, w1_s[...])
    fr = f[:, :ps]
    fi = f[:, ps:]
    row = lax.broadcasted_iota(jnp.int32, (ROWS, ps), 0) & (SUBLANES - 1)
    for lv, k in enumerate((1, 2, 4)):
        ar = lvl_ref[0, 0, lv, 0]
        ai = lvl_ref[0, 0, lv, 1]
        sr = pltpu.roll(fr, k, axis=0)
        si = pltpu.roll(fi, k, axis=0)
        m = row >= k
        fr, fi = (fr + jnp.where(m, ar * sr - ai * si, 0.0),
                  fi + jnp.where(m, ar * si + ai * sr, 0.0))
    st_ref[0:SUBLANES, :] = jnp.zeros((SUBLANES, 2 * ps), F32)
    st_ref[SUBLANES:, :ps] = fr
    st_ref[SUBLANES:, ps:] = fi
    pr = rp_ref[0, 0, 0]
    pi = rp_ref[0, 0, 1]

    def body(i, carry):
        cr, ci = carry
        r = pl.multiple_of(SUBLANES + i * SUBLANES, SUBLANES)
        xr = st_ref[pl.ds(r, SUBLANES), :ps]
        xi = st_ref[pl.ds(r, SUBLANES), ps:]
        hr = xr + pr * cr - pi * ci
        hi = xi + pr * ci + pi * cr
        st_ref[pl.ds(r, SUBLANES), :ps] = hr
        st_ref[pl.ds(r, SUBLANES), ps:] = hi
        return hr[SUBLANES - 1:SUBLANES], hi[SUBLANES - 1:SUBLANES]

    zero = jnp.zeros((1, ps), F32)
    lax.fori_loop(0, ROWS // SUBLANES, body, (zero, zero))
    h_prev = st_ref[pl.ds(SUBLANES - 1, ROWS), :].astype(BF16)
    y = jax.nn.gelu(_dot(u, tv_s[:FOLD_W, :]) + _dot(h_prev, tv_s[FOLD_W:, :]))
    for j in range(FOLD):
        y_ref[0, 0, pl.ds(j, ROWS, stride=FOLD), :] = y[:, j * PART_W:(j + 1) * PART_W]


def _s5_scan(u_parts, ops, layer, bsz):
    xq, rc, vc, lvl, rowpow = ops
    n = u_parts.shape[1]
    u4 = u_parts.reshape(S5_PARTS, bsz, T_PAD, PART_W)
    lay4 = lambda q, b: (layer, q, 0, 0)
    y4 = pl.pallas_call(
        _s5_kernel,
        grid=(S5_PARTS, bsz),
        in_specs=[
            pl.BlockSpec((1, 1, T_PAD, PART_W), lambda q, b: (q, b, 0, 0)),
            pl.BlockSpec((1, 1, FOLD_W, 2 * S5_STATE), lay4),
            pl.BlockSpec((1, 1, PART_W, FOLD * S5_GROUP), lay4),
            pl.BlockSpec((1, 1, 2 * PART_STATE, FOLD * S5_GROUP), lay4),
            pl.BlockSpec((1, 1, 3, 2, 1, PART_STATE), lambda q, b: (layer, q, 0, 0, 0, 0)),
            pl.BlockSpec((1, 1, 2, FOLD, PART_STATE), lambda q, b: (layer, q, 0, 0, 0)),
        ],
        out_specs=pl.BlockSpec((1, 1, T_PAD, PART_W), lambda q, b: (q, b, 0, 0)),
        out_shape=jax.ShapeDtypeStruct((S5_PARTS, bsz, T_PAD, PART_W), F32),
        scratch_shapes=[
            pltpu.VMEM((FOLD_W, 2 * PART_STATE), BF16),
            pltpu.VMEM((FOLD_W + 2 * PART_STATE, FOLD_W), BF16),
            pltpu.VMEM((ROWS + SUBLANES, 2 * PART_STATE), F32),
        ],
        compiler_params=pltpu.CompilerParams(
            dimension_semantics=("arbitrary", "arbitrary"), vmem_limit_bytes=VMEM_LIMIT),
        name="s5_scan",
    )(u4, xq, rc, vc, lvl, rowpow)
    return y4.reshape(S5_PARTS, n, PART_W)


def _lru_kernel(x_ref, g_ref, cw_ref, cb_ref, wri_ref, bri_ref, nsp_ref, o_ref, xp_ref, a_ref, b_ref, h_ref):
    tc = LRU_CHUNK
    c = LRU_WIDTH

    @pl.when(pl.program_id(1) == 0)
    def _():
        xp_ref[0:SUBLANES, :] = jnp.zeros((SUBLANES, c), F32)
        h_ref[...] = jnp.zeros((1, c), F32)

    xp_ref[SUBLANES:, :] = x_ref[0].astype(F32)
    xc = cb_ref[0]
    for k in range(CONV_WIDTH):
        off = SUBLANES - (CONV_WIDTH - 1) + k
        xc = xc + cw_ref[0, k:k + 1, :] * xp_ref[pl.ds(off, tc), :]
    xp_ref[0:SUBLANES, :] = xp_ref[tc:tc + SUBLANES, :]
    ri = jax.nn.sigmoid(_dot(xc.astype(BF16), wri_ref[0]) + bri_ref[0])
    r = ri[:, :c]
    ig = ri[:, c:]
    log_a = r * nsp_ref[0]
    a = jnp.exp(log_a)
    b = jnp.sqrt(1.0 - a * a) * (ig * xc)
    row = lax.broadcasted_iota(jnp.int32, (tc, c), 0) & (SUBLANES - 1)
    for k in (1, 2, 4):
        a_s = pltpu.roll(a, k, axis=0)
        b_s = pltpu.roll(b, k, axis=0)
        m = row >= k
        b = b + jnp.where(m, a * b_s, 0.0)
        a = jnp.where(m, a * a_s, a)
    a_ref[...] = a
    b_ref[...] = b

    def body(i, carry):
        r0 = pl.multiple_of(i * SUBLANES, SUBLANES)
        h = b_ref[pl.ds(r0, SUBLANES), :] + a_ref[pl.ds(r0, SUBLANES), :] * carry
        b_ref[pl.ds(r0, SUBLANES), :] = h
        return h[SUBLANES - 1:SUBLANES]

    h_ref[...] = lax.fori_loop(0, tc // SUBLANES, body, h_ref[...])
    o_ref[0] = (b_ref[...] * jax.nn.gelu(g_ref[0].astype(F32))).astype(BF16)


def _lru(x_lru, g_lru, conv_w, conv_b, w_ri, b_ri, neg_sp, layer, bsz):
    n = x_lru.shape[0]
    c = LRU_WIDTH
    x3 = x_lru.reshape(bsz, T_PAD, c)
    g3 = g_lru.reshape(bsz, T_PAD, c)
    lay = lambda b, t: (layer, 0, 0)
    out = pl.pallas_call(
        _lru_kernel,
        grid=(bsz, T_PAD // LRU_CHUNK),
        in_specs=[
            pl.BlockSpec((1, LRU_CHUNK, c), lambda b, t: (b, t, 0)),
            pl.BlockSpec((1, LRU_CHUNK, c), lambda b, t: (b, t, 0)),
            _const_spec((1, CONV_WIDTH, c), lay),
            _const_spec((1, 1, c), lay),
            _const_spec((1, c, 2 * c), lay),
            _const_spec((1, 1, 2 * c), lay),
            _const_spec((1, 1, c), lay),
        ],
        out_specs=pl.BlockSpec((1, LRU_CHUNK, c), lambda b, t: (b, t, 0)),
        out_shape=jax.ShapeDtypeStruct((bsz, T_PAD, c), BF16),
        scratch_shapes=[
            pltpu.VMEM((LRU_CHUNK + SUBLANES, c), F32),
            pltpu.VMEM((LRU_CHUNK, c), F32),
            pltpu.VMEM((LRU_CHUNK, c), F32),
            pltpu.VMEM((1, c), F32),
        ],
        compiler_params=pltpu.CompilerParams(
            dimension_semantics=("arbitrary", "arbitrary"), vmem_limit_bytes=VMEM_LIMIT),
        name="rglru",
    )(x3, g3, conv_w, conv_b, w_ri, b_ri, neg_sp)
    return out.reshape(n, c)


def _merge_kernel(hs_ref, ys_ref, yl_ref, gt_ref, wglu_ref, bglu_ref, wsp_ref, wlp_ref, wout_ref, g_ref,
                  *rest, with_router):
    if with_router:
        rw_ref, rb_ref, hs_out_ref, hn_ref, rt_ref = rest
    else:
        hs_out_ref, hn_ref = rest
    ys = jnp.concatenate([ys_ref[q] for q in range(S5_PARTS)], axis=-1)
    glu = ys * jax.nn.sigmoid(_dot(ys.astype(BF16), wglu_ref[0]) + bglu_ref[0])
    y_a = _dot(glu.astype(BF16), wsp_ref[0])
    y_b = _dot(yl_ref[...], wlp_ref[0])
    y = gt_ref[:, :D_MODEL].astype(F32) * y_a + gt_ref[:, D_MODEL:].astype(F32) * y_b
    hs = hs_ref[...] + _dot(y.astype(BF16), wout_ref[0])
    hs_out_ref[...] = hs
    hn = _rms(hs, g_ref[0])
    hn_ref[...] = hn.astype(hn_ref.dtype)
    if with_router:
        logits = jnp.dot(hn, rw_ref[0], preferred_element_type=F32,
                         precision=lax.Precision.HIGHEST) + rb_ref[0]
        lane = lax.broadcasted_iota(jnp.int32, logits.shape, 1).astype(F32)
        m1 = jnp.max(logits, axis=-1, keepdims=True)
        i1 = jnp.min(jnp.where(logits == m1, lane, float(LANES)), axis=-1, keepdims=True)
        rest_l = jnp.where(lane == i1, MASKED_LOGIT, logits)
        m2 = jnp.max(rest_l, axis=-1, keepdims=True)
        i2 = jnp.min(jnp.where(rest_l == m2, lane, float(LANES)), axis=-1, keepdims=True)
        e2 = jnp.exp(m2 - m1)
        g1 = 1.0 / (1.0 + e2)
        g2 = e2 / (1.0 + e2)
        rt_ref[...] = (jnp.where(lane == 0.0, i1, 0.0) + jnp.where(lane == 1.0, i2, 0.0)
                       + jnp.where(lane == 2.0, g1, 0.0) + jnp.where(lane == 3.0, g2, 0.0))


def _merge(hs, ys_parts, y_lru, gates, w_glu, b_glu, w_sp, w_lp, w_out, ffn_norm, layer, router=None):
    n = hs.shape[0]
    lay = lambda i: (layer, 0, 0)
    in_specs = [
        pl.BlockSpec((TM, D_MODEL), lambda i: (i, 0)),
        pl.BlockSpec((S5_PARTS, TM, PART_W), lambda i: (0, i, 0)),
        pl.BlockSpec((TM, LRU_WIDTH), lambda i: (i, 0)),
        pl.BlockSpec((TM, 2 * D_MODEL), lambda i: (i, 0)),
        _const_spec((1, S5_WIDTH, S5_WIDTH), lay),
        _const_spec((1, 1, S5_WIDTH), lay),
        _const_spec((1, S5_WIDTH, D_MODEL), lay),
        _const_spec((1, LRU_WIDTH, D_MODEL), lay),
        _const_spec((1, D_MODEL, D_MODEL), lay),
        _const_spec((1, 1, D_MODEL), lay),
    ]
    out_specs = [pl.BlockSpec((TM, D_MODEL), lambda i: (i, 0)),
                 pl.BlockSpec((TM, D_MODEL), lambda i: (i, 0))]
    out_shape = [jax.ShapeDtypeStruct((n, D_MODEL), F32),
                 jax.ShapeDtypeStruct((n, D_MODEL), BF16 if router is None else F32)]
    args = [hs, ys_parts, y_lru, gates, w_glu, b_glu, w_sp, w_lp, w_out, ffn_norm]
    if router is not None:
        rw, rb, j = router
        in_specs += [_const_spec((1, D_MODEL, LANES), lambda i: (j, 0, 0)),
                     _const_spec((1, 1, LANES), lambda i: (j, 0, 0))]
        out_specs.append(pl.BlockSpec((TM, LANES), lambda i: (i, 0)))
        out_shape.append(jax.ShapeDtypeStruct((n, LANES), F32))
        args += [rw, rb]
    return pl.pallas_call(
        functools.partial(_merge_kernel, with_router=router is not None),
        grid=(n // TM,),
        in_specs=in_specs,
        out_specs=out_specs,
        out_shape=out_shape,
        compiler_params=pltpu.CompilerParams(
            dimension_semantics=("arbitrary",), vmem_limit_bytes=VMEM_LIMIT),
        name="merge_router" if router is not None else "merge",
    )(*args)


def _ffn_kernel(x_ref, hs_ref, wg_ref, wu_ref, wd_ref, o_ref, acc_ref):
    c = pl.program_id(1)

    @pl.when(c == 0)
    def _():
        acc_ref[...] = jnp.zeros_like(acc_ref)

    x = x_ref[...]
    g = _dot(x, wg_ref[0].astype(BF16))
    h = g * jax.nn.sigmoid(g) * _dot(x, wu_ref[0].astype(BF16))
    acc_ref[...] += _dot(h.astype(BF16), wd_ref[0].astype(BF16))

    @pl.when(c == pl.num_programs(1) - 1)
    def _():
        o_ref[...] = hs_ref[...] + acc_ref[...]


def _ffn(hn, hs, w_gate, w_up, w_down, layer):
    n = hn.shape[0]
    ff = w_gate.shape[-1]
    return pl.pallas_call(
        _ffn_kernel,
        grid=(n // TM_FFN, ff // FF_CHUNK),
        in_specs=[
            pl.BlockSpec((TM_FFN, D_MODEL), lambda i, c: (i, 0)),
            pl.BlockSpec((TM_FFN, D_MODEL), lambda i, c: (i, 0)),
            pl.BlockSpec((1, D_MODEL, FF_CHUNK), lambda i, c: (layer, 0, c)),
            pl.BlockSpec((1, D_MODEL, FF_CHUNK), lambda i, c: (layer, 0, c)),
            pl.BlockSpec((1, FF_CHUNK, D_MODEL), lambda i, c: (layer, c, 0)),
        ],
        out_specs=pl.BlockSpec((TM_FFN, D_MODEL), lambda i, c: (i, 0)),
        out_shape=jax.ShapeDtypeStruct((n, D_MODEL), F32),
        scratch_shapes=[pltpu.VMEM((TM_FFN, D_MODEL), F32)],
        compiler_params=pltpu.CompilerParams(
            dimension_semantics=("arbitrary", "arbitrary"), vmem_limit_bytes=VMEM_LIMIT),
        name="dense_ffn",
    )(hn, hs, w_gate, w_up, w_down)


def _moe_plan(route, n):
    n_pairs = 2 * n
    n_blocks = -(-n_pairs // MOE_BLOCK) + N_EXPERTS
    e = route[:, :2].astype(jnp.int32).reshape(n_pairs)
    onehot = (e[:, None] == jnp.arange(N_EXPERTS, dtype=jnp.int32)[None, :]).astype(jnp.int32)
    csum = jnp.cumsum(onehot, axis=0)
    rank = jnp.sum(csum * onehot, axis=1) - 1
    counts = csum[-1]
    padded = ((counts + MOE_BLOCK - 1) // MOE_BLOCK) * MOE_BLOCK
    cum_pad = jnp.cumsum(padded)
    pos = jnp.sum((cum_pad - padded)[None, :] * onehot, axis=1) + rank
    slot_tok = jnp.zeros((n_blocks * MOE_BLOCK,), jnp.int32).at[pos].set(
        jnp.arange(n_pairs, dtype=jnp.int32) // 2)
    block_start = jnp.arange(n_blocks, dtype=jnp.int32) * MOE_BLOCK
    block_expert = jnp.minimum(jnp.searchsorted(cum_pad, block_start, side='right'),
                               N_EXPERTS - 1).astype(jnp.int32)
    n_used = (cum_pad[-1] // MOE_BLOCK).astype(jnp.int32).reshape(1)
    return slot_tok, pos.astype(jnp.int32), block_expert, n_used


def _row_gather(src_hbm, dst, sem, rows, index_of):
    def body(r, carry):
        pltpu.make_async_copy(src_hbm.at[pl.ds(index_of(r), 1)], dst.at[pl.ds(r, 1)], sem).start()
        return carry
    lax.fori_loop(0, rows, body, 0, unroll=8)


def _row_gather_wait(src_hbm, dst, sem, rows):
    pltpu.make_async_copy(src_hbm.at[pl.ds(0, rows)], dst, sem).wait()


def _moe_ffn_kernel(be_ref, st_ref, nu_ref, x_hbm, wg_ref, wu_ref, wd_ref, y_ref, xbuf, sem):
    i = pl.program_id(0)
    n_used = nu_ref[0]
    slot = i % 2

    def start(blk, s):
        _row_gather(x_hbm, xbuf.at[s], sem.at[s], MOE_BLOCK, lambda r: st_ref[blk * MOE_BLOCK + r])

    @pl.when((i == 0) & (n_used > 0))
    def _():
        start(0, 0)

    @pl.when(i + 1 < n_used)
    def _():
        start(i + 1, 1 - slot)

    @pl.when(i < n_used)
    def _():
        _row_gather_wait(x_hbm, xbuf.at[slot], sem.at[slot], MOE_BLOCK)
        x = xbuf[slot].astype(BF16)
        g = _dot(x, wg_ref[0])
        h = g * jax.nn.sigmoid(g) * _dot(x, wu_ref[0])
        y_ref[...] = _dot(h.astype(BF16), wd_ref[0])

    @pl.when(i >= n_used)
    def _():
        y_ref[...] = jnp.zeros_like(y_ref)


def _moe_ffn(hn, slot_tok, block_expert, n_used, w_gate, w_up, w_down, first):
    n_blocks = block_expert.shape[0]
    ff = w_gate.shape[-1]
    wmap = lambda i, be, st, nu: (first + be[i], 0, 0)
    return pl.pallas_call(
        _moe_ffn_kernel,
        grid_spec=pltpu.PrefetchScalarGridSpec(
            num_scalar_prefetch=3,
            grid=(n_blocks,),
            in_specs=[
                pl.BlockSpec(memory_space=pl.ANY),
                pl.BlockSpec((1, D_MODEL, ff), wmap),
                pl.BlockSpec((1, D_MODEL, ff), wmap),
                pl.BlockSpec((1, ff, D_MODEL), wmap),
            ],
            out_specs=pl.BlockSpec((MOE_BLOCK, D_MODEL), lambda i, be, st, nu: (i, 0)),
            scratch_shapes=[pltpu.VMEM((2, MOE_BLOCK, D_MODEL), F32),
                            pltpu.SemaphoreType.DMA((2,))],
        ),
        out_shape=jax.ShapeDtypeStruct((n_blocks * MOE_BLOCK, D_MODEL), F32),
        compiler_params=pltpu.CompilerParams(
            dimension_semantics=("arbitrary",), vmem_limit_bytes=VMEM_LIMIT),
        name="moe_ffn",
    )(block_expert, slot_tok, n_used, hn, w_gate, w_up, w_down)


def _moe_combine_kernel(pos_ref, hs_ref, rt_ref, ys_hbm, o_ref, ybuf, sem):
    i = pl.program_id(0)
    nt = pl.num_programs(0)
    slot = i % 2

    def start(t, s):
        for k in range(2):
            _row_gather(ys_hbm, ybuf.at[s, k], sem.at[s], TM, lambda r: pos_ref[2 * (t * TM + r) + k])

    @pl.when(i == 0)
    def _():
        start(0, 0)

    @pl.when(i + 1 < nt)
    def _():
        start(i + 1, 1 - slot)

    for k in range(2):
        _row_gather_wait(ys_hbm, ybuf.at[slot, k], sem.at[slot], TM)
    rt = rt_ref[...]
    lane = lax.broadcasted_iota(jnp.int32, rt.shape, 1)
    g1 = jnp.sum(jnp.where(lane == 2, rt, 0.0), axis=-1, keepdims=True)
    g2 = jnp.sum(jnp.where(lane == 3, rt, 0.0), axis=-1, keepdims=True)
    o_ref[...] = hs_ref[...] + g1 * ybuf[slot, 0] + g2 * ybuf[slot, 1]


def _moe_combine(hs, route, ys, pos):
    n = hs.shape[0]
    return pl.pallas_call(
        _moe_combine_kernel,
        grid_spec=pltpu.PrefetchScalarGridSpec(
            num_scalar_prefetch=1,
            grid=(n // TM,),
            in_specs=[
                pl.BlockSpec((TM, D_MODEL), lambda i, p: (i, 0)),
                pl.BlockSpec((TM, LANES), lambda i, p: (i, 0)),
                pl.BlockSpec(memory_space=pl.ANY),
            ],
            out_specs=pl.BlockSpec((TM, D_MODEL), lambda i, p: (i, 0)),
            scratch_shapes=[pltpu.VMEM((2, 2, TM, D_MODEL), F32),
                            pltpu.SemaphoreType.DMA((2,))],
        ),
        out_shape=jax.ShapeDtypeStruct((n, D_MODEL), F32),
        compiler_params=pltpu.CompilerParams(
            dimension_semantics=("arbitrary",), vmem_limit_bytes=VMEM_LIMIT),
        name="moe_combine",
    )(pos, hs, route, ys)


def _final_kernel(a_ref, b_ref, g_ref, o_ref):
    tb = a_ref.shape[1]
    o_ref[0, :tb - N_META] = _rms(a_ref[0, N_META:], g_ref[...])
    o_ref[0, tb - N_META:] = _rms(b_ref[0], g_ref[...])


def _final_norm(hs, g, bsz, seq):
    hs3 = hs.reshape(bsz, T_PAD, D_MODEL)
    return pl.pallas_call(
        _final_kernel,
        grid=(bsz, seq // TB_FINAL),
        in_specs=[pl.BlockSpec((1, TB_FINAL, D_MODEL), lambda b, i: (b, i, 0)),
                  pl.BlockSpec((1, N_META, D_MODEL), lambda b, i: (b, (i + 1) * (TB_FINAL // N_META), 0)),
                  _const_spec((1, D_MODEL), lambda b, i: (0, 0))],
        out_specs=pl.BlockSpec((1, TB_FINAL, D_MODEL), lambda b, i: (b, i, 0)),
        out_shape=jax.ShapeDtypeStruct((bsz, seq, D_MODEL), F32),
        compiler_params=pltpu.CompilerParams(
            dimension_semantics=("arbitrary", "arbitrary"), vmem_limit_bytes=VMEM_LIMIT),
        name="final_norm",
    )(hs3, hs3, g)


def _head_blockdiag(w):
    eye = jnp.eye(LRU_HEADS, dtype=w.dtype)
    out = jnp.einsum('lnhk,nm->lnhmk', w, eye)
    return out.reshape(w.shape[0], LRU_WIDTH, LRU_WIDTH)


def kernel(x, meta_tokens, mix_norm, w_in, merge_bias, s5_lambda_re, s5_lambda_im, s5_log_dt, s5_b_re, s5_b_im, s5_c_re, s5_c_im, s5_d, s5_w_glu, s5_b_glu, s5_w_proj, lru_conv_w, lru_conv_b, lru_w_rgate, lru_b_rgate, lru_w_igate, lru_b_igate, lru_lambda, lru_w_proj, w_out, ffn_norm, dense_w_gate, dense_w_up, dense_w_down, router_w, router_b, moe_w_gate, moe_w_up, moe_w_down, final_norm):
    bsz, seq, d = x.shape
    depth = w_in.shape[0]
    assert d == D_MODEL and N_META + seq <= T_PAD
    n = bsz * T_PAD
    assert n % TM == 0 and n % TM_FFN == 0 and seq % TB_FINAL == 0 and TB_FINAL % N_META == 0

    meta = jnp.broadcast_to(meta_tokens[None].astype(x.dtype), (bsz, N_META, d))
    pad = jnp.zeros((bsz, T_PAD - N_META - seq, d), x.dtype)
    hs = jnp.concatenate([meta, x, pad], axis=1).reshape(n, d)

    row3 = lambda a: a[:, None, :]
    w_in_b = w_in.astype(BF16)
    w_glu_b = s5_w_glu.astype(BF16)
    w_sp_b = s5_w_proj.astype(BF16)
    w_lp_b = lru_w_proj.astype(BF16)
    w_out_b = w_out.astype(BF16)
    w_ri = jnp.concatenate([_head_blockdiag(lru_w_rgate), _head_blockdiag(lru_w_igate)], axis=-1).astype(BF16)
    b_ri = jnp.concatenate([lru_b_rgate, lru_b_igate], axis=-1)
    neg_sp = -LRU_C * jax.nn.softplus(-lru_lambda)
    dense = (dense_w_gate, dense_w_up, dense_w_down)
    n_moe = router_w.shape[0]
    moe = [w.astype(BF16).reshape((n_moe * N_EXPERTS,) + w.shape[2:]) for w in (moe_w_gate, moe_w_up, moe_w_down)]
    s5_ops = jax.vmap(_s5_prep)(s5_lambda_re, s5_lambda_im, s5_log_dt, s5_b_re, s5_b_im, s5_c_re, s5_c_im, s5_d)
    rw_pad = jnp.pad(router_w, ((0, 0), (0, 0), (0, LANES - N_EXPERTS)))
    rb_pad = jnp.pad(router_b, ((0, 0), (0, LANES - N_EXPERTS)), constant_values=MASKED_LOGIT)

    for layer in range(depth):
        u_parts, x_lru, g_lru, gates = _in_proj(hs, row3(mix_norm), w_in_b, row3(merge_bias), layer)
        ys_parts = _s5_scan(u_parts, s5_ops, layer, bsz)
        y_lru = _lru(x_lru, g_lru, lru_conv_w, row3(lru_conv_b), w_ri, row3(b_ri), row3(neg_sp), layer, bsz)
        j = layer // 2
        router = (rw_pad, row3(rb_pad), j) if layer % 2 == 1 else None
        res = _merge(hs, ys_parts, y_lru, gates, w_glu_b, row3(s5_b_glu), w_sp_b, w_lp_b, w_out_b,
                     row3(ffn_norm), layer, router)
        if layer % 2 == 0:
            hs, hn = res
            hs = _ffn(hn, hs, *dense, layer=j)
        else:
            hs, hn, route = res
            slot_tok, pos, block_expert, n_used = _moe_plan(route, n)
            ys = _moe_ffn(hn, slot_tok, block_expert, n_used, *moe, first=j * N_EXPERTS)
            hs = _moe_combine(hs, route, ys, pos)

    return _final_norm(hs, final_norm[None, :], bsz, seq)
```

```python
import functools

import jax
import jax.numpy as jnp
from jax import lax
from jax.experimental import pallas as pl
from jax.experimental.pallas import tpu as pltpu

F32 = jnp.float32
BF16 = jnp.bfloat16

D_MODEL = 1024
N_META = 16
S5_WIDTH = 512
S5_GROUP = 16
S5_GROUPS = 32
S5_STATE = 64
LRU_WIDTH = 512
LRU_HEADS = 8
LRU_HEAD_DIM = 64
CONV_WIDTH = 4
LRU_C = 8.0
N_EXPERTS = 8
EPS = 1e-6

FOLD = 8
S5_PARTS = 4
PART_W = S5_WIDTH // S5_PARTS
PART_GROUPS = PART_W // S5_GROUP
PART_STATE = PART_GROUPS * S5_STATE
FOLD_W = FOLD * PART_W

T_PAD = 8256
ROWS = T_PAD // FOLD
TM = 688
TM_FFN = 1376
FF_CHUNK = 512
MOE_BLOCK = 512
LRU_CHUNK = 1032
TB_FINAL = 512
VMEM_LIMIT = 56 * 1024 * 1024
LANES = 128
SUBLANES = 8
ROW_TILES = D_MODEL // LANES
MASKED_LOGIT = float("-inf")


def _dot(a, b):
    return jnp.dot(a, b, preferred_element_type=F32)


def _const_spec(block_shape, index_map):
    return pl.BlockSpec(block_shape, index_map, pipeline_mode=pl.Buffered(1))


def _rms(x, g):
    ms = jnp.mean(x * x, axis=-1, keepdims=True)
    return x * lax.rsqrt(ms + EPS) * g


def _rows_to_tiles(ref, x):
    rows = x.shape[0]
    for s in range(ROW_TILES):
        ref[pl.ds(s, rows, stride=ROW_TILES), :] = x[:, s * LANES:(s + 1) * LANES]


def _rows_from_tiles(ref, rows):
    return jnp.concatenate([ref[pl.ds(s, rows, stride=ROW_TILES), :] for s in range(ROW_TILES)], axis=-1)


def _in_proj_kernel(hs_ref, g_ref, w_ref, mb_ref, u_ref, xl_ref, gl_ref, gt_ref):
    hn = _rms(hs_ref[...], g_ref[0]).astype(BF16)
    u = _dot(hn, w_ref[0, :, 0:S5_WIDTH])
    for q in range(S5_PARTS):
        u_ref[q] = u[:, q * PART_W:(q + 1) * PART_W]
    o_x = S5_WIDTH
    o_g = o_x + LRU_WIDTH
    o_m = o_g + LRU_WIDTH
    xl_ref[...] = _dot(hn, w_ref[0, :, o_x:o_g]).astype(BF16)
    gl_ref[...] = _dot(hn, w_ref[0, :, o_g:o_m]).astype(BF16)
    z = _dot(hn, w_ref[0, :, o_m:]) + mb_ref[0]
    gt_ref[...] = jax.nn.sigmoid(z).astype(BF16)


def _in_proj(hs, mix_norm, w_in, merge_bias, layer):
    n = hs.shape[0]
    d_in = w_in.shape[-1]
    lay = lambda i: (layer, 0, 0)
    return pl.pallas_call(
        _in_proj_kernel,
        grid=(n // TM,),
        in_specs=[
            pl.BlockSpec((TM, D_MODEL), lambda i: (i, 0)),
            _const_spec((1, 1, D_MODEL), lay),
            _const_spec((1, D_MODEL, d_in), lay),
            _const_spec((1, 1, 2 * D_MODEL), lay),
        ],
        out_specs=[
            pl.BlockSpec((S5_PARTS, TM, PART_W), lambda i: (0, i, 0)),
            pl.BlockSpec((TM, LRU_WIDTH), lambda i: (i, 0)),
            pl.BlockSpec((TM, LRU_WIDTH), lambda i: (i, 0)),
            pl.BlockSpec((TM, 2 * D_MODEL), lambda i: (i, 0)),
        ],
        out_shape=[
            jax.ShapeDtypeStruct((S5_PARTS, n, PART_W), F32),
            jax.ShapeDtypeStruct((n, LRU_WIDTH), BF16),
            jax.ShapeDtypeStruct((n, LRU_WIDTH), BF16),
            jax.ShapeDtypeStruct((n, 2 * D_MODEL), BF16),
        ],
        compiler_params=pltpu.CompilerParams(
            dimension_semantics=("arbitrary",), vmem_limit_bytes=VMEM_LIMIT),
        name="in_proj",
    )(hs, mix_norm, w_in, merge_bias)


def _s5_prep(lam_re, lam_im, log_dt, b_re, b_im, c_re, c_im, d_skip):
    dt = jnp.exp(log_dt)[:, None]
    mag = jnp.exp(lam_re * dt)
    a_re = mag * jnp.cos(lam_im * dt)
    a_im = mag * jnp.sin(lam_im * dt)
    den = lam_re * lam_re + lam_im * lam_im
    num_re = a_re - 1.0
    coef_re = (num_re * lam_re + a_im * lam_im) / den
    coef_im = (a_im * lam_re - num_re * lam_im) / den
    bb_re = coef_re[..., None] * b_re - coef_im[..., None] * b_im
    bb_im = coef_re[..., None] * b_im + coef_im[..., None] * b_re

    def cmul(xr, xi, yr, yi):
        return xr * yr - xi * yi, xr * yi + xi * yr

    def powers(br, bi, n):
        pr, pi = [jnp.ones_like(br)], [jnp.zeros_like(bi)]
        for _ in range(n):
            r, i = cmul(pr[-1], pi[-1], br, bi)
            pr.append(r)
            pi.append(i)
        return jnp.stack(pr), jnp.stack(pi)

    p_re, p_im = powers(a_re, a_im, FOLD)
    q_re, q_im = powers(p_re[FOLD], p_im[FOLD], FOLD)

    def per_part(x):
        lead = x.shape[:-3]
        xp = x.reshape(lead + (S5_PARTS, PART_GROUPS) + x.shape[-2:])
        return jnp.moveaxis(xp, len(lead), 0)

    rev_re = jnp.stack([p_re[FOLD - 1 - j] for j in range(FOLD)])
    rev_im = jnp.stack([p_im[FOLD - 1 - j] for j in range(FOLD)])
    wr, wi = cmul(rev_re[..., None], rev_im[..., None], bb_re[None], bb_im[None])
    w_ri = jnp.swapaxes(jnp.stack([wr, wi], axis=1), -1, -2)
    xq = jnp.transpose(per_part(w_ri), (0, 1, 3, 4, 2, 5)).reshape(S5_PARTS, FOLD_W, 2 * S5_STATE)

    ca_re, ca_im = cmul(c_re[None], c_im[None], p_re[:, :, None, :], p_im[:, :, None, :])
    bt_re = jnp.swapaxes(bb_re, -1, -2)[None, :, :, None, :]
    bt_im = jnp.swapaxes(bb_im, -1, -2)[None, :, :, None, :]
    taps = jnp.sum(ca_re[:FOLD, :, None] * bt_re - ca_im[:FOLD, :, None] * bt_im, axis=-1)
    skip = d_skip.reshape(S5_GROUPS, S5_GROUP)
    taps = taps.at[0].add(skip[:, :, None] * jnp.eye(S5_GROUP, dtype=F32)[None])
    rc = jnp.transpose(per_part(taps), (0, 2, 3, 1, 4)).reshape(S5_PARTS, PART_W, FOLD * S5_GROUP)

    v_ri = jnp.swapaxes(jnp.stack([ca_re[1:], -ca_im[1:]], axis=0), -1, -2)
    vc = jnp.transpose(per_part(v_ri), (0, 1, 3, 4, 2, 5)).reshape(S5_PARTS, 2 * PART_STATE, FOLD * S5_GROUP)

    def part_vec(x):
        lead = x.shape[:-2]
        xp = x.reshape(lead + (S5_PARTS, PART_STATE))
        return jnp.moveaxis(xp, -2, 0)

    lvl = jnp.stack([jnp.stack([part_vec(q_re[k]), part_vec(q_im[k])], axis=1) for k in (1, 2, 4)], axis=1)
    lvl = lvl[:, :, :, None, :]
    rowpow = jnp.stack([part_vec(q_re[1:]), part_vec(q_im[1:])], axis=1)
    return xq, rc, vc, lvl, rowpow


def _iota2(shape):
    return (lax.broadcasted_iota(jnp.int32, shape, 0), lax.broadcasted_iota(jnp.int32, shape, 1))


def _s5_expand(xq, rc, vc, w1_s, tv_s):
    ps = PART_STATE
    lg_state, lg_group, lg_part = (v.bit_length() - 1 for v in (S5_STATE, S5_GROUP, PART_W))
    lg_pg = PART_GROUPS.bit_length() - 1
    grp = PART_GROUPS - 1
    one_hot = lambda m: jnp.where(m, 1.0, 0.0).astype(BF16)
    r, c = _iota2((2 * S5_STATE, 2 * ps))
    e1 = one_hot(((r >> lg_state) == (c >> (lg_state + lg_pg))) & ((r & (S5_STATE - 1)) == (c & (S5_STATE - 1))))
    r, c = _iota2((FOLD * S5_GROUP, FOLD_W))
    e2 = one_hot(((r >> lg_group) == (c >> lg_part)) & ((r & (S5_GROUP - 1)) == (c & (S5_GROUP - 1))))
    r, c = _iota2((FOLD_W, 2 * ps))
    m1 = ((r >> lg_group) & grp) == ((c >> lg_state) & grp)
    w1_s[...] = jnp.where(m1, _dot(xq.astype(BF16), e1), 0.0).astype(BF16)
    r, c = _iota2((PART_W, FOLD_W))
    m2 = (r >> lg_group) == ((c >> lg_group) & grp)
    r0 = jnp.where(m2, _dot(rc.astype(BF16), e2), 0.0).astype(BF16)
    for j in range(FOLD):
        if j == 0:
            blk = r0
        else:
            blk = jnp.concatenate([jnp.zeros((PART_W, j * PART_W), BF16), r0[:, :FOLD_W - j * PART_W]], axis=1)
        tv_s[j * PART_W:(j + 1) * PART_W, :] = blk
    r, c = _iota2((2 * ps, FOLD_W))
    m3 = ((r >> lg_state) & grp) == ((c >> lg_group) & grp)
    tv_s[FOLD_W:, :] = jnp.where(m3, _dot(vc.astype(BF16), e2), 0.0).astype(BF16)


def _s5_kernel(u_ref, xq_ref, rc_ref, vc_ref, lvl_ref, rp_ref, y_ref, w1_s, tv_s, st_ref):
    ps = PART_STATE

    @pl.when(pl.program_id(1) == 0)
    def _():
        _s5_expand(xq_ref[0, 0], rc_ref[0, 0], vc_ref[0, 0], w1_s, tv_s)

    u = jnp.concatenate([u_ref[0, 0, pl.ds(j, ROWS, stride=FOLD), :] for j in range(FOLD)],
                        axis=-1).astype(BF16)
    f = _dot(u, w1_s[...])
    fr = f[:, :ps]
    fi = f[:, ps:]
    row = lax.broadcasted_iota(jnp.int32, (ROWS, ps), 0) & (SUBLANES - 1)
    for lv, k in enumerate((1, 2, 4)):
        ar = lvl_ref[0, 0, lv, 0]
        ai = lvl_ref[0, 0, lv, 1]
        sr = pltpu.roll(fr, k, axis=0)
        si = pltpu.roll(fi, k, axis=0)
        m = row >= k
        fr, fi = (fr + jnp.where(m, ar * sr - ai * si, 0.0),
                  fi + jnp.where(m, ar * si + ai * sr, 0.0))
    st_ref[0:SUBLANES, :] = jnp.zeros((SUBLANES, 2 * ps), F32)
    st_ref[SUBLANES:, :ps] = fr
    st_ref[SUBLANES:, ps:] = fi
    pr = rp_ref[0, 0, 0]
    pi = rp_ref[0, 0, 1]

    def body(i, carry):
        cr, ci = carry
        r = pl.multiple_of(SUBLANES + i * SUBLANES, SUBLANES)
        xr = st_ref[pl.ds(r, SUBLANES), :ps]
        xi = st_ref[pl.ds(r, SUBLANES), ps:]
        hr = xr + pr * cr - pi * ci
        hi = xi + pr * ci + pi * cr
        st_ref[pl.ds(r, SUBLANES), :ps] = hr
        st_ref[pl.ds(r, SUBLANES), ps:] = hi
        return hr[SUBLANES - 1:SUBLANES], hi[SUBLANES - 1:SUBLANES]

    zero = jnp.zeros((1, ps), F32)
    lax.fori_loop(0, ROWS // SUBLANES, body, (zero, zero))
    h_prev = st_ref[pl.ds(SUBLANES - 1, ROWS), :].astype(BF16)
    y = jax.nn.gelu(_dot(u, tv_s[:FOLD_W, :]) + _dot(h_prev, tv_s[FOLD_W:, :]))
    for j in range(FOLD):
        y_ref[0, 0, pl.ds(j, ROWS, stride=FOLD), :] = y[:, j * PART_W:(j + 1) * PART_W]


def _s5_scan(u_parts, ops, layer, bsz):
    xq, rc, vc, lvl, rowpow = ops
    n = u_parts.shape[1]
    u4 = u_parts.reshape(S5_PARTS, bsz, T_PAD, PART_W)
    lay4 = lambda q, b: (layer, q, 0, 0)
    y4 = pl.pallas_call(
        _s5_kernel,
        grid=(S5_PARTS, bsz),
        in_specs=[
            pl.BlockSpec((1, 1, T_PAD, PART_W), lambda q, b: (q, b, 0, 0)),
            pl.BlockSpec((1, 1, FOLD_W, 2 * S5_STATE), lay4),
            pl.BlockSpec((1, 1, PART_W, FOLD * S5_GROUP), lay4),
            pl.BlockSpec((1, 1, 2 * PART_STATE, FOLD * S5_GROUP), lay4),
            pl.BlockSpec((1, 1, 3, 2, 1, PART_STATE), lambda q, b: (layer, q, 0, 0, 0, 0)),
            pl.BlockSpec((1, 1, 2, FOLD, PART_STATE), lambda q, b: (layer, q, 0, 0, 0)),
        ],
        out_specs=pl.BlockSpec((1, 1, T_PAD, PART_W), lambda q, b: (q, b, 0, 0)),
        out_shape=jax.ShapeDtypeStruct((S5_PARTS, bsz, T_PAD, PART_W), F32),
        scratch_shapes=[
            pltpu.VMEM((FOLD_W, 2 * PART_STATE), BF16),
            pltpu.VMEM((FOLD_W + 2 * PART_STATE, FOLD_W), BF16),
            pltpu.VMEM((ROWS + SUBLANES, 2 * PART_STATE), F32),
        ],
        compiler_params=pltpu.CompilerParams(
            dimension_semantics=("arbitrary", "arbitrary"), vmem_limit_bytes=VMEM_LIMIT),
        name="s5_scan",
    )(u4, xq, rc, vc, lvl, rowpow)
    return y4.reshape(S5_PARTS, n, PART_W)


def _lru_kernel(x_ref, g_ref, cw_ref, cb_ref, wri_ref, bri_ref, nsp_ref, o_ref, xp_ref, a_ref, b_ref, h_ref):
    tc = LRU_CHUNK
    c = LRU_WIDTH

    @pl.when(pl.program_id(1) == 0)
    def _():
        xp_ref[0:SUBLANES, :] = jnp.zeros((SUBLANES, c), F32)
        h_ref[...] = jnp.zeros((1, c), F32)

    xp_ref[SUBLANES:, :] = x_ref[0].astype(F32)
    xc = cb_ref[0]
    for k in range(CONV_WIDTH):
        off = SUBLANES - (CONV_WIDTH - 1) + k
        xc = xc + cw_ref[0, k:k + 1, :] * xp_ref[pl.ds(off, tc), :]
    xp_ref[0:SUBLANES, :] = xp_ref[tc:tc + SUBLANES, :]
    ri = jax.nn.sigmoid(_dot(xc.astype(BF16), wri_ref[0]) + bri_ref[0])
    r = ri[:, :c]
    ig = ri[:, c:]
    log_a = r * nsp_ref[0]
    a = jnp.exp(log_a)
    b = jnp.sqrt(1.0 - a * a) * (ig * xc)
    row = lax.broadcasted_iota(jnp.int32, (tc, c), 0) & (SUBLANES - 1)
    for k in (1, 2, 4):
        a_s = pltpu.roll(a, k, axis=0)
        b_s = pltpu.roll(b, k, axis=0)
        m = row >= k
        b = b + jnp.where(m, a * b_s, 0.0)
        a = jnp.where(m, a * a_s, a)
    a_ref[...] = a
    b_ref[...] = b

    def body(i, carry):
        r0 = pl.multiple_of(i * SUBLANES, SUBLANES)
        h = b_ref[pl.ds(r0, SUBLANES), :] + a_ref[pl.ds(r0, SUBLANES), :] * carry
        b_ref[pl.ds(r0, SUBLANES), :] = h
        return h[SUBLANES - 1:SUBLANES]

    h_ref[...] = lax.fori_loop(0, tc // SUBLANES, body, h_ref[...])
    o_ref[0] = (b_ref[...] * jax.nn.gelu(g_ref[0].astype(F32))).astype(BF16)


def _lru(x_lru, g_lru, conv_w, conv_b, w_ri, b_ri, neg_sp, layer, bsz):
    n = x_lru.shape[0]
    c = LRU_WIDTH
    x3 = x_lru.reshape(bsz, T_PAD, c)
    g3 = g_lru.reshape(bsz, T_PAD, c)
    lay = lambda b, t: (layer, 0, 0)
    out = pl.pallas_call(
        _lru_kernel,
        grid=(bsz, T_PAD // LRU_CHUNK),
        in_specs=[
            pl.BlockSpec((1, LRU_CHUNK, c), lambda b, t: (b, t, 0)),
            pl.BlockSpec((1, LRU_CHUNK, c), lambda b, t: (b, t, 0)),
            _const_spec((1, CONV_WIDTH, c), lay),
            _const_spec((1, 1, c), lay),
            _const_spec((1, c, 2 * c), lay),
            _const_spec((1, 1, 2 * c), lay),
            _const_spec((1, 1, c), lay),
        ],
        out_specs=pl.BlockSpec((1, LRU_CHUNK, c), lambda b, t: (b, t, 0)),
        out_shape=jax.ShapeDtypeStruct((bsz, T_PAD, c), BF16),
        scratch_shapes=[
            pltpu.VMEM((LRU_CHUNK + SUBLANES, c), F32),
            pltpu.VMEM((LRU_CHUNK, c), F32),
            pltpu.VMEM((LRU_CHUNK, c), F32),
            pltpu.VMEM((1, c), F32),
        ],
        compiler_params=pltpu.CompilerParams(
            dimension_semantics=("arbitrary", "arbitrary"), vmem_limit_bytes=VMEM_LIMIT),
        name="rglru",
    )(x3, g3, conv_w, conv_b, w_ri, b_ri, neg_sp)
    return out.reshape(n, c)


def _merge_kernel(hs_ref, ys_ref, yl_ref, gt_ref, wglu_ref, bglu_ref, wsp_ref, wlp_ref, wout_ref, g_ref,
                  *rest, with_router):
    if with_router:
        rw_ref, rb_ref, hs_out_ref, hn_ref, rt_ref = rest
    else:
        hs_out_ref, hn_ref = rest
    ys = jnp.concatenate([ys_ref[q] for q in range(S5_PARTS)], axis=-1)
    glu = ys * jax.nn.sigmoid(_dot(ys.astype(BF16), wglu_ref[0]) + bglu_ref[0])
    y_a = _dot(glu.astype(BF16), wsp_ref[0])
    y_b = _dot(yl_ref[...], wlp_ref[0])
    y = gt_ref[:, :D_MODEL].astype(F32) * y_a + gt_ref[:, D_MODEL:].astype(F32) * y_b
    hs = hs_ref[...] + _dot(y.astype(BF16), wout_ref[0])
    hs_out_ref[...] = hs
    hn = _rms(hs, g_ref[0])
    if not with_router:
        hn_ref[...] = hn.astype(BF16)
    else:
        _rows_to_tiles(hn_ref, hn)
        logits = _dot(hn.astype(BF16), rw_ref[0].astype(BF16)) + rb_ref[0]
        lane = lax.broadcasted_iota(jnp.int32, logits.shape, 1).astype(F32)
        m1 = jnp.max(logits, axis=-1, keepdims=True)
        i1 = jnp.min(jnp.where(logits == m1, lane, float(LANES)), axis=-1, keepdims=True)
        rest_l = jnp.where(lane == i1, MASKED_LOGIT, logits)
        m2 = jnp.max(rest_l, axis=-1, keepdims=True)
        i2 = jnp.min(jnp.where(rest_l == m2, lane, float(LANES)), axis=-1, keepdims=True)
        e2 = jnp.exp(m2 - m1)
        g1 = 1.0 / (1.0 + e2)
        g2 = e2 / (1.0 + e2)
        rt_ref[...] = (jnp.where(lane == 0.0, i1, 0.0) + jnp.where(lane == 1.0, i2, 0.0)
                       + jnp.where(lane == 2.0, g1, 0.0) + jnp.where(lane == 3.0, g2, 0.0))


def _merge(hs, ys_parts, y_lru, gates, w_glu, b_glu, w_sp, w_lp, w_out, ffn_norm, layer, router=None):
    n = hs.shape[0]
    lay = lambda i: (layer, 0, 0)
    in_specs = [
        pl.BlockSpec((TM, D_MODEL), lambda i: (i, 0)),
        pl.BlockSpec((S5_PARTS, TM, PART_W), lambda i: (0, i, 0)),
        pl.BlockSpec((TM, LRU_WIDTH), lambda i: (i, 0)),
        pl.BlockSpec((TM, 2 * D_MODEL), lambda i: (i, 0)),
        _const_spec((1, S5_WIDTH, S5_WIDTH), lay),
        _const_spec((1, 1, S5_WIDTH), lay),
        _const_spec((1, S5_WIDTH, D_MODEL), lay),
        _const_spec((1, LRU_WIDTH, D_MODEL), lay),
        _const_spec((1, D_MODEL, D_MODEL), lay),
        _const_spec((1, 1, D_MODEL), lay),
    ]
    out_specs = [pl.BlockSpec((TM, D_MODEL), lambda i: (i, 0))]
    out_shape = [jax.ShapeDtypeStruct((n, D_MODEL), F32)]
    if router is None:
        out_specs.append(pl.BlockSpec((TM, D_MODEL), lambda i: (i, 0)))
        out_shape.append(jax.ShapeDtypeStruct((n, D_MODEL), BF16))
    else:
        out_specs.append(pl.BlockSpec((TM * ROW_TILES, LANES), lambda i: (i, 0)))
        out_shape.append(jax.ShapeDtypeStruct((n * ROW_TILES, LANES), F32))
    args = [hs, ys_parts, y_lru, gates, w_glu, b_glu, w_sp, w_lp, w_out, ffn_norm]
    if router is not None:
        rw, rb, j = router
        in_specs += [_const_spec((1, D_MODEL, LANES), lambda i: (j, 0, 0)),
                     _const_spec((1, 1, LANES), lambda i: (j, 0, 0))]
        out_specs.append(pl.BlockSpec((TM, LANES), lambda i: (i, 0)))
        out_shape.append(jax.ShapeDtypeStruct((n, LANES), F32))
        args += [rw, rb]
    return pl.pallas_call(
        functools.partial(_merge_kernel, with_router=router is not None),
        grid=(n // TM,),
        in_specs=in_specs,
        out_specs=out_specs,
        out_shape=out_shape,
        compiler_params=pltpu.CompilerParams(
            dimension_semantics=("arbitrary",), vmem_limit_bytes=VMEM_LIMIT),
        name="merge_router" if router is not None else "merge",
    )(*args)


def _ffn_kernel(x_ref, hs_ref, wg_ref, wu_ref, wd_ref, o_ref, acc_ref):
    c = pl.program_id(1)

    @pl.when(c == 0)
    def _():
        acc_ref[...] = jnp.zeros_like(acc_ref)

    x = x_ref[...]
    g = _dot(x, wg_ref[0].astype(BF16))
    h = g * jax.nn.sigmoid(g) * _dot(x, wu_ref[0].astype(BF16))
    acc_ref[...] += _dot(h.astype(BF16), wd_ref[0].astype(BF16))

    @pl.when(c == pl.num_programs(1) - 1)
    def _():
        o_ref[...] = hs_ref[...] + acc_ref[...]


def _ffn(hn, hs, w_gate, w_up, w_down, layer):
    n = hn.shape[0]
    ff = w_gate.shape[-1]
    return pl.pallas_call(
        _ffn_kernel,
        grid=(n // TM_FFN, ff // FF_CHUNK),
        in_specs=[
            pl.BlockSpec((TM_FFN, D_MODEL), lambda i, c: (i, 0)),
            pl.BlockSpec((TM_FFN, D_MODEL), lambda i, c: (i, 0)),
            pl.BlockSpec((1, D_MODEL, FF_CHUNK), lambda i, c: (layer, 0, c)),
            pl.BlockSpec((1, D_MODEL, FF_CHUNK), lambda i, c: (layer, 0, c)),
            pl.BlockSpec((1, FF_CHUNK, D_MODEL), lambda i, c: (layer, c, 0)),
        ],
        out_specs=pl.BlockSpec((TM_FFN, D_MODEL), lambda i, c: (i, 0)),
        out_shape=jax.ShapeDtypeStruct((n, D_MODEL), F32),
        scratch_shapes=[pltpu.VMEM((TM_FFN, D_MODEL), F32)],
        compiler_params=pltpu.CompilerParams(
            dimension_semantics=("arbitrary", "arbitrary"), vmem_limit_bytes=VMEM_LIMIT),
        name="dense_ffn",
    )(hn, hs, w_gate, w_up, w_down)


def _moe_plan(route, n):
    n_pairs = 2 * n
    n_blocks = -(-n_pairs // MOE_BLOCK) + N_EXPERTS
    e = route[:, :2].astype(jnp.int32).reshape(n_pairs)
    onehot = (e[:, None] == jnp.arange(N_EXPERTS, dtype=jnp.int32)[None, :]).astype(jnp.int32)
    csum = jnp.cumsum(onehot, axis=0)
    rank = jnp.sum(csum * onehot, axis=1) - 1
    counts = csum[-1]
    padded = ((counts + MOE_BLOCK - 1) // MOE_BLOCK) * MOE_BLOCK
    cum_pad = jnp.cumsum(padded)
    pad_start = cum_pad - padded
    pos = jnp.sum(pad_start[None, :] * onehot, axis=1) + rank
    block_start = jnp.arange(n_blocks, dtype=jnp.int32) * MOE_BLOCK
    block_expert = jnp.minimum(jnp.searchsorted(cum_pad, block_start, side='right'),
                               N_EXPERTS - 1).astype(jnp.int32)
    n_used = (cum_pad[-1] // MOE_BLOCK).astype(jnp.int32).reshape(1)
    pad_range = jnp.stack([pad_start + counts, cum_pad], axis=1).reshape(2 * N_EXPERTS).astype(jnp.int32)
    return pos.astype(jnp.int32), block_expert, n_used, pad_range


def _tile_gather(src_hbm, dst, sem, rows, index_of):
    def body(r, carry):
        src = pl.multiple_of(index_of(r) * ROW_TILES, ROW_TILES)
        out = pl.multiple_of(r * ROW_TILES, ROW_TILES)
        pltpu.make_async_copy(src_hbm.at[pl.ds(src, ROW_TILES)], dst.at[pl.ds(out, ROW_TILES)], sem).start()
        return carry
    lax.fori_loop(0, rows, body, 0, unroll=8)


def _tile_gather_wait(src_hbm, dst, sem, rows):
    pltpu.make_async_copy(src_hbm.at[pl.ds(0, rows * ROW_TILES)], dst, sem).wait()


def _moe_ffn_kernel(be_ref, pos_ref, pad_ref, nu_ref, x_hbm, wg_ref, wu_ref, wd_ref, y_ref, xbuf, st_ref, sem):
    i = pl.program_id(0)
    n_used = nu_ref[0]
    slot = i % 2

    @pl.when(i == 0)
    def _():
        for e in range(N_EXPERTS):
            def clear(s, carry):
                st_ref[s] = 0
                return carry
            lax.fori_loop(pad_ref[2 * e], pad_ref[2 * e + 1], clear, 0)

        def put(p, carry):
            st_ref[pos_ref[p]] = lax.shift_right_logical(p, 1)
            return carry
        lax.fori_loop(0, pos_ref.shape[0], put, 0, unroll=8)

    def start(blk, s):
        _tile_gather(x_hbm, xbuf.at[s], sem.at[s], MOE_BLOCK, lambda r: st_ref[blk * MOE_BLOCK + r])

    @pl.when((i == 0) & (n_used > 0))
    def _():
        start(0, 0)

    @pl.when(i + 1 < n_used)
    def _():
        start(i + 1, 1 - slot)

    @pl.when(i < n_used)
    def _():
        _tile_gather_wait(x_hbm, xbuf.at[slot], sem.at[slot], MOE_BLOCK)
        x = _rows_from_tiles(xbuf.at[slot], MOE_BLOCK).astype(BF16)
        g = _dot(x, wg_ref[0])
        h = g * jax.nn.sigmoid(g) * _dot(x, wu_ref[0])
        _rows_to_tiles(y_ref, _dot(h.astype(BF16), wd_ref[0]))

    @pl.when(i >= n_used)
    def _():
        y_ref[...] = jnp.zeros_like(y_ref)


def _moe_ffn(hn_tiles, pos, block_expert, n_used, pad_range, w_gate, w_up, w_down, first):
    n_blocks = block_expert.shape[0]
    n_slots = n_blocks * MOE_BLOCK
    ff = w_gate.shape[-1]
    wmap = lambda i, be, ps, pr, nu: (first + be[i], 0, 0)
    return pl.pallas_call(
        _moe_ffn_kernel,
        grid_spec=pltpu.PrefetchScalarGridSpec(
            num_scalar_prefetch=4,
            grid=(n_blocks,),
            in_specs=[
                pl.BlockSpec(memory_space=pl.ANY),
                pl.BlockSpec((1, D_MODEL, ff), wmap),
                pl.BlockSpec((1, D_MODEL, ff), wmap),
                pl.BlockSpec((1, ff, D_MODEL), wmap),
            ],
            out_specs=pl.BlockSpec((MOE_BLOCK * ROW_TILES, LANES), lambda i, be, ps, pr, nu: (i, 0)),
            scratch_shapes=[pltpu.VMEM((2, MOE_BLOCK * ROW_TILES, LANES), F32),
                            pltpu.SMEM((n_slots,), jnp.int32),
                            pltpu.SemaphoreType.DMA((2,))],
        ),
        out_shape=jax.ShapeDtypeStruct((n_slots * ROW_TILES, LANES), F32),
        compiler_params=pltpu.CompilerParams(
            dimension_semantics=("arbitrary",), vmem_limit_bytes=VMEM_LIMIT),
        name="moe_ffn",
    )(block_expert, pos, pad_range, n_used, hn_tiles, w_gate, w_up, w_down)


def _moe_combine_kernel(pos_ref, hs_ref, rt_ref, ys_hbm, o_ref, ybuf, sem):
    i = pl.program_id(0)
    nt = pl.num_programs(0)
    slot = i % 2

    def start(t, s):
        for k in range(2):
            _tile_gather(ys_hbm, ybuf.at[s, k], sem.at[s], TM, lambda r: pos_ref[2 * (t * TM + r) + k])

    @pl.when(i == 0)
    def _():
        start(0, 0)

    @pl.when(i + 1 < nt)
    def _():
        start(i + 1, 1 - slot)

    for k in range(2):
        _tile_gather_wait(ys_hbm, ybuf.at[slot, k], sem.at[slot], TM)
    rt = rt_ref[...]
    lane = lax.broadcasted_iota(jnp.int32, rt.shape, 1)
    g1 = jnp.sum(jnp.where(lane == 2, rt, 0.0), axis=-1, keepdims=True)
    g2 = jnp.sum(jnp.where(lane == 3, rt, 0.0), axis=-1, keepdims=True)
    o_ref[...] = (hs_ref[...] + g1 * _rows_from_tiles(ybuf.at[slot, 0], TM)
                  + g2 * _rows_from_tiles(ybuf.at[slot, 1], TM))


def _moe_combine(hs, route, ys_tiles, pos):
    n = hs.shape[0]
    return pl.pallas_call(
        _moe_combine_kernel,
        grid_spec=pltpu.PrefetchScalarGridSpec(
            num_scalar_prefetch=1,
            grid=(n // TM,),
            in_specs=[
                pl.BlockSpec((TM, D_MODEL), lambda i, p: (i, 0)),
                pl.BlockSpec((TM, LANES), lambda i, p: (i, 0)),
                pl.BlockSpec(memory_space=pl.ANY),
            ],
            out_specs=pl.BlockSpec((TM, D_MODEL), lambda i, p: (i, 0)),
            scratch_shapes=[pltpu.VMEM((2, 2, TM * ROW_TILES, LANES), F32),
                            pltpu.SemaphoreType.DMA((2,))],
        ),
        out_shape=jax.ShapeDtypeStruct((n, D_MODEL), F32),
        compiler_params=pltpu.CompilerParams(
            dimension_semantics=("arbitrary",), vmem_limit_bytes=VMEM_LIMIT),
        name="moe_combine",
    )(pos, hs, route, ys_tiles)


def _final_kernel(a_ref, b_ref, g_ref, o_ref):
    tb = a_ref.shape[1]
    o_ref[0, :tb - N_META] = _rms(a_ref[0, N_META:], g_ref[...])
    o_ref[0, tb - N_META:] = _rms(b_ref[0], g_ref[...])


def _final_norm(hs, g, bsz, seq):
    hs3 = hs.reshape(bsz, T_PAD, D_MODEL)
    return pl.pallas_call(
        _final_kernel,
        grid=(bsz, seq // TB_FINAL),
        in_specs=[pl.BlockSpec((1, TB_FINAL, D_MODEL), lambda b, i: (b, i, 0)),
                  pl.BlockSpec((1, N_META, D_MODEL), lambda b, i: (b, (i + 1) * (TB_FINAL // N_META), 0)),
                  _const_spec((1, D_MODEL), lambda b, i: (0, 0))],
        out_specs=pl.BlockSpec((1, TB_FINAL, D_MODEL), lambda b, i: (b, i, 0)),
        out_shape=jax.ShapeDtypeStruct((bsz, seq, D_MODEL), F32),
        compiler_params=pltpu.CompilerParams(
            dimension_semantics=("arbitrary", "arbitrary"), vmem_limit_bytes=VMEM_LIMIT),
        name="final_norm",
    )(hs3, hs3, g)


def _head_blockdiag(w):
    eye = jnp.eye(LRU_HEADS, dtype=w.dtype)
    out = jnp.einsum('lnhk,nm->lnhmk', w, eye)
    return out.reshape(w.shape[0], LRU_WIDTH, LRU_WIDTH)


def kernel(x, meta_tokens, mix_norm, w_in, merge_bias, s5_lambda_re, s5_lambda_im, s5_log_dt, s5_b_re, s5_b_im, s5_c_re, s5_c_im, s5_d, s5_w_glu, s5_b_glu, s5_w_proj, lru_conv_w, lru_conv_b, lru_w_rgate, lru_b_rgate, lru_w_igate, lru_b_igate, lru_lambda, lru_w_proj, w_out, ffn_norm, dense_w_gate, dense_w_up, dense_w_down, router_w, router_b, moe_w_gate, moe_w_up, moe_w_down, final_norm):
    bsz, seq, d = x.shape
    depth = w_in.shape[0]
    assert d == D_MODEL and N_META + seq <= T_PAD
    n = bsz * T_PAD
    assert n % TM == 0 and n % TM_FFN == 0 and seq % TB_FINAL == 0 and TB_FINAL % N_META == 0

    meta = jnp.broadcast_to(meta_tokens[None].astype(x.dtype), (bsz, N_META, d))
    pad = jnp.zeros((bsz, T_PAD - N_META - seq, d), x.dtype)
    hs = jnp.concatenate([meta, x, pad], axis=1).reshape(n, d)

    row3 = lambda a: a[:, None, :]
    w_in_b = w_in.astype(BF16)
    w_glu_b = s5_w_glu.astype(BF16)
    w_sp_b = s5_w_proj.astype(BF16)
    w_lp_b = lru_w_proj.astype(BF16)
    w_out_b = w_out.astype(BF16)
    w_ri = jnp.concatenate([_head_blockdiag(lru_w_rgate), _head_blockdiag(lru_w_igate)], axis=-1).astype(BF16)
    b_ri = jnp.concatenate([lru_b_rgate, lru_b_igate], axis=-1)
    neg_sp = -LRU_C * jax.nn.softplus(-lru_lambda)
    dense = (dense_w_gate, dense_w_up, dense_w_down)
    n_moe = router_w.shape[0]
    moe = [w.astype(BF16).reshape((n_moe * N_EXPERTS,) + w.shape[2:]) for w in (moe_w_gate, moe_w_up, moe_w_down)]
    s5_ops = jax.vmap(_s5_prep)(s5_lambda_re, s5_lambda_im, s5_log_dt, s5_b_re, s5_b_im, s5_c_re, s5_c_im, s5_d)
    rw_pad = jnp.pad(router_w, ((0, 0), (0, 0), (0, LANES - N_EXPERTS)))
    rb_pad = jnp.pad(router_b, ((0, 0), (0, LANES - N_EXPERTS)), constant_values=MASKED_LOGIT)

    for layer in range(depth):
        u_parts, x_lru, g_lru, gates = _in_proj(hs, row3(mix_norm), w_in_b, row3(merge_bias), layer)
        ys_parts = _s5_scan(u_parts, s5_ops, layer, bsz)
        y_lru = _lru(x_lru, g_lru, lru_conv_w, row3(lru_conv_b), w_ri, row3(b_ri), row3(neg_sp), layer, bsz)
        j = layer // 2
        router = (rw_pad, row3(rb_pad), j) if layer % 2 == 1 else None
        res = _merge(hs, ys_parts, y_lru, gates, w_glu_b, row3(s5_b_glu), w_sp_b, w_lp_b, w_out_b,
                     row3(ffn_norm), layer, router)
        if layer % 2 == 0:
            hs, hn = res
            hs = _ffn(hn, hs, *dense, layer=j)
        else:
            hs, hn, route = res
            pos, block_expert, n_used, pad_range = _moe_plan(route, n)
            ys = _moe_ffn(hn, pos, block_expert, n_used, pad_range, *moe, first=j * N_EXPERTS)
            hs = _moe_combine(hs, route, ys, pos)

    return _final_norm(hs, final_norm[None, :], bsz, seq)
```

```python
import functools

import jax
import jax.numpy as jnp
from jax import lax
from jax.experimental import pallas as pl
from jax.experimental.pallas import tpu as pltpu

F32 = jnp.float32
BF16 = jnp.bfloat16

D_MODEL = 1024
N_META = 16
S5_WIDTH = 512
S5_GROUP = 16
S5_GROUPS = 32
S5_STATE = 64
LRU_WIDTH = 512
LRU_HEADS = 8
LRU_HEAD_DIM = 64
CONV_WIDTH = 4
LRU_C = 8.0
N_EXPERTS = 8
EPS = 1e-6

FOLD = 8
S5_PARTS = 4
PART_W = S5_WIDTH // S5_PARTS
PART_GROUPS = PART_W // S5_GROUP
PART_STATE = PART_GROUPS * S5_STATE
FOLD_W = FOLD * PART_W

T_PAD = 8256
ROWS = T_PAD // FOLD
TM = 688
TM_FFN = 1376
FF_CHUNK = 512
MOE_BLOCK = 512
LRU_CHUNK = 1032
LRU_UNROLL = 3
TB_FINAL = 512
VMEM_LIMIT = 56 * 1024 * 1024
LANES = 128
SUBLANES = 8
ROW_TILES = D_MODEL // LANES
MASKED_LOGIT = float("-inf")


def _dot(a, b):
    return jnp.dot(a, b, preferred_element_type=F32)


def _const_spec(block_shape, index_map):
    return pl.BlockSpec(block_shape, index_map, pipeline_mode=pl.Buffered(1))


def _rms(x, g):
    ms = jnp.mean(x * x, axis=-1, keepdims=True)
    return x * lax.rsqrt(ms + EPS) * g


def _rows_to_tiles(ref, x):
    rows = x.shape[0]
    for s in range(ROW_TILES):
        ref[pl.ds(s, rows, stride=ROW_TILES), :] = x[:, s * LANES:(s + 1) * LANES]


def _rows_from_tiles(ref, rows):
    return jnp.concatenate([ref[pl.ds(s, rows, stride=ROW_TILES), :] for s in range(ROW_TILES)], axis=-1)


def _in_proj_kernel(hs_ref, g_ref, w_ref, mb_ref, u_ref, xl_ref, gl_ref, gt_ref):
    hn = _rms(hs_ref[...], g_ref[0]).astype(BF16)
    u = _dot(hn, w_ref[0, :, 0:S5_WIDTH])
    for q in range(S5_PARTS):
        u_ref[q] = u[:, q * PART_W:(q + 1) * PART_W]
    o_x = S5_WIDTH
    o_g = o_x + LRU_WIDTH
    o_m = o_g + LRU_WIDTH
    xl_ref[...] = _dot(hn, w_ref[0, :, o_x:o_g]).astype(BF16)
    gl_ref[...] = _dot(hn, w_ref[0, :, o_g:o_m]).astype(BF16)
    z = _dot(hn, w_ref[0, :, o_m:]) + mb_ref[0]
    gt_ref[...] = jax.nn.sigmoid(z).astype(BF16)


def _in_proj(hs, mix_norm, w_in, merge_bias, layer):
    n = hs.shape[0]
    d_in = w_in.shape[-1]
    lay = lambda i: (layer, 0, 0)
    return pl.pallas_call(
        _in_proj_kernel,
        grid=(n // TM,),
        in_specs=[
            pl.BlockSpec((TM, D_MODEL), lambda i: (i, 0)),
            _const_spec((1, 1, D_MODEL), lay),
            _const_spec((1, D_MODEL, d_in), lay),
            _const_spec((1, 1, 2 * D_MODEL), lay),
        ],
        out_specs=[
            pl.BlockSpec((S5_PARTS, TM, PART_W), lambda i: (0, i, 0)),
            pl.BlockSpec((TM, LRU_WIDTH), lambda i: (i, 0)),
            pl.BlockSpec((TM, LRU_WIDTH), lambda i: (i, 0)),
            pl.BlockSpec((TM, 2 * D_MODEL), lambda i: (i, 0)),
        ],
        out_shape=[
            jax.ShapeDtypeStruct((S5_PARTS, n, PART_W), F32),
            jax.ShapeDtypeStruct((n, LRU_WIDTH), BF16),
            jax.ShapeDtypeStruct((n, LRU_WIDTH), BF16),
            jax.ShapeDtypeStruct((n, 2 * D_MODEL), BF16),
        ],
        compiler_params=pltpu.CompilerParams(
            dimension_semantics=("arbitrary",), vmem_limit_bytes=VMEM_LIMIT),
        name="in_proj",
    )(hs, mix_norm, w_in, merge_bias)


def _s5_prep(lam_re, lam_im, log_dt, b_re, b_im, c_re, c_im, d_skip):
    dt = jnp.exp(log_dt)[:, None]
    mag = jnp.exp(lam_re * dt)
    a_re = mag * jnp.cos(lam_im * dt)
    a_im = mag * jnp.sin(lam_im * dt)
    den = lam_re * lam_re + lam_im * lam_im
    num_re = a_re - 1.0
    coef_re = (num_re * lam_re + a_im * lam_im) / den
    coef_im = (a_im * lam_re - num_re * lam_im) / den
    bb_re = coef_re[..., None] * b_re - coef_im[..., None] * b_im
    bb_im = coef_re[..., None] * b_im + coef_im[..., None] * b_re

    def cmul(xr, xi, yr, yi):
        return xr * yr - xi * yi, xr * yi + xi * yr

    def powers(br, bi, n):
        pr, pi = [jnp.ones_like(br)], [jnp.zeros_like(bi)]
        for _ in range(n):
            r, i = cmul(pr[-1], pi[-1], br, bi)
            pr.append(r)
            pi.append(i)
        return jnp.stack(pr), jnp.stack(pi)

    p_re, p_im = powers(a_re, a_im, FOLD)
    q_re, q_im = powers(p_re[FOLD], p_im[FOLD], FOLD)

    def per_part(x):
        lead = x.shape[:-3]
        xp = x.reshape(lead + (S5_PARTS, PART_GROUPS) + x.shape[-2:])
        return jnp.moveaxis(xp, len(lead), 0)

    rev_re = jnp.stack([p_re[FOLD - 1 - j] for j in range(FOLD)])
    rev_im = jnp.stack([p_im[FOLD - 1 - j] for j in range(FOLD)])
    wr, wi = cmul(rev_re[..., None], rev_im[..., None], bb_re[None], bb_im[None])
    w_ri = jnp.swapaxes(jnp.stack([wr, wi], axis=1), -1, -2)
    xq = jnp.transpose(per_part(w_ri), (0, 1, 3, 4, 2, 5)).reshape(S5_PARTS, FOLD_W, 2 * S5_STATE)

    ca_re, ca_im = cmul(c_re[None], c_im[None], p_re[:, :, None, :], p_im[:, :, None, :])
    bt_re = jnp.swapaxes(bb_re, -1, -2)[None, :, :, None, :]
    bt_im = jnp.swapaxes(bb_im, -1, -2)[None, :, :, None, :]
    taps = jnp.sum(ca_re[:FOLD, :, None] * bt_re - ca_im[:FOLD, :, None] * bt_im, axis=-1)
    skip = d_skip.reshape(S5_GROUPS, S5_GROUP)
    taps = taps.at[0].add(skip[:, :, None] * jnp.eye(S5_GROUP, dtype=F32)[None])
    rc = jnp.transpose(per_part(taps), (0, 2, 3, 1, 4)).reshape(S5_PARTS, PART_W, FOLD * S5_GROUP)

    v_ri = jnp.swapaxes(jnp.stack([ca_re[1:], -ca_im[1:]], axis=0), -1, -2)
    vc = jnp.transpose(per_part(v_ri), (0, 1, 3, 4, 2, 5)).reshape(S5_PARTS, 2 * PART_STATE, FOLD * S5_GROUP)

    def part_vec(x):
        lead = x.shape[:-2]
        xp = x.reshape(lead + (S5_PARTS, PART_STATE))
        return jnp.moveaxis(xp, -2, 0)

    lvl = jnp.stack([jnp.stack([part_vec(q_re[k]), part_vec(q_im[k])], axis=1) for k in (1, 2, 4)], axis=1)
    lvl = lvl[:, :, :, None, :]
    rowpow = jnp.stack([part_vec(q_re[1:]), part_vec(q_im[1:])], axis=1)
    return xq, rc, vc, lvl, rowpow


def _iota2(shape):
    return (lax.broadcasted_iota(jnp.int32, shape, 0), lax.broadcasted_iota(jnp.int32, shape, 1))


def _s5_expand(xq, rc, vc, w1_s, tv_s):
    ps = PART_STATE
    lg_state, lg_group, lg_part = (v.bit_length() - 1 for v in (S5_STATE, S5_GROUP, PART_W))
    lg_pg = PART_GROUPS.bit_length() - 1
    grp = PART_GROUPS - 1
    one_hot = lambda m: jnp.where(m, 1.0, 0.0).astype(BF16)
    r, c = _iota2((2 * S5_STATE, 2 * ps))
    e1 = one_hot(((r >> lg_state) == (c >> (lg_state + lg_pg))) & ((r & (S5_STATE - 1)) == (c & (S5_STATE - 1))))
    r, c = _iota2((FOLD * S5_GROUP, FOLD_W))
    e2 = one_hot(((r >> lg_group) == (c >> lg_part)) & ((r & (S5_GROUP - 1)) == (c & (S5_GROUP - 1))))
    r, c = _iota2((FOLD_W, 2 * ps))
    m1 = ((r >> lg_group) & grp) == ((c >> lg_state) & grp)
    w1_s[...] = jnp.where(m1, _dot(xq.astype(BF16), e1), 0.0).astype(BF16)
    r, c = _iota2((PART_W, FOLD_W))
    m2 = (r >> lg_group) == ((c >> lg_group) & grp)
    r0 = jnp.where(m2, _dot(rc.astype(BF16), e2), 0.0).astype(BF16)
    for j in range(FOLD):
        if j == 0:
            blk = r0
        else:
            blk = jnp.concatenate([jnp.zeros((PART_W, j * PART_W), BF16), r0[:, :FOLD_W - j * PART_W]], axis=1)
        tv_s[j * PART_W:(j + 1) * PART_W, :] = blk
    r, c = _iota2((2 * ps, FOLD_W))
    m3 = ((r >> lg_state) & grp) == ((c >> lg_group) & grp)
    tv_s[FOLD_W:, :] = jnp.where(m3, _dot(vc.astype(BF16), e2), 0.0).astype(BF16)


def _s5_kernel(u_ref, xq_ref, rc_ref, vc_ref, lvl_ref, rp_ref, y_ref, w1_s, tv_s, st_ref):
    ps = PART_STATE

    @pl.when(pl.program_id(1) == 0)
    def _():
        _s5_expand(xq_ref[0, 0], rc_ref[0, 0], vc_ref[0, 0], w1_s, tv_s)

    u = jnp.concatenate([u_ref[0, 0, pl.ds(j, ROWS, stride=FOLD), :] for j in range(FOLD)],
                        axis=-1).astype(BF16)
    f = _dot(u, w1_s[...])
    fr = f[:, :ps]
    fi = f[:, ps:]
    row = lax.broadcasted_iota(jnp.int32, (ROWS, ps), 0) & (SUBLANES - 1)
    for lv, k in enumerate((1, 2, 4)):
        ar = lvl_ref[0, 0, lv, 0]
        ai = lvl_ref[0, 0, lv, 1]
        sr = pltpu.roll(fr, k, axis=0)
        si = pltpu.roll(fi, k, axis=0)
        m = row >= k
        fr, fi = (fr + jnp.where(m, ar * sr - ai * si, 0.0),
                  fi + jnp.where(m, ar * si + ai * sr, 0.0))
    st_ref[0:SUBLANES, :] = jnp.zeros((SUBLANES, 2 * ps), F32)
    st_ref[SUBLANES:, :ps] = fr
    st_ref[SUBLANES:, ps:] = fi
    pr = rp_ref[0, 0, 0]
    pi = rp_ref[0, 0, 1]

    def body(i, carry):
        cr, ci = carry
        r = pl.multiple_of(SUBLANES + i * SUBLANES, SUBLANES)
        xr = st_ref[pl.ds(r, SUBLANES), :ps]
        xi = st_ref[pl.ds(r, SUBLANES), ps:]
        hr = xr + pr * cr - pi * ci
        hi = xi + pr * ci + pi * cr
        st_ref[pl.ds(r, SUBLANES), :ps] = hr
        st_ref[pl.ds(r, SUBLANES), ps:] = hi
        return hr[SUBLANES - 1:SUBLANES], hi[SUBLANES - 1:SUBLANES]

    zero = jnp.zeros((1, ps), F32)
    lax.fori_loop(0, ROWS // SUBLANES, body, (zero, zero))
    h_prev = st_ref[pl.ds(SUBLANES - 1, ROWS), :].astype(BF16)
    y = jax.nn.gelu(_dot(u, tv_s[:FOLD_W, :]) + _dot(h_prev, tv_s[FOLD_W:, :]))
    for j in range(FOLD):
        y_ref[0, 0, pl.ds(j, ROWS, stride=FOLD), :] = y[:, j * PART_W:(j + 1) * PART_W]


def _s5_scan(u_parts, ops, layer, bsz):
    xq, rc, vc, lvl, rowpow = ops
    n = u_parts.shape[1]
    u4 = u_parts.reshape(S5_PARTS, bsz, T_PAD, PART_W)
    lay4 = lambda q, b: (layer, q, 0, 0)
    y4 = pl.pallas_call(
        _s5_kernel,
        grid=(S5_PARTS, bsz),
        in_specs=[
            pl.BlockSpec((1, 1, T_PAD, PART_W), lambda q, b: (q, b, 0, 0)),
            pl.BlockSpec((1, 1, FOLD_W, 2 * S5_STATE), lay4),
            pl.BlockSpec((1, 1, PART_W, FOLD * S5_GROUP), lay4),
            pl.BlockSpec((1, 1, 2 * PART_STATE, FOLD * S5_GROUP), lay4),
            pl.BlockSpec((1, 1, 3, 2, 1, PART_STATE), lambda q, b: (layer, q, 0, 0, 0, 0)),
            pl.BlockSpec((1, 1, 2, FOLD, PART_STATE), lambda q, b: (layer, q, 0, 0, 0)),
        ],
        out_specs=pl.BlockSpec((1, 1, T_PAD, PART_W), lambda q, b: (q, b, 0, 0)),
        out_shape=jax.ShapeDtypeStruct((S5_PARTS, bsz, T_PAD, PART_W), F32),
        scratch_shapes=[
            pltpu.VMEM((FOLD_W, 2 * PART_STATE), BF16),
            pltpu.VMEM((FOLD_W + 2 * PART_STATE, FOLD_W), BF16),
            pltpu.VMEM((ROWS + SUBLANES, 2 * PART_STATE), F32),
        ],
        compiler_params=pltpu.CompilerParams(
            dimension_semantics=("arbitrary", "arbitrary"), vmem_limit_bytes=VMEM_LIMIT),
        name="s5_scan",
    )(u4, xq, rc, vc, lvl, rowpow)
    return y4.reshape(S5_PARTS, n, PART_W)


def _lru_kernel(x_ref, g_ref, cw_ref, cb_ref, wri_ref, bri_ref, nsp_ref, o_ref,
                xs_ref, gs_ref, xc_ref, gp_ref, z_ref, a_ref, b_ref, os_ref, h_ref):
    tc = LRU_CHUNK
    c = LRU_WIDTH
    seg = tc // SUBLANES
    nq = c // LANES
    lanes = lambda q: slice(q * LANES, (q + 1) * LANES)
    halo = SUBLANES

    @pl.when(pl.program_id(1) == 0)
    def _():
        xs_ref[:, 0:halo, :] = jnp.zeros((nq, halo, LANES), F32)
        h_ref[...] = jnp.zeros((1, c), F32)

    x = x_ref[0].astype(F32)
    g = g_ref[0].astype(F32)
    for q in range(nq):
        xs_ref[q, halo:, :] = x[:, lanes(q)]
        gs_ref[q] = g[:, lanes(q)]
    taps = [[cw_ref[0, k:k + 1, lanes(q)] for k in range(CONV_WIDTH)] for q in range(nq)]
    bias = [cb_ref[0, :, lanes(q)] for q in range(nq)]

    def conv_body(i, carry):
        r0 = pl.multiple_of(i * SUBLANES, SUBLANES)
        for q in range(nq):
            acc = bias[q]
            for k in range(CONV_WIDTH):
                first = halo - (CONV_WIDTH - 1) + k + i
                acc = acc + taps[q][k] * xs_ref[q, pl.ds(first, SUBLANES, stride=seg), :]
            xc_ref[pl.ds(r0, SUBLANES), lanes(q)] = acc
            gp_ref[pl.ds(r0, SUBLANES), lanes(q)] = jax.nn.gelu(gs_ref[q, pl.ds(i, SUBLANES, stride=seg), :])
        return carry

    lax.fori_loop(0, seg, conv_body, 0, unroll=LRU_UNROLL)
    for q in range(nq):
        xs_ref[q, 0:halo, :] = xs_ref[q, tc:tc + halo, :]

    z_ref[...] = _dot(xc_ref[...].astype(BF16), wri_ref[0])
    b_r = jnp.broadcast_to(bri_ref[0, :, :c], (SUBLANES, c))
    b_i = jnp.broadcast_to(bri_ref[0, :, c:], (SUBLANES, c))
    nsp = jnp.broadcast_to(nsp_ref[0], (SUBLANES, c))

    def scan_body(i, carry):
        h, p = carry
        r0 = pl.multiple_of(i * SUBLANES, SUBLANES)
        a = jnp.exp(jax.nn.sigmoid(z_ref[pl.ds(r0, SUBLANES), :c] + b_r) * nsp)
        gated = jax.nn.sigmoid(z_ref[pl.ds(r0, SUBLANES), c:] + b_i) * xc_ref[pl.ds(r0, SUBLANES), :]
        h = a * h + jnp.sqrt(1.0 - a * a) * gated
        p = p * a
        b_ref[pl.ds(r0, SUBLANES), :] = h
        a_ref[pl.ds(r0, SUBLANES), :] = p
        return h, p

    h_end, p_end = lax.fori_loop(0, seg, scan_body, (jnp.zeros((SUBLANES, c), F32), jnp.ones((SUBLANES, c), F32)),
                                 unroll=LRU_UNROLL)
    row = lax.broadcasted_iota(jnp.int32, (SUBLANES, c), 0)
    enter = jnp.where(row == 0, h_ref[...], 0.0)
    for sgm in range(SUBLANES - 1):
        leave = h_end + p_end * enter
        enter = enter + jnp.where(row == sgm + 1, pltpu.roll(leave, 1, axis=0), 0.0)
    h_ref[...] = (h_end + p_end * enter)[SUBLANES - 1:SUBLANES]

    def out_body(i, carry):
        r0 = pl.multiple_of(i * SUBLANES, SUBLANES)
        h = b_ref[pl.ds(r0, SUBLANES), :] + a_ref[pl.ds(r0, SUBLANES), :] * enter
        y = h * gp_ref[pl.ds(r0, SUBLANES), :]
        for q in range(nq):
            os_ref[q, pl.ds(i, SUBLANES, stride=seg), :] = y[:, lanes(q)]
        return carry

    lax.fori_loop(0, seg, out_body, 0)
    o_ref[0] = jnp.concatenate([os_ref[q] for q in range(nq)], axis=-1).astype(BF16)


def _lru(x_lru, g_lru, conv_w, conv_b, w_ri, b_ri, neg_sp, layer, bsz):
    n = x_lru.shape[0]
    c = LRU_WIDTH
    x3 = x_lru.reshape(bsz, T_PAD, c)
    g3 = g_lru.reshape(bsz, T_PAD, c)
    lay = lambda b, t: (layer, 0, 0)
    out = pl.pallas_call(
        _lru_kernel,
        grid=(bsz, T_PAD // LRU_CHUNK),
        in_specs=[
            pl.BlockSpec((1, LRU_CHUNK, c), lambda b, t: (b, t, 0)),
            pl.BlockSpec((1, LRU_CHUNK, c), lambda b, t: (b, t, 0)),
            _const_spec((1, CONV_WIDTH, c), lay),
            _const_spec((1, 1, c), lay),
            _const_spec((1, c, 2 * c), lay),
            _const_spec((1, 1, 2 * c), lay),
            _const_spec((1, 1, c), lay),
        ],
        out_specs=pl.BlockSpec((1, LRU_CHUNK, c), lambda b, t: (b, t, 0)),
        out_shape=jax.ShapeDtypeStruct((bsz, T_PAD, c), BF16),
        scratch_shapes=[
            pltpu.VMEM((c // LANES, LRU_CHUNK + SUBLANES, LANES), F32),
            pltpu.VMEM((c // LANES, LRU_CHUNK, LANES), F32),
            pltpu.VMEM((LRU_CHUNK, c), F32),
            pltpu.VMEM((LRU_CHUNK, c), F32),
            pltpu.VMEM((LRU_CHUNK, 2 * c), F32),
            pltpu.VMEM((LRU_CHUNK, c), F32),
            pltpu.VMEM((LRU_CHUNK, c), F32),
            pltpu.VMEM((c // LANES, LRU_CHUNK, LANES), F32),
            pltpu.VMEM((1, c), F32),
        ],
        compiler_params=pltpu.CompilerParams(
            dimension_semantics=("arbitrary", "arbitrary"), vmem_limit_bytes=VMEM_LIMIT),
        name="rglru",
    )(x3, g3, conv_w, conv_b, w_ri, b_ri, neg_sp)
    return out.reshape(n, c)


def _merge_kernel(hs_ref, ys_ref, yl_ref, gt_ref, wglu_ref, bglu_ref, wsp_ref, wlp_ref, wout_ref, g_ref,
                  *rest, with_router):
    if with_router:
        rw_ref, rb_ref, hs_out_ref, hn_ref, rt_ref, cnt_ref, run_ref, tri_ref = rest
    else:
        hs_out_ref, hn_ref = rest
    ys = jnp.concatenate([ys_ref[q] for q in range(S5_PARTS)], axis=-1)
    glu = ys * jax.nn.sigmoid(_dot(ys.astype(BF16), wglu_ref[0]) + bglu_ref[0])
    y_a = _dot(glu.astype(BF16), wsp_ref[0])
    y_b = _dot(yl_ref[...], wlp_ref[0])
    y = gt_ref[:, :D_MODEL].astype(F32) * y_a + gt_ref[:, D_MODEL:].astype(F32) * y_b
    hs = hs_ref[...] + _dot(y.astype(BF16), wout_ref[0])
    hs_out_ref[...] = hs
    hn = _rms(hs, g_ref[0])
    if not with_router:
        hn_ref[...] = hn.astype(BF16)
    else:
        _rows_to_tiles(hn_ref, hn)
        logits = _dot(hn.astype(BF16), rw_ref[0].astype(BF16)) + rb_ref[0]
        lane = lax.broadcasted_iota(jnp.int32, logits.shape, 1).astype(F32)
        m1 = jnp.max(logits, axis=-1, keepdims=True)
        i1 = jnp.min(jnp.where(logits == m1, lane, float(LANES)), axis=-1, keepdims=True)
        rest_l = jnp.where(lane == i1, MASKED_LOGIT, logits)
        m2 = jnp.max(rest_l, axis=-1, keepdims=True)
        i2 = jnp.min(jnp.where(rest_l == m2, lane, float(LANES)), axis=-1, keepdims=True)
        e2 = jnp.exp(m2 - m1)
        g1 = 1.0 / (1.0 + e2)
        g2 = e2 / (1.0 + e2)
        @pl.when(pl.program_id(0) == 0)
        def _():
            run_ref[...] = jnp.zeros_like(run_ref)
            r, c = _iota2((TM, TM))
            tri_ref[...] = jnp.where(c < r, 1.0, 0.0).astype(BF16)

        first = lane == i1
        second = lane == i2
        picked = jnp.where(first | second, 1.0, 0.0)
        before = _dot(tri_ref[...], picked.astype(BF16)) + run_ref[...]
        rank1 = jnp.sum(jnp.where(first, before, 0.0), axis=-1, keepdims=True)
        rank2 = jnp.sum(jnp.where(second, before, 0.0), axis=-1, keepdims=True)
        run_ref[...] += jnp.sum(picked, axis=0, keepdims=True)
        cnt_ref[...] = jnp.broadcast_to(run_ref[...], cnt_ref.shape)
        rt_ref[...] = (jnp.where(lane == 0.0, i1, 0.0) + jnp.where(lane == 1.0, i2, 0.0)
                       + jnp.where(lane == 2.0, g1, 0.0) + jnp.where(lane == 3.0, g2, 0.0)
                       + jnp.where(lane == 4.0, rank1, 0.0) + jnp.where(lane == 5.0, rank2, 0.0))


def _merge(hs, ys_parts, y_lru, gates, w_glu, b_glu, w_sp, w_lp, w_out, ffn_norm, layer, router=None):
    n = hs.shape[0]
    lay = lambda i: (layer, 0, 0)
    in_specs = [
        pl.BlockSpec((TM, D_MODEL), lambda i: (i, 0)),
        pl.BlockSpec((S5_PARTS, TM, PART_W), lambda i: (0, i, 0)),
        pl.BlockSpec((TM, LRU_WIDTH), lambda i: (i, 0)),
        pl.BlockSpec((TM, 2 * D_MODEL), lambda i: (i, 0)),
        _const_spec((1, S5_WIDTH, S5_WIDTH), lay),
        _const_spec((1, 1, S5_WIDTH), lay),
        _const_spec((1, S5_WIDTH, D_MODEL), lay),
        _const_spec((1, LRU_WIDTH, D_MODEL), lay),
        _const_spec((1, D_MODEL, D_MODEL), lay),
        _const_spec((1, 1, D_MODEL), lay),
    ]
    out_specs = [pl.BlockSpec((TM, D_MODEL), lambda i: (i, 0))]
    out_shape = [jax.ShapeDtypeStruct((n, D_MODEL), F32)]
    if router is None:
        out_specs.append(pl.BlockSpec((TM, D_MODEL), lambda i: (i, 0)))
        out_shape.append(jax.ShapeDtypeStruct((n, D_MODEL), BF16))
    else:
        out_specs.append(pl.BlockSpec((TM * ROW_TILES, LANES), lambda i: (i, 0)))
        out_shape.append(jax.ShapeDtypeStruct((n * ROW_TILES, LANES), F32))
    args = [hs, ys_parts, y_lru, gates, w_glu, b_glu, w_sp, w_lp, w_out, ffn_norm]
    if router is not None:
        rw, rb, j = router
        in_specs += [_const_spec((1, D_MODEL, LANES), lambda i: (j, 0, 0)),
                     _const_spec((1, 1, LANES), lambda i: (j, 0, 0))]
        out_specs += [pl.BlockSpec((TM, LANES), lambda i: (i, 0)),
                      pl.BlockSpec((SUBLANES, LANES), lambda i: (0, 0))]
        out_shape += [jax.ShapeDtypeStruct((n, LANES), F32),
                      jax.ShapeDtypeStruct((SUBLANES, LANES), F32)]
        args += [rw, rb]
    return pl.pallas_call(
        functools.partial(_merge_kernel, with_router=router is not None),
        grid=(n // TM,),
        in_specs=in_specs,
        out_specs=out_specs,
        out_shape=out_shape,
        scratch_shapes=[pltpu.VMEM((1, LANES), F32), pltpu.VMEM((TM, TM), BF16)] if router is not None else [],
        compiler_params=pltpu.CompilerParams(
            dimension_semantics=("arbitrary",), vmem_limit_bytes=VMEM_LIMIT),
        name="merge_router" if router is not None else "merge",
    )(*args)


def _ffn_kernel(x_ref, hs_ref, wg_ref, wu_ref, wd_ref, o_ref, acc_ref):
    c = pl.program_id(1)

    @pl.when(c == 0)
    def _():
        acc_ref[...] = jnp.zeros_like(acc_ref)

    x = x_ref[...]
    g = _dot(x, wg_ref[0].astype(BF16))
    h = g * jax.nn.sigmoid(g) * _dot(x, wu_ref[0].astype(BF16))
    acc_ref[...] += _dot(h.astype(BF16), wd_ref[0].astype(BF16))

    @pl.when(c == pl.num_programs(1) - 1)
    def _():
        o_ref[...] = hs_ref[...] + acc_ref[...]


def _ffn(hn, hs, w_gate, w_up, w_down, layer):
    n = hn.shape[0]
    ff = w_gate.shape[-1]
    return pl.pallas_call(
        _ffn_kernel,
        grid=(n // TM_FFN, ff // FF_CHUNK),
        in_specs=[
            pl.BlockSpec((TM_FFN, D_MODEL), lambda i, c: (i, 0)),
            pl.BlockSpec((TM_FFN, D_MODEL), lambda i, c: (i, 0)),
            pl.BlockSpec((1, D_MODEL, FF_CHUNK), lambda i, c: (layer, 0, c)),
            pl.BlockSpec((1, D_MODEL, FF_CHUNK), lambda i, c: (layer, 0, c)),
            pl.BlockSpec((1, FF_CHUNK, D_MODEL), lambda i, c: (layer, c, 0)),
        ],
        out_specs=pl.BlockSpec((TM_FFN, D_MODEL), lambda i, c: (i, 0)),
        out_shape=jax.ShapeDtypeStruct((n, D_MODEL), F32),
        scratch_shapes=[pltpu.VMEM((TM_FFN, D_MODEL), F32)],
        compiler_params=pltpu.CompilerParams(
            dimension_semantics=("arbitrary", "arbitrary"), vmem_limit_bytes=VMEM_LIMIT),
        name="dense_ffn",
    )(hn, hs, w_gate, w_up, w_down)


def _moe_plan(route, counts_f, n):
    n_pairs = 2 * n
    n_blocks = -(-n_pairs // MOE_BLOCK) + N_EXPERTS
    e = route[:, 0:2].astype(jnp.int32).reshape(n_pairs)
    rank = route[:, 4:6].astype(jnp.int32).reshape(n_pairs)
    counts = counts_f[0, :N_EXPERTS].astype(jnp.int32)
    onehot = (e[:, None] == jnp.arange(N_EXPERTS, dtype=jnp.int32)[None, :]).astype(jnp.int32)
    padded = ((counts + MOE_BLOCK - 1) // MOE_BLOCK) * MOE_BLOCK
    cum_pad = jnp.cumsum(padded)
    pad_start = cum_pad - padded
    pos = jnp.sum(pad_start[None, :] * onehot, axis=1) + rank
    block_start = jnp.arange(n_blocks, dtype=jnp.int32) * MOE_BLOCK
    block_expert = jnp.minimum(jnp.searchsorted(cum_pad, block_start, side='right'),
                               N_EXPERTS - 1).astype(jnp.int32)
    n_used = (cum_pad[-1] // MOE_BLOCK).astype(jnp.int32).reshape(1)
    pad_range = jnp.stack([pad_start + counts, cum_pad], axis=1).reshape(2 * N_EXPERTS).astype(jnp.int32)
    return pos.astype(jnp.int32), block_expert, n_used, pad_range


def _tile_gather(src_hbm, dst, sem, rows, index_of):
    def body(r, carry):
        src = pl.multiple_of(index_of(r) * ROW_TILES, ROW_TILES)
        out = pl.multiple_of(r * ROW_TILES, ROW_TILES)
        pltpu.make_async_copy(src_hbm.at[pl.ds(src, ROW_TILES)], dst.at[pl.ds(out, ROW_TILES)], sem).start()
        return carry
    lax.fori_loop(0, rows, body, 0, unroll=8)


def _tile_gather_wait(src_hbm, dst, sem, rows):
    pltpu.make_async_copy(src_hbm.at[pl.ds(0, rows * ROW_TILES)], dst, sem).wait()


def _moe_ffn_kernel(be_ref, pos_ref, pad_ref, nu_ref, x_hbm, wg_ref, wu_ref, wd_ref, y_ref, xbuf, st_ref, sem):
    i = pl.program_id(0)
    n_used = nu_ref[0]
    slot = i % 2

    @pl.when(i == 0)
    def _():
        for e in range(N_EXPERTS):
            def clear(s, carry):
                st_ref[s] = 0
                return carry
            lax.fori_loop(pad_ref[2 * e], pad_ref[2 * e + 1], clear, 0)

        def put(p, carry):
            st_ref[pos_ref[p]] = lax.shift_right_logical(p, 1)
            return carry
        lax.fori_loop(0, pos_ref.shape[0], put, 0, unroll=8)

    def start(blk, s):
        _tile_gather(x_hbm, xbuf.at[s], sem.at[s], MOE_BLOCK, lambda r: st_ref[blk * MOE_BLOCK + r])

    @pl.when((i == 0) & (n_used > 0))
    def _():
        start(0, 0)

    @pl.when(i + 1 < n_used)
    def _():
        start(i + 1, 1 - slot)

    @pl.when(i < n_used)
    def _():
        _tile_gather_wait(x_hbm, xbuf.at[slot], sem.at[slot], MOE_BLOCK)
        x = _rows_from_tiles(xbuf.at[slot], MOE_BLOCK).astype(BF16)
        g = _dot(x, wg_ref[0])
        h = g * jax.nn.sigmoid(g) * _dot(x, wu_ref[0])
        _rows_to_tiles(y_ref, _dot(h.astype(BF16), wd_ref[0]))

    @pl.when(i >= n_used)
    def _():
        y_ref[...] = jnp.zeros_like(y_ref)


def _moe_ffn(hn_tiles, pos, block_expert, n_used, pad_range, w_gate, w_up, w_down, first):
    n_blocks = block_expert.shape[0]
    n_slots = n_blocks * MOE_BLOCK
    ff = w_gate.shape[-1]
    wmap = lambda i, be, ps, pr, nu: (first + be[i], 0, 0)
    return pl.pallas_call(
        _moe_ffn_kernel,
        grid_spec=pltpu.PrefetchScalarGridSpec(
            num_scalar_prefetch=4,
            grid=(n_blocks,),
            in_specs=[
                pl.BlockSpec(memory_space=pl.ANY),
                pl.BlockSpec((1, D_MODEL, ff), wmap),
                pl.BlockSpec((1, D_MODEL, ff), wmap),
                pl.BlockSpec((1, ff, D_MODEL), wmap),
            ],
            out_specs=pl.BlockSpec((MOE_BLOCK * ROW_TILES, LANES), lambda i, be, ps, pr, nu: (i, 0)),
            scratch_shapes=[pltpu.VMEM((2, MOE_BLOCK * ROW_TILES, LANES), F32),
                            pltpu.SMEM((n_slots,), jnp.int32),
                            pltpu.SemaphoreType.DMA((2,))],
        ),
        out_shape=jax.ShapeDtypeStruct((n_slots * ROW_TILES, LANES), F32),
        compiler_params=pltpu.CompilerParams(
            dimension_semantics=("arbitrary",), vmem_limit_bytes=VMEM_LIMIT),
        name="moe_ffn",
    )(block_expert, pos, pad_range, n_used, hn_tiles, w_gate, w_up, w_down)


def _moe_combine_kernel(pos_ref, hs_ref, rt_ref, ys_hbm, o_ref, ybuf, sem):
    i = pl.program_id(0)
    nt = pl.num_programs(0)
    slot = i % 2

    def start(t, s):
        for k in range(2):
            _tile_gather(ys_hbm, ybuf.at[s, k], sem.at[s], TM, lambda r: pos_ref[2 * (t * TM + r) + k])

    @pl.when(i == 0)
    def _():
        start(0, 0)

    @pl.when(i + 1 < nt)
    def _():
        start(i + 1, 1 - slot)

    for k in range(2):
        _tile_gather_wait(ys_hbm, ybuf.at[slot, k], sem.at[slot], TM)
    rt = rt_ref[...]
    lane = lax.broadcasted_iota(jnp.int32, rt.shape, 1)
    g1 = jnp.sum(jnp.where(lane == 2, rt, 0.0), axis=-1, keepdims=True)
    g2 = jnp.sum(jnp.where(lane == 3, rt, 0.0), axis=-1, keepdims=True)
    o_ref[...] = (hs_ref[...] + g1 * _rows_from_tiles(ybuf.at[slot, 0], TM)
                  + g2 * _rows_from_tiles(ybuf.at[slot, 1], TM))


def _moe_combine(hs, route, ys_tiles, pos):
    n = hs.shape[0]
    return pl.pallas_call(
        _moe_combine_kernel,
        grid_spec=pltpu.PrefetchScalarGridSpec(
            num_scalar_prefetch=1,
            grid=(n // TM,),
            in_specs=[
                pl.BlockSpec((TM, D_MODEL), lambda i, p: (i, 0)),
                pl.BlockSpec((TM, LANES), lambda i, p: (i, 0)),
                pl.BlockSpec(memory_space=pl.ANY),
            ],
            out_specs=pl.BlockSpec((TM, D_MODEL), lambda i, p: (i, 0)),
            scratch_shapes=[pltpu.VMEM((2, 2, TM * ROW_TILES, LANES), F32),
                            pltpu.SemaphoreType.DMA((2,))],
        ),
        out_shape=jax.ShapeDtypeStruct((n, D_MODEL), F32),
        compiler_params=pltpu.CompilerParams(
            dimension_semantics=("arbitrary",), vmem_limit_bytes=VMEM_LIMIT),
        name="moe_combine",
    )(pos, hs, route, ys_tiles)


def _final_kernel(a_ref, b_ref, g_ref, o_ref):
    tb = a_ref.shape[1]
    o_ref[0, :tb - N_META] = _rms(a_ref[0, N_META:], g_ref[...])
    o_ref[0, tb - N_META:] = _rms(b_ref[0], g_ref[...])


def _final_norm(hs, g, bsz, seq):
    hs3 = hs.reshape(bsz, T_PAD, D_MODEL)
    return pl.pallas_call(
        _final_kernel,
        grid=(bsz, seq // TB_FINAL),
        in_specs=[pl.BlockSpec((1, TB_FINAL, D_MODEL), lambda b, i: (b, i, 0)),
                  pl.BlockSpec((1, N_META, D_MODEL), lambda b, i: (b, (i + 1) * (TB_FINAL // N_META), 0)),
                  _const_spec((1, D_MODEL), lambda b, i: (0, 0))],
        out_specs=pl.BlockSpec((1, TB_FINAL, D_MODEL), lambda b, i: (b, i, 0)),
        out_shape=jax.ShapeDtypeStruct((bsz, seq, D_MODEL), F32),
        compiler_params=pltpu.CompilerParams(
            dimension_semantics=("arbitrary", "arbitrary"), vmem_limit_bytes=VMEM_LIMIT),
        name="final_norm",
    )(hs3, hs3, g)


def _head_blockdiag(w):
    eye = jnp.eye(LRU_HEADS, dtype=w.dtype)
    out = jnp.einsum('lnhk,nm->lnhmk', w, eye)
    return out.reshape(w.shape[0], LRU_WIDTH, LRU_WIDTH)


def kernel(x, meta_tokens, mix_norm, w_in, merge_bias, s5_lambda_re, s5_lambda_im, s5_log_dt, s5_b_re, s5_b_im, s5_c_re, s5_c_im, s5_d, s5_w_glu, s5_b_glu, s5_w_proj, lru_conv_w, lru_conv_b, lru_w_rgate, lru_b_rgate, lru_w_igate, lru_b_igate, lru_lambda, lru_w_proj, w_out, ffn_norm, dense_w_gate, dense_w_up, dense_w_down, router_w, router_b, moe_w_gate, moe_w_up, moe_w_down, final_norm):
    bsz, seq, d = x.shape
    depth = w_in.shape[0]
    assert d == D_MODEL and N_META + seq <= T_PAD
    n = bsz * T_PAD
    assert n % TM == 0 and n % TM_FFN == 0 and seq % TB_FINAL == 0 and TB_FINAL % N_META == 0

    meta = jnp.broadcast_to(meta_tokens[None].astype(x.dtype), (bsz, N_META, d))
    pad = jnp.zeros((bsz, T_PAD - N_META - seq, d), x.dtype)
    hs = jnp.concatenate([meta, x, pad], axis=1).reshape(n, d)

    row3 = lambda a: a[:, None, :]
    w_in_b = w_in.astype(BF16)
    w_glu_b = s5_w_glu.astype(BF16)
    w_sp_b = s5_w_proj.astype(BF16)
    w_lp_b = lru_w_proj.astype(BF16)
    w_out_b = w_out.astype(BF16)
    w_ri = jnp.concatenate([_head_blockdiag(lru_w_rgate), _head_blockdiag(lru_w_igate)], axis=-1).astype(BF16)
    b_ri = jnp.concatenate([lru_b_rgate, lru_b_igate], axis=-1)
    neg_sp = -LRU_C * jax.nn.softplus(-lru_lambda)
    dense = (dense_w_gate, dense_w_up, dense_w_down)
    n_moe = router_w.shape[0]
    moe = [w.astype(BF16).reshape((n_moe * N_EXPERTS,) + w.shape[2:]) for w in (moe_w_gate, moe_w_up, moe_w_down)]
    s5_ops = jax.vmap(_s5_prep)(s5_lambda_re, s5_lambda_im, s5_log_dt, s5_b_re, s5_b_im, s5_c_re, s5_c_im, s5_d)
    rw_pad = jnp.pad(router_w, ((0, 0), (0, 0), (0, LANES - N_EXPERTS)))
    rb_pad = jnp.pad(router_b, ((0, 0), (0, LANES - N_EXPERTS)), constant_values=MASKED_LOGIT)

    for layer in range(depth):
        u_parts, x_lru, g_lru, gates = _in_proj(hs, row3(mix_norm), w_in_b, row3(merge_bias), layer)
        ys_parts = _s5_scan(u_parts, s5_ops, layer, bsz)
        y_lru = _lru(x_lru, g_lru, lru_conv_w, row3(lru_conv_b), w_ri, row3(b_ri), row3(neg_sp), layer, bsz)
        j = layer // 2
        router = (rw_pad, row3(rb_pad), j) if layer % 2 == 1 else None
        res = _merge(hs, ys_parts, y_lru, gates, w_glu_b, row3(s5_b_glu), w_sp_b, w_lp_b, w_out_b,
                     row3(ffn_norm), layer, router)
        if layer % 2 == 0:
            hs, hn = res
            hs = _ffn(hn, hs, *dense, layer=j)
        else:
            hs, hn, route, counts = res
            pos, block_expert, n_used, pad_range = _moe_plan(route, counts, n)
            ys = _moe_ffn(hn, pos, block_expert, n_used, pad_range, *moe, first=j * N_EXPERTS)
            hs = _moe_combine(hs, route, ys, pos)

    return _final_norm(hs, final_norm[None, :], bsz, seq)
```

```python
import functools

import jax
import jax.numpy as jnp
from jax import lax
from jax.experimental import pallas as pl
from jax.experimental.pallas import tpu as pltpu

F32 = jnp.float32
BF16 = jnp.bfloat16

D_MODEL = 1024
N_META = 16
S5_WIDTH = 512
S5_GROUP = 16
S5_GROUPS = 32
S5_STATE = 64
LRU_WIDTH = 512
LRU_HEADS = 8
LRU_HEAD_DIM = 64
CONV_WIDTH = 4
LRU_C = 8.0
N_EXPERTS = 8
EPS = 1e-6

FOLD = 8
S5_PARTS = 4
PART_W = S5_WIDTH // S5_PARTS
PART_GROUPS = PART_W // S5_GROUP
PART_STATE = PART_GROUPS * S5_STATE
FOLD_W = FOLD * PART_W

T_PAD = 8256
ROWS = T_PAD // FOLD
TM = 688
TM_ROUTER = 384
TM_FFN = 1376
FF_CHUNK = 512
MOE_BLOCK = 512
DISPATCH_CHUNK = 516
DISPATCH_GROUP = 12
GATHER_GROUP = 8
LRU_CHUNK = 1032
LRU_UNROLL = 3
TB_FINAL = 512
VMEM_LIMIT = 56 * 1024 * 1024
LANES = 128
SUBLANES = 8
ROW_TILES = D_MODEL // LANES
MASKED_LOGIT = float("-inf")


def _dot(a, b):
    return jnp.dot(a, b, preferred_element_type=F32)


def _const_spec(block_shape, index_map):
    return pl.BlockSpec(block_shape, index_map, pipeline_mode=pl.Buffered(1))


def _rms(x, g):
    ms = jnp.mean(x * x, axis=-1, keepdims=True)
    return x * lax.rsqrt(ms + EPS) * g


def _rows_to_tiles(ref, x):
    rows = x.shape[0]
    for s in range(ROW_TILES):
        ref[pl.ds(s, rows, stride=ROW_TILES), :] = x[:, s * LANES:(s + 1) * LANES]


def _rows_from_tiles(ref, rows):
    return jnp.concatenate([ref[pl.ds(s, rows, stride=ROW_TILES), :] for s in range(ROW_TILES)], axis=-1)


def _in_proj_kernel(hs_ref, g_ref, w_ref, mb_ref, u_ref, xl_ref, gl_ref, gt_ref):
    hn = _rms(hs_ref[...], g_ref[0]).astype(BF16)
    u = _dot(hn, w_ref[0, :, 0:S5_WIDTH])
    for q in range(S5_PARTS):
        u_ref[q] = u[:, q * PART_W:(q + 1) * PART_W]
    o_x = S5_WIDTH
    o_g = o_x + LRU_WIDTH
    o_m = o_g + LRU_WIDTH
    xl_ref[...] = _dot(hn, w_ref[0, :, o_x:o_g]).astype(BF16)
    gl_ref[...] = _dot(hn, w_ref[0, :, o_g:o_m]).astype(BF16)
    z = _dot(hn, w_ref[0, :, o_m:]) + mb_ref[0]
    gt_ref[...] = jax.nn.sigmoid(z).astype(BF16)


def _in_proj(hs, mix_norm, w_in, merge_bias, layer):
    n = hs.shape[0]
    d_in = w_in.shape[-1]
    lay = lambda i: (layer, 0, 0)
    return pl.pallas_call(
        _in_proj_kernel,
        grid=(n // TM,),
        in_specs=[
            pl.BlockSpec((TM, D_MODEL), lambda i: (i, 0)),
            _const_spec((1, 1, D_MODEL), lay),
            _const_spec((1, D_MODEL, d_in), lay),
            _const_spec((1, 1, 2 * D_MODEL), lay),
        ],
        out_specs=[
            pl.BlockSpec((S5_PARTS, TM, PART_W), lambda i: (0, i, 0)),
            pl.BlockSpec((TM, LRU_WIDTH), lambda i: (i, 0)),
            pl.BlockSpec((TM, LRU_WIDTH), lambda i: (i, 0)),
            pl.BlockSpec((TM, 2 * D_MODEL), lambda i: (i, 0)),
        ],
        out_shape=[
            jax.ShapeDtypeStruct((S5_PARTS, n, PART_W), F32),
            jax.ShapeDtypeStruct((n, LRU_WIDTH), BF16),
            jax.ShapeDtypeStruct((n, LRU_WIDTH), BF16),
            jax.ShapeDtypeStruct((n, 2 * D_MODEL), BF16),
        ],
        compiler_params=pltpu.CompilerParams(
            dimension_semantics=("arbitrary",), vmem_limit_bytes=VMEM_LIMIT),
        name="in_proj",
    )(hs, mix_norm, w_in, merge_bias)


def _s5_prep(lam_re, lam_im, log_dt, b_re, b_im, c_re, c_im, d_skip):
    dt = jnp.exp(log_dt)[:, None]
    mag = jnp.exp(lam_re * dt)
    a_re = mag * jnp.cos(lam_im * dt)
    a_im = mag * jnp.sin(lam_im * dt)
    den = lam_re * lam_re + lam_im * lam_im
    num_re = a_re - 1.0
    coef_re = (num_re * lam_re + a_im * lam_im) / den
    coef_im = (a_im * lam_re - num_re * lam_im) / den
    bb_re = coef_re[..., None] * b_re - coef_im[..., None] * b_im
    bb_im = coef_re[..., None] * b_im + coef_im[..., None] * b_re

    def cmul(xr, xi, yr, yi):
        return xr * yr - xi * yi, xr * yi + xi * yr

    def powers(br, bi, n):
        pr, pi = [jnp.ones_like(br)], [jnp.zeros_like(bi)]
        for _ in range(n):
            r, i = cmul(pr[-1], pi[-1], br, bi)
            pr.append(r)
            pi.append(i)
        return jnp.stack(pr), jnp.stack(pi)

    p_re, p_im = powers(a_re, a_im, FOLD)
    q_re, q_im = powers(p_re[FOLD], p_im[FOLD], FOLD)

    def per_part(x):
        lead = x.shape[:-3]
        xp = x.reshape(lead + (S5_PARTS, PART_GROUPS) + x.shape[-2:])
        return jnp.moveaxis(xp, len(lead), 0)

    rev_re = jnp.stack([p_re[FOLD - 1 - j] for j in range(FOLD)])
    rev_im = jnp.stack([p_im[FOLD - 1 - j] for j in range(FOLD)])
    wr, wi = cmul(rev_re[..., None], rev_im[..., None], bb_re[None], bb_im[None])
    w_ri = jnp.swapaxes(jnp.stack([wr, wi], axis=1), -1, -2)
    xq = jnp.transpose(per_part(w_ri), (0, 1, 3, 4, 2, 5)).reshape(S5_PARTS, FOLD_W, 2 * S5_STATE)

    ca_re, ca_im = cmul(c_re[None], c_im[None], p_re[:, :, None, :], p_im[:, :, None, :])
    bt_re = jnp.swapaxes(bb_re, -1, -2)[None, :, :, None, :]
    bt_im = jnp.swapaxes(bb_im, -1, -2)[None, :, :, None, :]
    taps = jnp.sum(ca_re[:FOLD, :, None] * bt_re - ca_im[:FOLD, :, None] * bt_im, axis=-1)
    skip = d_skip.reshape(S5_GROUPS, S5_GROUP)
    taps = taps.at[0].add(skip[:, :, None] * jnp.eye(S5_GROUP, dtype=F32)[None])
    rc = jnp.transpose(per_part(taps), (0, 2, 3, 1, 4)).reshape(S5_PARTS, PART_W, FOLD * S5_GROUP)

    v_ri = jnp.swapaxes(jnp.stack([ca_re[1:], -ca_im[1:]], axis=0), -1, -2)
    vc = jnp.transpose(per_part(v_ri), (0, 1, 3, 4, 2, 5)).reshape(S5_PARTS, 2 * PART_STATE, FOLD * S5_GROUP)

    def part_vec(x):
        lead = x.shape[:-2]
        xp = x.reshape(lead + (S5_PARTS, PART_STATE))
        return jnp.moveaxis(xp, -2, 0)

    lvl = jnp.stack([jnp.stack([part_vec(q_re[k]), part_vec(q_im[k])], axis=1) for k in (1, 2, 4)], axis=1)
    lvl = lvl[:, :, :, None, :]
    rowpow = jnp.stack([part_vec(q_re[1:]), part_vec(q_im[1:])], axis=1)
    return xq, rc, vc, lvl, rowpow


def _iota2(shape):
    return (lax.broadcasted_iota(jnp.int32, shape, 0), lax.broadcasted_iota(jnp.int32, shape, 1))


def _s5_expand(xq, rc, vc, w1_s, tv_s):
    ps = PART_STATE
    lg_state, lg_group, lg_part = (v.bit_length() - 1 for v in (S5_STATE, S5_GROUP, PART_W))
    lg_pg = PART_GROUPS.bit_length() - 1
    grp = PART_GROUPS - 1
    one_hot = lambda m: jnp.where(m, 1.0, 0.0).astype(BF16)
    r, c = _iota2((2 * S5_STATE, 2 * ps))
    e1 = one_hot(((r >> lg_state) == (c >> (lg_state + lg_pg))) & ((r & (S5_STATE - 1)) == (c & (S5_STATE - 1))))
    r, c = _iota2((FOLD * S5_GROUP, FOLD_W))
    e2 = one_hot(((r >> lg_group) == (c >> lg_part)) & ((r & (S5_GROUP - 1)) == (c & (S5_GROUP - 1))))
    r, c = _iota2((FOLD_W, 2 * ps))
    m1 = ((r >> lg_group) & grp) == ((c >> lg_state) & grp)
    w1_s[...] = jnp.where(m1, _dot(xq.astype(BF16), e1), 0.0).astype(BF16)
    r, c = _iota2((PART_W, FOLD_W))
    m2 = (r >> lg_group) == ((c >> lg_group) & grp)
    r0 = jnp.where(m2, _dot(rc.astype(BF16), e2), 0.0).astype(BF16)
    for j in range(FOLD):
        if j == 0:
            blk = r0
        else:
            blk = jnp.concatenate([jnp.zeros((PART_W, j * PART_W), BF16), r0[:, :FOLD_W - j * PART_W]], axis=1)
        tv_s[j * PART_W:(j + 1) * PART_W, :] = blk
    r, c = _iota2((2 * ps, FOLD_W))
    m3 = ((r >> lg_state) & grp) == ((c >> lg_group) & grp)
    tv_s[FOLD_W:, :] = jnp.where(m3, _dot(vc.astype(BF16), e2), 0.0).astype(BF16)


def _s5_kernel(u_ref, xq_ref, rc_ref, vc_ref, lvl_ref, rp_ref, y_ref, w1_s, tv_s, st_ref):
    ps = PART_STATE

    @pl.when(pl.program_id(1) == 0)
    def _():
        _s5_expand(xq_ref[0, 0], rc_ref[0, 0], vc_ref[0, 0], w1_s, tv_s)

    u = jnp.concatenate([u_ref[0, 0, pl.ds(j, ROWS, stride=FOLD), :] for j in range(FOLD)],
                        axis=-1).astype(BF16)
    f = _dot(u, w1_s[...])
    fr = f[:, :ps]
    fi = f[:, ps:]
    row = lax.broadcasted_iota(jnp.int32, (ROWS, ps), 0) & (SUBLANES - 1)
    for lv, k in enumerate((1, 2, 4)):
        ar = lvl_ref[0, 0, lv, 0]
        ai = lvl_ref[0, 0, lv, 1]
        sr = pltpu.roll(fr, k, axis=0)
        si = pltpu.roll(fi, k, axis=0)
        m = row >= k
        fr, fi = (fr + jnp.where(m, ar * sr - ai * si, 0.0),
                  fi + jnp.where(m, ar * si + ai * sr, 0.0))
    st_ref[0:SUBLANES, :] = jnp.zeros((SUBLANES, 2 * ps), F32)
    st_ref[SUBLANES:, :ps] = fr
    st_ref[SUBLANES:, ps:] = fi
    pr = rp_ref[0, 0, 0]
    pi = rp_ref[0, 0, 1]

    def body(i, carry):
        cr, ci = carry
        r = pl.multiple_of(SUBLANES + i * SUBLANES, SUBLANES)
        xr = st_ref[pl.ds(r, SUBLANES), :ps]
        xi = st_ref[pl.ds(r, SUBLANES), ps:]
        hr = xr + pr * cr - pi * ci
        hi = xi + pr * ci + pi * cr
        st_ref[pl.ds(r, SUBLANES), :ps] = hr
        st_ref[pl.ds(r, SUBLANES), ps:] = hi
        return hr[SUBLANES - 1:SUBLANES], hi[SUBLANES - 1:SUBLANES]

    zero = jnp.zeros((1, ps), F32)
    lax.fori_loop(0, ROWS // SUBLANES, body, (zero, zero))
    h_prev = st_ref[pl.ds(SUBLANES - 1, ROWS), :].astype(BF16)
    y = jax.nn.gelu(_dot(u, tv_s[:FOLD_W, :]) + _dot(h_prev, tv_s[FOLD_W:, :]))
    for j in range(FOLD):
        y_ref[0, 0, pl.ds(j, ROWS, stride=FOLD), :] = y[:, j * PART_W:(j + 1) * PART_W]


def _s5_scan(u_parts, ops, layer, bsz):
    xq, rc, vc, lvl, rowpow = ops
    n = u_parts.shape[1]
    u4 = u_parts.reshape(S5_PARTS, bsz, T_PAD, PART_W)
    lay4 = lambda q, b: (layer, q, 0, 0)
    y4 = pl.pallas_call(
        _s5_kernel,
        grid=(S5_PARTS, bsz),
        in_specs=[
            pl.BlockSpec((1, 1, T_PAD, PART_W), lambda q, b: (q, b, 0, 0)),
            pl.BlockSpec((1, 1, FOLD_W, 2 * S5_STATE), lay4),
            pl.BlockSpec((1, 1, PART_W, FOLD * S5_GROUP), lay4),
            pl.BlockSpec((1, 1, 2 * PART_STATE, FOLD * S5_GROUP), lay4),
            pl.BlockSpec((1, 1, 3, 2, 1, PART_STATE), lambda q, b: (layer, q, 0, 0, 0, 0)),
            pl.BlockSpec((1, 1, 2, FOLD, PART_STATE), lambda q, b: (layer, q, 0, 0, 0)),
        ],
        out_specs=pl.BlockSpec((1, 1, T_PAD, PART_W), lambda q, b: (q, b, 0, 0)),
        out_shape=jax.ShapeDtypeStruct((S5_PARTS, bsz, T_PAD, PART_W), F32),
        scratch_shapes=[
            pltpu.VMEM((FOLD_W, 2 * PART_STATE), BF16),
            pltpu.VMEM((FOLD_W + 2 * PART_STATE, FOLD_W), BF16),
            pltpu.VMEM((ROWS + SUBLANES, 2 * PART_STATE), F32),
        ],
        compiler_params=pltpu.CompilerParams(
            dimension_semantics=("arbitrary", "arbitrary"), vmem_limit_bytes=VMEM_LIMIT),
        name="s5_scan",
    )(u4, xq, rc, vc, lvl, rowpow)
    return y4.reshape(S5_PARTS, n, PART_W)


def _lru_kernel(x_ref, g_ref, cw_ref, cb_ref, wri_ref, bri_ref, nsp_ref, o_ref,
                xs_ref, gs_ref, xc_ref, gp_ref, z_ref, a_ref, b_ref, os_ref, h_ref):
    tc = LRU_CHUNK
    c = LRU_WIDTH
    seg = tc // SUBLANES
    nq = c // LANES
    lanes = lambda q: slice(q * LANES, (q + 1) * LANES)
    halo = SUBLANES

    @pl.when(pl.program_id(1) == 0)
    def _():
        xs_ref[:, 0:halo, :] = jnp.zeros((nq, halo, LANES), F32)
        h_ref[...] = jnp.zeros((1, c), F32)

    x = x_ref[0].astype(F32)
    g = g_ref[0].astype(F32)
    for q in range(nq):
        xs_ref[q, halo:, :] = x[:, lanes(q)]
        gs_ref[q] = g[:, lanes(q)]
    taps = [[cw_ref[0, k:k + 1, lanes(q)] for k in range(CONV_WIDTH)] for q in range(nq)]
    bias = [cb_ref[0, :, lanes(q)] for q in range(nq)]

    def conv_body(i, carry):
        r0 = pl.multiple_of(i * SUBLANES, SUBLANES)
        for q in range(nq):
            acc = bias[q]
            for k in range(CONV_WIDTH):
                first = halo - (CONV_WIDTH - 1) + k + i
                acc = acc + taps[q][k] * xs_ref[q, pl.ds(first, SUBLANES, stride=seg), :]
            xc_ref[pl.ds(r0, SUBLANES), lanes(q)] = acc
            gp_ref[pl.ds(r0, SUBLANES), lanes(q)] = jax.nn.gelu(gs_ref[q, pl.ds(i, SUBLANES, stride=seg), :])
        return carry

    lax.fori_loop(0, seg, conv_body, 0, unroll=LRU_UNROLL)
    for q in range(nq):
        xs_ref[q, 0:halo, :] = xs_ref[q, tc:tc + halo, :]

    z_ref[...] = _dot(xc_ref[...].astype(BF16), wri_ref[0])
    b_r = jnp.broadcast_to(bri_ref[0, :, :c], (SUBLANES, c))
    b_i = jnp.broadcast_to(bri_ref[0, :, c:], (SUBLANES, c))
    nsp = jnp.broadcast_to(nsp_ref[0], (SUBLANES, c))

    def scan_body(i, carry):
        h, p = carry
        r0 = pl.multiple_of(i * SUBLANES, SUBLANES)
        a = jnp.exp(jax.nn.sigmoid(z_ref[pl.ds(r0, SUBLANES), :c] + b_r) * nsp)
        gated = jax.nn.sigmoid(z_ref[pl.ds(r0, SUBLANES), c:] + b_i) * xc_ref[pl.ds(r0, SUBLANES), :]
        h = a * h + jnp.sqrt(1.0 - a * a) * gated
        p = p * a
        b_ref[pl.ds(r0, SUBLANES), :] = h
        a_ref[pl.ds(r0, SUBLANES), :] = p
        return h, p

    h_end, p_end = lax.fori_loop(0, seg, scan_body, (jnp.zeros((SUBLANES, c), F32), jnp.ones((SUBLANES, c), F32)),
                                 unroll=LRU_UNROLL)
    row = lax.broadcasted_iota(jnp.int32, (SUBLANES, c), 0)
    enter = jnp.where(row == 0, h_ref[...], 0.0)
    for sgm in range(SUBLANES - 1):
        leave = h_end + p_end * enter
        enter = enter + jnp.where(row == sgm + 1, pltpu.roll(leave, 1, axis=0), 0.0)
    h_ref[...] = (h_end + p_end * enter)[SUBLANES - 1:SUBLANES]

    def out_body(i, carry):
        r0 = pl.multiple_of(i * SUBLANES, SUBLANES)
        h = b_ref[pl.ds(r0, SUBLANES), :] + a_ref[pl.ds(r0, SUBLANES), :] * enter
        y = h * gp_ref[pl.ds(r0, SUBLANES), :]
        for q in range(nq):
            os_ref[q, pl.ds(i, SUBLANES, stride=seg), :] = y[:, lanes(q)]
        return carry

    lax.fori_loop(0, seg, out_body, 0)
    o_ref[0] = jnp.concatenate([os_ref[q] for q in range(nq)], axis=-1).astype(BF16)


def _lru(x_lru, g_lru, conv_w, conv_b, w_ri, b_ri, neg_sp, layer, bsz):
    n = x_lru.shape[0]
    c = LRU_WIDTH
    x3 = x_lru.reshape(bsz, T_PAD, c)
    g3 = g_lru.reshape(bsz, T_PAD, c)
    lay = lambda b, t: (layer, 0, 0)
    out = pl.pallas_call(
        _lru_kernel,
        grid=(bsz, T_PAD // LRU_CHUNK),
        in_specs=[
            pl.BlockSpec((1, LRU_CHUNK, c), lambda b, t: (b, t, 0)),
            pl.BlockSpec((1, LRU_CHUNK, c), lambda b, t: (b, t, 0)),
            _const_spec((1, CONV_WIDTH, c), lay),
            _const_spec((1, 1, c), lay),
            _const_spec((1, c, 2 * c), lay),
            _const_spec((1, 1, 2 * c), lay),
            _const_spec((1, 1, c), lay),
        ],
        out_specs=pl.BlockSpec((1, LRU_CHUNK, c), lambda b, t: (b, t, 0)),
        out_shape=jax.ShapeDtypeStruct((bsz, T_PAD, c), BF16),
        scratch_shapes=[
            pltpu.VMEM((c // LANES, LRU_CHUNK + SUBLANES, LANES), F32),
            pltpu.VMEM((c // LANES, LRU_CHUNK, LANES), F32),
            pltpu.VMEM((LRU_CHUNK, c), F32),
            pltpu.VMEM((LRU_CHUNK, c), F32),
            pltpu.VMEM((LRU_CHUNK, 2 * c), F32),
            pltpu.VMEM((LRU_CHUNK, c), F32),
            pltpu.VMEM((LRU_CHUNK, c), F32),
            pltpu.VMEM((c // LANES, LRU_CHUNK, LANES), F32),
            pltpu.VMEM((1, c), F32),
        ],
        compiler_params=pltpu.CompilerParams(
            dimension_semantics=("arbitrary", "arbitrary"), vmem_limit_bytes=VMEM_LIMIT),
        name="rglru",
    )(x3, g3, conv_w, conv_b, w_ri, b_ri, neg_sp)
    return out.reshape(n, c)


def _merge_kernel(hs_ref, ys_ref, yl_ref, gt_ref, wglu_ref, bglu_ref, wsp_ref, wlp_ref, wout_ref, g_ref,
                  *rest, with_router):
    if with_router:
        rw_ref, rb_ref, hs_out_ref, hn_ref, rt_ref, rtt_ref, cnt_ref, run_ref, tri_ref = rest
    else:
        hs_out_ref, hn_ref = rest
    ys = jnp.concatenate([ys_ref[q] for q in range(S5_PARTS)], axis=-1)
    glu = ys * jax.nn.sigmoid(_dot(ys.astype(BF16), wglu_ref[0]) + bglu_ref[0])
    y_a = _dot(glu.astype(BF16), wsp_ref[0])
    y_b = _dot(yl_ref[...], wlp_ref[0])
    y = gt_ref[:, :D_MODEL].astype(F32) * y_a + gt_ref[:, D_MODEL:].astype(F32) * y_b
    hs = hs_ref[...] + _dot(y.astype(BF16), wout_ref[0])
    hs_out_ref[...] = hs
    hn = _rms(hs, g_ref[0])
    if not with_router:
        hn_ref[...] = hn.astype(BF16)
    else:
        _rows_to_tiles(hn_ref, hn)
        logits = _dot(hn.astype(BF16), rw_ref[0].astype(BF16)) + rb_ref[0]
        lane = lax.broadcasted_iota(jnp.int32, logits.shape, 1).astype(F32)
        m1 = jnp.max(logits, axis=-1, keepdims=True)
        i1 = jnp.min(jnp.where(logits == m1, lane, float(LANES)), axis=-1, keepdims=True)
        rest_l = jnp.where(lane == i1, MASKED_LOGIT, logits)
        m2 = jnp.max(rest_l, axis=-1, keepdims=True)
        i2 = jnp.min(jnp.where(rest_l == m2, lane, float(LANES)), axis=-1, keepdims=True)
        e2 = jnp.exp(m2 - m1)
        g1 = 1.0 / (1.0 + e2)
        g2 = e2 / (1.0 + e2)
        @pl.when(pl.program_id(0) == 0)
        def _():
            run_ref[...] = jnp.zeros_like(run_ref)
            r, c = _iota2(tri_ref.shape)
            tri_ref[...] = jnp.where(c < r, 1.0, 0.0).astype(BF16)

        first = lane == i1
        second = lane == i2
        picked = jnp.where(first | second, 1.0, 0.0)
        before = _dot(tri_ref[...], picked.astype(BF16)) + run_ref[...]
        rank1 = jnp.sum(jnp.where(first, before, 0.0), axis=-1, keepdims=True)
        rank2 = jnp.sum(jnp.where(second, before, 0.0), axis=-1, keepdims=True)
        run_ref[...] += jnp.sum(picked, axis=0, keepdims=True)
        cnt_ref[...] = jnp.broadcast_to(run_ref[...], cnt_ref.shape)
        rt = (jnp.where(lane == 0.0, i1, 0.0) + jnp.where(lane == 1.0, i2, 0.0)
              + jnp.where(lane == 2.0, g1, 0.0) + jnp.where(lane == 3.0, g2, 0.0)
              + jnp.where(lane == 4.0, rank1, 0.0) + jnp.where(lane == 5.0, rank2, 0.0))
        rt_ref[...] = rt
        rtt_ref[0] = jnp.transpose(rt)[:SUBLANES, :]


def _merge(hs, ys_parts, y_lru, gates, w_glu, b_glu, w_sp, w_lp, w_out, ffn_norm, layer, router=None):
    n = hs.shape[0]
    tm = TM if router is None else TM_ROUTER
    lay = lambda i: (layer, 0, 0)
    in_specs = [
        pl.BlockSpec((tm, D_MODEL), lambda i: (i, 0)),
        pl.BlockSpec((S5_PARTS, tm, PART_W), lambda i: (0, i, 0)),
        pl.BlockSpec((tm, LRU_WIDTH), lambda i: (i, 0)),
        pl.BlockSpec((tm, 2 * D_MODEL), lambda i: (i, 0)),
        _const_spec((1, S5_WIDTH, S5_WIDTH), lay),
        _const_spec((1, 1, S5_WIDTH), lay),
        _const_spec((1, S5_WIDTH, D_MODEL), lay),
        _const_spec((1, LRU_WIDTH, D_MODEL), lay),
        _const_spec((1, D_MODEL, D_MODEL), lay),
        _const_spec((1, 1, D_MODEL), lay),
    ]
    out_specs = [pl.BlockSpec((tm, D_MODEL), lambda i: (i, 0))]
    out_shape = [jax.ShapeDtypeStruct((n, D_MODEL), F32)]
    if router is None:
        out_specs.append(pl.BlockSpec((tm, D_MODEL), lambda i: (i, 0)))
        out_shape.append(jax.ShapeDtypeStruct((n, D_MODEL), BF16))
    else:
        out_specs.append(pl.BlockSpec((tm * ROW_TILES, LANES), lambda i: (i, 0)))
        out_shape.append(jax.ShapeDtypeStruct((n * ROW_TILES, LANES), F32))
    args = [hs, ys_parts, y_lru, gates, w_glu, b_glu, w_sp, w_lp, w_out, ffn_norm]
    if router is not None:
        rw, rb, j = router
        in_specs += [_const_spec((1, D_MODEL, LANES), lambda i: (j, 0, 0)),
                     _const_spec((1, 1, LANES), lambda i: (j, 0, 0))]
        out_specs += [pl.BlockSpec((tm, LANES), lambda i: (i, 0)),
                      pl.BlockSpec((1, SUBLANES, tm), lambda i: (i, 0, 0)),
                      pl.BlockSpec((SUBLANES, LANES), lambda i: (0, 0))]
        out_shape += [jax.ShapeDtypeStruct((n, LANES), F32),
                      jax.ShapeDtypeStruct((n // tm, SUBLANES, tm), F32),
                      jax.ShapeDtypeStruct((SUBLANES, LANES), F32)]
        args += [rw, rb]
    return pl.pallas_call(
        functools.partial(_merge_kernel, with_router=router is not None),
        grid=(n // tm,),
        in_specs=in_specs,
        out_specs=out_specs,
        out_shape=out_shape,
        scratch_shapes=[pltpu.VMEM((1, LANES), F32), pltpu.VMEM((tm, tm), BF16)] if router is not None else [],
        compiler_params=pltpu.CompilerParams(
            dimension_semantics=("arbitrary",), vmem_limit_bytes=VMEM_LIMIT),
        name="merge_router" if router is not None else "merge",
    )(*args)


def _ffn_kernel(x_ref, hs_ref, wg_ref, wu_ref, wd_ref, o_ref, acc_ref):
    c = pl.program_id(1)

    @pl.when(c == 0)
    def _():
        acc_ref[...] = jnp.zeros_like(acc_ref)

    x = x_ref[...]
    g = _dot(x, wg_ref[0].astype(BF16))
    h = g * jax.nn.sigmoid(g) * _dot(x, wu_ref[0].astype(BF16))
    acc_ref[...] += _dot(h.astype(BF16), wd_ref[0].astype(BF16))

    @pl.when(c == pl.num_programs(1) - 1)
    def _():
        o_ref[...] = hs_ref[...] + acc_ref[...]


def _ffn(hn, hs, w_gate, w_up, w_down, layer):
    n = hn.shape[0]
    ff = w_gate.shape[-1]
    return pl.pallas_call(
        _ffn_kernel,
        grid=(n // TM_FFN, ff // FF_CHUNK),
        in_specs=[
            pl.BlockSpec((TM_FFN, D_MODEL), lambda i, c: (i, 0)),
            pl.BlockSpec((TM_FFN, D_MODEL), lambda i, c: (i, 0)),
            pl.BlockSpec((1, D_MODEL, FF_CHUNK), lambda i, c: (layer, 0, c)),
            pl.BlockSpec((1, D_MODEL, FF_CHUNK), lambda i, c: (layer, 0, c)),
            pl.BlockSpec((1, FF_CHUNK, D_MODEL), lambda i, c: (layer, c, 0)),
        ],
        out_specs=pl.BlockSpec((TM_FFN, D_MODEL), lambda i, c: (i, 0)),
        out_shape=jax.ShapeDtypeStruct((n, D_MODEL), F32),
        scratch_shapes=[pltpu.VMEM((TM_FFN, D_MODEL), F32)],
        compiler_params=pltpu.CompilerParams(
            dimension_semantics=("arbitrary", "arbitrary"), vmem_limit_bytes=VMEM_LIMIT),
        name="dense_ffn",
    )(hn, hs, w_gate, w_up, w_down)


def _moe_plan(route_t, counts_f, n):
    n_blocks = -(-2 * n // MOE_BLOCK) + N_EXPERTS
    e = jnp.stack([route_t[:, 0, :], route_t[:, 1, :]]).astype(jnp.int32)
    rank = jnp.stack([route_t[:, 4, :], route_t[:, 5, :]]).astype(jnp.int32)
    counts = counts_f[0, :N_EXPERTS].astype(jnp.int32)
    padded = ((counts + MOE_BLOCK - 1) // MOE_BLOCK) * MOE_BLOCK
    cum_pad = jnp.cumsum(padded)
    pad_start = cum_pad - padded
    pos = rank
    for x in range(N_EXPERTS):
        pos = pos + jnp.where(e == x, pad_start[x], 0)
    block_start = jnp.arange(n_blocks, dtype=jnp.int32) * MOE_BLOCK
    block_expert = jnp.minimum(jnp.sum((block_start[:, None] >= cum_pad[None, :]).astype(jnp.int32), axis=1),
                               N_EXPERTS - 1)
    n_used = (cum_pad[-1] // MOE_BLOCK).astype(jnp.int32).reshape(1)
    pad_range = jnp.stack([pad_start + counts, cum_pad], axis=1).reshape(2 * N_EXPERTS).astype(jnp.int32)
    return pos.reshape(2 * n), block_expert, n_used, pad_range


def _tile(ref, index):
    return ref.at[pl.ds(pl.multiple_of(index * ROW_TILES, ROW_TILES), ROW_TILES)]


def _tile_gather(src_hbm, dst, sem, rows, index_of):
    def body(grp, carry):
        r0 = grp * GATHER_GROUP
        index = [index_of(r0 + j) for j in range(GATHER_GROUP)]
        for j in range(GATHER_GROUP):
            pltpu.make_async_copy(_tile(src_hbm, index[j]), _tile(dst, r0 + j), sem).start()
        return carry
    lax.fori_loop(0, rows // GATHER_GROUP, body, 0)


def _tile_gather_wait(src_hbm, dst, sem, rows):
    pltpu.make_async_copy(src_hbm.at[pl.ds(0, rows * ROW_TILES)], dst, sem).wait()


def _moe_dispatch_kernel(pos_ref, pad_ref, nu_ref, x_hbm, xs_hbm, zero_ref, sem):
    n = x_hbm.shape[0] // ROW_TILES
    chunks = n // DISPATCH_CHUNK
    block_rows = MOE_BLOCK * ROW_TILES
    n_blocks = xs_hbm.shape[0] // block_rows
    zero_ref[...] = jnp.zeros_like(zero_ref)
    zero_tile = zero_ref.at[pl.ds(0, ROW_TILES)]

    def chunk_wait():
        pltpu.make_async_copy(x_hbm.at[pl.ds(0, DISPATCH_CHUNK * ROW_TILES)],
                              xs_hbm.at[pl.ds(0, DISPATCH_CHUNK * ROW_TILES)], sem.at[0]).wait()

    for k in range(2):
        def chunk(ci, carry):
            def put(grp, c):
                t0 = ci * DISPATCH_CHUNK + grp * DISPATCH_GROUP
                slot = [pos_ref[k * n + t0 + j] for j in range(DISPATCH_GROUP)]
                for j in range(DISPATCH_GROUP):
                    pltpu.make_async_copy(_tile(x_hbm, t0 + j), _tile(xs_hbm, slot[j]), sem.at[0]).start()
                return c
            lax.fori_loop(0, DISPATCH_CHUNK // DISPATCH_GROUP, put, 0)

            @pl.when((ci > 0) | (k > 0))
            def _():
                chunk_wait()
            return carry
        lax.fori_loop(0, chunks, chunk, 0)
    chunk_wait()

    for e in range(N_EXPERTS):
        def fill(slot, carry):
            pltpu.make_async_copy(zero_tile, _tile(xs_hbm, slot), sem.at[1]).start()
            return carry

        def fill_wait(slot, carry):
            pltpu.make_async_copy(zero_tile, _tile(xs_hbm, slot), sem.at[1]).wait()
            return carry
        lax.fori_loop(pad_ref[2 * e], pad_ref[2 * e + 1], fill, 0)
        lax.fori_loop(pad_ref[2 * e], pad_ref[2 * e + 1], fill_wait, 0)

    def block_of(blk):
        return xs_hbm.at[pl.ds(pl.multiple_of(blk * block_rows, block_rows), block_rows)]

    def fill_block(blk, carry):
        pltpu.make_async_copy(zero_ref, block_of(blk), sem.at[1]).start()
        return carry

    def fill_block_wait(blk, carry):
        pltpu.make_async_copy(zero_ref, block_of(blk), sem.at[1]).wait()
        return carry
    lax.fori_loop(nu_ref[0], n_blocks, fill_block, 0)
    lax.fori_loop(nu_ref[0], n_blocks, fill_block_wait, 0)


def _moe_dispatch(hn_tiles, pos, pad_range, n_used, n_slots):
    return pl.pallas_call(
        _moe_dispatch_kernel,
        grid_spec=pltpu.PrefetchScalarGridSpec(
            num_scalar_prefetch=3,
            grid=(1,),
            in_specs=[pl.BlockSpec(memory_space=pl.ANY)],
            out_specs=pl.BlockSpec(memory_space=pl.ANY),
            scratch_shapes=[pltpu.VMEM((MOE_BLOCK * ROW_TILES, LANES), F32),
                            pltpu.SemaphoreType.DMA((2,))],
        ),
        out_shape=jax.ShapeDtypeStruct((n_slots * ROW_TILES, LANES), F32),
        compiler_params=pltpu.CompilerParams(dimension_semantics=("arbitrary",)),
        name="moe_dispatch",
    )(pos, pad_range, n_used, hn_tiles)


def _moe_ffn_kernel(be_ref, nu_ref, x_ref, wg_ref, wu_ref, wd_ref, y_ref):
    i = pl.program_id(0)

    @pl.when(i < nu_ref[0])
    def _():
        x = _rows_from_tiles(x_ref, MOE_BLOCK).astype(BF16)
        g = _dot(x, wg_ref[0])
        h = g * jax.nn.sigmoid(g) * _dot(x, wu_ref[0])
        _rows_to_tiles(y_ref, _dot(h.astype(BF16), wd_ref[0]))

    @pl.when(i >= nu_ref[0])
    def _():
        y_ref[...] = jnp.zeros_like(y_ref)


def _moe_ffn(xs_tiles, block_expert, n_used, w_gate, w_up, w_down, first):
    n_blocks = block_expert.shape[0]
    ff = w_gate.shape[-1]
    wmap = lambda i, be, nu: (first + be[i], 0, 0)
    return pl.pallas_call(
        _moe_ffn_kernel,
        grid_spec=pltpu.PrefetchScalarGridSpec(
            num_scalar_prefetch=2,
            grid=(n_blocks,),
            in_specs=[
                pl.BlockSpec((MOE_BLOCK * ROW_TILES, LANES), lambda i, be, nu: (jnp.minimum(i, nu[0] - 1), 0)),
                pl.BlockSpec((1, D_MODEL, ff), wmap),
                pl.BlockSpec((1, D_MODEL, ff), wmap),
                pl.BlockSpec((1, ff, D_MODEL), wmap),
            ],
            out_specs=pl.BlockSpec((MOE_BLOCK * ROW_TILES, LANES), lambda i, be, nu: (i, 0)),
        ),
        out_shape=jax.ShapeDtypeStruct((n_blocks * MOE_BLOCK * ROW_TILES, LANES), F32),
        compiler_params=pltpu.CompilerParams(
            dimension_semantics=("arbitrary",), vmem_limit_bytes=VMEM_LIMIT),
        name="moe_ffn",
    )(block_expert, n_used, xs_tiles, w_gate, w_up, w_down)


def _moe_combine_kernel(pos_ref, hs_ref, rt_ref, ys_hbm, o_ref, ybuf, sem):
    i = pl.program_id(0)
    nt = pl.num_programs(0)
    slot = i % 2

    def start(t, s):
        for k in range(2):
            _tile_gather(ys_hbm, ybuf.at[s, k], sem.at[s], TM, lambda r: pos_ref[k * (nt * TM) + t * TM + r])

    @pl.when(i == 0)
    def _():
        start(0, 0)

    @pl.when(i + 1 < nt)
    def _():
        start(i + 1, 1 - slot)

    for k in range(2):
        _tile_gather_wait(ys_hbm, ybuf.at[slot, k], sem.at[slot], TM)
    rt = rt_ref[...]
    lane = lax.broadcasted_iota(jnp.int32, rt.shape, 1)
    g1 = jnp.sum(jnp.where(lane == 2, rt, 0.0), axis=-1, keepdims=True)
    g2 = jnp.sum(jnp.where(lane == 3, rt, 0.0), axis=-1, keepdims=True)
    o_ref[...] = (hs_ref[...] + g1 * _rows_from_tiles(ybuf.at[slot, 0], TM)
                  + g2 * _rows_from_tiles(ybuf.at[slot, 1], TM))


def _moe_combine(hs, route, ys_tiles, pos):
    n = hs.shape[0]
    return pl.pallas_call(
        _moe_combine_kernel,
        grid_spec=pltpu.PrefetchScalarGridSpec(
            num_scalar_prefetch=1,
            grid=(n // TM,),
            in_specs=[
                pl.BlockSpec((TM, D_MODEL), lambda i, p: (i, 0)),
                pl.BlockSpec((TM, LANES), lambda i, p: (i, 0)),
                pl.BlockSpec(memory_space=pl.ANY),
            ],
            out_specs=pl.BlockSpec((TM, D_MODEL), lambda i, p: (i, 0)),
            scratch_shapes=[pltpu.VMEM((2, 2, TM * ROW_TILES, LANES), F32),
                            pltpu.SemaphoreType.DMA((2,))],
        ),
        out_shape=jax.ShapeDtypeStruct((n, D_MODEL), F32),
        compiler_params=pltpu.CompilerParams(
            dimension_semantics=("arbitrary",), vmem_limit_bytes=VMEM_LIMIT),
        name="moe_combine",
    )(pos, hs, route, ys_tiles)


def _final_kernel(a_ref, b_ref, g_ref, o_ref):
    tb = a_ref.shape[1]
    o_ref[0, :tb - N_META] = _rms(a_ref[0, N_META:], g_ref[...])
    o_ref[0, tb - N_META:] = _rms(b_ref[0], g_ref[...])


def _final_norm(hs, g, bsz, seq):
    hs3 = hs.reshape(bsz, T_PAD, D_MODEL)
    return pl.pallas_call(
        _final_kernel,
        grid=(bsz, seq // TB_FINAL),
        in_specs=[pl.BlockSpec((1, TB_FINAL, D_MODEL), lambda b, i: (b, i, 0)),
                  pl.BlockSpec((1, N_META, D_MODEL), lambda b, i: (b, (i + 1) * (TB_FINAL // N_META), 0)),
                  _const_spec((1, D_MODEL), lambda b, i: (0, 0))],
        out_specs=pl.BlockSpec((1, TB_FINAL, D_MODEL), lambda b, i: (b, i, 0)),
        out_shape=jax.ShapeDtypeStruct((bsz, seq, D_MODEL), F32),
        compiler_params=pltpu.CompilerParams(
            dimension_semantics=("arbitrary", "arbitrary"), vmem_limit_bytes=VMEM_LIMIT),
        name="final_norm",
    )(hs3, hs3, g)


def _head_blockdiag(w):
    eye = jnp.eye(LRU_HEADS, dtype=w.dtype)
    out = jnp.einsum('lnhk,nm->lnhmk', w, eye)
    return out.reshape(w.shape[0], LRU_WIDTH, LRU_WIDTH)


def kernel(x, meta_tokens, mix_norm, w_in, merge_bias, s5_lambda_re, s5_lambda_im, s5_log_dt, s5_b_re, s5_b_im, s5_c_re, s5_c_im, s5_d, s5_w_glu, s5_b_glu, s5_w_proj, lru_conv_w, lru_conv_b, lru_w_rgate, lru_b_rgate, lru_w_igate, lru_b_igate, lru_lambda, lru_w_proj, w_out, ffn_norm, dense_w_gate, dense_w_up, dense_w_down, router_w, router_b, moe_w_gate, moe_w_up, moe_w_down, final_norm):
    bsz, seq, d = x.shape
    depth = w_in.shape[0]
    assert d == D_MODEL and N_META + seq <= T_PAD
    n = bsz * T_PAD
    assert n % TM == 0 and n % TM_ROUTER == 0 and n % TM_FFN == 0 and n % DISPATCH_CHUNK == 0 and seq % TB_FINAL == 0 and TB_FINAL % N_META == 0

    meta = jnp.broadcast_to(meta_tokens[None].astype(x.dtype), (bsz, N_META, d))
    pad = jnp.zeros((bsz, T_PAD - N_META - seq, d), x.dtype)
    hs = jnp.concatenate([meta, x, pad], axis=1).reshape(n, d)

    row3 = lambda a: a[:, None, :]
    w_in_b = w_in.astype(BF16)
    w_glu_b = s5_w_glu.astype(BF16)
    w_sp_b = s5_w_proj.astype(BF16)
    w_lp_b = lru_w_proj.astype(BF16)
    w_out_b = w_out.astype(BF16)
    w_ri = jnp.concatenate([_head_blockdiag(lru_w_rgate), _head_blockdiag(lru_w_igate)], axis=-1).astype(BF16)
    b_ri = jnp.concatenate([lru_b_rgate, lru_b_igate], axis=-1)
    neg_sp = -LRU_C * jax.nn.softplus(-lru_lambda)
    dense = (dense_w_gate, dense_w_up, dense_w_down)
    n_moe = router_w.shape[0]
    moe = [w.astype(BF16).reshape((n_moe * N_EXPERTS,) + w.shape[2:]) for w in (moe_w_gate, moe_w_up, moe_w_down)]
    s5_ops = jax.vmap(_s5_prep)(s5_lambda_re, s5_lambda_im, s5_log_dt, s5_b_re, s5_b_im, s5_c_re, s5_c_im, s5_d)
    rw_pad = jnp.pad(router_w, ((0, 0), (0, 0), (0, LANES - N_EXPERTS)))
    rb_pad = jnp.pad(router_b, ((0, 0), (0, LANES - N_EXPERTS)), constant_values=MASKED_LOGIT)

    for layer in range(depth):
        u_parts, x_lru, g_lru, gates = _in_proj(hs, row3(mix_norm), w_in_b, row3(merge_bias), layer)
        ys_parts = _s5_scan(u_parts, s5_ops, layer, bsz)
        y_lru = _lru(x_lru, g_lru, lru_conv_w, row3(lru_conv_b), w_ri, row3(b_ri), row3(neg_sp), layer, bsz)
        j = layer // 2
        router = (rw_pad, row3(rb_pad), j) if layer % 2 == 1 else None
        res = _merge(hs, ys_parts, y_lru, gates, w_glu_b, row3(s5_b_glu), w_sp_b, w_lp_b, w_out_b,
                     row3(ffn_norm), layer, router)
        if layer % 2 == 0:
            hs, hn = res
            hs = _ffn(hn, hs, *dense, layer=j)
        else:
            hs, hn, route, route_t, counts = res
            pos, block_expert, n_used, pad_range = _moe_plan(route_t, counts, n)
            xs = _moe_dispatch(hn, pos, pad_range, n_used, block_expert.shape[0] * MOE_BLOCK)
            ys = _moe_ffn(xs, block_expert, n_used, *moe, first=j * N_EXPERTS)
            hs = _moe_combine(hs, route, ys, pos)

    return _final_norm(hs, final_norm[None, :], bsz, seq)
```

```python
import functools

import jax
import jax.numpy as jnp
from jax import lax
from jax.experimental import pallas as pl
from jax.experimental.pallas import tpu as pltpu

F32 = jnp.float32
BF16 = jnp.bfloat16

D_MODEL = 1024
N_META = 16
S5_WIDTH = 512
S5_GROUP = 16
S5_GROUPS = 32
S5_STATE = 64
LRU_WIDTH = 512
LRU_HEADS = 8
LRU_HEAD_DIM = 64
CONV_WIDTH = 4
LRU_C = 8.0
N_EXPERTS = 8
EPS = 1e-6

FOLD = 8
S5_PARTS = 4
PART_W = S5_WIDTH // S5_PARTS
PART_GROUPS = PART_W // S5_GROUP
PART_STATE = PART_GROUPS * S5_STATE
FOLD_W = FOLD * PART_W

T_PAD = 8256
ROWS = T_PAD // FOLD
TM = 688
TM_ROUTER = 384
TM_FFN = 1376
FF_CHUNK = 512
MOE_BLOCK = 512
GATHER_GROUP = 8
LRU_CHUNK = 1032
LRU_UNROLL = 3
TB_FINAL = 512
VMEM_LIMIT = 56 * 1024 * 1024
LANES = 128
SUBLANES = 8
ROW_TILES = D_MODEL // LANES
MASKED_LOGIT = float("-inf")


def _dot(a, b):
    return jnp.dot(a, b, preferred_element_type=F32)


def _const_spec(block_shape, index_map):
    return pl.BlockSpec(block_shape, index_map, pipeline_mode=pl.Buffered(1))


def _rms(x, g):
    ms = jnp.mean(x * x, axis=-1, keepdims=True)
    return x * lax.rsqrt(ms + EPS) * g


def _rows_to_tiles(ref, x):
    rows = x.shape[0]
    for s in range(ROW_TILES):
        ref[pl.ds(s, rows, stride=ROW_TILES), :] = x[:, s * LANES:(s + 1) * LANES]


def _rows_from_tiles(ref, rows):
    return jnp.concatenate([ref[pl.ds(s, rows, stride=ROW_TILES), :] for s in range(ROW_TILES)], axis=-1)


def _in_proj_kernel(hs_ref, g_ref, w_ref, mb_ref, u_ref, xl_ref, gl_ref, gt_ref):
    hn = _rms(hs_ref[...], g_ref[0]).astype(BF16)
    u = _dot(hn, w_ref[0, :, 0:S5_WIDTH])
    for q in range(S5_PARTS):
        u_ref[q] = u[:, q * PART_W:(q + 1) * PART_W]
    o_x = S5_WIDTH
    o_g = o_x + LRU_WIDTH
    o_m = o_g + LRU_WIDTH
    xl_ref[...] = _dot(hn, w_ref[0, :, o_x:o_g]).astype(BF16)
    gl_ref[...] = _dot(hn, w_ref[0, :, o_g:o_m]).astype(BF16)
    z = _dot(hn, w_ref[0, :, o_m:]) + mb_ref[0]
    gt_ref[...] = jax.nn.sigmoid(z).astype(BF16)


def _in_proj(hs, mix_norm, w_in, merge_bias, layer):
    n = hs.shape[0]
    d_in = w_in.shape[-1]
    lay = lambda i: (layer, 0, 0)
    return pl.pallas_call(
        _in_proj_kernel,
        grid=(n // TM,),
        in_specs=[
            pl.BlockSpec((TM, D_MODEL), lambda i: (i, 0)),
            _const_spec((1, 1, D_MODEL), lay),
            _const_spec((1, D_MODEL, d_in), lay),
            _const_spec((1, 1, 2 * D_MODEL), lay),
        ],
        out_specs=[
            pl.BlockSpec((S5_PARTS, TM, PART_W), lambda i: (0, i, 0)),
            pl.BlockSpec((TM, LRU_WIDTH), lambda i: (i, 0)),
            pl.BlockSpec((TM, LRU_WIDTH), lambda i: (i, 0)),
            pl.BlockSpec((TM, 2 * D_MODEL), lambda i: (i, 0)),
        ],
        out_shape=[
            jax.ShapeDtypeStruct((S5_PARTS, n, PART_W), F32),
            jax.ShapeDtypeStruct((n, LRU_WIDTH), BF16),
            jax.ShapeDtypeStruct((n, LRU_WIDTH), BF16),
            jax.ShapeDtypeStruct((n, 2 * D_MODEL), BF16),
        ],
        compiler_params=pltpu.CompilerParams(
            dimension_semantics=("arbitrary",), vmem_limit_bytes=VMEM_LIMIT),
        name="in_proj",
    )(hs, mix_norm, w_in, merge_bias)


def _s5_prep(lam_re, lam_im, log_dt, b_re, b_im, c_re, c_im, d_skip):
    dt = jnp.exp(log_dt)[:, None]
    mag = jnp.exp(lam_re * dt)
    a_re = mag * jnp.cos(lam_im * dt)
    a_im = mag * jnp.sin(lam_im * dt)
    den = lam_re * lam_re + lam_im * lam_im
    num_re = a_re - 1.0
    coef_re = (num_re * lam_re + a_im * lam_im) / den
    coef_im = (a_im * lam_re - num_re * lam_im) / den
    bb_re = coef_re[..., None] * b_re - coef_im[..., None] * b_im
    bb_im = coef_re[..., None] * b_im + coef_im[..., None] * b_re

    def cmul(xr, xi, yr, yi):
        return xr * yr - xi * yi, xr * yi + xi * yr

    def powers(br, bi, n):
        pr, pi = [jnp.ones_like(br)], [jnp.zeros_like(bi)]
        for _ in range(n):
            r, i = cmul(pr[-1], pi[-1], br, bi)
            pr.append(r)
            pi.append(i)
        return jnp.stack(pr), jnp.stack(pi)

    p_re, p_im = powers(a_re, a_im, FOLD)
    q_re, q_im = powers(p_re[FOLD], p_im[FOLD], FOLD)

    def per_part(x):
        lead = x.shape[:-3]
        xp = x.reshape(lead + (S5_PARTS, PART_GROUPS) + x.shape[-2:])
        return jnp.moveaxis(xp, len(lead), 0)

    rev_re = jnp.stack([p_re[FOLD - 1 - j] for j in range(FOLD)])
    rev_im = jnp.stack([p_im[FOLD - 1 - j] for j in range(FOLD)])
    wr, wi = cmul(rev_re[..., None], rev_im[..., None], bb_re[None], bb_im[None])
    w_ri = jnp.swapaxes(jnp.stack([wr, wi], axis=1), -1, -2)
    xq = jnp.transpose(per_part(w_ri), (0, 1, 3, 4, 2, 5)).reshape(S5_PARTS, FOLD_W, 2 * S5_STATE)

    ca_re, ca_im = cmul(c_re[None], c_im[None], p_re[:, :, None, :], p_im[:, :, None, :])
    bt_re = jnp.swapaxes(bb_re, -1, -2)[None, :, :, None, :]
    bt_im = jnp.swapaxes(bb_im, -1, -2)[None, :, :, None, :]
    taps = jnp.sum(ca_re[:FOLD, :, None] * bt_re - ca_im[:FOLD, :, None] * bt_im, axis=-1)
    skip = d_skip.reshape(S5_GROUPS, S5_GROUP)
    taps = taps.at[0].add(skip[:, :, None] * jnp.eye(S5_GROUP, dtype=F32)[None])
    rc = jnp.transpose(per_part(taps), (0, 2, 3, 1, 4)).reshape(S5_PARTS, PART_W, FOLD * S5_GROUP)

    v_ri = jnp.swapaxes(jnp.stack([ca_re[1:], -ca_im[1:]], axis=0), -1, -2)
    vc = jnp.transpose(per_part(v_ri), (0, 1, 3, 4, 2, 5)).reshape(S5_PARTS, 2 * PART_STATE, FOLD * S5_GROUP)

    def part_vec(x):
        lead = x.shape[:-2]
        xp = x.reshape(lead + (S5_PARTS, PART_STATE))
        return jnp.moveaxis(xp, -2, 0)

    lvl = jnp.stack([jnp.stack([part_vec(q_re[k]), part_vec(q_im[k])], axis=1) for k in (1, 2, 4)], axis=1)
    lvl = lvl[:, :, :, None, :]
    rowpow = jnp.stack([part_vec(q_re[1:]), part_vec(q_im[1:])], axis=1)
    return xq, rc, vc, lvl, rowpow


def _iota2(shape):
    return (lax.broadcasted_iota(jnp.int32, shape, 0), lax.broadcasted_iota(jnp.int32, shape, 1))


def _s5_expand(xq, rc, vc, w1_s, tv_s):
    ps = PART_STATE
    lg_state, lg_group, lg_part = (v.bit_length() - 1 for v in (S5_STATE, S5_GROUP, PART_W))
    lg_pg = PART_GROUPS.bit_length() - 1
    grp = PART_GROUPS - 1
    one_hot = lambda m: jnp.where(m, 1.0, 0.0).astype(BF16)
    r, c = _iota2((2 * S5_STATE, 2 * ps))
    e1 = one_hot(((r >> lg_state) == (c >> (lg_state + lg_pg))) & ((r & (S5_STATE - 1)) == (c & (S5_STATE - 1))))
    r, c = _iota2((FOLD * S5_GROUP, FOLD_W))
    e2 = one_hot(((r >> lg_group) == (c >> lg_part)) & ((r & (S5_GROUP - 1)) == (c & (S5_GROUP - 1))))
    r, c = _iota2((FOLD_W, 2 * ps))
    m1 = ((r >> lg_group) & grp) == ((c >> lg_state) & grp)
    w1_s[...] = jnp.where(m1, _dot(xq.astype(BF16), e1), 0.0).astype(BF16)
    r, c = _iota2((PART_W, FOLD_W))
    m2 = (r >> lg_group) == ((c >> lg_group) & grp)
    r0 = jnp.where(m2, _dot(rc.astype(BF16), e2), 0.0).astype(BF16)
    for j in range(FOLD):
        if j == 0:
            blk = r0
        else:
            blk = jnp.concatenate([jnp.zeros((PART_W, j * PART_W), BF16), r0[:, :FOLD_W - j * PART_W]], axis=1)
        tv_s[j * PART_W:(j + 1) * PART_W, :] = blk
    r, c = _iota2((2 * ps, FOLD_W))
    m3 = ((r >> lg_state) & grp) == ((c >> lg_group) & grp)
    tv_s[FOLD_W:, :] = jnp.where(m3, _dot(vc.astype(BF16), e2), 0.0).astype(BF16)


def _s5_kernel(u_ref, xq_ref, rc_ref, vc_ref, lvl_ref, rp_ref, y_ref, w1_s, tv_s, st_ref):
    ps = PART_STATE

    @pl.when(pl.program_id(1) == 0)
    def _():
        _s5_expand(xq_ref[0, 0], rc_ref[0, 0], vc_ref[0, 0], w1_s, tv_s)

    u = jnp.concatenate([u_ref[0, 0, pl.ds(j, ROWS, stride=FOLD), :] for j in range(FOLD)],
                        axis=-1).astype(BF16)
    f = _dot(u, w1_s[...])
    fr = f[:, :ps]
    fi = f[:, ps:]
    row = lax.broadcasted_iota(jnp.int32, (ROWS, ps), 0) & (SUBLANES - 1)
    for lv, k in enumerate((1, 2, 4)):
        ar = lvl_ref[0, 0, lv, 0]
        ai = lvl_ref[0, 0, lv, 1]
        sr = pltpu.roll(fr, k, axis=0)
        si = pltpu.roll(fi, k, axis=0)
        m = row >= k
        fr, fi = (fr + jnp.where(m, ar * sr - ai * si, 0.0),
                  fi + jnp.where(m, ar * si + ai * sr, 0.0))
    st_ref[0:SUBLANES, :] = jnp.zeros((SUBLANES, 2 * ps), F32)
    st_ref[SUBLANES:, :ps] = fr
    st_ref[SUBLANES:, ps:] = fi
    pr = rp_ref[0, 0, 0]
    pi = rp_ref[0, 0, 1]

    def body(i, carry):
        cr, ci = carry
        r = pl.multiple_of(SUBLANES + i * SUBLANES, SUBLANES)
        xr = st_ref[pl.ds(r, SUBLANES), :ps]
        xi = st_ref[pl.ds(r, SUBLANES), ps:]
        hr = xr + pr * cr - pi * ci
        hi = xi + pr * ci + pi * cr
        st_ref[pl.ds(r, SUBLANES), :ps] = hr
        st_ref[pl.ds(r, SUBLANES), ps:] = hi
        return hr[SUBLANES - 1:SUBLANES], hi[SUBLANES - 1:SUBLANES]

    zero = jnp.zeros((1, ps), F32)
    lax.fori_loop(0, ROWS // SUBLANES, body, (zero, zero))
    h_prev = st_ref[pl.ds(SUBLANES - 1, ROWS), :].astype(BF16)
    y = jax.nn.gelu(_dot(u, tv_s[:FOLD_W, :]) + _dot(h_prev, tv_s[FOLD_W:, :]))
    for j in range(FOLD):
        y_ref[0, 0, pl.ds(j, ROWS, stride=FOLD), :] = y[:, j * PART_W:(j + 1) * PART_W]


def _s5_scan(u_parts, ops, layer, bsz):
    xq, rc, vc, lvl, rowpow = ops
    n = u_parts.shape[1]
    u4 = u_parts.reshape(S5_PARTS, bsz, T_PAD, PART_W)
    lay4 = lambda q, b: (layer, q, 0, 0)
    y4 = pl.pallas_call(
        _s5_kernel,
        grid=(S5_PARTS, bsz),
        in_specs=[
            pl.BlockSpec((1, 1, T_PAD, PART_W), lambda q, b: (q, b, 0, 0)),
            pl.BlockSpec((1, 1, FOLD_W, 2 * S5_STATE), lay4),
            pl.BlockSpec((1, 1, PART_W, FOLD * S5_GROUP), lay4),
            pl.BlockSpec((1, 1, 2 * PART_STATE, FOLD * S5_GROUP), lay4),
            pl.BlockSpec((1, 1, 3, 2, 1, PART_STATE), lambda q, b: (layer, q, 0, 0, 0, 0)),
            pl.BlockSpec((1, 1, 2, FOLD, PART_STATE), lambda q, b: (layer, q, 0, 0, 0)),
        ],
        out_specs=pl.BlockSpec((1, 1, T_PAD, PART_W), lambda q, b: (q, b, 0, 0)),
        out_shape=jax.ShapeDtypeStruct((S5_PARTS, bsz, T_PAD, PART_W), F32),
        scratch_shapes=[
            pltpu.VMEM((FOLD_W, 2 * PART_STATE), BF16),
            pltpu.VMEM((FOLD_W + 2 * PART_STATE, FOLD_W), BF16),
            pltpu.VMEM((ROWS + SUBLANES, 2 * PART_STATE), F32),
        ],
        compiler_params=pltpu.CompilerParams(
            dimension_semantics=("arbitrary", "arbitrary"), vmem_limit_bytes=VMEM_LIMIT),
        name="s5_scan",
    )(u4, xq, rc, vc, lvl, rowpow)
    return y4.reshape(S5_PARTS, n, PART_W)


def _lru_kernel(x_ref, g_ref, cw_ref, cb_ref, wri_ref, bri_ref, nsp_ref, o_ref,
                xs_ref, gs_ref, xc_ref, gp_ref, z_ref, a_ref, b_ref, os_ref, h_ref):
    tc = LRU_CHUNK
    c = LRU_WIDTH
    seg = tc // SUBLANES
    nq = c // LANES
    lanes = lambda q: slice(q * LANES, (q + 1) * LANES)
    halo = SUBLANES

    @pl.when(pl.program_id(1) == 0)
    def _():
        xs_ref[:, 0:halo, :] = jnp.zeros((nq, halo, LANES), F32)
        h_ref[...] = jnp.zeros((1, c), F32)

    x = x_ref[0].astype(F32)
    g = g_ref[0].astype(F32)
    for q in range(nq):
        xs_ref[q, halo:, :] = x[:, lanes(q)]
        gs_ref[q] = g[:, lanes(q)]
    taps = [[cw_ref[0, k:k + 1, lanes(q)] for k in range(CONV_WIDTH)] for q in range(nq)]
    bias = [cb_ref[0, :, lanes(q)] for q in range(nq)]

    def conv_body(i, carry):
        r0 = pl.multiple_of(i * SUBLANES, SUBLANES)
        for q in range(nq):
            acc = bias[q]
            for k in range(CONV_WIDTH):
                first = halo - (CONV_WIDTH - 1) + k + i
                acc = acc + taps[q][k] * xs_ref[q, pl.ds(first, SUBLANES, stride=seg), :]
            xc_ref[pl.ds(r0, SUBLANES), lanes(q)] = acc
            gp_ref[pl.ds(r0, SUBLANES), lanes(q)] = jax.nn.gelu(gs_ref[q, pl.ds(i, SUBLANES, stride=seg), :])
        return carry

    lax.fori_loop(0, seg, conv_body, 0, unroll=LRU_UNROLL)
    for q in range(nq):
        xs_ref[q, 0:halo, :] = xs_ref[q, tc:tc + halo, :]

    z_ref[...] = _dot(xc_ref[...].astype(BF16), wri_ref[0])
    b_r = jnp.broadcast_to(bri_ref[0, :, :c], (SUBLANES, c))
    b_i = jnp.broadcast_to(bri_ref[0, :, c:], (SUBLANES, c))
    nsp = jnp.broadcast_to(nsp_ref[0], (SUBLANES, c))

    def scan_body(i, carry):
        h, p = carry
        r0 = pl.multiple_of(i * SUBLANES, SUBLANES)
        a = jnp.exp(jax.nn.sigmoid(z_ref[pl.ds(r0, SUBLANES), :c] + b_r) * nsp)
        gated = jax.nn.sigmoid(z_ref[pl.ds(r0, SUBLANES), c:] + b_i) * xc_ref[pl.ds(r0, SUBLANES), :]
        h = a * h + jnp.sqrt(1.0 - a * a) * gated
        p = p * a
        b_ref[pl.ds(r0, SUBLANES), :] = h
        a_ref[pl.ds(r0, SUBLANES), :] = p
        return h, p

    h_end, p_end = lax.fori_loop(0, seg, scan_body, (jnp.zeros((SUBLANES, c), F32), jnp.ones((SUBLANES, c), F32)),
                                 unroll=LRU_UNROLL)
    row = lax.broadcasted_iota(jnp.int32, (SUBLANES, c), 0)
    enter = jnp.where(row == 0, h_ref[...], 0.0)
    for sgm in range(SUBLANES - 1):
        leave = h_end + p_end * enter
        enter = enter + jnp.where(row == sgm + 1, pltpu.roll(leave, 1, axis=0), 0.0)
    h_ref[...] = (h_end + p_end * enter)[SUBLANES - 1:SUBLANES]

    def out_body(i, carry):
        r0 = pl.multiple_of(i * SUBLANES, SUBLANES)
        h = b_ref[pl.ds(r0, SUBLANES), :] + a_ref[pl.ds(r0, SUBLANES), :] * enter
        y = h * gp_ref[pl.ds(r0, SUBLANES), :]
        for q in range(nq):
            os_ref[q, pl.ds(i, SUBLANES, stride=seg), :] = y[:, lanes(q)]
        return carry

    lax.fori_loop(0, seg, out_body, 0)
    o_ref[0] = jnp.concatenate([os_ref[q] for q in range(nq)], axis=-1).astype(BF16)


def _lru(x_lru, g_lru, conv_w, conv_b, w_ri, b_ri, neg_sp, layer, bsz):
    n = x_lru.shape[0]
    c = LRU_WIDTH
    x3 = x_lru.reshape(bsz, T_PAD, c)
    g3 = g_lru.reshape(bsz, T_PAD, c)
    lay = lambda b, t: (layer, 0, 0)
    out = pl.pallas_call(
        _lru_kernel,
        grid=(bsz, T_PAD // LRU_CHUNK),
        in_specs=[
            pl.BlockSpec((1, LRU_CHUNK, c), lambda b, t: (b, t, 0)),
            pl.BlockSpec((1, LRU_CHUNK, c), lambda b, t: (b, t, 0)),
            _const_spec((1, CONV_WIDTH, c), lay),
            _const_spec((1, 1, c), lay),
            _const_spec((1, c, 2 * c), lay),
            _const_spec((1, 1, 2 * c), lay),
            _const_spec((1, 1, c), lay),
        ],
        out_specs=pl.BlockSpec((1, LRU_CHUNK, c), lambda b, t: (b, t, 0)),
        out_shape=jax.ShapeDtypeStruct((bsz, T_PAD, c), BF16),
        scratch_shapes=[
            pltpu.VMEM((c // LANES, LRU_CHUNK + SUBLANES, LANES), F32),
            pltpu.VMEM((c // LANES, LRU_CHUNK, LANES), F32),
            pltpu.VMEM((LRU_CHUNK, c), F32),
            pltpu.VMEM((LRU_CHUNK, c), F32),
            pltpu.VMEM((LRU_CHUNK, 2 * c), F32),
            pltpu.VMEM((LRU_CHUNK, c), F32),
            pltpu.VMEM((LRU_CHUNK, c), F32),
            pltpu.VMEM((c // LANES, LRU_CHUNK, LANES), F32),
            pltpu.VMEM((1, c), F32),
        ],
        compiler_params=pltpu.CompilerParams(
            dimension_semantics=("arbitrary", "arbitrary"), vmem_limit_bytes=VMEM_LIMIT),
        name="rglru",
    )(x3, g3, conv_w, conv_b, w_ri, b_ri, neg_sp)
    return out.reshape(n, c)


def _merge_kernel(hs_ref, ys_ref, yl_ref, gt_ref, wglu_ref, bglu_ref, wsp_ref, wlp_ref, wout_ref, g_ref,
                  *rest, with_router):
    if with_router:
        rw_ref, rb_ref, hs_out_ref, hn_ref, rt_ref, rtt_ref, cnt_ref, run_ref, tri_ref = rest
    else:
        hs_out_ref, hn_ref = rest
    ys = jnp.concatenate([ys_ref[q] for q in range(S5_PARTS)], axis=-1)
    glu = ys * jax.nn.sigmoid(_dot(ys.astype(BF16), wglu_ref[0]) + bglu_ref[0])
    y_a = _dot(glu.astype(BF16), wsp_ref[0])
    y_b = _dot(yl_ref[...], wlp_ref[0])
    y = gt_ref[:, :D_MODEL].astype(F32) * y_a + gt_ref[:, D_MODEL:].astype(F32) * y_b
    hs = hs_ref[...] + _dot(y.astype(BF16), wout_ref[0])
    hs_out_ref[...] = hs
    hn = _rms(hs, g_ref[0])
    if not with_router:
        hn_ref[...] = hn.astype(BF16)
    else:
        _rows_to_tiles(hn_ref, hn)
        logits = _dot(hn.astype(BF16), rw_ref[0].astype(BF16)) + rb_ref[0]
        lane = lax.broadcasted_iota(jnp.int32, logits.shape, 1).astype(F32)
        m1 = jnp.max(logits, axis=-1, keepdims=True)
        i1 = jnp.min(jnp.where(logits == m1, lane, float(LANES)), axis=-1, keepdims=True)
        rest_l = jnp.where(lane == i1, MASKED_LOGIT, logits)
        m2 = jnp.max(rest_l, axis=-1, keepdims=True)
        i2 = jnp.min(jnp.where(rest_l == m2, lane, float(LANES)), axis=-1, keepdims=True)
        e2 = jnp.exp(m2 - m1)
        g1 = 1.0 / (1.0 + e2)
        g2 = e2 / (1.0 + e2)
        @pl.when(pl.program_id(0) == 0)
        def _():
            run_ref[...] = jnp.zeros_like(run_ref)
            r, c = _iota2(tri_ref.shape)
            tri_ref[...] = jnp.where(c < r, 1.0, 0.0).astype(BF16)

        first = lane == i1
        second = lane == i2
        picked = jnp.where(first | second, 1.0, 0.0)
        before = _dot(tri_ref[...], picked.astype(BF16)) + run_ref[...]
        rank1 = jnp.sum(jnp.where(first, before, 0.0), axis=-1, keepdims=True)
        rank2 = jnp.sum(jnp.where(second, before, 0.0), axis=-1, keepdims=True)
        run_ref[...] += jnp.sum(picked, axis=0, keepdims=True)
        cnt_ref[...] = jnp.broadcast_to(run_ref[...], cnt_ref.shape)
        rt = (jnp.where(lane == 0.0, i1, 0.0) + jnp.where(lane == 1.0, i2, 0.0)
              + jnp.where(lane == 2.0, g1, 0.0) + jnp.where(lane == 3.0, g2, 0.0)
              + jnp.where(lane == 4.0, rank1, 0.0) + jnp.where(lane == 5.0, rank2, 0.0))
        rt_ref[...] = rt
        rtt_ref[0] = jnp.transpose(rt)[:SUBLANES, :]


def _merge(hs, ys_parts, y_lru, gates, w_glu, b_glu, w_sp, w_lp, w_out, ffn_norm, layer, router=None):
    n = hs.shape[0]
    tm = TM if router is None else TM_ROUTER
    lay = lambda i: (layer, 0, 0)
    in_specs = [
        pl.BlockSpec((tm, D_MODEL), lambda i: (i, 0)),
        pl.BlockSpec((S5_PARTS, tm, PART_W), lambda i: (0, i, 0)),
        pl.BlockSpec((tm, LRU_WIDTH), lambda i: (i, 0)),
        pl.BlockSpec((tm, 2 * D_MODEL), lambda i: (i, 0)),
        _const_spec((1, S5_WIDTH, S5_WIDTH), lay),
        _const_spec((1, 1, S5_WIDTH), lay),
        _const_spec((1, S5_WIDTH, D_MODEL), lay),
        _const_spec((1, LRU_WIDTH, D_MODEL), lay),
        _const_spec((1, D_MODEL, D_MODEL), lay),
        _const_spec((1, 1, D_MODEL), lay),
    ]
    out_specs = [pl.BlockSpec((tm, D_MODEL), lambda i: (i, 0))]
    out_shape = [jax.ShapeDtypeStruct((n, D_MODEL), F32)]
    if router is None:
        out_specs.append(pl.BlockSpec((tm, D_MODEL), lambda i: (i, 0)))
        out_shape.append(jax.ShapeDtypeStruct((n, D_MODEL), BF16))
    else:
        out_specs.append(pl.BlockSpec((tm * ROW_TILES, LANES), lambda i: (i, 0)))
        out_shape.append(jax.ShapeDtypeStruct((n * ROW_TILES, LANES), F32))
    args = [hs, ys_parts, y_lru, gates, w_glu, b_glu, w_sp, w_lp, w_out, ffn_norm]
    if router is not None:
        rw, rb, j = router
        in_specs += [_const_spec((1, D_MODEL, LANES), lambda i: (j, 0, 0)),
                     _const_spec((1, 1, LANES), lambda i: (j, 0, 0))]
        out_specs += [pl.BlockSpec((tm, LANES), lambda i: (i, 0)),
                      pl.BlockSpec((1, SUBLANES, tm), lambda i: (i, 0, 0)),
                      pl.BlockSpec((SUBLANES, LANES), lambda i: (0, 0))]
        out_shape += [jax.ShapeDtypeStruct((n, LANES), F32),
                      jax.ShapeDtypeStruct((n // tm, SUBLANES, tm), F32),
                      jax.ShapeDtypeStruct((SUBLANES, LANES), F32)]
        args += [rw, rb]
    return pl.pallas_call(
        functools.partial(_merge_kernel, with_router=router is not None),
        grid=(n // tm,),
        in_specs=in_specs,
        out_specs=out_specs,
        out_shape=out_shape,
        scratch_shapes=[pltpu.VMEM((1, LANES), F32), pltpu.VMEM((tm, tm), BF16)] if router is not None else [],
        compiler_params=pltpu.CompilerParams(
            dimension_semantics=("arbitrary",), vmem_limit_bytes=VMEM_LIMIT),
        name="merge_router" if router is not None else "merge",
    )(*args)


def _ffn_kernel(x_ref, hs_ref, wg_ref, wu_ref, wd_ref, o_ref, acc_ref):
    c = pl.program_id(1)

    @pl.when(c == 0)
    def _():
        acc_ref[...] = jnp.zeros_like(acc_ref)

    x = x_ref[...]
    g = _dot(x, wg_ref[0].astype(BF16))
    h = g * jax.nn.sigmoid(g) * _dot(x, wu_ref[0].astype(BF16))
    acc_ref[...] += _dot(h.astype(BF16), wd_ref[0].astype(BF16))

    @pl.when(c == pl.num_programs(1) - 1)
    def _():
        o_ref[...] = hs_ref[...] + acc_ref[...]


def _ffn(hn, hs, w_gate, w_up, w_down, layer):
    n = hn.shape[0]
    ff = w_gate.shape[-1]
    return pl.pallas_call(
        _ffn_kernel,
        grid=(n // TM_FFN, ff // FF_CHUNK),
        in_specs=[
            pl.BlockSpec((TM_FFN, D_MODEL), lambda i, c: (i, 0)),
            pl.BlockSpec((TM_FFN, D_MODEL), lambda i, c: (i, 0)),
            pl.BlockSpec((1, D_MODEL, FF_CHUNK), lambda i, c: (layer, 0, c)),
            pl.BlockSpec((1, D_MODEL, FF_CHUNK), lambda i, c: (layer, 0, c)),
            pl.BlockSpec((1, FF_CHUNK, D_MODEL), lambda i, c: (layer, c, 0)),
        ],
        out_specs=pl.BlockSpec((TM_FFN, D_MODEL), lambda i, c: (i, 0)),
        out_shape=jax.ShapeDtypeStruct((n, D_MODEL), F32),
        scratch_shapes=[pltpu.VMEM((TM_FFN, D_MODEL), F32)],
        compiler_params=pltpu.CompilerParams(
            dimension_semantics=("arbitrary", "arbitrary"), vmem_limit_bytes=VMEM_LIMIT),
        name="dense_ffn",
    )(hn, hs, w_gate, w_up, w_down)


def _moe_plan(route_t, counts_f, n):
    n_blocks = -(-2 * n // MOE_BLOCK) + N_EXPERTS
    e = jnp.stack([route_t[:, 0, :], route_t[:, 1, :]]).astype(jnp.int32)
    rank = jnp.stack([route_t[:, 4, :], route_t[:, 5, :]]).astype(jnp.int32)
    counts = counts_f[0, :N_EXPERTS].astype(jnp.int32)
    padded = ((counts + MOE_BLOCK - 1) // MOE_BLOCK) * MOE_BLOCK
    cum_pad = jnp.cumsum(padded)
    pad_start = cum_pad - padded
    pos = rank
    for x in range(N_EXPERTS):
        pos = pos + jnp.where(e == x, pad_start[x], 0)
    block_start = jnp.arange(n_blocks, dtype=jnp.int32) * MOE_BLOCK
    block_expert = jnp.minimum(jnp.sum((block_start[:, None] >= cum_pad[None, :]).astype(jnp.int32), axis=1),
                               N_EXPERTS - 1)
    n_used = (cum_pad[-1] // MOE_BLOCK).astype(jnp.int32).reshape(1)
    pad_range = jnp.stack([pad_start + counts, cum_pad], axis=1).reshape(2 * N_EXPERTS).astype(jnp.int32)
    return pos.reshape(2 * n), block_expert, n_used, pad_range


def _tile(ref, index):
    return ref.at[pl.ds(pl.multiple_of(index * ROW_TILES, ROW_TILES), ROW_TILES)]


def _tile_gather(src_hbm, dst, sem, rows, index_of):
    def body(grp, carry):
        r0 = grp * GATHER_GROUP
        index = [index_of(r0 + j) for j in range(GATHER_GROUP)]
        for j in range(GATHER_GROUP):
            pltpu.make_async_copy(_tile(src_hbm, index[j]), _tile(dst, r0 + j), sem).start()
        return carry
    lax.fori_loop(0, rows // GATHER_GROUP, body, 0)


def _tile_gather_wait(src_hbm, dst, sem, rows):
    pltpu.make_async_copy(src_hbm.at[pl.ds(0, rows * ROW_TILES)], dst, sem).wait()


def _moe_dispatch_kernel(pos_ref, pad_ref, nu_ref, x_ref, xs_hbm, zero_ref, sem):
    t = pl.program_id(0)
    n = pl.num_programs(0) * TM
    block_rows = MOE_BLOCK * ROW_TILES
    n_blocks = xs_hbm.shape[0] // block_rows

    @pl.when(t == 0)
    def _():
        zero_ref[...] = jnp.zeros_like(zero_ref)
        zero_tile = zero_ref.at[pl.ds(0, ROW_TILES)]
        for e in range(N_EXPERTS):
            def fill(slot, carry):
                pltpu.make_async_copy(zero_tile, _tile(xs_hbm, slot), sem.at[1]).start()
                return carry

            def fill_wait(slot, carry):
                pltpu.make_async_copy(zero_tile, _tile(xs_hbm, slot), sem.at[1]).wait()
                return carry
            lax.fori_loop(pad_ref[2 * e], pad_ref[2 * e + 1], fill, 0)
            lax.fori_loop(pad_ref[2 * e], pad_ref[2 * e + 1], fill_wait, 0)

        def block_of(blk):
            return xs_hbm.at[pl.ds(pl.multiple_of(blk * block_rows, block_rows), block_rows)]

        def fill_block(blk, carry):
            pltpu.make_async_copy(zero_ref, block_of(blk), sem.at[1]).start()
            return carry

        def fill_block_wait(blk, carry):
            pltpu.make_async_copy(zero_ref, block_of(blk), sem.at[1]).wait()
            return carry
        lax.fori_loop(nu_ref[0], n_blocks, fill_block, 0)
        lax.fori_loop(nu_ref[0], n_blocks, fill_block_wait, 0)

    for k in range(2):
        def put(grp, carry):
            r0 = grp * GATHER_GROUP
            slot = [pos_ref[k * n + t * TM + r0 + j] for j in range(GATHER_GROUP)]
            for j in range(GATHER_GROUP):
                pltpu.make_async_copy(_tile(x_ref, r0 + j), _tile(xs_hbm, slot[j]), sem.at[0]).start()
            return carry
        lax.fori_loop(0, TM // GATHER_GROUP, put, 0)
    for k in range(2):
        pltpu.make_async_copy(x_ref, xs_hbm.at[pl.ds(0, TM * ROW_TILES)], sem.at[0]).wait()


def _moe_dispatch(hn_tiles, pos, pad_range, n_used, n_slots):
    n = hn_tiles.shape[0] // ROW_TILES
    return pl.pallas_call(
        _moe_dispatch_kernel,
        grid_spec=pltpu.PrefetchScalarGridSpec(
            num_scalar_prefetch=3,
            grid=(n // TM,),
            in_specs=[pl.BlockSpec((TM * ROW_TILES, LANES), lambda t, ps, pr, nu: (t, 0))],
            out_specs=pl.BlockSpec(memory_space=pl.ANY),
            scratch_shapes=[pltpu.VMEM((MOE_BLOCK * ROW_TILES, LANES), F32),
                            pltpu.SemaphoreType.DMA((2,))],
        ),
        out_shape=jax.ShapeDtypeStruct((n_slots * ROW_TILES, LANES), F32),
        compiler_params=pltpu.CompilerParams(dimension_semantics=("arbitrary",)),
        name="moe_dispatch",
    )(pos, pad_range, n_used, hn_tiles)


def _moe_ffn_kernel(be_ref, nu_ref, x_ref, wg_ref, wu_ref, wd_ref, y_ref):
    i = pl.program_id(0)

    @pl.when(i < nu_ref[0])
    def _():
        x = _rows_from_tiles(x_ref, MOE_BLOCK).astype(BF16)
        g = _dot(x, wg_ref[0])
        h = g * jax.nn.sigmoid(g) * _dot(x, wu_ref[0])
        _rows_to_tiles(y_ref, _dot(h.astype(BF16), wd_ref[0]))

    @pl.when(i >= nu_ref[0])
    def _():
        y_ref[...] = jnp.zeros_like(y_ref)


def _moe_ffn(xs_tiles, block_expert, n_used, w_gate, w_up, w_down, first):
    n_blocks = block_expert.shape[0]
    ff = w_gate.shape[-1]
    wmap = lambda i, be, nu: (first + be[i], 0, 0)
    return pl.pallas_call(
        _moe_ffn_kernel,
        grid_spec=pltpu.PrefetchScalarGridSpec(
            num_scalar_prefetch=2,
            grid=(n_blocks,),
            in_specs=[
                pl.BlockSpec((MOE_BLOCK * ROW_TILES, LANES), lambda i, be, nu: (jnp.minimum(i, nu[0] - 1), 0)),
                pl.BlockSpec((1, D_MODEL, ff), wmap),
                pl.BlockSpec((1, D_MODEL, ff), wmap),
                pl.BlockSpec((1, ff, D_MODEL), wmap),
            ],
            out_specs=pl.BlockSpec((MOE_BLOCK * ROW_TILES, LANES), lambda i, be, nu: (i, 0)),
        ),
        out_shape=jax.ShapeDtypeStruct((n_blocks * MOE_BLOCK * ROW_TILES, LANES), F32),
        compiler_params=pltpu.CompilerParams(
            dimension_semantics=("arbitrary",), vmem_limit_bytes=VMEM_LIMIT),
        name="moe_ffn",
    )(block_expert, n_used, xs_tiles, w_gate, w_up, w_down)


def _moe_combine_kernel(pos_ref, hs_ref, rt_ref, ys_hbm, o_ref, ybuf, sem):
    i = pl.program_id(0)
    nt = pl.num_programs(0)
    slot = i % 2

    def start(t, s):
        for k in range(2):
            _tile_gather(ys_hbm, ybuf.at[s, k], sem.at[s], TM, lambda r: pos_ref[k * (nt * TM) + t * TM + r])

    @pl.when(i == 0)
    def _():
        start(0, 0)

    @pl.when(i + 1 < nt)
    def _():
        start(i + 1, 1 - slot)

    for k in range(2):
        _tile_gather_wait(ys_hbm, ybuf.at[slot, k], sem.at[slot], TM)
    rt = rt_ref[...]
    lane = lax.broadcasted_iota(jnp.int32, rt.shape, 1)
    g1 = jnp.sum(jnp.where(lane == 2, rt, 0.0), axis=-1, keepdims=True)
    g2 = jnp.sum(jnp.where(lane == 3, rt, 0.0), axis=-1, keepdims=True)
    o_ref[...] = (hs_ref[...] + g1 * _rows_from_tiles(ybuf.at[slot, 0], TM)
                  + g2 * _rows_from_tiles(ybuf.at[slot, 1], TM))


def _moe_combine(hs, route, ys_tiles, pos):
    n = hs.shape[0]
    return pl.pallas_call(
        _moe_combine_kernel,
        grid_spec=pltpu.PrefetchScalarGridSpec(
            num_scalar_prefetch=1,
            grid=(n // TM,),
            in_specs=[
                pl.BlockSpec((TM, D_MODEL), lambda i, p: (i, 0)),
                pl.BlockSpec((TM, LANES), lambda i, p: (i, 0)),
                pl.BlockSpec(memory_space=pl.ANY),
            ],
            out_specs=pl.BlockSpec((TM, D_MODEL), lambda i, p: (i, 0)),
            scratch_shapes=[pltpu.VMEM((2, 2, TM * ROW_TILES, LANES), F32),
                            pltpu.SemaphoreType.DMA((2,))],
        ),
        out_shape=jax.ShapeDtypeStruct((n, D_MODEL), F32),
        compiler_params=pltpu.CompilerParams(
            dimension_semantics=("arbitrary",), vmem_limit_bytes=VMEM_LIMIT),
        name="moe_combine",
    )(pos, hs, route, ys_tiles)


def _final_kernel(a_ref, b_ref, g_ref, o_ref):
    tb = a_ref.shape[1]
    o_ref[0, :tb - N_META] = _rms(a_ref[0, N_META:], g_ref[...])
    o_ref[0, tb - N_META:] = _rms(b_ref[0], g_ref[...])


def _final_norm(hs, g, bsz, seq):
    hs3 = hs.reshape(bsz, T_PAD, D_MODEL)
    return pl.pallas_call(
        _final_kernel,
        grid=(bsz, seq // TB_FINAL),
        in_specs=[pl.BlockSpec((1, TB_FINAL, D_MODEL), lambda b, i: (b, i, 0)),
                  pl.BlockSpec((1, N_META, D_MODEL), lambda b, i: (b, (i + 1) * (TB_FINAL // N_META), 0)),
                  _const_spec((1, D_MODEL), lambda b, i: (0, 0))],
        out_specs=pl.BlockSpec((1, TB_FINAL, D_MODEL), lambda b, i: (b, i, 0)),
        out_shape=jax.ShapeDtypeStruct((bsz, seq, D_MODEL), F32),
        compiler_params=pltpu.CompilerParams(
            dimension_semantics=("arbitrary", "arbitrary"), vmem_limit_bytes=VMEM_LIMIT),
        name="final_norm",
    )(hs3, hs3, g)


def _head_blockdiag(w):
    eye = jnp.eye(LRU_HEADS, dtype=w.dtype)
    out = jnp.einsum('lnhk,nm->lnhmk', w, eye)
    return out.reshape(w.shape[0], LRU_WIDTH, LRU_WIDTH)


def kernel(x, meta_tokens, mix_norm, w_in, merge_bias, s5_lambda_re, s5_lambda_im, s5_log_dt, s5_b_re, s5_b_im, s5_c_re, s5_c_im, s5_d, s5_w_glu, s5_b_glu, s5_w_proj, lru_conv_w, lru_conv_b, lru_w_rgate, lru_b_rgate, lru_w_igate, lru_b_igate, lru_lambda, lru_w_proj, w_out, ffn_norm, dense_w_gate, dense_w_up, dense_w_down, router_w, router_b, moe_w_gate, moe_w_up, moe_w_down, final_norm):
    bsz, seq, d = x.shape
    depth = w_in.shape[0]
    assert d == D_MODEL and N_META + seq <= T_PAD
    n = bsz * T_PAD
    assert n % TM == 0 and n % TM_ROUTER == 0 and n % TM_FFN == 0 and seq % TB_FINAL == 0 and TB_FINAL % N_META == 0

    meta = jnp.broadcast_to(meta_tokens[None].astype(x.dtype), (bsz, N_META, d))
    pad = jnp.zeros((bsz, T_PAD - N_META - seq, d), x.dtype)
    hs = jnp.concatenate([meta, x, pad], axis=1).reshape(n, d)

    row3 = lambda a: a[:, None, :]
    w_in_b = w_in.astype(BF16)
    w_glu_b = s5_w_glu.astype(BF16)
    w_sp_b = s5_w_proj.astype(BF16)
    w_lp_b = lru_w_proj.astype(BF16)
    w_out_b = w_out.astype(BF16)
    w_ri = jnp.concatenate([_head_blockdiag(lru_w_rgate), _head_blockdiag(lru_w_igate)], axis=-1).astype(BF16)
    b_ri = jnp.concatenate([lru_b_rgate, lru_b_igate], axis=-1)
    neg_sp = -LRU_C * jax.nn.softplus(-lru_lambda)
    dense = (dense_w_gate, dense_w_up, dense_w_down)
    n_moe = router_w.shape[0]
    moe = [w.astype(BF16).reshape((n_moe * N_EXPERTS,) + w.shape[2:]) for w in (moe_w_gate, moe_w_up, moe_w_down)]
    s5_ops = jax.vmap(_s5_prep)(s5_lambda_re, s5_lambda_im, s5_log_dt, s5_b_re, s5_b_im, s5_c_re, s5_c_im, s5_d)
    rw_pad = jnp.pad(router_w, ((0, 0), (0, 0), (0, LANES - N_EXPERTS)))
    rb_pad = jnp.pad(router_b, ((0, 0), (0, LANES - N_EXPERTS)), constant_values=MASKED_LOGIT)

    for layer in range(depth):
        u_parts, x_lru, g_lru, gates = _in_proj(hs, row3(mix_norm), w_in_b, row3(merge_bias), layer)
        ys_parts = _s5_scan(u_parts, s5_ops, layer, bsz)
        y_lru = _lru(x_lru, g_lru, lru_conv_w, row3(lru_conv_b), w_ri, row3(b_ri), row3(neg_sp), layer, bsz)
        j = layer // 2
        router = (rw_pad, row3(rb_pad), j) if layer % 2 == 1 else None
        res = _merge(hs, ys_parts, y_lru, gates, w_glu_b, row3(s5_b_glu), w_sp_b, w_lp_b, w_out_b,
                     row3(ffn_norm), layer, router)
        if layer % 2 == 0:
            hs, hn = res
            hs = _ffn(hn, hs, *dense, layer=j)
        else:
            hs, hn, route, route_t, counts = res
            pos, block_expert, n_used, pad_range = _moe_plan(route_t, counts, n)
            xs = _moe_dispatch(hn, pos, pad_range, n_used, block_expert.shape[0] * MOE_BLOCK)
            ys = _moe_ffn(xs, block_expert, n_used, *moe, first=j * N_EXPERTS)
            hs = _moe_combine(hs, route, ys, pos)

    return _final_norm(hs, final_norm[None, :], bsz, seq)
```

```python
import functools

import jax
import jax.numpy as jnp
from jax import lax
from jax.experimental import pallas as pl
from jax.experimental.pallas import tpu as pltpu

F32 = jnp.float32
BF16 = jnp.bfloat16

D_MODEL = 1024
N_META = 16
S5_WIDTH = 512
S5_GROUP = 16
S5_GROUPS = 32
S5_STATE = 64
LRU_WIDTH = 512
LRU_HEADS = 8
LRU_HEAD_DIM = 64
CONV_WIDTH = 4
LRU_C = 8.0
N_EXPERTS = 8
EPS = 1e-6

FOLD = 8
S5_PARTS = 4
PART_W = S5_WIDTH // S5_PARTS
PART_GROUPS = PART_W // S5_GROUP
PART_STATE = PART_GROUPS * S5_STATE
FOLD_W = FOLD * PART_W

T_PAD = 8256
ROWS = T_PAD // FOLD
TM = 688
TM_FFN = 1376
FF_CHUNK = 512
MOE_BLOCK = 512
GATHER_GROUP = 8
LRU_CHUNK = 1032
LRU_UNROLL = 3
TB_FINAL = 512
VMEM_LIMIT = 56 * 1024 * 1024
LANES = 128
SUBLANES = 8
S5_SEG = ROWS // SUBLANES
ROW_TILES = D_MODEL // LANES
MASKED_LOGIT = float("-inf")


def _dot(a, b):
    return jnp.dot(a, b, preferred_element_type=F32)


def _const_spec(block_shape, index_map):
    return pl.BlockSpec(block_shape, index_map, pipeline_mode=pl.Buffered(1))


def _rms(x, g):
    ms = jnp.mean(x * x, axis=-1, keepdims=True)
    return x * lax.rsqrt(ms + EPS) * g


def _rows_to_tiles(ref, x):
    rows = x.shape[0]
    for s in range(ROW_TILES):
        ref[pl.ds(s, rows, stride=ROW_TILES), :] = x[:, s * LANES:(s + 1) * LANES]


def _rows_from_tiles(ref, rows):
    return jnp.concatenate([ref[pl.ds(s, rows, stride=ROW_TILES), :] for s in range(ROW_TILES)], axis=-1)


def _in_proj_kernel(hs_ref, g_ref, w_ref, mb_ref, u_ref, xl_ref, gl_ref, gt_ref):
    hn = _rms(hs_ref[...], g_ref[0]).astype(BF16)
    u = _dot(hn, w_ref[0, :, 0:S5_WIDTH])
    for q in range(S5_PARTS):
        u_ref[q] = u[:, q * PART_W:(q + 1) * PART_W]
    o_x = S5_WIDTH
    o_g = o_x + LRU_WIDTH
    o_m = o_g + LRU_WIDTH
    xl_ref[...] = _dot(hn, w_ref[0, :, o_x:o_g]).astype(BF16)
    gl_ref[...] = _dot(hn, w_ref[0, :, o_g:o_m]).astype(BF16)
    z = _dot(hn, w_ref[0, :, o_m:]) + mb_ref[0]
    gt_ref[...] = jax.nn.sigmoid(z).astype(BF16)


def _in_proj(hs, mix_norm, w_in, merge_bias, layer):
    n = hs.shape[0]
    d_in = w_in.shape[-1]
    lay = lambda i: (layer, 0, 0)
    return pl.pallas_call(
        _in_proj_kernel,
        grid=(n // TM,),
        in_specs=[
            pl.BlockSpec((TM, D_MODEL), lambda i: (i, 0)),
            _const_spec((1, 1, D_MODEL), lay),
            _const_spec((1, D_MODEL, d_in), lay),
            _const_spec((1, 1, 2 * D_MODEL), lay),
        ],
        out_specs=[
            pl.BlockSpec((S5_PARTS, TM, PART_W), lambda i: (0, i, 0)),
            pl.BlockSpec((TM, LRU_WIDTH), lambda i: (i, 0)),
            pl.BlockSpec((TM, LRU_WIDTH), lambda i: (i, 0)),
            pl.BlockSpec((TM, 2 * D_MODEL), lambda i: (i, 0)),
        ],
        out_shape=[
            jax.ShapeDtypeStruct((S5_PARTS, n, PART_W), F32),
            jax.ShapeDtypeStruct((n, LRU_WIDTH), BF16),
            jax.ShapeDtypeStruct((n, LRU_WIDTH), BF16),
            jax.ShapeDtypeStruct((n, 2 * D_MODEL), BF16),
        ],
        compiler_params=pltpu.CompilerParams(
            dimension_semantics=("arbitrary",), vmem_limit_bytes=VMEM_LIMIT),
        name="in_proj",
    )(hs, mix_norm, w_in, merge_bias)


def _s5_prep(lam_re, lam_im, log_dt, b_re, b_im, c_re, c_im, d_skip):
    dt = jnp.exp(log_dt)[:, None]
    mag = jnp.exp(lam_re * dt)
    a_re = mag * jnp.cos(lam_im * dt)
    a_im = mag * jnp.sin(lam_im * dt)
    den = lam_re * lam_re + lam_im * lam_im
    num_re = a_re - 1.0
    coef_re = (num_re * lam_re + a_im * lam_im) / den
    coef_im = (a_im * lam_re - num_re * lam_im) / den
    bb_re = coef_re[..., None] * b_re - coef_im[..., None] * b_im
    bb_im = coef_re[..., None] * b_im + coef_im[..., None] * b_re

    def cmul(xr, xi, yr, yi):
        return xr * yr - xi * yi, xr * yi + xi * yr

    def powers(br, bi, n):
        pr, pi = [jnp.ones_like(br)], [jnp.zeros_like(bi)]
        for _ in range(n):
            r, i = cmul(pr[-1], pi[-1], br, bi)
            pr.append(r)
            pi.append(i)
        return jnp.stack(pr), jnp.stack(pi)

    p_re, p_im = powers(a_re, a_im, FOLD)

    def per_part(x):
        lead = x.shape[:-3]
        xp = x.reshape(lead + (S5_PARTS, PART_GROUPS) + x.shape[-2:])
        return jnp.moveaxis(xp, len(lead), 0)

    rev_re = jnp.stack([p_re[FOLD - 1 - j] for j in range(FOLD)])
    rev_im = jnp.stack([p_im[FOLD - 1 - j] for j in range(FOLD)])
    wr, wi = cmul(rev_re[..., None], rev_im[..., None], bb_re[None], bb_im[None])
    w_ri = jnp.swapaxes(jnp.stack([wr, wi], axis=1), -1, -2)
    xq = jnp.transpose(per_part(w_ri), (0, 1, 3, 4, 2, 5)).reshape(S5_PARTS, FOLD_W, 2 * S5_STATE)

    ca_re, ca_im = cmul(c_re[None], c_im[None], p_re[:, :, None, :], p_im[:, :, None, :])
    bt_re = jnp.swapaxes(bb_re, -1, -2)[None, :, :, None, :]
    bt_im = jnp.swapaxes(bb_im, -1, -2)[None, :, :, None, :]
    taps = jnp.sum(ca_re[:FOLD, :, None] * bt_re - ca_im[:FOLD, :, None] * bt_im, axis=-1)
    skip = d_skip.reshape(S5_GROUPS, S5_GROUP)
    taps = taps.at[0].add(skip[:, :, None] * jnp.eye(S5_GROUP, dtype=F32)[None])
    rc = jnp.transpose(per_part(taps), (0, 2, 3, 1, 4)).reshape(S5_PARTS, PART_W, FOLD * S5_GROUP)

    v_ri = jnp.swapaxes(jnp.stack([ca_re[1:], -ca_im[1:]], axis=0), -1, -2)
    vc = jnp.transpose(per_part(v_ri), (0, 1, 3, 4, 2, 5)).reshape(S5_PARTS, 2 * PART_STATE, FOLD * S5_GROUP)

    def part_vec(x):
        lead = x.shape[:-2]
        xp = x.reshape(lead + (S5_PARTS, PART_STATE))
        return jnp.moveaxis(xp, -2, 0)

    row_re, row_im = p_re[FOLD], p_im[FOLD]
    seg_re, seg_im = jnp.ones_like(row_re), jnp.zeros_like(row_im)
    for bit in bin(S5_SEG)[2:]:
        seg_re, seg_im = cmul(seg_re, seg_im, seg_re, seg_im)
        if bit == '1':
            seg_re, seg_im = cmul(seg_re, seg_im, row_re, row_im)
    decay = jnp.stack([jnp.stack([part_vec(row_re), part_vec(row_im)], axis=1),
                       jnp.stack([part_vec(seg_re), part_vec(seg_im)], axis=1)], axis=1)
    return xq, rc, vc, decay[:, :, :, None, :]


def _iota2(shape):
    return (lax.broadcasted_iota(jnp.int32, shape, 0), lax.broadcasted_iota(jnp.int32, shape, 1))


def _s5_expand(xq, rc, vc, w1_s, tv_s):
    ps = PART_STATE
    lg_state, lg_group, lg_part = (v.bit_length() - 1 for v in (S5_STATE, S5_GROUP, PART_W))
    lg_pg = PART_GROUPS.bit_length() - 1
    grp = PART_GROUPS - 1
    one_hot = lambda m: jnp.where(m, 1.0, 0.0).astype(BF16)
    r, c = _iota2((2 * S5_STATE, 2 * ps))
    e1 = one_hot(((r >> lg_state) == (c >> (lg_state + lg_pg))) & ((r & (S5_STATE - 1)) == (c & (S5_STATE - 1))))
    r, c = _iota2((FOLD * S5_GROUP, FOLD_W))
    e2 = one_hot(((r >> lg_group) == (c >> lg_part)) & ((r & (S5_GROUP - 1)) == (c & (S5_GROUP - 1))))
    r, c = _iota2((FOLD_W, 2 * ps))
    m1 = ((r >> lg_group) & grp) == ((c >> lg_state) & grp)
    w1_s[...] = jnp.where(m1, _dot(xq.astype(BF16), e1), 0.0).astype(BF16)
    r, c = _iota2((PART_W, FOLD_W))
    m2 = (r >> lg_group) == ((c >> lg_group) & grp)
    r0 = jnp.where(m2, _dot(rc.astype(BF16), e2), 0.0).astype(BF16)
    for j in range(FOLD):
        if j == 0:
            blk = r0
        else:
            blk = jnp.concatenate([jnp.zeros((PART_W, j * PART_W), BF16), r0[:, :FOLD_W - j * PART_W]], axis=1)
        tv_s[j * PART_W:(j + 1) * PART_W, :] = blk
    r, c = _iota2((2 * ps, FOLD_W))
    m3 = ((r >> lg_state) & grp) == ((c >> lg_group) & grp)
    tv_s[FOLD_W:, :] = jnp.where(m3, _dot(vc.astype(BF16), e2), 0.0).astype(BF16)


def _s5_kernel(u_ref, xq_ref, rc_ref, vc_ref, dec_ref, y_ref, w1_s, tv_s, up_ref, f_ref, hp_ref):
    ps = PART_STATE
    tstride = S5_SEG * FOLD
    cols = lambda j: slice(j * PART_W, (j + 1) * PART_W)

    @pl.when(pl.program_id(1) == 0)
    def _():
        _s5_expand(xq_ref[0, 0], rc_ref[0, 0], vc_ref[0, 0], w1_s, tv_s)

    def fold_body(i, carry):
        r0 = pl.multiple_of(i * SUBLANES, SUBLANES)
        for j in range(FOLD):
            up_ref[pl.ds(r0, SUBLANES), cols(j)] = u_ref[0, 0, pl.ds(i * FOLD + j, SUBLANES, stride=tstride), :]
        return carry

    lax.fori_loop(0, S5_SEG, fold_body, 0)
    u = up_ref[...].astype(BF16)
    f_ref[...] = _dot(u, w1_s[...])
    ar = jnp.broadcast_to(dec_ref[0, 0, 0, 0], (SUBLANES, ps))
    ai = jnp.broadcast_to(dec_ref[0, 0, 0, 1], (SUBLANES, ps))

    def step(i, hr, hi):
        r0 = pl.multiple_of(i * SUBLANES, SUBLANES)
        return (ar * hr - ai * hi + f_ref[pl.ds(r0, SUBLANES), :ps],
                ar * hi + ai * hr + f_ref[pl.ds(r0, SUBLANES), ps:])

    zero = jnp.zeros((SUBLANES, ps), F32)
    er, ei = lax.fori_loop(0, S5_SEG, lambda i, c: step(i, *c), (zero, zero))
    sr = dec_ref[0, 0, 1, 0]
    si = dec_ref[0, 0, 1, 1]
    row = lax.broadcasted_iota(jnp.int32, (SUBLANES, ps), 0)
    nr, ni = zero, zero
    for sgm in range(SUBLANES - 1):
        lr = er + sr * nr - si * ni
        li = ei + sr * ni + si * nr
        nr = nr + jnp.where(row == sgm + 1, pltpu.roll(lr, 1, axis=0), 0.0)
        ni = ni + jnp.where(row == sgm + 1, pltpu.roll(li, 1, axis=0), 0.0)

    def state_body(i, carry):
        hr, hi = carry
        r0 = pl.multiple_of(i * SUBLANES, SUBLANES)
        hp_ref[pl.ds(r0, SUBLANES), :ps] = hr
        hp_ref[pl.ds(r0, SUBLANES), ps:] = hi
        return step(i, hr, hi)

    lax.fori_loop(0, S5_SEG, state_body, (nr, ni))
    hp = hp_ref[...].astype(BF16)
    wide = 2 * PART_W
    for c0 in range(0, FOLD_W, wide):
        y = _dot(u[:, :c0 + wide], tv_s[:c0 + wide, c0:c0 + wide]) + _dot(hp, tv_s[FOLD_W:, c0:c0 + wide])
        f_ref[:, c0:c0 + wide] = jax.nn.gelu(y)

    def unfold_body(i, carry):
        r0 = pl.multiple_of(i * SUBLANES, SUBLANES)
        for j in range(FOLD):
            y_ref[0, 0, pl.ds(i * FOLD + j, SUBLANES, stride=tstride), :] = f_ref[pl.ds(r0, SUBLANES), cols(j)]
        return carry

    lax.fori_loop(0, S5_SEG, unfold_body, 0)


def _s5_scan(u_parts, ops, layer, bsz):
    xq, rc, vc, decay = ops
    n = u_parts.shape[1]
    u4 = u_parts.reshape(S5_PARTS, bsz, T_PAD, PART_W)
    lay4 = lambda q, b: (layer, q, 0, 0)
    y4 = pl.pallas_call(
        _s5_kernel,
        grid=(S5_PARTS, bsz),
        in_specs=[
            pl.BlockSpec((1, 1, T_PAD, PART_W), lambda q, b: (q, b, 0, 0)),
            pl.BlockSpec((1, 1, FOLD_W, 2 * S5_STATE), lay4),
            pl.BlockSpec((1, 1, PART_W, FOLD * S5_GROUP), lay4),
            pl.BlockSpec((1, 1, 2 * PART_STATE, FOLD * S5_GROUP), lay4),
            pl.BlockSpec((1, 1, 2, 2, 1, PART_STATE), lambda q, b: (layer, q, 0, 0, 0, 0)),
        ],
        out_specs=pl.BlockSpec((1, 1, T_PAD, PART_W), lambda q, b: (q, b, 0, 0)),
        out_shape=jax.ShapeDtypeStruct((S5_PARTS, bsz, T_PAD, PART_W), F32),
        scratch_shapes=[
            pltpu.VMEM((FOLD_W, 2 * PART_STATE), BF16),
            pltpu.VMEM((FOLD_W + 2 * PART_STATE, FOLD_W), BF16),
            pltpu.VMEM((ROWS, FOLD_W), F32),
            pltpu.VMEM((ROWS, 2 * PART_STATE), F32),
            pltpu.VMEM((ROWS, 2 * PART_STATE), F32),
        ],
        compiler_params=pltpu.CompilerParams(
            dimension_semantics=("arbitrary", "arbitrary"), vmem_limit_bytes=VMEM_LIMIT),
        name="s5_scan",
    )(u4, xq, rc, vc, decay)
    return y4.reshape(S5_PARTS, n, PART_W)


def _lru_kernel(x_ref, g_ref, cw_ref, cb_ref, wri_ref, bri_ref, nsp_ref, o_ref,
                xs_ref, gs_ref, xc_ref, gp_ref, z_ref, a_ref, b_ref, os_ref, h_ref):
    tc = LRU_CHUNK
    c = LRU_WIDTH
    seg = tc // SUBLANES
    nq = c // LANES
    lanes = lambda q: slice(q * LANES, (q + 1) * LANES)
    halo = SUBLANES

    @pl.when(pl.program_id(1) == 0)
    def _():
        xs_ref[:, 0:halo, :] = jnp.zeros((nq, halo, LANES), F32)
        h_ref[...] = jnp.zeros((1, c), F32)

    x = x_ref[0].astype(F32)
    g = g_ref[0].astype(F32)
    for q in range(nq):
        xs_ref[q, halo:, :] = x[:, lanes(q)]
        gs_ref[q] = g[:, lanes(q)]
    taps = [[cw_ref[0, k:k + 1, lanes(q)] for k in range(CONV_WIDTH)] for q in range(nq)]
    bias = [cb_ref[0, :, lanes(q)] for q in range(nq)]

    def conv_body(i, carry):
        r0 = pl.multiple_of(i * SUBLANES, SUBLANES)
        for q in range(nq):
            acc = bias[q]
            for k in range(CONV_WIDTH):
                first = halo - (CONV_WIDTH - 1) + k + i
                acc = acc + taps[q][k] * xs_ref[q, pl.ds(first, SUBLANES, stride=seg), :]
            xc_ref[pl.ds(r0, SUBLANES), lanes(q)] = acc
            gp_ref[pl.ds(r0, SUBLANES), lanes(q)] = jax.nn.gelu(gs_ref[q, pl.ds(i, SUBLANES, stride=seg), :])
        return carry

    lax.fori_loop(0, seg, conv_body, 0, unroll=LRU_UNROLL)
    for q in range(nq):
        xs_ref[q, 0:halo, :] = xs_ref[q, tc:tc + halo, :]

    z_ref[...] = _dot(xc_ref[...].astype(BF16), wri_ref[0])
    b_r = jnp.broadcast_to(bri_ref[0, :, :c], (SUBLANES, c))
    b_i = jnp.broadcast_to(bri_ref[0, :, c:], (SUBLANES, c))
    nsp = jnp.broadcast_to(nsp_ref[0], (SUBLANES, c))

    def scan_body(i, carry):
        h, p = carry
        r0 = pl.multiple_of(i * SUBLANES, SUBLANES)
        a = jnp.exp(jax.nn.sigmoid(z_ref[pl.ds(r0, SUBLANES), :c] + b_r) * nsp)
        gated = jax.nn.sigmoid(z_ref[pl.ds(r0, SUBLANES), c:] + b_i) * xc_ref[pl.ds(r0, SUBLANES), :]
        h = a * h + jnp.sqrt(1.0 - a * a) * gated
        p = p * a
        b_ref[pl.ds(r0, SUBLANES), :] = h
        a_ref[pl.ds(r0, SUBLANES), :] = p
        return h, p

    h_end, p_end = lax.fori_loop(0, seg, scan_body, (jnp.zeros((SUBLANES, c), F32), jnp.ones((SUBLANES, c), F32)),
                                 unroll=LRU_UNROLL)
    row = lax.broadcasted_iota(jnp.int32, (SUBLANES, c), 0)
    enter = jnp.where(row == 0, h_ref[...], 0.0)
    for sgm in range(SUBLANES - 1):
        leave = h_end + p_end * enter
        enter = enter + jnp.where(row == sgm + 1, pltpu.roll(leave, 1, axis=0), 0.0)
    h_ref[...] = (h_end + p_end * enter)[SUBLANES - 1:SUBLANES]

    def out_body(i, carry):
        r0 = pl.multiple_of(i * SUBLANES, SUBLANES)
        h = b_ref[pl.ds(r0, SUBLANES), :] + a_ref[pl.ds(r0, SUBLANES), :] * enter
        y = h * gp_ref[pl.ds(r0, SUBLANES), :]
        for q in range(nq):
            os_ref[q, pl.ds(i, SUBLANES, stride=seg), :] = y[:, lanes(q)]
        return carry

    lax.fori_loop(0, seg, out_body, 0)
    o_ref[0] = jnp.concatenate([os_ref[q] for q in range(nq)], axis=-1).astype(BF16)


def _lru(x_lru, g_lru, conv_w, conv_b, w_ri, b_ri, neg_sp, layer, bsz):
    n = x_lru.shape[0]
    c = LRU_WIDTH
    x3 = x_lru.reshape(bsz, T_PAD, c)
    g3 = g_lru.reshape(bsz, T_PAD, c)
    lay = lambda b, t: (layer, 0, 0)
    out = pl.pallas_call(
        _lru_kernel,
        grid=(bsz, T_PAD // LRU_CHUNK),
        in_specs=[
            pl.BlockSpec((1, LRU_CHUNK, c), lambda b, t: (b, t, 0)),
            pl.BlockSpec((1, LRU_CHUNK, c), lambda b, t: (b, t, 0)),
            _const_spec((1, CONV_WIDTH, c), lay),
            _const_spec((1, 1, c), lay),
            _const_spec((1, c, 2 * c), lay),
            _const_spec((1, 1, 2 * c), lay),
            _const_spec((1, 1, c), lay),
        ],
        out_specs=pl.BlockSpec((1, LRU_CHUNK, c), lambda b, t: (b, t, 0)),
        out_shape=jax.ShapeDtypeStruct((bsz, T_PAD, c), BF16),
        scratch_shapes=[
            pltpu.VMEM((c // LANES, LRU_CHUNK + SUBLANES, LANES), F32),
            pltpu.VMEM((c // LANES, LRU_CHUNK, LANES), F32),
            pltpu.VMEM((LRU_CHUNK, c), F32),
            pltpu.VMEM((LRU_CHUNK, c), F32),
            pltpu.VMEM((LRU_CHUNK, 2 * c), F32),
            pltpu.VMEM((LRU_CHUNK, c), F32),
            pltpu.VMEM((LRU_CHUNK, c), F32),
            pltpu.VMEM((c // LANES, LRU_CHUNK, LANES), F32),
            pltpu.VMEM((1, c), F32),
        ],
        compiler_params=pltpu.CompilerParams(
            dimension_semantics=("arbitrary", "arbitrary"), vmem_limit_bytes=VMEM_LIMIT),
        name="rglru",
    )(x3, g3, conv_w, conv_b, w_ri, b_ri, neg_sp)
    return out.reshape(n, c)


def _merge_kernel(hs_ref, ys_ref, yl_ref, gt_ref, wglu_ref, bglu_ref, wsp_ref, wlp_ref, wout_ref, g_ref,
                  *rest, with_router):
    if with_router:
        rw_ref, rb_ref, hs_out_ref, hn_ref, rt_ref, rtt_ref, cnt_ref, run_ref, tri_ref = rest
    else:
        hs_out_ref, hn_ref = rest
    ys = jnp.concatenate([ys_ref[q] for q in range(S5_PARTS)], axis=-1)
    glu = ys * jax.nn.sigmoid(_dot(ys.astype(BF16), wglu_ref[0]) + bglu_ref[0])
    y_a = _dot(glu.astype(BF16), wsp_ref[0])
    y_b = _dot(yl_ref[...], wlp_ref[0])
    y = gt_ref[:, :D_MODEL].astype(F32) * y_a + gt_ref[:, D_MODEL:].astype(F32) * y_b
    hs = hs_ref[...] + _dot(y.astype(BF16), wout_ref[0])
    hs_out_ref[...] = hs
    hn = _rms(hs, g_ref[0])
    if not with_router:
        hn_ref[...] = hn.astype(BF16)
    else:
        _rows_to_tiles(hn_ref, hn)
        logits = _dot(hn.astype(BF16), rw_ref[0].astype(BF16)) + rb_ref[0]
        lane = lax.broadcasted_iota(jnp.int32, logits.shape, 1).astype(F32)
        m1 = jnp.max(logits, axis=-1, keepdims=True)
        i1 = jnp.min(jnp.where(logits == m1, lane, float(LANES)), axis=-1, keepdims=True)
        rest_l = jnp.where(lane == i1, MASKED_LOGIT, logits)
        m2 = jnp.max(rest_l, axis=-1, keepdims=True)
        i2 = jnp.min(jnp.where(rest_l == m2, lane, float(LANES)), axis=-1, keepdims=True)
        e2 = jnp.exp(m2 - m1)
        g1 = 1.0 / (1.0 + e2)
        g2 = e2 / (1.0 + e2)
        @pl.when(pl.program_id(0) == 0)
        def _():
            run_ref[...] = jnp.zeros_like(run_ref)
            r, c = _iota2(tri_ref.shape)
            tri_ref[...] = jnp.where(c < r, 1.0, 0.0).astype(BF16)

        first = lane == i1
        second = lane == i2
        picked = jnp.where(first | second, 1.0, 0.0)
        before = _dot(tri_ref[...], picked.astype(BF16)) + run_ref[...]
        rank1 = jnp.sum(jnp.where(first, before, 0.0), axis=-1, keepdims=True)
        rank2 = jnp.sum(jnp.where(second, before, 0.0), axis=-1, keepdims=True)
        run_ref[...] += jnp.sum(picked, axis=0, keepdims=True)
        cnt_ref[...] = jnp.broadcast_to(run_ref[...], cnt_ref.shape)
        rt = (jnp.where(lane == 0.0, i1, 0.0) + jnp.where(lane == 1.0, i2, 0.0)
              + jnp.where(lane == 2.0, g1, 0.0) + jnp.where(lane == 3.0, g2, 0.0)
              + jnp.where(lane == 4.0, rank1, 0.0) + jnp.where(lane == 5.0, rank2, 0.0))
        rt_ref[...] = rt
        r, c = _iota2((SUBLANES, LANES))
        rtt_ref[0] = lax.dot_general(jnp.where(r == c, 1.0, 0.0), rt, (((1,), (1,)), ((), ())),
                                     precision=lax.Precision.HIGHEST, preferred_element_type=F32)


def _merge(hs, ys_parts, y_lru, gates, w_glu, b_glu, w_sp, w_lp, w_out, ffn_norm, layer, router=None):
    n = hs.shape[0]
    tm = TM
    lay = lambda i: (layer, 0, 0)
    in_specs = [
        pl.BlockSpec((tm, D_MODEL), lambda i: (i, 0)),
        pl.BlockSpec((S5_PARTS, tm, PART_W), lambda i: (0, i, 0)),
        pl.BlockSpec((tm, LRU_WIDTH), lambda i: (i, 0)),
        pl.BlockSpec((tm, 2 * D_MODEL), lambda i: (i, 0)),
        _const_spec((1, S5_WIDTH, S5_WIDTH), lay),
        _const_spec((1, 1, S5_WIDTH), lay),
        _const_spec((1, S5_WIDTH, D_MODEL), lay),
        _const_spec((1, LRU_WIDTH, D_MODEL), lay),
        _const_spec((1, D_MODEL, D_MODEL), lay),
        _const_spec((1, 1, D_MODEL), lay),
    ]
    out_specs = [pl.BlockSpec((tm, D_MODEL), lambda i: (i, 0))]
    out_shape = [jax.ShapeDtypeStruct((n, D_MODEL), F32)]
    if router is None:
        out_specs.append(pl.BlockSpec((tm, D_MODEL), lambda i: (i, 0)))
        out_shape.append(jax.ShapeDtypeStruct((n, D_MODEL), BF16))
    else:
        out_specs.append(pl.BlockSpec((tm * ROW_TILES, LANES), lambda i: (i, 0)))
        out_shape.append(jax.ShapeDtypeStruct((n * ROW_TILES, LANES), F32))
    args = [hs, ys_parts, y_lru, gates, w_glu, b_glu, w_sp, w_lp, w_out, ffn_norm]
    if router is not None:
        rw, rb, j = router
        in_specs += [_const_spec((1, D_MODEL, LANES), lambda i: (j, 0, 0)),
                     _const_spec((1, 1, LANES), lambda i: (j, 0, 0))]
        out_specs += [pl.BlockSpec((tm, LANES), lambda i: (i, 0)),
                      pl.BlockSpec((1, SUBLANES, tm), lambda i: (i, 0, 0)),
                      pl.BlockSpec((SUBLANES, LANES), lambda i: (0, 0))]
        out_shape += [jax.ShapeDtypeStruct((n, LANES), F32),
                      jax.ShapeDtypeStruct((n // tm, SUBLANES, tm), F32),
                      jax.ShapeDtypeStruct((SUBLANES, LANES), F32)]
        args += [rw, rb]
    return pl.pallas_call(
        functools.partial(_merge_kernel, with_router=router is not None),
        grid=(n // tm,),
        in_specs=in_specs,
        out_specs=out_specs,
        out_shape=out_shape,
        scratch_shapes=[pltpu.VMEM((1, LANES), F32), pltpu.VMEM((tm, tm), BF16)] if router is not None else [],
        compiler_params=pltpu.CompilerParams(
            dimension_semantics=("arbitrary",), vmem_limit_bytes=VMEM_LIMIT),
        name="merge_router" if router is not None else "merge",
    )(*args)


def _ffn_kernel(x_ref, hs_ref, wg_ref, wu_ref, wd_ref, o_ref, acc_ref):
    c = pl.program_id(1)

    @pl.when(c == 0)
    def _():
        acc_ref[...] = jnp.zeros_like(acc_ref)

    x = x_ref[...]
    g = _dot(x, wg_ref[0].astype(BF16))
    h = g * jax.nn.sigmoid(g) * _dot(x, wu_ref[0].astype(BF16))
    acc_ref[...] += _dot(h.astype(BF16), wd_ref[0].astype(BF16))

    @pl.when(c == pl.num_programs(1) - 1)
    def _():
        o_ref[...] = hs_ref[...] + acc_ref[...]


def _ffn(hn, hs, w_gate, w_up, w_down, layer):
    n = hn.shape[0]
    ff = w_gate.shape[-1]
    return pl.pallas_call(
        _ffn_kernel,
        grid=(n // TM_FFN, ff // FF_CHUNK),
        in_specs=[
            pl.BlockSpec((TM_FFN, D_MODEL), lambda i, c: (i, 0)),
            pl.BlockSpec((TM_FFN, D_MODEL), lambda i, c: (i, 0)),
            pl.BlockSpec((1, D_MODEL, FF_CHUNK), lambda i, c: (layer, 0, c)),
            pl.BlockSpec((1, D_MODEL, FF_CHUNK), lambda i, c: (layer, 0, c)),
            pl.BlockSpec((1, FF_CHUNK, D_MODEL), lambda i, c: (layer, c, 0)),
        ],
        out_specs=pl.BlockSpec((TM_FFN, D_MODEL), lambda i, c: (i, 0)),
        out_shape=jax.ShapeDtypeStruct((n, D_MODEL), F32),
        scratch_shapes=[pltpu.VMEM((TM_FFN, D_MODEL), F32)],
        compiler_params=pltpu.CompilerParams(
            dimension_semantics=("arbitrary", "arbitrary"), vmem_limit_bytes=VMEM_LIMIT),
        name="dense_ffn",
    )(hn, hs, w_gate, w_up, w_down)


def _moe_plan(route_t, counts_f, n):
    n_blocks = -(-2 * n // MOE_BLOCK) + N_EXPERTS
    e = jnp.stack([route_t[:, 0, :], route_t[:, 1, :]]).astype(jnp.int32)
    rank = jnp.stack([route_t[:, 4, :], route_t[:, 5, :]]).astype(jnp.int32)
    counts = counts_f[0, :N_EXPERTS].astype(jnp.int32)
    padded = ((counts + MOE_BLOCK - 1) // MOE_BLOCK) * MOE_BLOCK
    cum_pad = jnp.cumsum(padded)
    pad_start = cum_pad - padded
    pos = rank
    for x in range(N_EXPERTS):
        pos = pos + jnp.where(e == x, pad_start[x], 0)
    block_start = jnp.arange(n_blocks, dtype=jnp.int32) * MOE_BLOCK
    block_expert = jnp.minimum(jnp.sum((block_start[:, None] >= cum_pad[None, :]).astype(jnp.int32), axis=1),
                               N_EXPERTS - 1)
    n_used = (cum_pad[-1] // MOE_BLOCK).astype(jnp.int32).reshape(1)
    pad_range = jnp.stack([pad_start + counts, cum_pad], axis=1).reshape(2 * N_EXPERTS).astype(jnp.int32)
    return pos.reshape(2 * n), block_expert, n_used, pad_range


def _tile(ref, index):
    return ref.at[pl.ds(pl.multiple_of(index * ROW_TILES, ROW_TILES), ROW_TILES)]


def _tile_gather(src_hbm, dst, sem, rows, index_of):
    def body(grp, carry):
        r0 = grp * GATHER_GROUP
        index = [index_of(r0 + j) for j in range(GATHER_GROUP)]
        for j in range(GATHER_GROUP):
            pltpu.make_async_copy(_tile(src_hbm, index[j]), _tile(dst, r0 + j), sem).start()
        return carry
    lax.fori_loop(0, rows // GATHER_GROUP, body, 0)


def _tile_gather_wait(src_hbm, dst, sem, rows):
    pltpu.make_async_copy(src_hbm.at[pl.ds(0, rows * ROW_TILES)], dst, sem).wait()


def _moe_dispatch_kernel(pos_ref, pad_ref, nu_ref, x_ref, xs_hbm, zero_ref, sem):
    t = pl.program_id(0)
    n = pl.num_programs(0) * TM
    block_rows = MOE_BLOCK * ROW_TILES
    n_blocks = xs_hbm.shape[0] // block_rows

    @pl.when(t == 0)
    def _():
        zero_ref[...] = jnp.zeros_like(zero_ref)
        zero_tile = zero_ref.at[pl.ds(0, ROW_TILES)]
        for e in range(N_EXPERTS):
            def fill(slot, carry):
                pltpu.make_async_copy(zero_tile, _tile(xs_hbm, slot), sem.at[1]).start()
                return carry

            def fill_wait(slot, carry):
                pltpu.make_async_copy(zero_tile, _tile(xs_hbm, slot), sem.at[1]).wait()
                return carry
            lax.fori_loop(pad_ref[2 * e], pad_ref[2 * e + 1], fill, 0)
            lax.fori_loop(pad_ref[2 * e], pad_ref[2 * e + 1], fill_wait, 0)

        def block_of(blk):
            return xs_hbm.at[pl.ds(pl.multiple_of(blk * block_rows, block_rows), block_rows)]

        def fill_block(blk, carry):
            pltpu.make_async_copy(zero_ref, block_of(blk), sem.at[1]).start()
            return carry

        def fill_block_wait(blk, carry):
            pltpu.make_async_copy(zero_ref, block_of(blk), sem.at[1]).wait()
            return carry
        lax.fori_loop(nu_ref[0], n_blocks, fill_block, 0)
        lax.fori_loop(nu_ref[0], n_blocks, fill_block_wait, 0)

    for k in range(2):
        def put(grp, carry):
            r0 = grp * GATHER_GROUP
            slot = [pos_ref[k * n + t * TM + r0 + j] for j in range(GATHER_GROUP)]
            for j in range(GATHER_GROUP):
                pltpu.make_async_copy(_tile(x_ref, r0 + j), _tile(xs_hbm, slot[j]), sem.at[0]).start()
            return carry
        lax.fori_loop(0, TM // GATHER_GROUP, put, 0)
    for k in range(2):
        pltpu.make_async_copy(x_ref, xs_hbm.at[pl.ds(0, TM * ROW_TILES)], sem.at[0]).wait()


def _moe_dispatch(hn_tiles, pos, pad_range, n_used, n_slots):
    n = hn_tiles.shape[0] // ROW_TILES
    return pl.pallas_call(
        _moe_dispatch_kernel,
        grid_spec=pltpu.PrefetchScalarGridSpec(
            num_scalar_prefetch=3,
            grid=(n // TM,),
            in_specs=[pl.BlockSpec((TM * ROW_TILES, LANES), lambda t, ps, pr, nu: (t, 0))],
            out_specs=pl.BlockSpec(memory_space=pl.ANY),
            scratch_shapes=[pltpu.VMEM((MOE_BLOCK * ROW_TILES, LANES), F32),
                            pltpu.SemaphoreType.DMA((2,))],
        ),
        out_shape=jax.ShapeDtypeStruct((n_slots * ROW_TILES, LANES), F32),
        compiler_params=pltpu.CompilerParams(dimension_semantics=("arbitrary",)),
        name="moe_dispatch",
    )(pos, pad_range, n_used, hn_tiles)


def _moe_ffn_kernel(be_ref, nu_ref, x_ref, wg_ref, wu_ref, wd_ref, y_ref):
    i = pl.program_id(0)

    @pl.when(i < nu_ref[0])
    def _():
        x = _rows_from_tiles(x_ref, MOE_BLOCK).astype(BF16)
        g = _dot(x, wg_ref[0].astype(BF16))
        h = g * jax.nn.sigmoid(g) * _dot(x, wu_ref[0].astype(BF16))
        _rows_to_tiles(y_ref, _dot(h.astype(BF16), wd_ref[0].astype(BF16)))

    @pl.when(i >= nu_ref[0])
    def _():
        y_ref[...] = jnp.zeros_like(y_ref)


def _moe_ffn(xs_tiles, block_expert, n_used, w_gate, w_up, w_down, first):
    n_blocks = block_expert.shape[0]
    ff = w_gate.shape[-1]
    wmap = lambda i, be, nu: (first + be[i], 0, 0)
    return pl.pallas_call(
        _moe_ffn_kernel,
        grid_spec=pltpu.PrefetchScalarGridSpec(
            num_scalar_prefetch=2,
            grid=(n_blocks,),
            in_specs=[
                pl.BlockSpec((MOE_BLOCK * ROW_TILES, LANES), lambda i, be, nu: (jnp.minimum(i, nu[0] - 1), 0)),
                pl.BlockSpec((1, D_MODEL, ff), wmap),
                pl.BlockSpec((1, D_MODEL, ff), wmap),
                pl.BlockSpec((1, ff, D_MODEL), wmap),
            ],
            out_specs=pl.BlockSpec((MOE_BLOCK * ROW_TILES, LANES), lambda i, be, nu: (i, 0)),
        ),
        out_shape=jax.ShapeDtypeStruct((n_blocks * MOE_BLOCK * ROW_TILES, LANES), F32),
        compiler_params=pltpu.CompilerParams(
            dimension_semantics=("arbitrary",), vmem_limit_bytes=VMEM_LIMIT),
        name="moe_ffn",
    )(block_expert, n_used, xs_tiles, w_gate, w_up, w_down)


def _moe_combine_kernel(pos_ref, hs_ref, rt_ref, ys_hbm, o_ref, ybuf, sem):
    i = pl.program_id(0)
    nt = pl.num_programs(0)
    slot = i % 2

    def start(t, s):
        for k in range(2):
            _tile_gather(ys_hbm, ybuf.at[s, k], sem.at[s], TM, lambda r: pos_ref[k * (nt * TM) + t * TM + r])

    @pl.when(i == 0)
    def _():
        start(0, 0)

    @pl.when(i + 1 < nt)
    def _():
        start(i + 1, 1 - slot)

    for k in range(2):
        _tile_gather_wait(ys_hbm, ybuf.at[slot, k], sem.at[slot], TM)
    rt = rt_ref[...]
    lane = lax.broadcasted_iota(jnp.int32, rt.shape, 1)
    g1 = jnp.sum(jnp.where(lane == 2, rt, 0.0), axis=-1, keepdims=True)
    g2 = jnp.sum(jnp.where(lane == 3, rt, 0.0), axis=-1, keepdims=True)
    o_ref[...] = (hs_ref[...] + g1 * _rows_from_tiles(ybuf.at[slot, 0], TM)
                  + g2 * _rows_from_tiles(ybuf.at[slot, 1], TM))


def _moe_combine(hs, route, ys_tiles, pos):
    n = hs.shape[0]
    return pl.pallas_call(
        _moe_combine_kernel,
        grid_spec=pltpu.PrefetchScalarGridSpec(
            num_scalar_prefetch=1,
            grid=(n // TM,),
            in_specs=[
                pl.BlockSpec((TM, D_MODEL), lambda i, p: (i, 0)),
                pl.BlockSpec((TM, LANES), lambda i, p: (i, 0)),
                pl.BlockSpec(memory_space=pl.ANY),
            ],
            out_specs=pl.BlockSpec((TM, D_MODEL), lambda i, p: (i, 0)),
            scratch_shapes=[pltpu.VMEM((2, 2, TM * ROW_TILES, LANES), F32),
                            pltpu.SemaphoreType.DMA((2,))],
        ),
        out_shape=jax.ShapeDtypeStruct((n, D_MODEL), F32),
        compiler_params=pltpu.CompilerParams(
            dimension_semantics=("arbitrary",), vmem_limit_bytes=VMEM_LIMIT),
        name="moe_combine",
    )(pos, hs, route, ys_tiles)


def _final_kernel(a_ref, b_ref, g_ref, o_ref):
    tb = a_ref.shape[1]
    o_ref[0, :tb - N_META] = _rms(a_ref[0, N_META:], g_ref[...])
    o_ref[0, tb - N_META:] = _rms(b_ref[0], g_ref[...])


def _final_norm(hs, g, bsz, seq):
    hs3 = hs.reshape(bsz, T_PAD, D_MODEL)
    return pl.pallas_call(
        _final_kernel,
        grid=(bsz, seq // TB_FINAL),
        in_specs=[pl.BlockSpec((1, TB_FINAL, D_MODEL), lambda b, i: (b, i, 0)),
                  pl.BlockSpec((1, N_META, D_MODEL), lambda b, i: (b, (i + 1) * (TB_FINAL // N_META), 0)),
                  _const_spec((1, D_MODEL), lambda b, i: (0, 0))],
        out_specs=pl.BlockSpec((1, TB_FINAL, D_MODEL), lambda b, i: (b, i, 0)),
        out_shape=jax.ShapeDtypeStruct((bsz, seq, D_MODEL), F32),
        compiler_params=pltpu.CompilerParams(
            dimension_semantics=("arbitrary", "arbitrary"), vmem_limit_bytes=VMEM_LIMIT),
        name="final_norm",
    )(hs3, hs3, g)


def _head_blockdiag(w):
    eye = jnp.eye(LRU_HEADS, dtype=w.dtype)
    out = jnp.einsum('lnhk,nm->lnhmk', w, eye)
    return out.reshape(w.shape[0], LRU_WIDTH, LRU_WIDTH)


def kernel(x, meta_tokens, mix_norm, w_in, merge_bias, s5_lambda_re, s5_lambda_im, s5_log_dt, s5_b_re, s5_b_im, s5_c_re, s5_c_im, s5_d, s5_w_glu, s5_b_glu, s5_w_proj, lru_conv_w, lru_conv_b, lru_w_rgate, lru_b_rgate, lru_w_igate, lru_b_igate, lru_lambda, lru_w_proj, w_out, ffn_norm, dense_w_gate, dense_w_up, dense_w_down, router_w, router_b, moe_w_gate, moe_w_up, moe_w_down, final_norm):
    bsz, seq, d = x.shape
    depth = w_in.shape[0]
    assert d == D_MODEL and N_META + seq <= T_PAD
    n = bsz * T_PAD
    assert n % TM == 0 and n % TM_FFN == 0 and seq % TB_FINAL == 0 and TB_FINAL % N_META == 0

    meta = jnp.broadcast_to(meta_tokens[None].astype(x.dtype), (bsz, N_META, d))
    pad = jnp.zeros((bsz, T_PAD - N_META - seq, d), x.dtype)
    hs = jnp.concatenate([meta, x, pad], axis=1).reshape(n, d)

    row3 = lambda a: a[:, None, :]
    w_in_b = w_in.astype(BF16)
    w_glu_b = s5_w_glu.astype(BF16)
    w_sp_b = s5_w_proj.astype(BF16)
    w_lp_b = lru_w_proj.astype(BF16)
    w_out_b = w_out.astype(BF16)
    w_ri = jnp.concatenate([_head_blockdiag(lru_w_rgate), _head_blockdiag(lru_w_igate)], axis=-1).astype(BF16)
    b_ri = jnp.concatenate([lru_b_rgate, lru_b_igate], axis=-1)
    neg_sp = -LRU_C * jax.nn.softplus(-lru_lambda)
    dense = (dense_w_gate, dense_w_up, dense_w_down)
    n_moe = router_w.shape[0]
    moe = [w.reshape((n_moe * N_EXPERTS,) + w.shape[2:]) for w in (moe_w_gate, moe_w_up, moe_w_down)]
    s5_ops = jax.vmap(_s5_prep)(s5_lambda_re, s5_lambda_im, s5_log_dt, s5_b_re, s5_b_im, s5_c_re, s5_c_im, s5_d)
    rw_pad = jnp.pad(router_w, ((0, 0), (0, 0), (0, LANES - N_EXPERTS)))
    rb_pad = jnp.pad(router_b, ((0, 0), (0, LANES - N_EXPERTS)), constant_values=MASKED_LOGIT)

    for layer in range(depth):
        u_parts, x_lru, g_lru, gates = _in_proj(hs, row3(mix_norm), w_in_b, row3(merge_bias), layer)
        ys_parts = _s5_scan(u_parts, s5_ops, layer, bsz)
        y_lru = _lru(x_lru, g_lru, lru_conv_w, row3(lru_conv_b), w_ri, row3(b_ri), row3(neg_sp), layer, bsz)
        j = layer // 2
        router = (rw_pad, row3(rb_pad), j) if layer % 2 == 1 else None
        res = _merge(hs, ys_parts, y_lru, gates, w_glu_b, row3(s5_b_glu), w_sp_b, w_lp_b, w_out_b,
                     row3(ffn_norm), layer, router)
        if layer % 2 == 0:
            hs, hn = res
            hs = _ffn(hn, hs, *dense, layer=j)
        else:
            hs, hn, route, route_t, counts = res
            pos, block_expert, n_used, pad_range = _moe_plan(route_t, counts, n)
            xs = _moe_dispatch(hn, pos, pad_range, n_used, block_expert.shape[0] * MOE_BLOCK)
            ys = _moe_ffn(xs, block_expert, n_used, *moe, first=j * N_EXPERTS)
            hs = _moe_combine(hs, route, ys, pos)

    return _final_norm(hs, final_norm[None, :], bsz, seq)
```

```python
import functools

import jax
import jax.numpy as jnp
from jax import lax
from jax.experimental import pallas as pl
from jax.experimental.pallas import tpu as pltpu

F32 = jnp.float32
BF16 = jnp.bfloat16

D_MODEL = 1024
N_META = 16
S5_WIDTH = 512
S5_GROUP = 16
S5_GROUPS = 32
S5_STATE = 64
LRU_WIDTH = 512
LRU_HEADS = 8
LRU_HEAD_DIM = 64
CONV_WIDTH = 4
LRU_C = 8.0
N_EXPERTS = 8
EPS = 1e-6

FOLD = 8
S5_PARTS = 4
PART_W = S5_WIDTH // S5_PARTS
PART_GROUPS = PART_W // S5_GROUP
PART_STATE = PART_GROUPS * S5_STATE
FOLD_W = FOLD * PART_W

T_PAD = 8256
ROWS = T_PAD // FOLD
TM = 688
TM_FFN = 1376
FF_CHUNK = 512
MOE_BLOCK = 512
GATHER_GROUP = 8
LRU_CHUNK = 1032
LRU_UNROLL = 3
TB_FINAL = 512
VMEM_LIMIT = 56 * 1024 * 1024
LANES = 128
SUBLANES = 8
S5_SEG = ROWS // SUBLANES
ROW_TILES = D_MODEL // LANES
MASKED_LOGIT = float("-inf")


def _dot(a, b):
    return jnp.dot(a, b, preferred_element_type=F32)


def _const_spec(block_shape, index_map):
    return pl.BlockSpec(block_shape, index_map, pipeline_mode=pl.Buffered(1))


def _rms(x, g):
    ms = jnp.mean(x * x, axis=-1, keepdims=True)
    return x * lax.rsqrt(ms + EPS) * g


def _rows_to_tiles(ref, x):
    rows = x.shape[0]
    for s in range(ROW_TILES):
        ref[pl.ds(s, rows, stride=ROW_TILES), :] = x[:, s * LANES:(s + 1) * LANES]


def _rows_from_tiles(ref, rows):
    return jnp.concatenate([ref[pl.ds(s, rows, stride=ROW_TILES), :] for s in range(ROW_TILES)], axis=-1)


def _in_proj_kernel(hs_ref, g_ref, w_ref, mb_ref, u_ref, xl_ref, gl_ref, gt_ref):
    hn = _rms(hs_ref[...], g_ref[0]).astype(BF16)
    u = _dot(hn, w_ref[0, :, 0:S5_WIDTH])
    for q in range(S5_PARTS):
        u_ref[q] = u[:, q * PART_W:(q + 1) * PART_W]
    o_x = S5_WIDTH
    o_g = o_x + LRU_WIDTH
    o_m = o_g + LRU_WIDTH
    xl_ref[...] = _dot(hn, w_ref[0, :, o_x:o_g]).astype(BF16)
    gl_ref[...] = _dot(hn, w_ref[0, :, o_g:o_m]).astype(BF16)
    z = _dot(hn, w_ref[0, :, o_m:]) + mb_ref[0]
    gt_ref[...] = jax.nn.sigmoid(z).astype(BF16)


def _in_proj(hs, mix_norm, w_in, merge_bias, layer):
    n = hs.shape[0]
    d_in = w_in.shape[-1]
    lay = lambda i: (layer, 0, 0)
    return pl.pallas_call(
        _in_proj_kernel,
        grid=(n // TM,),
        in_specs=[
            pl.BlockSpec((TM, D_MODEL), lambda i: (i, 0)),
            _const_spec((1, 1, D_MODEL), lay),
            _const_spec((1, D_MODEL, d_in), lay),
            _const_spec((1, 1, 2 * D_MODEL), lay),
        ],
        out_specs=[
            pl.BlockSpec((S5_PARTS, TM, PART_W), lambda i: (0, i, 0)),
            pl.BlockSpec((TM, LRU_WIDTH), lambda i: (i, 0)),
            pl.BlockSpec((TM, LRU_WIDTH), lambda i: (i, 0)),
            pl.BlockSpec((TM, 2 * D_MODEL), lambda i: (i, 0)),
        ],
        out_shape=[
            jax.ShapeDtypeStruct((S5_PARTS, n, PART_W), F32),
            jax.ShapeDtypeStruct((n, LRU_WIDTH), BF16),
            jax.ShapeDtypeStruct((n, LRU_WIDTH), BF16),
            jax.ShapeDtypeStruct((n, 2 * D_MODEL), BF16),
        ],
        compiler_params=pltpu.CompilerParams(
            dimension_semantics=("arbitrary",), vmem_limit_bytes=VMEM_LIMIT),
        name="in_proj",
    )(hs, mix_norm, w_in, merge_bias)


def _s5_prep(lam_re, lam_im, log_dt, b_re, b_im, c_re, c_im, d_skip):
    dt = jnp.exp(log_dt)[:, None]
    mag = jnp.exp(lam_re * dt)
    a_re = mag * jnp.cos(lam_im * dt)
    a_im = mag * jnp.sin(lam_im * dt)
    den = lam_re * lam_re + lam_im * lam_im
    num_re = a_re - 1.0
    coef_re = (num_re * lam_re + a_im * lam_im) / den
    coef_im = (a_im * lam_re - num_re * lam_im) / den
    bb_re = coef_re[..., None] * b_re - coef_im[..., None] * b_im
    bb_im = coef_re[..., None] * b_im + coef_im[..., None] * b_re

    def cmul(xr, xi, yr, yi):
        return xr * yr - xi * yi, xr * yi + xi * yr

    def powers(br, bi, n):
        pr, pi = [jnp.ones_like(br)], [jnp.zeros_like(bi)]
        for _ in range(n):
            r, i = cmul(pr[-1], pi[-1], br, bi)
            pr.append(r)
            pi.append(i)
        return jnp.stack(pr), jnp.stack(pi)

    p_re, p_im = powers(a_re, a_im, FOLD)

    def per_part(x):
        lead = x.shape[:-3]
        xp = x.reshape(lead + (S5_PARTS, PART_GROUPS) + x.shape[-2:])
        return jnp.moveaxis(xp, len(lead), 0)

    rev_re = jnp.stack([p_re[FOLD - 1 - j] for j in range(FOLD)])
    rev_im = jnp.stack([p_im[FOLD - 1 - j] for j in range(FOLD)])
    wr, wi = cmul(rev_re[..., None], rev_im[..., None], bb_re[None], bb_im[None])
    w_ri = jnp.swapaxes(jnp.stack([wr, wi], axis=1), -1, -2)
    xq = jnp.transpose(per_part(w_ri), (0, 1, 3, 4, 2, 5)).reshape(S5_PARTS, FOLD_W, 2 * S5_STATE)

    ca_re, ca_im = cmul(c_re[None], c_im[None], p_re[:, :, None, :], p_im[:, :, None, :])
    bt_re = jnp.swapaxes(bb_re, -1, -2)[None, :, :, None, :]
    bt_im = jnp.swapaxes(bb_im, -1, -2)[None, :, :, None, :]
    taps = jnp.sum(ca_re[:FOLD, :, None] * bt_re - ca_im[:FOLD, :, None] * bt_im, axis=-1)
    skip = d_skip.reshape(S5_GROUPS, S5_GROUP)
    taps = taps.at[0].add(skip[:, :, None] * jnp.eye(S5_GROUP, dtype=F32)[None])
    rc = jnp.transpose(per_part(taps), (0, 2, 3, 1, 4)).reshape(S5_PARTS, PART_W, FOLD * S5_GROUP)

    v_ri = jnp.swapaxes(jnp.stack([ca_re[1:], -ca_im[1:]], axis=0), -1, -2)
    vc = jnp.transpose(per_part(v_ri), (0, 1, 3, 4, 2, 5)).reshape(S5_PARTS, 2 * PART_STATE, FOLD * S5_GROUP)

    def part_vec(x):
        lead = x.shape[:-2]
        xp = x.reshape(lead + (S5_PARTS, PART_STATE))
        return jnp.moveaxis(xp, -2, 0)

    row_re, row_im = p_re[FOLD], p_im[FOLD]
    seg_re, seg_im = jnp.ones_like(row_re), jnp.zeros_like(row_im)
    for bit in bin(S5_SEG)[2:]:
        seg_re, seg_im = cmul(seg_re, seg_im, seg_re, seg_im)
        if bit == '1':
            seg_re, seg_im = cmul(seg_re, seg_im, row_re, row_im)
    decay = jnp.stack([jnp.stack([part_vec(row_re), part_vec(row_im)], axis=1),
                       jnp.stack([part_vec(seg_re), part_vec(seg_im)], axis=1)], axis=1)
    return xq, rc, vc, decay[:, :, :, None, :]


def _iota2(shape):
    return (lax.broadcasted_iota(jnp.int32, shape, 0), lax.broadcasted_iota(jnp.int32, shape, 1))


def _s5_expand(xq, rc, vc, w1_s, tv_s):
    ps = PART_STATE
    lg_state, lg_group, lg_part = (v.bit_length() - 1 for v in (S5_STATE, S5_GROUP, PART_W))
    lg_pg = PART_GROUPS.bit_length() - 1
    grp = PART_GROUPS - 1
    one_hot = lambda m: jnp.where(m, 1.0, 0.0).astype(BF16)
    r, c = _iota2((2 * S5_STATE, 2 * ps))
    e1 = one_hot(((r >> lg_state) == (c >> (lg_state + lg_pg))) & ((r & (S5_STATE - 1)) == (c & (S5_STATE - 1))))
    r, c = _iota2((FOLD * S5_GROUP, FOLD_W))
    e2 = one_hot(((r >> lg_group) == (c >> lg_part)) & ((r & (S5_GROUP - 1)) == (c & (S5_GROUP - 1))))
    r, c = _iota2((FOLD_W, 2 * ps))
    m1 = ((r >> lg_group) & grp) == ((c >> lg_state) & grp)
    w1_s[...] = jnp.where(m1, _dot(xq.astype(BF16), e1), 0.0).astype(BF16)
    r, c = _iota2((PART_W, FOLD_W))
    m2 = (r >> lg_group) == ((c >> lg_group) & grp)
    r0 = jnp.where(m2, _dot(rc.astype(BF16), e2), 0.0).astype(BF16)
    for j in range(FOLD):
        if j == 0:
            blk = r0
        else:
            blk = jnp.concatenate([jnp.zeros((PART_W, j * PART_W), BF16), r0[:, :FOLD_W - j * PART_W]], axis=1)
        tv_s[j * PART_W:(j + 1) * PART_W, :] = blk
    r, c = _iota2((2 * ps, FOLD_W))
    m3 = ((r >> lg_state) & grp) == ((c >> lg_group) & grp)
    tv_s[FOLD_W:, :] = jnp.where(m3, _dot(vc.astype(BF16), e2), 0.0).astype(BF16)


def _s5_kernel(u_ref, xq_ref, rc_ref, vc_ref, dec_ref, y_ref, w1_s, tv_s, up_ref, f_ref, hp_ref):
    ps = PART_STATE
    tstride = S5_SEG * FOLD
    cols = lambda j: slice(j * PART_W, (j + 1) * PART_W)

    @pl.when(pl.program_id(1) == 0)
    def _():
        _s5_expand(xq_ref[0, 0], rc_ref[0, 0], vc_ref[0, 0], w1_s, tv_s)

    def fold_body(i, carry):
        r0 = pl.multiple_of(i * SUBLANES, SUBLANES)
        for j in range(FOLD):
            up_ref[pl.ds(r0, SUBLANES), cols(j)] = u_ref[0, 0, pl.ds(i * FOLD + j, SUBLANES, stride=tstride), :]
        return carry

    lax.fori_loop(0, S5_SEG, fold_body, 0)
    u = up_ref[...].astype(BF16)
    f_ref[...] = _dot(u, w1_s[...])
    ar = jnp.broadcast_to(dec_ref[0, 0, 0, 0], (SUBLANES, ps))
    ai = jnp.broadcast_to(dec_ref[0, 0, 0, 1], (SUBLANES, ps))

    def step(i, hr, hi):
        r0 = pl.multiple_of(i * SUBLANES, SUBLANES)
        return (ar * hr - ai * hi + f_ref[pl.ds(r0, SUBLANES), :ps],
                ar * hi + ai * hr + f_ref[pl.ds(r0, SUBLANES), ps:])

    zero = jnp.zeros((SUBLANES, ps), F32)
    er, ei = lax.fori_loop(0, S5_SEG, lambda i, c: step(i, *c), (zero, zero))
    sr = dec_ref[0, 0, 1, 0]
    si = dec_ref[0, 0, 1, 1]
    row = lax.broadcasted_iota(jnp.int32, (SUBLANES, ps), 0)
    nr, ni = zero, zero
    for sgm in range(SUBLANES - 1):
        lr = er + sr * nr - si * ni
        li = ei + sr * ni + si * nr
        nr = nr + jnp.where(row == sgm + 1, pltpu.roll(lr, 1, axis=0), 0.0)
        ni = ni + jnp.where(row == sgm + 1, pltpu.roll(li, 1, axis=0), 0.0)

    def state_body(i, carry):
        hr, hi = carry
        r0 = pl.multiple_of(i * SUBLANES, SUBLANES)
        hp_ref[pl.ds(r0, SUBLANES), :ps] = hr
        hp_ref[pl.ds(r0, SUBLANES), ps:] = hi
        return step(i, hr, hi)

    lax.fori_loop(0, S5_SEG, state_body, (nr, ni))
    hp = hp_ref[...].astype(BF16)
    wide = 2 * PART_W
    for c0 in range(0, FOLD_W, wide):
        y = _dot(u[:, :c0 + wide], tv_s[:c0 + wide, c0:c0 + wide]) + _dot(hp, tv_s[FOLD_W:, c0:c0 + wide])
        f_ref[:, c0:c0 + wide] = jax.nn.gelu(y)

    def unfold_body(i, carry):
        r0 = pl.multiple_of(i * SUBLANES, SUBLANES)
        for j in range(FOLD):
            y_ref[0, 0, pl.ds(i * FOLD + j, SUBLANES, stride=tstride), :] = f_ref[pl.ds(r0, SUBLANES), cols(j)]
        return carry

    lax.fori_loop(0, S5_SEG, unfold_body, 0)


def _s5_scan(u_parts, ops, layer, bsz):
    xq, rc, vc, decay = ops
    n = u_parts.shape[1]
    u4 = u_parts.reshape(S5_PARTS, bsz, T_PAD, PART_W)
    lay4 = lambda q, b: (layer, q, 0, 0)
    y4 = pl.pallas_call(
        _s5_kernel,
        grid=(S5_PARTS, bsz),
        in_specs=[
            pl.BlockSpec((1, 1, T_PAD, PART_W), lambda q, b: (q, b, 0, 0)),
            pl.BlockSpec((1, 1, FOLD_W, 2 * S5_STATE), lay4),
            pl.BlockSpec((1, 1, PART_W, FOLD * S5_GROUP), lay4),
            pl.BlockSpec((1, 1, 2 * PART_STATE, FOLD * S5_GROUP), lay4),
            pl.BlockSpec((1, 1, 2, 2, 1, PART_STATE), lambda q, b: (layer, q, 0, 0, 0, 0)),
        ],
        out_specs=pl.BlockSpec((1, 1, T_PAD, PART_W), lambda q, b: (q, b, 0, 0)),
        out_shape=jax.ShapeDtypeStruct((S5_PARTS, bsz, T_PAD, PART_W), F32),
        scratch_shapes=[
            pltpu.VMEM((FOLD_W, 2 * PART_STATE), BF16),
            pltpu.VMEM((FOLD_W + 2 * PART_STATE, FOLD_W), BF16),
            pltpu.VMEM((ROWS, FOLD_W), F32),
            pltpu.VMEM((ROWS, 2 * PART_STATE), F32),
            pltpu.VMEM((ROWS, 2 * PART_STATE), F32),
        ],
        compiler_params=pltpu.CompilerParams(
            dimension_semantics=("arbitrary", "arbitrary"), vmem_limit_bytes=VMEM_LIMIT),
        name="s5_scan",
    )(u4, xq, rc, vc, decay)
    return y4.reshape(S5_PARTS, n, PART_W)


def _lru_kernel(x_ref, g_ref, cw_ref, cb_ref, wri_ref, bri_ref, nsp_ref, o_ref,
                xs_ref, gs_ref, xc_ref, gp_ref, z_ref, a_ref, b_ref, os_ref, h_ref):
    tc = LRU_CHUNK
    c = LRU_WIDTH
    seg = tc // SUBLANES
    nq = c // LANES
    lanes = lambda q: slice(q * LANES, (q + 1) * LANES)
    halo = SUBLANES

    @pl.when(pl.program_id(1) == 0)
    def _():
        xs_ref[:, 0:halo, :] = jnp.zeros((nq, halo, LANES), F32)
        h_ref[...] = jnp.zeros((1, c), F32)

    x = x_ref[0].astype(F32)
    g = g_ref[0].astype(F32)
    for q in range(nq):
        xs_ref[q, halo:, :] = x[:, lanes(q)]
        gs_ref[q] = g[:, lanes(q)]
    taps = [[cw_ref[0, k:k + 1, lanes(q)] for k in range(CONV_WIDTH)] for q in range(nq)]
    bias = [cb_ref[0, :, lanes(q)] for q in range(nq)]

    def conv_body(i, carry):
        r0 = pl.multiple_of(i * SUBLANES, SUBLANES)
        for q in range(nq):
            acc = bias[q]
            for k in range(CONV_WIDTH):
                first = halo - (CONV_WIDTH - 1) + k + i
                acc = acc + taps[q][k] * xs_ref[q, pl.ds(first, SUBLANES, stride=seg), :]
            xc_ref[pl.ds(r0, SUBLANES), lanes(q)] = acc
            gp_ref[pl.ds(r0, SUBLANES), lanes(q)] = jax.nn.gelu(gs_ref[q, pl.ds(i, SUBLANES, stride=seg), :])
        return carry

    lax.fori_loop(0, seg, conv_body, 0, unroll=LRU_UNROLL)
    for q in range(nq):
        xs_ref[q, 0:halo, :] = xs_ref[q, tc:tc + halo, :]

    z_ref[...] = _dot(xc_ref[...].astype(BF16), wri_ref[0])
    b_r = jnp.broadcast_to(bri_ref[0, :, :c], (SUBLANES, c))
    b_i = jnp.broadcast_to(bri_ref[0, :, c:], (SUBLANES, c))
    nsp = jnp.broadcast_to(nsp_ref[0], (SUBLANES, c))

    def scan_body(i, carry):
        h, p = carry
        r0 = pl.multiple_of(i * SUBLANES, SUBLANES)
        a = jnp.exp(jax.nn.sigmoid(z_ref[pl.ds(r0, SUBLANES), :c] + b_r) * nsp)
        gated = jax.nn.sigmoid(z_ref[pl.ds(r0, SUBLANES), c:] + b_i) * xc_ref[pl.ds(r0, SUBLANES), :]
        h = a * h + jnp.sqrt(1.0 - a * a) * gated
        p = p * a
        b_ref[pl.ds(r0, SUBLANES), :] = h
        a_ref[pl.ds(r0, SUBLANES), :] = p
        return h, p

    h_end, p_end = lax.fori_loop(0, seg, scan_body, (jnp.zeros((SUBLANES, c), F32), jnp.ones((SUBLANES, c), F32)),
                                 unroll=LRU_UNROLL)
    row = lax.broadcasted_iota(jnp.int32, (SUBLANES, c), 0)
    enter = jnp.where(row == 0, h_ref[...], 0.0)
    for sgm in range(SUBLANES - 1):
        leave = h_end + p_end * enter
        enter = enter + jnp.where(row == sgm + 1, pltpu.roll(leave, 1, axis=0), 0.0)
    h_ref[...] = (h_end + p_end * enter)[SUBLANES - 1:SUBLANES]

    def out_body(i, carry):
        r0 = pl.multiple_of(i * SUBLANES, SUBLANES)
        h = b_ref[pl.ds(r0, SUBLANES), :] + a_ref[pl.ds(r0, SUBLANES), :] * enter
        y = h * gp_ref[pl.ds(r0, SUBLANES), :]
        for q in range(nq):
            os_ref[q, pl.ds(i, SUBLANES, stride=seg), :] = y[:, lanes(q)]
        return carry

    lax.fori_loop(0, seg, out_body, 0)
    o_ref[0] = jnp.concatenate([os_ref[q] for q in range(nq)], axis=-1).astype(BF16)


def _lru(x_lru, g_lru, conv_w, conv_b, w_ri, b_ri, neg_sp, layer, bsz):
    n = x_lru.shape[0]
    c = LRU_WIDTH
    x3 = x_lru.reshape(bsz, T_PAD, c)
    g3 = g_lru.reshape(bsz, T_PAD, c)
    lay = lambda b, t: (layer, 0, 0)
    out = pl.pallas_call(
        _lru_kernel,
        grid=(bsz, T_PAD // LRU_CHUNK),
        in_specs=[
            pl.BlockSpec((1, LRU_CHUNK, c), lambda b, t: (b, t, 0)),
            pl.BlockSpec((1, LRU_CHUNK, c), lambda b, t: (b, t, 0)),
            _const_spec((1, CONV_WIDTH, c), lay),
            _const_spec((1, 1, c), lay),
            _const_spec((1, c, 2 * c), lay),
            _const_spec((1, 1, 2 * c), lay),
            _const_spec((1, 1, c), lay),
        ],
        out_specs=pl.BlockSpec((1, LRU_CHUNK, c), lambda b, t: (b, t, 0)),
        out_shape=jax.ShapeDtypeStruct((bsz, T_PAD, c), BF16),
        scratch_shapes=[
            pltpu.VMEM((c // LANES, LRU_CHUNK + SUBLANES, LANES), F32),
            pltpu.VMEM((c // LANES, LRU_CHUNK, LANES), F32),
            pltpu.VMEM((LRU_CHUNK, c), F32),
            pltpu.VMEM((LRU_CHUNK, c), F32),
            pltpu.VMEM((LRU_CHUNK, 2 * c), F32),
            pltpu.VMEM((LRU_CHUNK, c), F32),
            pltpu.VMEM((LRU_CHUNK, c), F32),
            pltpu.VMEM((c // LANES, LRU_CHUNK, LANES), F32),
            pltpu.VMEM((1, c), F32),
        ],
        compiler_params=pltpu.CompilerParams(
            dimension_semantics=("arbitrary", "arbitrary"), vmem_limit_bytes=VMEM_LIMIT),
        name="rglru",
    )(x3, g3, conv_w, conv_b, w_ri, b_ri, neg_sp)
    return out.reshape(n, c)


def _merge_kernel(hs_ref, ys_ref, yl_ref, gt_ref, wglu_ref, bglu_ref, wsp_ref, wlp_ref, wout_ref, g_ref,
                  *rest, with_router):
    if with_router:
        rw_ref, rb_ref, hs_out_ref, hn_ref, rt_ref, rtt_ref, cnt_ref, run_ref, tri_ref = rest
    else:
        hs_out_ref, hn_ref = rest
    ys = jnp.concatenate([ys_ref[q] for q in range(S5_PARTS)], axis=-1)
    glu = ys * jax.nn.sigmoid(_dot(ys.astype(BF16), wglu_ref[0]) + bglu_ref[0])
    y_a = _dot(glu.astype(BF16), wsp_ref[0])
    y_b = _dot(yl_ref[...], wlp_ref[0])
    y = gt_ref[:, :D_MODEL].astype(F32) * y_a + gt_ref[:, D_MODEL:].astype(F32) * y_b
    hs = hs_ref[...] + _dot(y.astype(BF16), wout_ref[0])
    hs_out_ref[...] = hs
    hn = _rms(hs, g_ref[0])
    if not with_router:
        hn_ref[...] = hn.astype(BF16)
    else:
        _rows_to_tiles(hn_ref, hn)
        logits = _dot(hn.astype(BF16), rw_ref[0].astype(BF16)) + rb_ref[0]
        lane = lax.broadcasted_iota(jnp.int32, logits.shape, 1).astype(F32)
        m1 = jnp.max(logits, axis=-1, keepdims=True)
        i1 = jnp.min(jnp.where(logits == m1, lane, float(LANES)), axis=-1, keepdims=True)
        rest_l = jnp.where(lane == i1, MASKED_LOGIT, logits)
        m2 = jnp.max(rest_l, axis=-1, keepdims=True)
        i2 = jnp.min(jnp.where(rest_l == m2, lane, float(LANES)), axis=-1, keepdims=True)
        e2 = jnp.exp(m2 - m1)
        g1 = 1.0 / (1.0 + e2)
        g2 = e2 / (1.0 + e2)
        @pl.when(pl.program_id(0) == 0)
        def _():
            run_ref[...] = jnp.zeros_like(run_ref)
            r, c = _iota2(tri_ref.shape)
            tri_ref[...] = jnp.where(c < r, 1.0, 0.0).astype(BF16)

        first = lane == i1
        second = lane == i2
        picked = jnp.where(first | second, 1.0, 0.0)
        before = _dot(tri_ref[...], picked.astype(BF16)) + run_ref[...]
        rank1 = jnp.sum(jnp.where(first, before, 0.0), axis=-1, keepdims=True)
        rank2 = jnp.sum(jnp.where(second, before, 0.0), axis=-1, keepdims=True)
        run_ref[...] += jnp.sum(picked, axis=0, keepdims=True)
        cnt_ref[...] = jnp.broadcast_to(run_ref[...], cnt_ref.shape)
        rt = (jnp.where(lane == 0.0, i1, 0.0) + jnp.where(lane == 1.0, i2, 0.0)
              + jnp.where(lane == 2.0, g1, 0.0) + jnp.where(lane == 3.0, g2, 0.0)
              + jnp.where(lane == 4.0, rank1, 0.0) + jnp.where(lane == 5.0, rank2, 0.0))
        rt_ref[...] = rt
        r, c = _iota2((SUBLANES, LANES))
        pick = jnp.where(r == c, 1.0, 0.0).astype(BF16)
        hi = rt.astype(BF16)
        mid = (rt - hi.astype(F32)).astype(BF16)
        lo = (rt - hi.astype(F32) - mid.astype(F32)).astype(BF16)
        nt = (((1,), (1,)), ((), ()))
        rtt_ref[0] = (lax.dot_general(pick, hi, nt, preferred_element_type=F32)
                      + lax.dot_general(pick, mid, nt, preferred_element_type=F32)
                      + lax.dot_general(pick, lo, nt, preferred_element_type=F32))


def _merge(hs, ys_parts, y_lru, gates, w_glu, b_glu, w_sp, w_lp, w_out, ffn_norm, layer, router=None):
    n = hs.shape[0]
    tm = TM
    lay = lambda i: (layer, 0, 0)
    in_specs = [
        pl.BlockSpec((tm, D_MODEL), lambda i: (i, 0)),
        pl.BlockSpec((S5_PARTS, tm, PART_W), lambda i: (0, i, 0)),
        pl.BlockSpec((tm, LRU_WIDTH), lambda i: (i, 0)),
        pl.BlockSpec((tm, 2 * D_MODEL), lambda i: (i, 0)),
        _const_spec((1, S5_WIDTH, S5_WIDTH), lay),
        _const_spec((1, 1, S5_WIDTH), lay),
        _const_spec((1, S5_WIDTH, D_MODEL), lay),
        _const_spec((1, LRU_WIDTH, D_MODEL), lay),
        _const_spec((1, D_MODEL, D_MODEL), lay),
        _const_spec((1, 1, D_MODEL), lay),
    ]
    out_specs = [pl.BlockSpec((tm, D_MODEL), lambda i: (i, 0))]
    out_shape = [jax.ShapeDtypeStruct((n, D_MODEL), F32)]
    if router is None:
        out_specs.append(pl.BlockSpec((tm, D_MODEL), lambda i: (i, 0)))
        out_shape.append(jax.ShapeDtypeStruct((n, D_MODEL), BF16))
    else:
        out_specs.append(pl.BlockSpec((tm * ROW_TILES, LANES), lambda i: (i, 0)))
        out_shape.append(jax.ShapeDtypeStruct((n * ROW_TILES, LANES), F32))
    args = [hs, ys_parts, y_lru, gates, w_glu, b_glu, w_sp, w_lp, w_out, ffn_norm]
    if router is not None:
        rw, rb, j = router
        in_specs += [_const_spec((1, D_MODEL, LANES), lambda i: (j, 0, 0)),
                     _const_spec((1, 1, LANES), lambda i: (j, 0, 0))]
        out_specs += [pl.BlockSpec((tm, LANES), lambda i: (i, 0)),
                      pl.BlockSpec((1, SUBLANES, tm), lambda i: (i, 0, 0)),
                      pl.BlockSpec((SUBLANES, LANES), lambda i: (0, 0))]
        out_shape += [jax.ShapeDtypeStruct((n, LANES), F32),
                      jax.ShapeDtypeStruct((n // tm, SUBLANES, tm), F32),
                      jax.ShapeDtypeStruct((SUBLANES, LANES), F32)]
        args += [rw, rb]
    return pl.pallas_call(
        functools.partial(_merge_kernel, with_router=router is not None),
        grid=(n // tm,),
        in_specs=in_specs,
        out_specs=out_specs,
        out_shape=out_shape,
        scratch_shapes=[pltpu.VMEM((1, LANES), F32), pltpu.VMEM((tm, tm), BF16)] if router is not None else [],
        compiler_params=pltpu.CompilerParams(
            dimension_semantics=("arbitrary",), vmem_limit_bytes=VMEM_LIMIT),
        name="merge_router" if router is not None else "merge",
    )(*args)


def _ffn_kernel(x_ref, hs_ref, wg_ref, wu_ref, wd_ref, o_ref, acc_ref):
    c = pl.program_id(1)

    @pl.when(c == 0)
    def _():
        acc_ref[...] = jnp.zeros_like(acc_ref)

    x = x_ref[...]
    g = _dot(x, wg_ref[0].astype(BF16))
    h = g * jax.nn.sigmoid(g) * _dot(x, wu_ref[0].astype(BF16))
    acc_ref[...] += _dot(h.astype(BF16), wd_ref[0].astype(BF16))

    @pl.when(c == pl.num_programs(1) - 1)
    def _():
        o_ref[...] = hs_ref[...] + acc_ref[...]


def _ffn(hn, hs, w_gate, w_up, w_down, layer):
    n = hn.shape[0]
    ff = w_gate.shape[-1]
    return pl.pallas_call(
        _ffn_kernel,
        grid=(n // TM_FFN, ff // FF_CHUNK),
        in_specs=[
            pl.BlockSpec((TM_FFN, D_MODEL), lambda i, c: (i, 0)),
            pl.BlockSpec((TM_FFN, D_MODEL), lambda i, c: (i, 0)),
            pl.BlockSpec((1, D_MODEL, FF_CHUNK), lambda i, c: (layer, 0, c)),
            pl.BlockSpec((1, D_MODEL, FF_CHUNK), lambda i, c: (layer, 0, c)),
            pl.BlockSpec((1, FF_CHUNK, D_MODEL), lambda i, c: (layer, c, 0)),
        ],
        out_specs=pl.BlockSpec((TM_FFN, D_MODEL), lambda i, c: (i, 0)),
        out_shape=jax.ShapeDtypeStruct((n, D_MODEL), F32),
        scratch_shapes=[pltpu.VMEM((TM_FFN, D_MODEL), F32)],
        compiler_params=pltpu.CompilerParams(
            dimension_semantics=("arbitrary", "arbitrary"), vmem_limit_bytes=VMEM_LIMIT),
        name="dense_ffn",
    )(hn, hs, w_gate, w_up, w_down)


def _moe_plan(route_t, counts_f, n):
    n_blocks = -(-2 * n // MOE_BLOCK) + N_EXPERTS
    e = jnp.stack([route_t[:, 0, :], route_t[:, 1, :]]).astype(jnp.int32)
    rank = jnp.stack([route_t[:, 4, :], route_t[:, 5, :]]).astype(jnp.int32)
    counts = counts_f[0, :N_EXPERTS].astype(jnp.int32)
    padded = ((counts + MOE_BLOCK - 1) // MOE_BLOCK) * MOE_BLOCK
    cum_pad = jnp.cumsum(padded)
    pad_start = cum_pad - padded
    pos = rank
    for x in range(N_EXPERTS):
        pos = pos + jnp.where(e == x, pad_start[x], 0)
    block_start = jnp.arange(n_blocks, dtype=jnp.int32) * MOE_BLOCK
    block_expert = jnp.minimum(jnp.sum((block_start[:, None] >= cum_pad[None, :]).astype(jnp.int32), axis=1),
                               N_EXPERTS - 1)
    n_used = (cum_pad[-1] // MOE_BLOCK).astype(jnp.int32).reshape(1)
    pad_range = jnp.stack([pad_start + counts, cum_pad], axis=1).reshape(2 * N_EXPERTS).astype(jnp.int32)
    return pos.reshape(2 * n), block_expert, n_used, pad_range


def _tile(ref, index):
    return ref.at[pl.ds(pl.multiple_of(index * ROW_TILES, ROW_TILES), ROW_TILES)]


def _tile_gather(src_hbm, dst, sem, rows, index_of):
    def body(grp, carry):
        r0 = grp * GATHER_GROUP
        index = [index_of(r0 + j) for j in range(GATHER_GROUP)]
        for j in range(GATHER_GROUP):
            pltpu.make_async_copy(_tile(src_hbm, index[j]), _tile(dst, r0 + j), sem).start(priority=j % 2)
        return carry
    lax.fori_loop(0, rows // GATHER_GROUP, body, 0)


def _tile_gather_wait(src_hbm, dst, sem, rows):
    pltpu.make_async_copy(src_hbm.at[pl.ds(0, rows * ROW_TILES)], dst, sem).wait()


def _moe_dispatch_kernel(pos_ref, pad_ref, nu_ref, x_ref, xs_hbm, zero_ref, sem):
    t = pl.program_id(0)
    n = pl.num_programs(0) * TM
    block_rows = MOE_BLOCK * ROW_TILES
    n_blocks = xs_hbm.shape[0] // block_rows

    @pl.when(t == 0)
    def _():
        zero_ref[...] = jnp.zeros_like(zero_ref)
        zero_tile = zero_ref.at[pl.ds(0, ROW_TILES)]
        for e in range(N_EXPERTS):
            def fill(slot, carry):
                pltpu.make_async_copy(zero_tile, _tile(xs_hbm, slot), sem.at[1]).start()
                return carry

            def fill_wait(slot, carry):
                pltpu.make_async_copy(zero_tile, _tile(xs_hbm, slot), sem.at[1]).wait()
                return carry
            lax.fori_loop(pad_ref[2 * e], pad_ref[2 * e + 1], fill, 0)
            lax.fori_loop(pad_ref[2 * e], pad_ref[2 * e + 1], fill_wait, 0)

        def block_of(blk):
            return xs_hbm.at[pl.ds(pl.multiple_of(blk * block_rows, block_rows), block_rows)]

        def fill_block(blk, carry):
            pltpu.make_async_copy(zero_ref, block_of(blk), sem.at[1]).start()
            return carry

        def fill_block_wait(blk, carry):
            pltpu.make_async_copy(zero_ref, block_of(blk), sem.at[1]).wait()
            return carry
        lax.fori_loop(nu_ref[0], n_blocks, fill_block, 0)
        lax.fori_loop(nu_ref[0], n_blocks, fill_block_wait, 0)

    for k in range(2):
        def put(grp, carry):
            r0 = grp * GATHER_GROUP
            slot = [pos_ref[k * n + t * TM + r0 + j] for j in range(GATHER_GROUP)]
            for j in range(GATHER_GROUP):
                pltpu.make_async_copy(_tile(x_ref, r0 + j), _tile(xs_hbm, slot[j]), sem.at[0]).start(priority=j % 2)
            return carry
        lax.fori_loop(0, TM // GATHER_GROUP, put, 0)
    for k in range(2):
        pltpu.make_async_copy(x_ref, xs_hbm.at[pl.ds(0, TM * ROW_TILES)], sem.at[0]).wait()


def _moe_dispatch(hn_tiles, pos, pad_range, n_used, n_slots):
    n = hn_tiles.shape[0] // ROW_TILES
    return pl.pallas_call(
        _moe_dispatch_kernel,
        grid_spec=pltpu.PrefetchScalarGridSpec(
            num_scalar_prefetch=3,
            grid=(n // TM,),
            in_specs=[pl.BlockSpec((TM * ROW_TILES, LANES), lambda t, ps, pr, nu: (t, 0))],
            out_specs=pl.BlockSpec(memory_space=pl.ANY),
            scratch_shapes=[pltpu.VMEM((MOE_BLOCK * ROW_TILES, LANES), F32),
                            pltpu.SemaphoreType.DMA((2,))],
        ),
        out_shape=jax.ShapeDtypeStruct((n_slots * ROW_TILES, LANES), F32),
        compiler_params=pltpu.CompilerParams(dimension_semantics=("arbitrary",)),
        name="moe_dispatch",
    )(pos, pad_range, n_used, hn_tiles)


def _moe_ffn_kernel(be_ref, nu_ref, x_ref, wg_ref, wu_ref, wd_ref, y_ref):
    i = pl.program_id(0)

    @pl.when(i < nu_ref[0])
    def _():
        x = _rows_from_tiles(x_ref, MOE_BLOCK).astype(BF16)
        g = _dot(x, wg_ref[0].astype(BF16))
        h = g * jax.nn.sigmoid(g) * _dot(x, wu_ref[0].astype(BF16))
        _rows_to_tiles(y_ref, _dot(h.astype(BF16), wd_ref[0].astype(BF16)))

    @pl.when(i >= nu_ref[0])
    def _():
        y_ref[...] = jnp.zeros_like(y_ref)


def _moe_ffn(xs_tiles, block_expert, n_used, w_gate, w_up, w_down, first):
    n_blocks = block_expert.shape[0]
    ff = w_gate.shape[-1]
    wmap = lambda i, be, nu: (first + be[i], 0, 0)
    return pl.pallas_call(
        _moe_ffn_kernel,
        grid_spec=pltpu.PrefetchScalarGridSpec(
            num_scalar_prefetch=2,
            grid=(n_blocks,),
            in_specs=[
                pl.BlockSpec((MOE_BLOCK * ROW_TILES, LANES), lambda i, be, nu: (jnp.maximum(jnp.minimum(i, nu[0] - 1), 0), 0)),
                pl.BlockSpec((1, D_MODEL, ff), wmap),
                pl.BlockSpec((1, D_MODEL, ff), wmap),
                pl.BlockSpec((1, ff, D_MODEL), wmap),
            ],
            out_specs=pl.BlockSpec((MOE_BLOCK * ROW_TILES, LANES), lambda i, be, nu: (i, 0)),
        ),
        out_shape=jax.ShapeDtypeStruct((n_blocks * MOE_BLOCK * ROW_TILES, LANES), F32),
        compiler_params=pltpu.CompilerParams(
            dimension_semantics=("arbitrary",), vmem_limit_bytes=VMEM_LIMIT),
        name="moe_ffn",
    )(block_expert, n_used, xs_tiles, w_gate, w_up, w_down)


def _moe_combine_kernel(pos_ref, hs_ref, rt_ref, ys_hbm, o_ref, ybuf, sem):
    i = pl.program_id(0)
    nt = pl.num_programs(0)
    slot = i % 2

    def start(t, s):
        for k in range(2):
            _tile_gather(ys_hbm, ybuf.at[s, k], sem.at[s], TM, lambda r: pos_ref[k * (nt * TM) + t * TM + r])

    @pl.when(i == 0)
    def _():
        start(0, 0)

    @pl.when(i + 1 < nt)
    def _():
        start(i + 1, 1 - slot)

    for k in range(2):
        _tile_gather_wait(ys_hbm, ybuf.at[slot, k], sem.at[slot], TM)
    rt = rt_ref[...]
    lane = lax.broadcasted_iota(jnp.int32, rt.shape, 1)
    g1 = jnp.sum(jnp.where(lane == 2, rt, 0.0), axis=-1, keepdims=True)
    g2 = jnp.sum(jnp.where(lane == 3, rt, 0.0), axis=-1, keepdims=True)
    o_ref[...] = (hs_ref[...] + g1 * _rows_from_tiles(ybuf.at[slot, 0], TM)
                  + g2 * _rows_from_tiles(ybuf.at[slot, 1], TM))


def _moe_combine(hs, route, ys_tiles, pos):
    n = hs.shape[0]
    return pl.pallas_call(
        _moe_combine_kernel,
        grid_spec=pltpu.PrefetchScalarGridSpec(
            num_scalar_prefetch=1,
            grid=(n // TM,),
            in_specs=[
                pl.BlockSpec((TM, D_MODEL), lambda i, p: (i, 0)),
                pl.BlockSpec((TM, LANES), lambda i, p: (i, 0)),
                pl.BlockSpec(memory_space=pl.ANY),
            ],
            out_specs=pl.BlockSpec((TM, D_MODEL), lambda i, p: (i, 0)),
            scratch_shapes=[pltpu.VMEM((2, 2, TM * ROW_TILES, LANES), F32),
                            pltpu.SemaphoreType.DMA((2,))],
        ),
        out_shape=jax.ShapeDtypeStruct((n, D_MODEL), F32),
        compiler_params=pltpu.CompilerParams(
            dimension_semantics=("arbitrary",), vmem_limit_bytes=VMEM_LIMIT),
        name="moe_combine",
    )(pos, hs, route, ys_tiles)


def _final_kernel(a_ref, b_ref, g_ref, o_ref):
    tb = a_ref.shape[1]
    o_ref[0, :tb - N_META] = _rms(a_ref[0, N_META:], g_ref[...])
    o_ref[0, tb - N_META:] = _rms(b_ref[0], g_ref[...])


def _final_norm(hs, g, bsz, seq):
    hs3 = hs.reshape(bsz, T_PAD, D_MODEL)
    return pl.pallas_call(
        _final_kernel,
        grid=(bsz, seq // TB_FINAL),
        in_specs=[pl.BlockSpec((1, TB_FINAL, D_MODEL), lambda b, i: (b, i, 0)),
                  pl.BlockSpec((1, N_META, D_MODEL), lambda b, i: (b, (i + 1) * (TB_FINAL // N_META), 0)),
                  _const_spec((1, D_MODEL), lambda b, i: (0, 0))],
        out_specs=pl.BlockSpec((1, TB_FINAL, D_MODEL), lambda b, i: (b, i, 0)),
        out_shape=jax.ShapeDtypeStruct((bsz, seq, D_MODEL), F32),
        compiler_params=pltpu.CompilerParams(
            dimension_semantics=("arbitrary", "arbitrary"), vmem_limit_bytes=VMEM_LIMIT),
        name="final_norm",
    )(hs3, hs3, g)


def _head_blockdiag(w):
    eye = jnp.eye(LRU_HEADS, dtype=w.dtype)
    out = jnp.einsum('lnhk,nm->lnhmk', w, eye)
    return out.reshape(w.shape[0], LRU_WIDTH, LRU_WIDTH)


def kernel(x, meta_tokens, mix_norm, w_in, merge_bias, s5_lambda_re, s5_lambda_im, s5_log_dt, s5_b_re, s5_b_im, s5_c_re, s5_c_im, s5_d, s5_w_glu, s5_b_glu, s5_w_proj, lru_conv_w, lru_conv_b, lru_w_rgate, lru_b_rgate, lru_w_igate, lru_b_igate, lru_lambda, lru_w_proj, w_out, ffn_norm, dense_w_gate, dense_w_up, dense_w_down, router_w, router_b, moe_w_gate, moe_w_up, moe_w_down, final_norm):
    bsz, seq, d = x.shape
    depth = w_in.shape[0]
    assert d == D_MODEL and N_META + seq <= T_PAD
    n = bsz * T_PAD
    assert n % TM == 0 and n % TM_FFN == 0 and seq % TB_FINAL == 0 and TB_FINAL % N_META == 0

    meta = jnp.broadcast_to(meta_tokens[None].astype(x.dtype), (bsz, N_META, d))
    pad = jnp.zeros((bsz, T_PAD - N_META - seq, d), x.dtype)
    hs = jnp.concatenate([meta, x, pad], axis=1).reshape(n, d)

    row3 = lambda a: a[:, None, :]
    w_in_b = w_in.astype(BF16)
    w_glu_b = s5_w_glu.astype(BF16)
    w_sp_b = s5_w_proj.astype(BF16)
    w_lp_b = lru_w_proj.astype(BF16)
    w_out_b = w_out.astype(BF16)
    w_ri = jnp.concatenate([_head_blockdiag(lru_w_rgate), _head_blockdiag(lru_w_igate)], axis=-1).astype(BF16)
    b_ri = jnp.concatenate([lru_b_rgate, lru_b_igate], axis=-1)
    neg_sp = -LRU_C * jax.nn.softplus(-lru_lambda)
    dense = (dense_w_gate, dense_w_up, dense_w_down)
    n_moe = router_w.shape[0]
    moe = [w.reshape((n_moe * N_EXPERTS,) + w.shape[2:]) for w in (moe_w_gate, moe_w_up, moe_w_down)]
    s5_ops = jax.vmap(_s5_prep)(s5_lambda_re, s5_lambda_im, s5_log_dt, s5_b_re, s5_b_im, s5_c_re, s5_c_im, s5_d)
    rw_pad = jnp.pad(router_w, ((0, 0), (0, 0), (0, LANES - N_EXPERTS)))
    rb_pad = jnp.pad(router_b, ((0, 0), (0, LANES - N_EXPERTS)), constant_values=MASKED_LOGIT)

    for layer in range(depth):
        u_parts, x_lru, g_lru, gates = _in_proj(hs, row3(mix_norm), w_in_b, row3(merge_bias), layer)
        ys_parts = _s5_scan(u_parts, s5_ops, layer, bsz)
        y_lru = _lru(x_lru, g_lru, lru_conv_w, row3(lru_conv_b), w_ri, row3(b_ri), row3(neg_sp), layer, bsz)
        j = layer // 2
        router = (rw_pad, row3(rb_pad), j) if layer % 2 == 1 else None
        res = _merge(hs, ys_parts, y_lru, gates, w_glu_b, row3(s5_b_glu), w_sp_b, w_lp_b, w_out_b,
                     row3(ffn_norm), layer, router)
        if layer % 2 == 0:
            hs, hn = res
            hs = _ffn(hn, hs, *dense, layer=j)
        else:
            hs, hn, route, route_t, counts = res
            pos, block_expert, n_used, pad_range = _moe_plan(route_t, counts, n)
            xs = _moe_dispatch(hn, pos, pad_range, n_used, block_expert.shape[0] * MOE_BLOCK)
            ys = _moe_ffn(xs, block_expert, n_used, *moe, first=j * N_EXPERTS)
            hs = _moe_combine(hs, route, ys, pos)

    return _final_norm(hs, final_norm[None, :], bsz, seq)
```

```python
import functools

import jax
import jax.numpy as jnp
from jax import lax
from jax.experimental import pallas as pl
from jax.experimental.pallas import tpu as pltpu

F32 = jnp.float32
BF16 = jnp.bfloat16

D_MODEL = 1024
N_META = 16
S5_WIDTH = 512
S5_GROUP = 16
S5_GROUPS = 32
S5_STATE = 64
LRU_WIDTH = 512
LRU_HEADS = 8
LRU_HEAD_DIM = 64
CONV_WIDTH = 4
LRU_C = 8.0
N_EXPERTS = 8
EPS = 1e-6

FOLD = 8
S5_PARTS = 4
PART_W = S5_WIDTH // S5_PARTS
PART_GROUPS = PART_W // S5_GROUP
PART_STATE = PART_GROUPS * S5_STATE
FOLD_W = FOLD * PART_W

T_PAD = 8256
ROWS = T_PAD // FOLD
TM = 688
TM_FFN = 1376
FF_CHUNK = 512
MOE_BLOCK = 512
GATHER_GROUP = 8
DISPATCH_RING = 3
LRU_CHUNK = 1032
LRU_UNROLL = 3
TB_FINAL = 512
VMEM_LIMIT = 56 * 1024 * 1024
LANES = 128
SUBLANES = 8
S5_SEG = ROWS // SUBLANES
ROW_TILES = D_MODEL // LANES
MASKED_LOGIT = float("-inf")


def _dot(a, b):
    return jnp.dot(a, b, preferred_element_type=F32)


def _const_spec(block_shape, index_map):
    return pl.BlockSpec(block_shape, index_map, pipeline_mode=pl.Buffered(1))


def _rms(x, g):
    ms = jnp.mean(x * x, axis=-1, keepdims=True)
    return x * lax.rsqrt(ms + EPS) * g


def _rows_to_tiles(ref, x):
    rows = x.shape[0]
    for s in range(ROW_TILES):
        ref[pl.ds(s, rows, stride=ROW_TILES), :] = x[:, s * LANES:(s + 1) * LANES]


def _rows_from_tiles(ref, rows):
    return jnp.concatenate([ref[pl.ds(s, rows, stride=ROW_TILES), :] for s in range(ROW_TILES)], axis=-1)


def _in_proj_kernel(hs_ref, g_ref, wf_ref, mb_ref, u_ref, xl_ref, gl_ref, gt_ref, w_ref):
    @pl.when(pl.program_id(0) == 0)
    def _():
        w_ref[0] = wf_ref[0].astype(BF16)

    hn = _rms(hs_ref[...], g_ref[0]).astype(BF16)
    u = _dot(hn, w_ref[0, :, 0:S5_WIDTH])
    for q in range(S5_PARTS):
        u_ref[q] = u[:, q * PART_W:(q + 1) * PART_W]
    o_x = S5_WIDTH
    o_g = o_x + LRU_WIDTH
    o_m = o_g + LRU_WIDTH
    xl_ref[...] = _dot(hn, w_ref[0, :, o_x:o_g]).astype(BF16)
    gl_ref[...] = _dot(hn, w_ref[0, :, o_g:o_m]).astype(BF16)
    z = _dot(hn, w_ref[0, :, o_m:]) + mb_ref[0]
    gt_ref[...] = jax.nn.sigmoid(z).astype(BF16)


def _in_proj(hs, mix_norm, w_in, merge_bias, layer):
    n = hs.shape[0]
    d_in = w_in.shape[-1]
    lay = lambda i: (layer, 0, 0)
    return pl.pallas_call(
        _in_proj_kernel,
        grid=(n // TM,),
        in_specs=[
            pl.BlockSpec((TM, D_MODEL), lambda i: (i, 0)),
            _const_spec((1, 1, D_MODEL), lay),
            _const_spec((1, D_MODEL, d_in), lay),
            _const_spec((1, 1, 2 * D_MODEL), lay),
        ],
        out_specs=[
            pl.BlockSpec((S5_PARTS, TM, PART_W), lambda i: (0, i, 0)),
            pl.BlockSpec((TM, LRU_WIDTH), lambda i: (i, 0)),
            pl.BlockSpec((TM, LRU_WIDTH), lambda i: (i, 0)),
            pl.BlockSpec((TM, 2 * D_MODEL), lambda i: (i, 0)),
        ],
        out_shape=[
            jax.ShapeDtypeStruct((S5_PARTS, n, PART_W), F32),
            jax.ShapeDtypeStruct((n, LRU_WIDTH), BF16),
            jax.ShapeDtypeStruct((n, LRU_WIDTH), BF16),
            jax.ShapeDtypeStruct((n, 2 * D_MODEL), BF16),
        ],
        scratch_shapes=[pltpu.VMEM((1, D_MODEL, d_in), BF16)],
        compiler_params=pltpu.CompilerParams(
            dimension_semantics=("arbitrary",), vmem_limit_bytes=VMEM_LIMIT),
        name="in_proj",
    )(hs, mix_norm, w_in, merge_bias)


def _s5_prep(lam_re, lam_im, log_dt, b_re, b_im, c_re, c_im, d_skip):
    dt = jnp.exp(log_dt)[:, None]
    mag = jnp.exp(lam_re * dt)
    a_re = mag * jnp.cos(lam_im * dt)
    a_im = mag * jnp.sin(lam_im * dt)
    den = lam_re * lam_re + lam_im * lam_im
    num_re = a_re - 1.0
    coef_re = (num_re * lam_re + a_im * lam_im) / den
    coef_im = (a_im * lam_re - num_re * lam_im) / den
    bb_re = coef_re[..., None] * b_re - coef_im[..., None] * b_im
    bb_im = coef_re[..., None] * b_im + coef_im[..., None] * b_re

    def cmul(xr, xi, yr, yi):
        return xr * yr - xi * yi, xr * yi + xi * yr

    def powers(br, bi, n):
        pr, pi = [jnp.ones_like(br)], [jnp.zeros_like(bi)]
        for _ in range(n):
            r, i = cmul(pr[-1], pi[-1], br, bi)
            pr.append(r)
            pi.append(i)
        return jnp.stack(pr), jnp.stack(pi)

    p_re, p_im = powers(a_re, a_im, FOLD)

    def per_part(x):
        lead = x.shape[:-3]
        xp = x.reshape(lead + (S5_PARTS, PART_GROUPS) + x.shape[-2:])
        return jnp.moveaxis(xp, len(lead), 0)

    rev_re = jnp.stack([p_re[FOLD - 1 - j] for j in range(FOLD)])
    rev_im = jnp.stack([p_im[FOLD - 1 - j] for j in range(FOLD)])
    wr, wi = cmul(rev_re[..., None], rev_im[..., None], bb_re[None], bb_im[None])
    w_ri = jnp.swapaxes(jnp.stack([wr, wi], axis=1), -1, -2)
    xq = jnp.transpose(per_part(w_ri), (0, 1, 3, 4, 2, 5)).reshape(S5_PARTS, FOLD_W, 2 * S5_STATE)

    ca_re, ca_im = cmul(c_re[None], c_im[None], p_re[:, :, None, :], p_im[:, :, None, :])
    bt_re = jnp.swapaxes(bb_re, -1, -2)[None, :, :, None, :]
    bt_im = jnp.swapaxes(bb_im, -1, -2)[None, :, :, None, :]
    taps = jnp.sum(ca_re[:FOLD, :, None] * bt_re - ca_im[:FOLD, :, None] * bt_im, axis=-1)
    skip = d_skip.reshape(S5_GROUPS, S5_GROUP)
    taps = taps.at[0].add(skip[:, :, None] * jnp.eye(S5_GROUP, dtype=F32)[None])
    rc = jnp.transpose(per_part(taps), (0, 2, 3, 1, 4)).reshape(S5_PARTS, PART_W, FOLD * S5_GROUP)

    v_ri = jnp.swapaxes(jnp.stack([ca_re[1:], -ca_im[1:]], axis=0), -1, -2)
    vc = jnp.transpose(per_part(v_ri), (0, 1, 3, 4, 2, 5)).reshape(S5_PARTS, 2 * PART_STATE, FOLD * S5_GROUP)

    def part_vec(x):
        lead = x.shape[:-2]
        xp = x.reshape(lead + (S5_PARTS, PART_STATE))
        return jnp.moveaxis(xp, -2, 0)

    row_re, row_im = p_re[FOLD], p_im[FOLD]
    seg_re, seg_im = jnp.ones_like(row_re), jnp.zeros_like(row_im)
    for bit in bin(S5_SEG)[2:]:
        seg_re, seg_im = cmul(seg_re, seg_im, seg_re, seg_im)
        if bit == '1':
            seg_re, seg_im = cmul(seg_re, seg_im, row_re, row_im)
    decay = jnp.stack([jnp.stack([part_vec(row_re), part_vec(row_im)], axis=1),
                       jnp.stack([part_vec(seg_re), part_vec(seg_im)], axis=1)], axis=1)
    return xq, rc, vc, decay[:, :, :, None, :]


def _iota2(shape):
    return (lax.broadcasted_iota(jnp.int32, shape, 0), lax.broadcasted_iota(jnp.int32, shape, 1))


def _s5_expand(xq, rc, vc, w1_s, tv_s):
    ps = PART_STATE
    lg_state, lg_group, lg_part = (v.bit_length() - 1 for v in (S5_STATE, S5_GROUP, PART_W))
    lg_pg = PART_GROUPS.bit_length() - 1
    grp = PART_GROUPS - 1
    one_hot = lambda m: jnp.where(m, 1.0, 0.0).astype(BF16)
    r, c = _iota2((2 * S5_STATE, 2 * ps))
    e1 = one_hot(((r >> lg_state) == (c >> (lg_state + lg_pg))) & ((r & (S5_STATE - 1)) == (c & (S5_STATE - 1))))
    r, c = _iota2((FOLD * S5_GROUP, FOLD_W))
    e2 = one_hot(((r >> lg_group) == (c >> lg_part)) & ((r & (S5_GROUP - 1)) == (c & (S5_GROUP - 1))))
    r, c = _iota2((FOLD_W, 2 * ps))
    m1 = ((r >> lg_group) & grp) == ((c >> lg_state) & grp)
    w1_s[...] = jnp.where(m1, _dot(xq.astype(BF16), e1), 0.0).astype(BF16)
    r, c = _iota2((PART_W, FOLD_W))
    m2 = (r >> lg_group) == ((c >> lg_group) & grp)
    r0 = jnp.where(m2, _dot(rc.astype(BF16), e2), 0.0).astype(BF16)
    for j in range(FOLD):
        if j == 0:
            blk = r0
        else:
            blk = jnp.concatenate([jnp.zeros((PART_W, j * PART_W), BF16), r0[:, :FOLD_W - j * PART_W]], axis=1)
        tv_s[j * PART_W:(j + 1) * PART_W, :] = blk
    r, c = _iota2((2 * ps, FOLD_W))
    m3 = ((r >> lg_state) & grp) == ((c >> lg_group) & grp)
    tv_s[FOLD_W:, :] = jnp.where(m3, _dot(vc.astype(BF16), e2), 0.0).astype(BF16)


def _s5_kernel(u_ref, xq_ref, rc_ref, vc_ref, dec_ref, y_ref, w1_s, tv_s, up_ref, f_ref, hp_ref):
    ps = PART_STATE
    tstride = S5_SEG * FOLD
    cols = lambda j: slice(j * PART_W, (j + 1) * PART_W)

    @pl.when(pl.program_id(1) == 0)
    def _():
        _s5_expand(xq_ref[0, 0], rc_ref[0, 0], vc_ref[0, 0], w1_s, tv_s)

    def fold_body(i, carry):
        r0 = pl.multiple_of(i * SUBLANES, SUBLANES)
        for j in range(FOLD):
            up_ref[pl.ds(r0, SUBLANES), cols(j)] = u_ref[0, 0, pl.ds(i * FOLD + j, SUBLANES, stride=tstride), :]
        return carry

    lax.fori_loop(0, S5_SEG, fold_body, 0)
    u = up_ref[...].astype(BF16)
    f_ref[...] = _dot(u, w1_s[...])
    ar = jnp.broadcast_to(dec_ref[0, 0, 0, 0], (SUBLANES, ps))
    ai = jnp.broadcast_to(dec_ref[0, 0, 0, 1], (SUBLANES, ps))

    def step(i, hr, hi):
        r0 = pl.multiple_of(i * SUBLANES, SUBLANES)
        return (ar * hr - ai * hi + f_ref[pl.ds(r0, SUBLANES), :ps],
                ar * hi + ai * hr + f_ref[pl.ds(r0, SUBLANES), ps:])

    zero = jnp.zeros((SUBLANES, ps), F32)
    er, ei = lax.fori_loop(0, S5_SEG, lambda i, c: step(i, *c), (zero, zero))
    sr = dec_ref[0, 0, 1, 0]
    si = dec_ref[0, 0, 1, 1]
    row = lax.broadcasted_iota(jnp.int32, (SUBLANES, ps), 0)
    nr, ni = zero, zero
    for sgm in range(SUBLANES - 1):
        lr = er + sr * nr - si * ni
        li = ei + sr * ni + si * nr
        nr = nr + jnp.where(row == sgm + 1, pltpu.roll(lr, 1, axis=0), 0.0)
        ni = ni + jnp.where(row == sgm + 1, pltpu.roll(li, 1, axis=0), 0.0)

    def state_body(i, carry):
        hr, hi = carry
        r0 = pl.multiple_of(i * SUBLANES, SUBLANES)
        hp_ref[pl.ds(r0, SUBLANES), :ps] = hr
        hp_ref[pl.ds(r0, SUBLANES), ps:] = hi
        return step(i, hr, hi)

    lax.fori_loop(0, S5_SEG, state_body, (nr, ni))
    hp = hp_ref[...].astype(BF16)
    wide = 2 * PART_W
    for c0 in range(0, FOLD_W, wide):
        y = _dot(u[:, :c0 + wide], tv_s[:c0 + wide, c0:c0 + wide]) + _dot(hp, tv_s[FOLD_W:, c0:c0 + wide])
        f_ref[:, c0:c0 + wide] = jax.nn.gelu(y)

    def unfold_body(i, carry):
        r0 = pl.multiple_of(i * SUBLANES, SUBLANES)
        for j in range(FOLD):
            y_ref[0, 0, pl.ds(i * FOLD + j, SUBLANES, stride=tstride), :] = f_ref[pl.ds(r0, SUBLANES), cols(j)]
        return carry

    lax.fori_loop(0, S5_SEG, unfold_body, 0)


def _s5_scan(u_parts, ops, layer, bsz):
    xq, rc, vc, decay = ops
    n = u_parts.shape[1]
    u4 = u_parts.reshape(S5_PARTS, bsz, T_PAD, PART_W)
    lay4 = lambda q, b: (layer, q, 0, 0)
    y4 = pl.pallas_call(
        _s5_kernel,
        grid=(S5_PARTS, bsz),
        in_specs=[
            pl.BlockSpec((1, 1, T_PAD, PART_W), lambda q, b: (q, b, 0, 0)),
            pl.BlockSpec((1, 1, FOLD_W, 2 * S5_STATE), lay4),
            pl.BlockSpec((1, 1, PART_W, FOLD * S5_GROUP), lay4),
            pl.BlockSpec((1, 1, 2 * PART_STATE, FOLD * S5_GROUP), lay4),
            pl.BlockSpec((1, 1, 2, 2, 1, PART_STATE), lambda q, b: (layer, q, 0, 0, 0, 0)),
        ],
        out_specs=pl.BlockSpec((1, 1, T_PAD, PART_W), lambda q, b: (q, b, 0, 0)),
        out_shape=jax.ShapeDtypeStruct((S5_PARTS, bsz, T_PAD, PART_W), F32),
        scratch_shapes=[
            pltpu.VMEM((FOLD_W, 2 * PART_STATE), BF16),
            pltpu.VMEM((FOLD_W + 2 * PART_STATE, FOLD_W), BF16),
            pltpu.VMEM((ROWS, FOLD_W), F32),
            pltpu.VMEM((ROWS, 2 * PART_STATE), F32),
            pltpu.VMEM((ROWS, 2 * PART_STATE), F32),
        ],
        compiler_params=pltpu.CompilerParams(
            dimension_semantics=("arbitrary", "arbitrary"), vmem_limit_bytes=VMEM_LIMIT),
        name="s5_scan",
    )(u4, xq, rc, vc, decay)
    return y4.reshape(S5_PARTS, n, PART_W)


def _lru_kernel(x_ref, g_ref, cw_ref, cb_ref, wri_ref, bri_ref, nsp_ref, o_ref,
                xs_ref, gs_ref, xc_ref, gp_ref, z_ref, a_ref, b_ref, os_ref, h_ref):
    tc = LRU_CHUNK
    c = LRU_WIDTH
    seg = tc // SUBLANES
    nq = c // LANES
    lanes = lambda q: slice(q * LANES, (q + 1) * LANES)
    halo = SUBLANES

    @pl.when(pl.program_id(1) == 0)
    def _():
        xs_ref[:, 0:halo, :] = jnp.zeros((nq, halo, LANES), F32)
        h_ref[...] = jnp.zeros((1, c), F32)

    x = x_ref[0].astype(F32)
    g = g_ref[0].astype(F32)
    for q in range(nq):
        xs_ref[q, halo:, :] = x[:, lanes(q)]
        gs_ref[q] = g[:, lanes(q)]
    taps = [[cw_ref[0, k:k + 1, lanes(q)] for k in range(CONV_WIDTH)] for q in range(nq)]
    bias = [cb_ref[0, :, lanes(q)] for q in range(nq)]

    def conv_body(i, carry):
        r0 = pl.multiple_of(i * SUBLANES, SUBLANES)
        for q in range(nq):
            acc = bias[q]
            for k in range(CONV_WIDTH):
                first = halo - (CONV_WIDTH - 1) + k + i
                acc = acc + taps[q][k] * xs_ref[q, pl.ds(first, SUBLANES, stride=seg), :]
            xc_ref[pl.ds(r0, SUBLANES), lanes(q)] = acc
            gp_ref[pl.ds(r0, SUBLANES), lanes(q)] = jax.nn.gelu(gs_ref[q, pl.ds(i, SUBLANES, stride=seg), :])
        return carry

    lax.fori_loop(0, seg, conv_body, 0, unroll=LRU_UNROLL)
    for q in range(nq):
        xs_ref[q, 0:halo, :] = xs_ref[q, tc:tc + halo, :]

    z_ref[...] = _dot(xc_ref[...].astype(BF16), wri_ref[0])
    b_r = jnp.broadcast_to(bri_ref[0, :, :c], (SUBLANES, c))
    b_i = jnp.broadcast_to(bri_ref[0, :, c:], (SUBLANES, c))
    nsp = jnp.broadcast_to(nsp_ref[0], (SUBLANES, c))

    def scan_body(i, carry):
        h, p = carry
        r0 = pl.multiple_of(i * SUBLANES, SUBLANES)
        a = jnp.exp(jax.nn.sigmoid(z_ref[pl.ds(r0, SUBLANES), :c] + b_r) * nsp)
        gated = jax.nn.sigmoid(z_ref[pl.ds(r0, SUBLANES), c:] + b_i) * xc_ref[pl.ds(r0, SUBLANES), :]
        h = a * h + jnp.sqrt(1.0 - a * a) * gated
        p = p * a
        b_ref[pl.ds(r0, SUBLANES), :] = h
        a_ref[pl.ds(r0, SUBLANES), :] = p
        return h, p

    h_end, p_end = lax.fori_loop(0, seg, scan_body, (jnp.zeros((SUBLANES, c), F32), jnp.ones((SUBLANES, c), F32)),
                                 unroll=LRU_UNROLL)
    row = lax.broadcasted_iota(jnp.int32, (SUBLANES, c), 0)
    enter = jnp.where(row == 0, h_ref[...], 0.0)
    for sgm in range(SUBLANES - 1):
        leave = h_end + p_end * enter
        enter = enter + jnp.where(row == sgm + 1, pltpu.roll(leave, 1, axis=0), 0.0)
    h_ref[...] = (h_end + p_end * enter)[SUBLANES - 1:SUBLANES]

    def out_body(i, carry):
        r0 = pl.multiple_of(i * SUBLANES, SUBLANES)
        h = b_ref[pl.ds(r0, SUBLANES), :] + a_ref[pl.ds(r0, SUBLANES), :] * enter
        y = h * gp_ref[pl.ds(r0, SUBLANES), :]
        for q in range(nq):
            os_ref[q, pl.ds(i, SUBLANES, stride=seg), :] = y[:, lanes(q)]
        return carry

    lax.fori_loop(0, seg, out_body, 0)
    o_ref[0] = jnp.concatenate([os_ref[q] for q in range(nq)], axis=-1).astype(BF16)


def _lru(x_lru, g_lru, conv_w, conv_b, w_ri, b_ri, neg_sp, layer, bsz):
    n = x_lru.shape[0]
    c = LRU_WIDTH
    x3 = x_lru.reshape(bsz, T_PAD, c)
    g3 = g_lru.reshape(bsz, T_PAD, c)
    lay = lambda b, t: (layer, 0, 0)
    out = pl.pallas_call(
        _lru_kernel,
        grid=(bsz, T_PAD // LRU_CHUNK),
        in_specs=[
            pl.BlockSpec((1, LRU_CHUNK, c), lambda b, t: (b, t, 0)),
            pl.BlockSpec((1, LRU_CHUNK, c), lambda b, t: (b, t, 0)),
            _const_spec((1, CONV_WIDTH, c), lay),
            _const_spec((1, 1, c), lay),
            _const_spec((1, c, 2 * c), lay),
            _const_spec((1, 1, 2 * c), lay),
            _const_spec((1, 1, c), lay),
        ],
        out_specs=pl.BlockSpec((1, LRU_CHUNK, c), lambda b, t: (b, t, 0)),
        out_shape=jax.ShapeDtypeStruct((bsz, T_PAD, c), BF16),
        scratch_shapes=[
            pltpu.VMEM((c // LANES, LRU_CHUNK + SUBLANES, LANES), F32),
            pltpu.VMEM((c // LANES, LRU_CHUNK, LANES), F32),
            pltpu.VMEM((LRU_CHUNK, c), F32),
            pltpu.VMEM((LRU_CHUNK, c), F32),
            pltpu.VMEM((LRU_CHUNK, 2 * c), F32),
            pltpu.VMEM((LRU_CHUNK, c), F32),
            pltpu.VMEM((LRU_CHUNK, c), F32),
            pltpu.VMEM((c // LANES, LRU_CHUNK, LANES), F32),
            pltpu.VMEM((1, c), F32),
        ],
        compiler_params=pltpu.CompilerParams(
            dimension_semantics=("arbitrary", "arbitrary"), vmem_limit_bytes=VMEM_LIMIT),
        name="rglru",
    )(x3, g3, conv_w, conv_b, w_ri, b_ri, neg_sp)
    return out.reshape(n, c)


def _merge_kernel(hs_ref, ys_ref, yl_ref, gt_ref, wglu_ref, bglu_ref, wsp_ref, wlp_ref, wout_ref, g_ref,
                  *rest, with_router):
    if with_router:
        rw_ref, rb_ref, hs_out_ref, hn_ref, rt_ref, rtt_ref, cnt_ref, run_ref, tri_ref = rest
    else:
        hs_out_ref, hn_ref = rest
    ys = jnp.concatenate([ys_ref[q] for q in range(S5_PARTS)], axis=-1)
    glu = ys * jax.nn.sigmoid(_dot(ys.astype(BF16), wglu_ref[0]) + bglu_ref[0])
    y_a = _dot(glu.astype(BF16), wsp_ref[0])
    y_b = _dot(yl_ref[...], wlp_ref[0])
    y = gt_ref[:, :D_MODEL].astype(F32) * y_a + gt_ref[:, D_MODEL:].astype(F32) * y_b
    hs = hs_ref[...] + _dot(y.astype(BF16), wout_ref[0])
    hs_out_ref[...] = hs
    hn = _rms(hs, g_ref[0])
    if not with_router:
        hn_ref[...] = hn.astype(BF16)
    else:
        _rows_to_tiles(hn_ref, hn)
        logits = _dot(hn.astype(BF16), rw_ref[0].astype(BF16)) + rb_ref[0]
        lane = lax.broadcasted_iota(jnp.int32, logits.shape, 1).astype(F32)
        m1 = jnp.max(logits, axis=-1, keepdims=True)
        i1 = jnp.min(jnp.where(logits == m1, lane, float(LANES)), axis=-1, keepdims=True)
        rest_l = jnp.where(lane == i1, MASKED_LOGIT, logits)
        m2 = jnp.max(rest_l, axis=-1, keepdims=True)
        i2 = jnp.min(jnp.where(rest_l == m2, lane, float(LANES)), axis=-1, keepdims=True)
        e2 = jnp.exp(m2 - m1)
        g1 = 1.0 / (1.0 + e2)
        g2 = e2 / (1.0 + e2)
        @pl.when(pl.program_id(0) == 0)
        def _():
            run_ref[...] = jnp.zeros_like(run_ref)
            r, c = _iota2(tri_ref.shape)
            tri_ref[...] = jnp.where(c < r, 1.0, 0.0).astype(BF16)

        first = lane == i1
        second = lane == i2
        picked = jnp.where(first | second, 1.0, 0.0)
        before = _dot(tri_ref[...], picked.astype(BF16)) + run_ref[...]
        rank1 = jnp.sum(jnp.where(first, before, 0.0), axis=-1, keepdims=True)
        rank2 = jnp.sum(jnp.where(second, before, 0.0), axis=-1, keepdims=True)
        run_ref[...] += jnp.sum(picked, axis=0, keepdims=True)
        cnt_ref[...] = jnp.broadcast_to(run_ref[...], cnt_ref.shape)
        rt = (jnp.where(lane == 0.0, i1, 0.0) + jnp.where(lane == 1.0, i2, 0.0)
              + jnp.where(lane == 2.0, g1, 0.0) + jnp.where(lane == 3.0, g2, 0.0)
              + jnp.where(lane == 4.0, rank1, 0.0) + jnp.where(lane == 5.0, rank2, 0.0))
        rt_ref[...] = rt
        r, c = _iota2((SUBLANES, LANES))
        pick = jnp.where(r == c, 1.0, 0.0).astype(BF16)
        hi = rt.astype(BF16)
        mid = (rt - hi.astype(F32)).astype(BF16)
        lo = (rt - hi.astype(F32) - mid.astype(F32)).astype(BF16)
        nt = (((1,), (1,)), ((), ()))
        rtt_ref[0] = (lax.dot_general(pick, hi, nt, preferred_element_type=F32)
                      + lax.dot_general(pick, mid, nt, preferred_element_type=F32)
                      + lax.dot_general(pick, lo, nt, preferred_element_type=F32))


def _merge(hs, ys_parts, y_lru, gates, w_glu, b_glu, w_sp, w_lp, w_out, ffn_norm, layer, router=None):
    n = hs.shape[0]
    tm = TM
    lay = lambda i: (layer, 0, 0)
    in_specs = [
        pl.BlockSpec((tm, D_MODEL), lambda i: (i, 0)),
        pl.BlockSpec((S5_PARTS, tm, PART_W), lambda i: (0, i, 0)),
        pl.BlockSpec((tm, LRU_WIDTH), lambda i: (i, 0)),
        pl.BlockSpec((tm, 2 * D_MODEL), lambda i: (i, 0)),
        _const_spec((1, S5_WIDTH, S5_WIDTH), lay),
        _const_spec((1, 1, S5_WIDTH), lay),
        _const_spec((1, S5_WIDTH, D_MODEL), lay),
        _const_spec((1, LRU_WIDTH, D_MODEL), lay),
        _const_spec((1, D_MODEL, D_MODEL), lay),
        _const_spec((1, 1, D_MODEL), lay),
    ]
    out_specs = [pl.BlockSpec((tm, D_MODEL), lambda i: (i, 0))]
    out_shape = [jax.ShapeDtypeStruct((n, D_MODEL), F32)]
    if router is None:
        out_specs.append(pl.BlockSpec((tm, D_MODEL), lambda i: (i, 0)))
        out_shape.append(jax.ShapeDtypeStruct((n, D_MODEL), BF16))
    else:
        out_specs.append(pl.BlockSpec((tm * ROW_TILES, LANES), lambda i: (i, 0)))
        out_shape.append(jax.ShapeDtypeStruct((n * ROW_TILES, LANES), F32))
    args = [hs, ys_parts, y_lru, gates, w_glu, b_glu, w_sp, w_lp, w_out, ffn_norm]
    if router is not None:
        rw, rb, j = router
        in_specs += [_const_spec((1, D_MODEL, LANES), lambda i: (j, 0, 0)),
                     _const_spec((1, 1, LANES), lambda i: (j, 0, 0))]
        out_specs += [pl.BlockSpec((tm, LANES), lambda i: (i, 0)),
                      pl.BlockSpec((1, SUBLANES, tm), lambda i: (i, 0, 0)),
                      pl.BlockSpec((SUBLANES, LANES), lambda i: (0, 0))]
        out_shape += [jax.ShapeDtypeStruct((n, LANES), F32),
                      jax.ShapeDtypeStruct((n // tm, SUBLANES, tm), F32),
                      jax.ShapeDtypeStruct((SUBLANES, LANES), F32)]
        args += [rw, rb]
    return pl.pallas_call(
        functools.partial(_merge_kernel, with_router=router is not None),
        grid=(n // tm,),
        in_specs=in_specs,
        out_specs=out_specs,
        out_shape=out_shape,
        scratch_shapes=[pltpu.VMEM((1, LANES), F32), pltpu.VMEM((tm, tm), BF16)] if router is not None else [],
        compiler_params=pltpu.CompilerParams(
            dimension_semantics=("arbitrary",), vmem_limit_bytes=VMEM_LIMIT),
        name="merge_router" if router is not None else "merge",
    )(*args)


def _ffn_kernel(x_ref, hs_ref, wg_ref, wu_ref, wd_ref, o_ref, acc_ref):
    c = pl.program_id(1)

    @pl.when(c == 0)
    def _():
        acc_ref[...] = jnp.zeros_like(acc_ref)

    x = x_ref[...]
    g = _dot(x, wg_ref[0].astype(BF16))
    h = g * jax.nn.sigmoid(g) * _dot(x, wu_ref[0].astype(BF16))
    acc_ref[...] += _dot(h.astype(BF16), wd_ref[0].astype(BF16))

    @pl.when(c == pl.num_programs(1) - 1)
    def _():
        o_ref[...] = hs_ref[...] + acc_ref[...]


def _ffn(hn, hs, w_gate, w_up, w_down, layer):
    n = hn.shape[0]
    ff = w_gate.shape[-1]
    return pl.pallas_call(
        _ffn_kernel,
        grid=(n // TM_FFN, ff // FF_CHUNK),
        in_specs=[
            pl.BlockSpec((TM_FFN, D_MODEL), lambda i, c: (i, 0)),
            pl.BlockSpec((TM_FFN, D_MODEL), lambda i, c: (i, 0)),
            pl.BlockSpec((1, D_MODEL, FF_CHUNK), lambda i, c: (layer, 0, c)),
            pl.BlockSpec((1, D_MODEL, FF_CHUNK), lambda i, c: (layer, 0, c)),
            pl.BlockSpec((1, FF_CHUNK, D_MODEL), lambda i, c: (layer, c, 0)),
        ],
        out_specs=pl.BlockSpec((TM_FFN, D_MODEL), lambda i, c: (i, 0)),
        out_shape=jax.ShapeDtypeStruct((n, D_MODEL), F32),
        scratch_shapes=[pltpu.VMEM((TM_FFN, D_MODEL), F32)],
        compiler_params=pltpu.CompilerParams(
            dimension_semantics=("arbitrary", "arbitrary"), vmem_limit_bytes=VMEM_LIMIT),
        name="dense_ffn",
    )(hn, hs, w_gate, w_up, w_down)


def _moe_plan(route_t, counts_f, n):
    n_blocks = -(-2 * n // MOE_BLOCK) + N_EXPERTS
    e = jnp.stack([route_t[:, 0, :], route_t[:, 1, :]]).astype(jnp.int32)
    rank = jnp.stack([route_t[:, 4, :], route_t[:, 5, :]]).astype(jnp.int32)
    counts = counts_f[0, :N_EXPERTS].astype(jnp.int32)
    padded = ((counts + MOE_BLOCK - 1) // MOE_BLOCK) * MOE_BLOCK
    cum_pad = jnp.cumsum(padded)
    pad_start = cum_pad - padded
    pos = rank
    for x in range(N_EXPERTS):
        pos = pos + jnp.where(e == x, pad_start[x], 0)
    block_start = jnp.arange(n_blocks, dtype=jnp.int32) * MOE_BLOCK
    block_expert = jnp.minimum(jnp.sum((block_start[:, None] >= cum_pad[None, :]).astype(jnp.int32), axis=1),
                               N_EXPERTS - 1)
    n_used = (cum_pad[-1] // MOE_BLOCK).astype(jnp.int32).reshape(1)
    pad_range = jnp.stack([pad_start + counts, cum_pad], axis=1).reshape(2 * N_EXPERTS).astype(jnp.int32)
    return pos.reshape(2 * n), block_expert, n_used, pad_range


def _tile(ref, index):
    return ref.at[pl.ds(pl.multiple_of(index * ROW_TILES, ROW_TILES), ROW_TILES)]


def _tile_gather(src_hbm, dst, sem, rows, index_of):
    def body(grp, carry):
        r0 = grp * GATHER_GROUP
        index = [index_of(r0 + j) for j in range(GATHER_GROUP)]
        for j in range(GATHER_GROUP):
            pltpu.make_async_copy(_tile(src_hbm, index[j]), _tile(dst, r0 + j), sem).start(priority=j % 2)
        return carry
    lax.fori_loop(0, rows // GATHER_GROUP, body, 0)


def _tile_gather_wait(src_hbm, dst, sem, rows):
    pltpu.make_async_copy(src_hbm.at[pl.ds(0, rows * ROW_TILES)], dst, sem).wait()


def _moe_dispatch_kernel(pos_ref, pad_ref, nu_ref, x_hbm, xs_hbm, xbuf, zero_ref, sem_in, sem_out, sem_fill):
    t = pl.program_id(0)
    nt = pl.num_programs(0)
    n = nt * TM
    tile_rows = TM * ROW_TILES
    block_rows = MOE_BLOCK * ROW_TILES
    n_blocks = xs_hbm.shape[0] // block_rows

    def read(tile):
        buf = tile % DISPATCH_RING
        return pltpu.make_async_copy(x_hbm.at[pl.ds(pl.multiple_of(tile * tile_rows, tile_rows), tile_rows)],
                                     xbuf.at[buf], sem_in.at[buf])

    def scatter_wait(tile):
        buf = tile % DISPATCH_RING
        for k in range(2):
            pltpu.make_async_copy(xbuf.at[buf], xs_hbm.at[pl.ds(0, tile_rows)], sem_out.at[buf]).wait()

    @pl.when(t == 0)
    def _():
        read(0).start()
        zero_ref[...] = jnp.zeros_like(zero_ref)
        zero_tile = zero_ref.at[pl.ds(0, ROW_TILES)]
        for e in range(N_EXPERTS):
            def fill(slot, carry):
                pltpu.make_async_copy(zero_tile, _tile(xs_hbm, slot), sem_fill.at[0]).start()
                return carry

            def fill_wait(slot, carry):
                pltpu.make_async_copy(zero_tile, _tile(xs_hbm, slot), sem_fill.at[0]).wait()
                return carry
            lax.fori_loop(pad_ref[2 * e], pad_ref[2 * e + 1], fill, 0)
            lax.fori_loop(pad_ref[2 * e], pad_ref[2 * e + 1], fill_wait, 0)

        def block_of(blk):
            return xs_hbm.at[pl.ds(pl.multiple_of(blk * block_rows, block_rows), block_rows)]

        def fill_block(blk, carry):
            pltpu.make_async_copy(zero_ref, block_of(blk), sem_fill.at[0]).start()
            return carry

        def fill_block_wait(blk, carry):
            pltpu.make_async_copy(zero_ref, block_of(blk), sem_fill.at[0]).wait()
            return carry
        lax.fori_loop(nu_ref[0], n_blocks, fill_block, 0)
        lax.fori_loop(nu_ref[0], n_blocks, fill_block_wait, 0)

    @pl.when(t + 1 < nt)
    def _():
        @pl.when(t + 1 >= DISPATCH_RING)
        def _():
            scatter_wait(t + 1 - DISPATCH_RING)
        read(t + 1).start()

    read(t).wait()
    src = xbuf.at[t % DISPATCH_RING]
    for k in range(2):
        def put(grp, carry):
            r0 = grp * GATHER_GROUP
            slot = [pos_ref[k * n + t * TM + r0 + j] for j in range(GATHER_GROUP)]
            for j in range(GATHER_GROUP):
                pltpu.make_async_copy(_tile(src, r0 + j), _tile(xs_hbm, slot[j]),
                                      sem_out.at[t % DISPATCH_RING]).start(priority=j % 2)
            return carry
        lax.fori_loop(0, TM // GATHER_GROUP, put, 0)

    @pl.when(t == nt - 1)
    def _():
        for back in range(DISPATCH_RING - 1, -1, -1):
            @pl.when(t - back >= 0)
            def _():
                scatter_wait(t - back)


def _moe_dispatch(hn_tiles, pos, pad_range, n_used, n_slots):
    n = hn_tiles.shape[0] // ROW_TILES
    return pl.pallas_call(
        _moe_dispatch_kernel,
        grid_spec=pltpu.PrefetchScalarGridSpec(
            num_scalar_prefetch=3,
            grid=(n // TM,),
            in_specs=[pl.BlockSpec(memory_space=pl.ANY)],
            out_specs=pl.BlockSpec(memory_space=pl.ANY),
            scratch_shapes=[pltpu.VMEM((DISPATCH_RING, TM * ROW_TILES, LANES), F32),
                            pltpu.VMEM((MOE_BLOCK * ROW_TILES, LANES), F32),
                            pltpu.SemaphoreType.DMA((DISPATCH_RING,)),
                            pltpu.SemaphoreType.DMA((DISPATCH_RING,)),
                            pltpu.SemaphoreType.DMA((1,))],
        ),
        out_shape=jax.ShapeDtypeStruct((n_slots * ROW_TILES, LANES), F32),
        compiler_params=pltpu.CompilerParams(dimension_semantics=("arbitrary",)),
        name="moe_dispatch",
    )(pos, pad_range, n_used, hn_tiles)


def _moe_ffn_kernel(be_ref, nu_ref, x_ref, wg_ref, wu_ref, wd_ref, y_ref):
    i = pl.program_id(0)

    @pl.when(i < nu_ref[0])
    def _():
        x = _rows_from_tiles(x_ref, MOE_BLOCK).astype(BF16)
        g = _dot(x, wg_ref[0].astype(BF16))
        h = g * jax.nn.sigmoid(g) * _dot(x, wu_ref[0].astype(BF16))
        _rows_to_tiles(y_ref, _dot(h.astype(BF16), wd_ref[0].astype(BF16)))

    @pl.when(i >= nu_ref[0])
    def _():
        y_ref[...] = jnp.zeros_like(y_ref)


def _moe_ffn(xs_tiles, block_expert, n_used, w_gate, w_up, w_down, first):
    n_blocks = block_expert.shape[0]
    ff = w_gate.shape[-1]
    wmap = lambda i, be, nu: (first + be[i], 0, 0)
    return pl.pallas_call(
        _moe_ffn_kernel,
        grid_spec=pltpu.PrefetchScalarGridSpec(
            num_scalar_prefetch=2,
            grid=(n_blocks,),
            in_specs=[
                pl.BlockSpec((MOE_BLOCK * ROW_TILES, LANES), lambda i, be, nu: (jnp.maximum(jnp.minimum(i, nu[0] - 1), 0), 0)),
                pl.BlockSpec((1, D_MODEL, ff), wmap),
                pl.BlockSpec((1, D_MODEL, ff), wmap),
                pl.BlockSpec((1, ff, D_MODEL), wmap),
            ],
            out_specs=pl.BlockSpec((MOE_BLOCK * ROW_TILES, LANES), lambda i, be, nu: (i, 0)),
        ),
        out_shape=jax.ShapeDtypeStruct((n_blocks * MOE_BLOCK * ROW_TILES, LANES), F32),
        compiler_params=pltpu.CompilerParams(
            dimension_semantics=("arbitrary",), vmem_limit_bytes=VMEM_LIMIT),
        name="moe_ffn",
    )(block_expert, n_used, xs_tiles, w_gate, w_up, w_down)


def _moe_combine_kernel(pos_ref, hs_ref, rt_ref, ys_hbm, o_ref, ybuf, sem):
    i = pl.program_id(0)
    nt = pl.num_programs(0)
    slot = i % 2

    def start(t, s):
        for k in range(2):
            _tile_gather(ys_hbm, ybuf.at[s, k], sem.at[s], TM, lambda r: pos_ref[k * (nt * TM) + t * TM + r])

    @pl.when(i == 0)
    def _():
        start(0, 0)

    @pl.when(i + 1 < nt)
    def _():
        start(i + 1, 1 - slot)

    for k in range(2):
        _tile_gather_wait(ys_hbm, ybuf.at[slot, k], sem.at[slot], TM)
    rt = rt_ref[...]
    lane = lax.broadcasted_iota(jnp.int32, rt.shape, 1)
    g1 = jnp.sum(jnp.where(lane == 2, rt, 0.0), axis=-1, keepdims=True)
    g2 = jnp.sum(jnp.where(lane == 3, rt, 0.0), axis=-1, keepdims=True)
    o_ref[...] = (hs_ref[...] + g1 * _rows_from_tiles(ybuf.at[slot, 0], TM)
                  + g2 * _rows_from_tiles(ybuf.at[slot, 1], TM))


def _moe_combine(hs, route, ys_tiles, pos):
    n = hs.shape[0]
    return pl.pallas_call(
        _moe_combine_kernel,
        grid_spec=pltpu.PrefetchScalarGridSpec(
            num_scalar_prefetch=1,
            grid=(n // TM,),
            in_specs=[
                pl.BlockSpec((TM, D_MODEL), lambda i, p: (i, 0)),
                pl.BlockSpec((TM, LANES), lambda i, p: (i, 0)),
                pl.BlockSpec(memory_space=pl.ANY),
            ],
            out_specs=pl.BlockSpec((TM, D_MODEL), lambda i, p: (i, 0)),
            scratch_shapes=[pltpu.VMEM((2, 2, TM * ROW_TILES, LANES), F32),
                            pltpu.SemaphoreType.DMA((2,))],
        ),
        out_shape=jax.ShapeDtypeStruct((n, D_MODEL), F32),
        compiler_params=pltpu.CompilerParams(
            dimension_semantics=("arbitrary",), vmem_limit_bytes=VMEM_LIMIT),
        name="moe_combine",
    )(pos, hs, route, ys_tiles)


def _final_kernel(a_ref, b_ref, g_ref, o_ref):
    tb = a_ref.shape[1]
    o_ref[0, :tb - N_META] = _rms(a_ref[0, N_META:], g_ref[...])
    o_ref[0, tb - N_META:] = _rms(b_ref[0], g_ref[...])


def _final_norm(hs, g, bsz, seq):
    hs3 = hs.reshape(bsz, T_PAD, D_MODEL)
    return pl.pallas_call(
        _final_kernel,
        grid=(bsz, seq // TB_FINAL),
        in_specs=[pl.BlockSpec((1, TB_FINAL, D_MODEL), lambda b, i: (b, i, 0)),
                  pl.BlockSpec((1, N_META, D_MODEL), lambda b, i: (b, (i + 1) * (TB_FINAL // N_META), 0)),
                  _const_spec((1, D_MODEL), lambda b, i: (0, 0))],
        out_specs=pl.BlockSpec((1, TB_FINAL, D_MODEL), lambda b, i: (b, i, 0)),
        out_shape=jax.ShapeDtypeStruct((bsz, seq, D_MODEL), F32),
        compiler_params=pltpu.CompilerParams(
            dimension_semantics=("arbitrary", "arbitrary"), vmem_limit_bytes=VMEM_LIMIT),
        name="final_norm",
    )(hs3, hs3, g)


def _head_blockdiag(w):
    eye = jnp.eye(LRU_HEADS, dtype=w.dtype)
    out = jnp.einsum('lnhk,nm->lnhmk', w, eye)
    return out.reshape(w.shape[0], LRU_WIDTH, LRU_WIDTH)


def kernel(x, meta_tokens, mix_norm, w_in, merge_bias, s5_lambda_re, s5_lambda_im, s5_log_dt, s5_b_re, s5_b_im, s5_c_re, s5_c_im, s5_d, s5_w_glu, s5_b_glu, s5_w_proj, lru_conv_w, lru_conv_b, lru_w_rgate, lru_b_rgate, lru_w_igate, lru_b_igate, lru_lambda, lru_w_proj, w_out, ffn_norm, dense_w_gate, dense_w_up, dense_w_down, router_w, router_b, moe_w_gate, moe_w_up, moe_w_down, final_norm):
    bsz, seq, d = x.shape
    depth = w_in.shape[0]
    assert d == D_MODEL and N_META + seq <= T_PAD
    n = bsz * T_PAD
    assert n % TM == 0 and n % TM_FFN == 0 and seq % TB_FINAL == 0 and TB_FINAL % N_META == 0

    meta = jnp.broadcast_to(meta_tokens[None].astype(x.dtype), (bsz, N_META, d))
    pad = jnp.zeros((bsz, T_PAD - N_META - seq, d), x.dtype)
    hs = jnp.concatenate([meta, x, pad], axis=1).reshape(n, d)

    row3 = lambda a: a[:, None, :]
    w_glu_b = s5_w_glu.astype(BF16)
    w_sp_b = s5_w_proj.astype(BF16)
    w_lp_b = lru_w_proj.astype(BF16)
    w_out_b = w_out.astype(BF16)
    w_ri = jnp.concatenate([_head_blockdiag(lru_w_rgate), _head_blockdiag(lru_w_igate)], axis=-1).astype(BF16)
    b_ri = jnp.concatenate([lru_b_rgate, lru_b_igate], axis=-1)
    neg_sp = -LRU_C * jax.nn.softplus(-lru_lambda)
    dense = (dense_w_gate, dense_w_up, dense_w_down)
    n_moe = router_w.shape[0]
    moe = [w.reshape((n_moe * N_EXPERTS,) + w.shape[2:]) for w in (moe_w_gate, moe_w_up, moe_w_down)]
    s5_ops = jax.vmap(_s5_prep)(s5_lambda_re, s5_lambda_im, s5_log_dt, s5_b_re, s5_b_im, s5_c_re, s5_c_im, s5_d)
    rw_pad = jnp.pad(router_w, ((0, 0), (0, 0), (0, LANES - N_EXPERTS)))
    rb_pad = jnp.pad(router_b, ((0, 0), (0, LANES - N_EXPERTS)), constant_values=MASKED_LOGIT)

    for layer in range(depth):
        u_parts, x_lru, g_lru, gates = _in_proj(hs, row3(mix_norm), w_in, row3(merge_bias), layer)
        ys_parts = _s5_scan(u_parts, s5_ops, layer, bsz)
        y_lru = _lru(x_lru, g_lru, lru_conv_w, row3(lru_conv_b), w_ri, row3(b_ri), row3(neg_sp), layer, bsz)
        j = layer // 2
        router = (rw_pad, row3(rb_pad), j) if layer % 2 == 1 else None
        res = _merge(hs, ys_parts, y_lru, gates, w_glu_b, row3(s5_b_glu), w_sp_b, w_lp_b, w_out_b,
                     row3(ffn_norm), layer, router)
        if layer % 2 == 0:
            hs, hn = res
            hs = _ffn(hn, hs, *dense, layer=j)
        else:
            hs, hn, route, route_t, counts = res
            pos, block_expert, n_used, pad_range = _moe_plan(route_t, counts, n)
            xs = _moe_dispatch(hn, pos, pad_range, n_used, block_expert.shape[0] * MOE_BLOCK)
            ys = _moe_ffn(xs, block_expert, n_used, *moe, first=j * N_EXPERTS)
            hs = _moe_combine(hs, route, ys, pos)

    return _final_norm(hs, final_norm[None, :], bsz, seq)
```

```python
import functools

import jax
import jax.numpy as jnp
from jax import lax
from jax.experimental import pallas as pl
from jax.experimental.pallas import tpu as pltpu

F32 = jnp.float32
BF16 = jnp.bfloat16

D_MODEL = 1024
N_META = 16
S5_WIDTH = 512
S5_GROUP = 16
S5_GROUPS = 32
S5_STATE = 64
LRU_WIDTH = 512
LRU_HEADS = 8
LRU_HEAD_DIM = 64
CONV_WIDTH = 4
LRU_C = 8.0
N_EXPERTS = 8
EPS = 1e-6

FOLD = 8
S5_PARTS = 4
PART_W = S5_WIDTH // S5_PARTS
PART_GROUPS = PART_W // S5_GROUP
PART_STATE = PART_GROUPS * S5_STATE
FOLD_W = FOLD * PART_W

T_PAD = 8256
ROWS = T_PAD // FOLD
TM = 688
TM_FFN = 1376
FF_CHUNK = 512
MOE_BLOCK = 512
GATHER_GROUP = 8
DISPATCH_RING = 3
LRU_CHUNK = 1032
LRU_UNROLL = 3
LRU_SCAN_UNROLL = 43
TB_FINAL = 512
VMEM_LIMIT = 56 * 1024 * 1024
LANES = 128
SUBLANES = 8
S5_SEG = ROWS // SUBLANES
ROW_TILES = D_MODEL // LANES
MASKED_LOGIT = float("-inf")


def _dot(a, b):
    return jnp.dot(a, b, preferred_element_type=F32)


def _const_spec(block_shape, index_map):
    return pl.BlockSpec(block_shape, index_map, pipeline_mode=pl.Buffered(1))


def _rms(x, g):
    ms = jnp.mean(x * x, axis=-1, keepdims=True)
    return x * lax.rsqrt(ms + EPS) * g


def _rows_to_tiles(ref, x):
    rows = x.shape[0]
    for s in range(ROW_TILES):
        ref[pl.ds(s, rows, stride=ROW_TILES), :] = x[:, s * LANES:(s + 1) * LANES]


def _rows_from_tiles(ref, rows):
    return jnp.concatenate([ref[pl.ds(s, rows, stride=ROW_TILES), :] for s in range(ROW_TILES)], axis=-1)


def _in_proj_kernel(hs_ref, g_ref, wf_ref, mb_ref, u_ref, xl_ref, gl_ref, gt_ref, w_ref):
    @pl.when(pl.program_id(0) == 0)
    def _():
        w_ref[0] = wf_ref[0].astype(BF16)

    hn = _rms(hs_ref[...], g_ref[0]).astype(BF16)
    u = _dot(hn, w_ref[0, :, 0:S5_WIDTH])
    for q in range(S5_PARTS):
        u_ref[q] = u[:, q * PART_W:(q + 1) * PART_W]
    o_x = S5_WIDTH
    o_g = o_x + LRU_WIDTH
    o_m = o_g + LRU_WIDTH
    xl_ref[...] = _dot(hn, w_ref[0, :, o_x:o_g]).astype(BF16)
    gl_ref[...] = _dot(hn, w_ref[0, :, o_g:o_m]).astype(BF16)
    z = _dot(hn, w_ref[0, :, o_m:]) + mb_ref[0]
    gt_ref[...] = jax.nn.sigmoid(z).astype(BF16)


def _in_proj(hs, mix_norm, w_in, merge_bias, layer):
    n = hs.shape[0]
    d_in = w_in.shape[-1]
    lay = lambda i: (layer, 0, 0)
    return pl.pallas_call(
        _in_proj_kernel,
        grid=(n // TM,),
        in_specs=[
            pl.BlockSpec((TM, D_MODEL), lambda i: (i, 0)),
            _const_spec((1, 1, D_MODEL), lay),
            _const_spec((1, D_MODEL, d_in), lay),
            _const_spec((1, 1, 2 * D_MODEL), lay),
        ],
        out_specs=[
            pl.BlockSpec((S5_PARTS, TM, PART_W), lambda i: (0, i, 0)),
            pl.BlockSpec((TM, LRU_WIDTH), lambda i: (i, 0)),
            pl.BlockSpec((TM, LRU_WIDTH), lambda i: (i, 0)),
            pl.BlockSpec((TM, 2 * D_MODEL), lambda i: (i, 0)),
        ],
        out_shape=[
            jax.ShapeDtypeStruct((S5_PARTS, n, PART_W), F32),
            jax.ShapeDtypeStruct((n, LRU_WIDTH), BF16),
            jax.ShapeDtypeStruct((n, LRU_WIDTH), BF16),
            jax.ShapeDtypeStruct((n, 2 * D_MODEL), BF16),
        ],
        scratch_shapes=[pltpu.VMEM((1, D_MODEL, d_in), BF16)],
        compiler_params=pltpu.CompilerParams(
            dimension_semantics=("arbitrary",), vmem_limit_bytes=VMEM_LIMIT),
        name="in_proj",
    )(hs, mix_norm, w_in, merge_bias)


def _s5_prep(lam_re, lam_im, log_dt, b_re, b_im, c_re, c_im, d_skip):
    dt = jnp.exp(log_dt)[:, None]
    mag = jnp.exp(lam_re * dt)
    a_re = mag * jnp.cos(lam_im * dt)
    a_im = mag * jnp.sin(lam_im * dt)
    den = lam_re * lam_re + lam_im * lam_im
    num_re = a_re - 1.0
    coef_re = (num_re * lam_re + a_im * lam_im) / den
    coef_im = (a_im * lam_re - num_re * lam_im) / den
    bb_re = coef_re[..., None] * b_re - coef_im[..., None] * b_im
    bb_im = coef_re[..., None] * b_im + coef_im[..., None] * b_re

    def cmul(xr, xi, yr, yi):
        return xr * yr - xi * yi, xr * yi + xi * yr

    def powers(br, bi, n):
        pr, pi = [jnp.ones_like(br)], [jnp.zeros_like(bi)]
        for _ in range(n):
            r, i = cmul(pr[-1], pi[-1], br, bi)
            pr.append(r)
            pi.append(i)
        return jnp.stack(pr), jnp.stack(pi)

    p_re, p_im = powers(a_re, a_im, FOLD)

    def per_part(x):
        lead = x.shape[:-3]
        xp = x.reshape(lead + (S5_PARTS, PART_GROUPS) + x.shape[-2:])
        return jnp.moveaxis(xp, len(lead), 0)

    rev_re = jnp.stack([p_re[FOLD - 1 - j] for j in range(FOLD)])
    rev_im = jnp.stack([p_im[FOLD - 1 - j] for j in range(FOLD)])
    wr, wi = cmul(rev_re[..., None], rev_im[..., None], bb_re[None], bb_im[None])
    w_ri = jnp.swapaxes(jnp.stack([wr, wi], axis=1), -1, -2)
    xq = jnp.transpose(per_part(w_ri), (0, 1, 3, 4, 2, 5)).reshape(S5_PARTS, FOLD_W, 2 * S5_STATE)

    ca_re, ca_im = cmul(c_re[None], c_im[None], p_re[:, :, None, :], p_im[:, :, None, :])
    bt_re = jnp.swapaxes(bb_re, -1, -2)[None, :, :, None, :]
    bt_im = jnp.swapaxes(bb_im, -1, -2)[None, :, :, None, :]
    taps = jnp.sum(ca_re[:FOLD, :, None] * bt_re - ca_im[:FOLD, :, None] * bt_im, axis=-1)
    skip = d_skip.reshape(S5_GROUPS, S5_GROUP)
    taps = taps.at[0].add(skip[:, :, None] * jnp.eye(S5_GROUP, dtype=F32)[None])
    rc = jnp.transpose(per_part(taps), (0, 2, 3, 1, 4)).reshape(S5_PARTS, PART_W, FOLD * S5_GROUP)

    v_ri = jnp.swapaxes(jnp.stack([ca_re[1:], -ca_im[1:]], axis=0), -1, -2)
    vc = jnp.transpose(per_part(v_ri), (0, 1, 3, 4, 2, 5)).reshape(S5_PARTS, 2 * PART_STATE, FOLD * S5_GROUP)

    def part_vec(x):
        lead = x.shape[:-2]
        xp = x.reshape(lead + (S5_PARTS, PART_STATE))
        return jnp.moveaxis(xp, -2, 0)

    row_re, row_im = p_re[FOLD], p_im[FOLD]
    seg_re, seg_im = jnp.ones_like(row_re), jnp.zeros_like(row_im)
    for bit in bin(S5_SEG)[2:]:
        seg_re, seg_im = cmul(seg_re, seg_im, seg_re, seg_im)
        if bit == '1':
            seg_re, seg_im = cmul(seg_re, seg_im, row_re, row_im)
    decay = jnp.stack([jnp.stack([part_vec(row_re), part_vec(row_im)], axis=1),
                       jnp.stack([part_vec(seg_re), part_vec(seg_im)], axis=1)], axis=1)
    return xq, rc, vc, decay[:, :, :, None, :]


def _iota2(shape):
    return (lax.broadcasted_iota(jnp.int32, shape, 0), lax.broadcasted_iota(jnp.int32, shape, 1))


def _s5_expand(xq, rc, vc, w1_s, tv_s):
    ps = PART_STATE
    lg_state, lg_group, lg_part = (v.bit_length() - 1 for v in (S5_STATE, S5_GROUP, PART_W))
    lg_pg = PART_GROUPS.bit_length() - 1
    grp = PART_GROUPS - 1
    one_hot = lambda m: jnp.where(m, 1.0, 0.0).astype(BF16)
    r, c = _iota2((2 * S5_STATE, 2 * ps))
    e1 = one_hot(((r >> lg_state) == (c >> (lg_state + lg_pg))) & ((r & (S5_STATE - 1)) == (c & (S5_STATE - 1))))
    r, c = _iota2((FOLD * S5_GROUP, FOLD_W))
    e2 = one_hot(((r >> lg_group) == (c >> lg_part)) & ((r & (S5_GROUP - 1)) == (c & (S5_GROUP - 1))))
    r, c = _iota2((FOLD_W, 2 * ps))
    m1 = ((r >> lg_group) & grp) == ((c >> lg_state) & grp)
    w1_s[...] = jnp.where(m1, _dot(xq.astype(BF16), e1), 0.0).astype(BF16)
    r, c = _iota2((PART_W, FOLD_W))
    m2 = (r >> lg_group) == ((c >> lg_group) & grp)
    r0 = jnp.where(m2, _dot(rc.astype(BF16), e2), 0.0).astype(BF16)
    for j in range(FOLD):
        if j == 0:
            blk = r0
        else:
            blk = jnp.concatenate([jnp.zeros((PART_W, j * PART_W), BF16), r0[:, :FOLD_W - j * PART_W]], axis=1)
        tv_s[j * PART_W:(j + 1) * PART_W, :] = blk
    r, c = _iota2((2 * ps, FOLD_W))
    m3 = ((r >> lg_state) & grp) == ((c >> lg_group) & grp)
    tv_s[FOLD_W:, :] = jnp.where(m3, _dot(vc.astype(BF16), e2), 0.0).astype(BF16)


def _s5_kernel(u_ref, xq_ref, rc_ref, vc_ref, dec_ref, y_ref, w1_s, tv_s, up_ref, f_ref, hp_ref):
    ps = PART_STATE
    tstride = S5_SEG * FOLD
    cols = lambda j: slice(j * PART_W, (j + 1) * PART_W)

    @pl.when(pl.program_id(1) == 0)
    def _():
        _s5_expand(xq_ref[0, 0], rc_ref[0, 0], vc_ref[0, 0], w1_s, tv_s)

    def fold_body(i, carry):
        r0 = pl.multiple_of(i * SUBLANES, SUBLANES)
        for j in range(FOLD):
            up_ref[pl.ds(r0, SUBLANES), cols(j)] = u_ref[0, 0, pl.ds(i * FOLD + j, SUBLANES, stride=tstride), :]
        return carry

    lax.fori_loop(0, S5_SEG, fold_body, 0)
    u = up_ref[...].astype(BF16)
    f_ref[...] = _dot(u, w1_s[...])
    ar = jnp.broadcast_to(dec_ref[0, 0, 0, 0], (SUBLANES, ps))
    ai = jnp.broadcast_to(dec_ref[0, 0, 0, 1], (SUBLANES, ps))

    def step(i, hr, hi):
        r0 = pl.multiple_of(i * SUBLANES, SUBLANES)
        return (ar * hr - ai * hi + f_ref[pl.ds(r0, SUBLANES), :ps],
                ar * hi + ai * hr + f_ref[pl.ds(r0, SUBLANES), ps:])

    zero = jnp.zeros((SUBLANES, ps), F32)
    er, ei = lax.fori_loop(0, S5_SEG, lambda i, c: step(i, *c), (zero, zero))
    sr = dec_ref[0, 0, 1, 0]
    si = dec_ref[0, 0, 1, 1]
    row = lax.broadcasted_iota(jnp.int32, (SUBLANES, ps), 0)
    nr, ni = zero, zero
    for sgm in range(SUBLANES - 1):
        lr = er + sr * nr - si * ni
        li = ei + sr * ni + si * nr
        nr = nr + jnp.where(row == sgm + 1, pltpu.roll(lr, 1, axis=0), 0.0)
        ni = ni + jnp.where(row == sgm + 1, pltpu.roll(li, 1, axis=0), 0.0)

    def state_body(i, carry):
        hr, hi = carry
        r0 = pl.multiple_of(i * SUBLANES, SUBLANES)
        hp_ref[pl.ds(r0, SUBLANES), :ps] = hr
        hp_ref[pl.ds(r0, SUBLANES), ps:] = hi
        return step(i, hr, hi)

    lax.fori_loop(0, S5_SEG, state_body, (nr, ni))
    hp = hp_ref[...].astype(BF16)
    wide = 2 * PART_W
    for c0 in range(0, FOLD_W, wide):
        y = _dot(u[:, :c0 + wide], tv_s[:c0 + wide, c0:c0 + wide]) + _dot(hp, tv_s[FOLD_W:, c0:c0 + wide])
        f_ref[:, c0:c0 + wide] = jax.nn.gelu(y)

    def unfold_body(i, carry):
        r0 = pl.multiple_of(i * SUBLANES, SUBLANES)
        for j in range(FOLD):
            y_ref[0, 0, pl.ds(i * FOLD + j, SUBLANES, stride=tstride), :] = f_ref[pl.ds(r0, SUBLANES), cols(j)]
        return carry

    lax.fori_loop(0, S5_SEG, unfold_body, 0)


def _s5_scan(u_parts, ops, layer, bsz):
    xq, rc, vc, decay = ops
    n = u_parts.shape[1]
    u4 = u_parts.reshape(S5_PARTS, bsz, T_PAD, PART_W)
    lay4 = lambda q, b: (layer, q, 0, 0)
    y4 = pl.pallas_call(
        _s5_kernel,
        grid=(S5_PARTS, bsz),
        in_specs=[
            pl.BlockSpec((1, 1, T_PAD, PART_W), lambda q, b: (q, b, 0, 0)),
            pl.BlockSpec((1, 1, FOLD_W, 2 * S5_STATE), lay4),
            pl.BlockSpec((1, 1, PART_W, FOLD * S5_GROUP), lay4),
            pl.BlockSpec((1, 1, 2 * PART_STATE, FOLD * S5_GROUP), lay4),
            pl.BlockSpec((1, 1, 2, 2, 1, PART_STATE), lambda q, b: (layer, q, 0, 0, 0, 0)),
        ],
        out_specs=pl.BlockSpec((1, 1, T_PAD, PART_W), lambda q, b: (q, b, 0, 0)),
        out_shape=jax.ShapeDtypeStruct((S5_PARTS, bsz, T_PAD, PART_W), F32),
        scratch_shapes=[
            pltpu.VMEM((FOLD_W, 2 * PART_STATE), BF16),
            pltpu.VMEM((FOLD_W + 2 * PART_STATE, FOLD_W), BF16),
            pltpu.VMEM((ROWS, FOLD_W), F32),
            pltpu.VMEM((ROWS, 2 * PART_STATE), F32),
            pltpu.VMEM((ROWS, 2 * PART_STATE), F32),
        ],
        compiler_params=pltpu.CompilerParams(
            dimension_semantics=("arbitrary", "arbitrary"), vmem_limit_bytes=VMEM_LIMIT),
        name="s5_scan",
    )(u4, xq, rc, vc, decay)
    return y4.reshape(S5_PARTS, n, PART_W)


def _lru_kernel(x_ref, g_ref, cw_ref, cb_ref, wri_ref, bri_ref, nsp_ref, o_ref,
                xs_ref, gs_ref, xc_ref, gp_ref, z_ref, a_ref, b_ref, os_ref, h_ref):
    tc = LRU_CHUNK
    c = LRU_WIDTH
    seg = tc // SUBLANES
    nq = c // LANES
    lanes = lambda q: slice(q * LANES, (q + 1) * LANES)
    halo = SUBLANES

    @pl.when(pl.program_id(1) == 0)
    def _():
        xs_ref[:, 0:halo, :] = jnp.zeros((nq, halo, LANES), F32)
        h_ref[...] = jnp.zeros((1, c), F32)

    x = x_ref[0].astype(F32)
    g = g_ref[0].astype(F32)
    for q in range(nq):
        xs_ref[q, halo:, :] = x[:, lanes(q)]
        gs_ref[q] = g[:, lanes(q)]
    taps = [[cw_ref[0, k:k + 1, lanes(q)] for k in range(CONV_WIDTH)] for q in range(nq)]
    bias = [cb_ref[0, :, lanes(q)] for q in range(nq)]

    def conv_body(i, carry):
        r0 = pl.multiple_of(i * SUBLANES, SUBLANES)
        for q in range(nq):
            acc = bias[q]
            for k in range(CONV_WIDTH):
                first = halo - (CONV_WIDTH - 1) + k + i
                acc = acc + taps[q][k] * xs_ref[q, pl.ds(first, SUBLANES, stride=seg), :]
            xc_ref[pl.ds(r0, SUBLANES), lanes(q)] = acc
            gp_ref[pl.ds(r0, SUBLANES), lanes(q)] = jax.nn.gelu(gs_ref[q, pl.ds(i, SUBLANES, stride=seg), :])
        return carry

    lax.fori_loop(0, seg, conv_body, 0, unroll=LRU_UNROLL)
    for q in range(nq):
        xs_ref[q, 0:halo, :] = xs_ref[q, tc:tc + halo, :]

    z_ref[...] = _dot(xc_ref[...].astype(BF16), wri_ref[0])
    b_r = jnp.broadcast_to(bri_ref[0, :, :c], (SUBLANES, c))
    b_i = jnp.broadcast_to(bri_ref[0, :, c:], (SUBLANES, c))
    nsp = jnp.broadcast_to(nsp_ref[0], (SUBLANES, c))

    def scan_body(i, carry):
        h, p = carry
        r0 = pl.multiple_of(i * SUBLANES, SUBLANES)
        a = jnp.exp(jax.nn.sigmoid(z_ref[pl.ds(r0, SUBLANES), :c] + b_r) * nsp)
        gated = jax.nn.sigmoid(z_ref[pl.ds(r0, SUBLANES), c:] + b_i) * xc_ref[pl.ds(r0, SUBLANES), :]
        h = a * h + jnp.sqrt(1.0 - a * a) * gated
        p = p * a
        b_ref[pl.ds(r0, SUBLANES), :] = h
        a_ref[pl.ds(r0, SUBLANES), :] = p
        return h, p

    h_end, p_end = lax.fori_loop(0, seg, scan_body, (jnp.zeros((SUBLANES, c), F32), jnp.ones((SUBLANES, c), F32)),
                                 unroll=LRU_SCAN_UNROLL)
    row = lax.broadcasted_iota(jnp.int32, (SUBLANES, c), 0)
    enter = jnp.where(row == 0, h_ref[...], 0.0)
    for sgm in range(SUBLANES - 1):
        leave = h_end + p_end * enter
        enter = enter + jnp.where(row == sgm + 1, pltpu.roll(leave, 1, axis=0), 0.0)
    h_ref[...] = (h_end + p_end * enter)[SUBLANES - 1:SUBLANES]

    def out_body(i, carry):
        r0 = pl.multiple_of(i * SUBLANES, SUBLANES)
        h = b_ref[pl.ds(r0, SUBLANES), :] + a_ref[pl.ds(r0, SUBLANES), :] * enter
        y = h * gp_ref[pl.ds(r0, SUBLANES), :]
        for q in range(nq):
            os_ref[q, pl.ds(i, SUBLANES, stride=seg), :] = y[:, lanes(q)]
        return carry

    lax.fori_loop(0, seg, out_body, 0)
    o_ref[0] = jnp.concatenate([os_ref[q] for q in range(nq)], axis=-1).astype(BF16)


def _lru(x_lru, g_lru, conv_w, conv_b, w_ri, b_ri, neg_sp, layer, bsz):
    n = x_lru.shape[0]
    c = LRU_WIDTH
    x3 = x_lru.reshape(bsz, T_PAD, c)
    g3 = g_lru.reshape(bsz, T_PAD, c)
    lay = lambda b, t: (layer, 0, 0)
    out = pl.pallas_call(
        _lru_kernel,
        grid=(bsz, T_PAD // LRU_CHUNK),
        in_specs=[
            pl.BlockSpec((1, LRU_CHUNK, c), lambda b, t: (b, t, 0)),
            pl.BlockSpec((1, LRU_CHUNK, c), lambda b, t: (b, t, 0)),
            _const_spec((1, CONV_WIDTH, c), lay),
            _const_spec((1, 1, c), lay),
            _const_spec((1, c, 2 * c), lay),
            _const_spec((1, 1, 2 * c), lay),
            _const_spec((1, 1, c), lay),
        ],
        out_specs=pl.BlockSpec((1, LRU_CHUNK, c), lambda b, t: (b, t, 0)),
        out_shape=jax.ShapeDtypeStruct((bsz, T_PAD, c), BF16),
        scratch_shapes=[
            pltpu.VMEM((c // LANES, LRU_CHUNK + SUBLANES, LANES), F32),
            pltpu.VMEM((c // LANES, LRU_CHUNK, LANES), F32),
            pltpu.VMEM((LRU_CHUNK, c), F32),
            pltpu.VMEM((LRU_CHUNK, c), F32),
            pltpu.VMEM((LRU_CHUNK, 2 * c), F32),
            pltpu.VMEM((LRU_CHUNK, c), F32),
            pltpu.VMEM((LRU_CHUNK, c), F32),
            pltpu.VMEM((c // LANES, LRU_CHUNK, LANES), F32),
            pltpu.VMEM((1, c), F32),
        ],
        compiler_params=pltpu.CompilerParams(
            dimension_semantics=("arbitrary", "arbitrary"), vmem_limit_bytes=VMEM_LIMIT),
        name="rglru",
    )(x3, g3, conv_w, conv_b, w_ri, b_ri, neg_sp)
    return out.reshape(n, c)


def _merge_kernel(hs_ref, ys_ref, yl_ref, gt_ref, wglu_ref, bglu_ref, wsp_ref, wlp_ref, wout_ref, g_ref,
                  *rest, with_router):
    if with_router:
        rw_ref, rb_ref, hs_out_ref, hn_ref, rt_ref, rtt_ref, cnt_ref, run_ref, tri_ref, *w_bf16 = rest
    else:
        hs_out_ref, hn_ref, *w_bf16 = rest
    wglu_b, wsp_b, wlp_b, wout_b = w_bf16

    @pl.when(pl.program_id(0) == 0)
    def _():
        for w_f32, w_b in zip((wglu_ref, wsp_ref, wlp_ref, wout_ref), w_bf16):
            w_b[...] = w_f32[0].astype(BF16)

    ys = jnp.concatenate([ys_ref[q] for q in range(S5_PARTS)], axis=-1)
    glu = ys * jax.nn.sigmoid(_dot(ys.astype(BF16), wglu_b[...]) + bglu_ref[0])
    y_a = _dot(glu.astype(BF16), wsp_b[...])
    y_b = _dot(yl_ref[...], wlp_b[...])
    y = gt_ref[:, :D_MODEL].astype(F32) * y_a + gt_ref[:, D_MODEL:].astype(F32) * y_b
    hs = hs_ref[...] + _dot(y.astype(BF16), wout_b[...])
    hs_out_ref[...] = hs
    hn = _rms(hs, g_ref[0])
    if not with_router:
        hn_ref[...] = hn.astype(BF16)
    else:
        _rows_to_tiles(hn_ref, hn)
        logits = _dot(hn.astype(BF16), rw_ref[0].astype(BF16)) + rb_ref[0]
        lane = lax.broadcasted_iota(jnp.int32, logits.shape, 1).astype(F32)
        m1 = jnp.max(logits, axis=-1, keepdims=True)
        i1 = jnp.min(jnp.where(logits == m1, lane, float(LANES)), axis=-1, keepdims=True)
        rest_l = jnp.where(lane == i1, MASKED_LOGIT, logits)
        m2 = jnp.max(rest_l, axis=-1, keepdims=True)
        i2 = jnp.min(jnp.where(rest_l == m2, lane, float(LANES)), axis=-1, keepdims=True)
        e2 = jnp.exp(m2 - m1)
        g1 = 1.0 / (1.0 + e2)
        g2 = e2 / (1.0 + e2)
        @pl.when(pl.program_id(0) == 0)
        def _():
            run_ref[...] = jnp.zeros_like(run_ref)
            r, c = _iota2(tri_ref.shape)
            tri_ref[...] = jnp.where(c < r, 1.0, 0.0).astype(BF16)

        first = lane == i1
        second = lane == i2
        picked = jnp.where(first | second, 1.0, 0.0)
        before = _dot(tri_ref[...], picked.astype(BF16)) + run_ref[...]
        rank1 = jnp.sum(jnp.where(first, before, 0.0), axis=-1, keepdims=True)
        rank2 = jnp.sum(jnp.where(second, before, 0.0), axis=-1, keepdims=True)
        run_ref[...] += jnp.sum(picked, axis=0, keepdims=True)
        cnt_ref[...] = jnp.broadcast_to(run_ref[...], cnt_ref.shape)
        rt = (jnp.where(lane == 0.0, i1, 0.0) + jnp.where(lane == 1.0, i2, 0.0)
              + jnp.where(lane == 2.0, g1, 0.0) + jnp.where(lane == 3.0, g2, 0.0)
              + jnp.where(lane == 4.0, rank1, 0.0) + jnp.where(lane == 5.0, rank2, 0.0))
        rt_ref[...] = rt
        r, c = _iota2((SUBLANES, LANES))
        pick = jnp.where(r == c, 1.0, 0.0).astype(BF16)
        hi = rt.astype(BF16)
        mid = (rt - hi.astype(F32)).astype(BF16)
        lo = (rt - hi.astype(F32) - mid.astype(F32)).astype(BF16)
        nt = (((1,), (1,)), ((), ()))
        rtt_ref[0] = (lax.dot_general(pick, hi, nt, preferred_element_type=F32)
                      + lax.dot_general(pick, mid, nt, preferred_element_type=F32)
                      + lax.dot_general(pick, lo, nt, preferred_element_type=F32))


def _merge(hs, ys_parts, y_lru, gates, w_glu, b_glu, w_sp, w_lp, w_out, ffn_norm, layer, router=None):
    n = hs.shape[0]
    tm = TM
    lay = lambda i: (layer, 0, 0)
    in_specs = [
        pl.BlockSpec((tm, D_MODEL), lambda i: (i, 0)),
        pl.BlockSpec((S5_PARTS, tm, PART_W), lambda i: (0, i, 0)),
        pl.BlockSpec((tm, LRU_WIDTH), lambda i: (i, 0)),
        pl.BlockSpec((tm, 2 * D_MODEL), lambda i: (i, 0)),
        _const_spec((1, S5_WIDTH, S5_WIDTH), lay),
        _const_spec((1, 1, S5_WIDTH), lay),
        _const_spec((1, S5_WIDTH, D_MODEL), lay),
        _const_spec((1, LRU_WIDTH, D_MODEL), lay),
        _const_spec((1, D_MODEL, D_MODEL), lay),
        _const_spec((1, 1, D_MODEL), lay),
    ]
    out_specs = [pl.BlockSpec((tm, D_MODEL), lambda i: (i, 0))]
    out_shape = [jax.ShapeDtypeStruct((n, D_MODEL), F32)]
    if router is None:
        out_specs.append(pl.BlockSpec((tm, D_MODEL), lambda i: (i, 0)))
        out_shape.append(jax.ShapeDtypeStruct((n, D_MODEL), BF16))
    else:
        out_specs.append(pl.BlockSpec((tm * ROW_TILES, LANES), lambda i: (i, 0)))
        out_shape.append(jax.ShapeDtypeStruct((n * ROW_TILES, LANES), F32))
    args = [hs, ys_parts, y_lru, gates, w_glu, b_glu, w_sp, w_lp, w_out, ffn_norm]
    if router is not None:
        rw, rb, j = router
        in_specs += [_const_spec((1, D_MODEL, LANES), lambda i: (j, 0, 0)),
                     _const_spec((1, 1, LANES), lambda i: (j, 0, 0))]
        out_specs += [pl.BlockSpec((tm, LANES), lambda i: (i, 0)),
                      pl.BlockSpec((1, SUBLANES, tm), lambda i: (i, 0, 0)),
                      pl.BlockSpec((SUBLANES, LANES), lambda i: (0, 0))]
        out_shape += [jax.ShapeDtypeStruct((n, LANES), F32),
                      jax.ShapeDtypeStruct((n // tm, SUBLANES, tm), F32),
                      jax.ShapeDtypeStruct((SUBLANES, LANES), F32)]
        args += [rw, rb]
    return pl.pallas_call(
        functools.partial(_merge_kernel, with_router=router is not None),
        grid=(n // tm,),
        in_specs=in_specs,
        out_specs=out_specs,
        out_shape=out_shape,
        scratch_shapes=([pltpu.VMEM((1, LANES), F32), pltpu.VMEM((tm, tm), BF16)] if router is not None else [])
        + [pltpu.VMEM(w.shape[1:], BF16) for w in (w_glu, w_sp, w_lp, w_out)],
        compiler_params=pltpu.CompilerParams(
            dimension_semantics=("arbitrary",), vmem_limit_bytes=VMEM_LIMIT),
        name="merge_router" if router is not None else "merge",
    )(*args)


def _ffn_kernel(x_ref, hs_ref, wg_ref, wu_ref, wd_ref, o_ref, acc_ref):
    c = pl.program_id(1)

    @pl.when(c == 0)
    def _():
        acc_ref[...] = jnp.zeros_like(acc_ref)

    x = x_ref[...]
    g = _dot(x, wg_ref[0].astype(BF16))
    h = g * jax.nn.sigmoid(g) * _dot(x, wu_ref[0].astype(BF16))
    acc_ref[...] += _dot(h.astype(BF16), wd_ref[0].astype(BF16))

    @pl.when(c == pl.num_programs(1) - 1)
    def _():
        o_ref[...] = hs_ref[...] + acc_ref[...]


def _ffn(hn, hs, w_gate, w_up, w_down, layer):
    n = hn.shape[0]
    ff = w_gate.shape[-1]
    return pl.pallas_call(
        _ffn_kernel,
        grid=(n // TM_FFN, ff // FF_CHUNK),
        in_specs=[
            pl.BlockSpec((TM_FFN, D_MODEL), lambda i, c: (i, 0)),
            pl.BlockSpec((TM_FFN, D_MODEL), lambda i, c: (i, 0)),
            pl.BlockSpec((1, D_MODEL, FF_CHUNK), lambda i, c: (layer, 0, c)),
            pl.BlockSpec((1, D_MODEL, FF_CHUNK), lambda i, c: (layer, 0, c)),
            pl.BlockSpec((1, FF_CHUNK, D_MODEL), lambda i, c: (layer, c, 0)),
        ],
        out_specs=pl.BlockSpec((TM_FFN, D_MODEL), lambda i, c: (i, 0)),
        out_shape=jax.ShapeDtypeStruct((n, D_MODEL), F32),
        scratch_shapes=[pltpu.VMEM((TM_FFN, D_MODEL), F32)],
        compiler_params=pltpu.CompilerParams(
            dimension_semantics=("arbitrary", "arbitrary"), vmem_limit_bytes=VMEM_LIMIT),
        name="dense_ffn",
    )(hn, hs, w_gate, w_up, w_down)


def _moe_plan(route_t, counts_f, n):
    n_blocks = -(-2 * n // MOE_BLOCK) + N_EXPERTS
    e = jnp.stack([route_t[:, 0, :], route_t[:, 1, :]]).astype(jnp.int32)
    rank = jnp.stack([route_t[:, 4, :], route_t[:, 5, :]]).astype(jnp.int32)
    counts = counts_f[0, :N_EXPERTS].astype(jnp.int32)
    padded = ((counts + MOE_BLOCK - 1) // MOE_BLOCK) * MOE_BLOCK
    cum_pad = jnp.cumsum(padded)
    pad_start = cum_pad - padded
    pos = rank
    for x in range(N_EXPERTS):
        pos = pos + jnp.where(e == x, pad_start[x], 0)
    block_start = jnp.arange(n_blocks, dtype=jnp.int32) * MOE_BLOCK
    block_expert = jnp.minimum(jnp.sum((block_start[:, None] >= cum_pad[None, :]).astype(jnp.int32), axis=1),
                               N_EXPERTS - 1)
    n_used = (cum_pad[-1] // MOE_BLOCK).astype(jnp.int32).reshape(1)
    pad_range = jnp.stack([pad_start + counts, cum_pad], axis=1).reshape(2 * N_EXPERTS).astype(jnp.int32)
    return pos.reshape(2 * n), block_expert, n_used, pad_range


def _tile(ref, index):
    return ref.at[pl.ds(pl.multiple_of(index * ROW_TILES, ROW_TILES), ROW_TILES)]


def _tile_gather(src_hbm, dst, sem, rows, index_of):
    def body(grp, carry):
        r0 = grp * GATHER_GROUP
        index = [index_of(r0 + j) for j in range(GATHER_GROUP)]
        for j in range(GATHER_GROUP):
            pltpu.make_async_copy(_tile(src_hbm, index[j]), _tile(dst, r0 + j), sem).start(priority=j % 2)
        return carry
    lax.fori_loop(0, rows // GATHER_GROUP, body, 0)


def _tile_gather_wait(src_hbm, dst, sem, rows):
    pltpu.make_async_copy(src_hbm.at[pl.ds(0, rows * ROW_TILES)], dst, sem).wait()


def _moe_dispatch_kernel(pos_ref, pad_ref, nu_ref, x_hbm, xs_hbm, xbuf, zero_ref, sem_in, sem_out, sem_fill):
    t = pl.program_id(0)
    nt = pl.num_programs(0)
    n = nt * TM
    tile_rows = TM * ROW_TILES
    block_rows = MOE_BLOCK * ROW_TILES
    n_blocks = xs_hbm.shape[0] // block_rows

    def read(tile):
        buf = tile % DISPATCH_RING
        return pltpu.make_async_copy(x_hbm.at[pl.ds(pl.multiple_of(tile * tile_rows, tile_rows), tile_rows)],
                                     xbuf.at[buf], sem_in.at[buf])

    def scatter_wait(tile):
        buf = tile % DISPATCH_RING
        for k in range(2):
            pltpu.make_async_copy(xbuf.at[buf], xs_hbm.at[pl.ds(0, tile_rows)], sem_out.at[buf]).wait()

    @pl.when(t == 0)
    def _():
        read(0).start()
        zero_ref[...] = jnp.zeros_like(zero_ref)
        zero_tile = zero_ref.at[pl.ds(0, ROW_TILES)]
        for e in range(N_EXPERTS):
            def fill(slot, carry):
                pltpu.make_async_copy(zero_tile, _tile(xs_hbm, slot), sem_fill.at[0]).start()
                return carry

            def fill_wait(slot, carry):
                pltpu.make_async_copy(zero_tile, _tile(xs_hbm, slot), sem_fill.at[0]).wait()
                return carry
            lax.fori_loop(pad_ref[2 * e], pad_ref[2 * e + 1], fill, 0)
            lax.fori_loop(pad_ref[2 * e], pad_ref[2 * e + 1], fill_wait, 0)

        def block_of(blk):
            return xs_hbm.at[pl.ds(pl.multiple_of(blk * block_rows, block_rows), block_rows)]

        def fill_block(blk, carry):
            pltpu.make_async_copy(zero_ref, block_of(blk), sem_fill.at[0]).start()
            return carry

        def fill_block_wait(blk, carry):
            pltpu.make_async_copy(zero_ref, block_of(blk), sem_fill.at[0]).wait()
            return carry
        lax.fori_loop(nu_ref[0], n_blocks, fill_block, 0)
        lax.fori_loop(nu_ref[0], n_blocks, fill_block_wait, 0)

    @pl.when(t + 1 < nt)
    def _():
        @pl.when(t + 1 >= DISPATCH_RING)
        def _():
            scatter_wait(t + 1 - DISPATCH_RING)
        read(t + 1).start()

    read(t).wait()
    src = xbuf.at[t % DISPATCH_RING]
    for k in range(2):
        def put(grp, carry):
            r0 = grp * GATHER_GROUP
            slot = [pos_ref[k * n + t * TM + r0 + j] for j in range(GATHER_GROUP)]
            for j in range(GATHER_GROUP):
                pltpu.make_async_copy(_tile(src, r0 + j), _tile(xs_hbm, slot[j]),
                                      sem_out.at[t % DISPATCH_RING]).start(priority=j % 2)
            return carry
        lax.fori_loop(0, TM // GATHER_GROUP, put, 0)

    @pl.when(t == nt - 1)
    def _():
        for back in range(DISPATCH_RING - 1, -1, -1):
            @pl.when(t - back >= 0)
            def _():
                scatter_wait(t - back)


def _moe_dispatch(hn_tiles, pos, pad_range, n_used, n_slots):
    n = hn_tiles.shape[0] // ROW_TILES
    return pl.pallas_call(
        _moe_dispatch_kernel,
        grid_spec=pltpu.PrefetchScalarGridSpec(
            num_scalar_prefetch=3,
            grid=(n // TM,),
            in_specs=[pl.BlockSpec(memory_space=pl.ANY)],
            out_specs=pl.BlockSpec(memory_space=pl.ANY),
            scratch_shapes=[pltpu.VMEM((DISPATCH_RING, TM * ROW_TILES, LANES), F32),
                            pltpu.VMEM((MOE_BLOCK * ROW_TILES, LANES), F32),
                            pltpu.SemaphoreType.DMA((DISPATCH_RING,)),
                            pltpu.SemaphoreType.DMA((DISPATCH_RING,)),
                            pltpu.SemaphoreType.DMA((1,))],
        ),
        out_shape=jax.ShapeDtypeStruct((n_slots * ROW_TILES, LANES), F32),
        compiler_params=pltpu.CompilerParams(dimension_semantics=("arbitrary",)),
        name="moe_dispatch",
    )(pos, pad_range, n_used, hn_tiles)


def _moe_ffn_kernel(be_ref, nu_ref, x_ref, wg_ref, wu_ref, wd_ref, y_ref):
    i = pl.program_id(0)

    @pl.when(i < nu_ref[0])
    def _():
        x = _rows_from_tiles(x_ref, MOE_BLOCK).astype(BF16)
        g = _dot(x, wg_ref[0].astype(BF16))
        h = g * jax.nn.sigmoid(g) * _dot(x, wu_ref[0].astype(BF16))
        _rows_to_tiles(y_ref, _dot(h.astype(BF16), wd_ref[0].astype(BF16)))

    @pl.when(i >= nu_ref[0])
    def _():
        y_ref[...] = jnp.zeros_like(y_ref)


def _moe_ffn(xs_tiles, block_expert, n_used, w_gate, w_up, w_down, first):
    n_blocks = block_expert.shape[0]
    ff = w_gate.shape[-1]
    wmap = lambda i, be, nu: (first + be[i], 0, 0)
    return pl.pallas_call(
        _moe_ffn_kernel,
        grid_spec=pltpu.PrefetchScalarGridSpec(
            num_scalar_prefetch=2,
            grid=(n_blocks,),
            in_specs=[
                pl.BlockSpec((MOE_BLOCK * ROW_TILES, LANES), lambda i, be, nu: (jnp.maximum(jnp.minimum(i, nu[0] - 1), 0), 0)),
                pl.BlockSpec((1, D_MODEL, ff), wmap),
                pl.BlockSpec((1, D_MODEL, ff), wmap),
                pl.BlockSpec((1, ff, D_MODEL), wmap),
            ],
            out_specs=pl.BlockSpec((MOE_BLOCK * ROW_TILES, LANES), lambda i, be, nu: (i, 0)),
        ),
        out_shape=jax.ShapeDtypeStruct((n_blocks * MOE_BLOCK * ROW_TILES, LANES), F32),
        compiler_params=pltpu.CompilerParams(
            dimension_semantics=("arbitrary",), vmem_limit_bytes=VMEM_LIMIT),
        name="moe_ffn",
    )(block_expert, n_used, xs_tiles, w_gate, w_up, w_down)


def _moe_combine_kernel(pos_ref, hs_ref, rt_ref, ys_hbm, o_ref, ybuf, sem):
    i = pl.program_id(0)
    nt = pl.num_programs(0)
    slot = i % 2

    def start(t, s):
        for k in range(2):
            _tile_gather(ys_hbm, ybuf.at[s, k], sem.at[s], TM, lambda r: pos_ref[k * (nt * TM) + t * TM + r])

    @pl.when(i == 0)
    def _():
        start(0, 0)

    @pl.when(i + 1 < nt)
    def _():
        start(i + 1, 1 - slot)

    for k in range(2):
        _tile_gather_wait(ys_hbm, ybuf.at[slot, k], sem.at[slot], TM)
    rt = rt_ref[...]
    lane = lax.broadcasted_iota(jnp.int32, rt.shape, 1)
    g1 = jnp.sum(jnp.where(lane == 2, rt, 0.0), axis=-1, keepdims=True)
    g2 = jnp.sum(jnp.where(lane == 3, rt, 0.0), axis=-1, keepdims=True)
    o_ref[...] = (hs_ref[...] + g1 * _rows_from_tiles(ybuf.at[slot, 0], TM)
                  + g2 * _rows_from_tiles(ybuf.at[slot, 1], TM))


def _moe_combine(hs, route, ys_tiles, pos):
    n = hs.shape[0]
    return pl.pallas_call(
        _moe_combine_kernel,
        grid_spec=pltpu.PrefetchScalarGridSpec(
            num_scalar_prefetch=1,
            grid=(n // TM,),
            in_specs=[
                pl.BlockSpec((TM, D_MODEL), lambda i, p: (i, 0)),
                pl.BlockSpec((TM, LANES), lambda i, p: (i, 0)),
                pl.BlockSpec(memory_space=pl.ANY),
            ],
            out_specs=pl.BlockSpec((TM, D_MODEL), lambda i, p: (i, 0)),
            scratch_shapes=[pltpu.VMEM((2, 2, TM * ROW_TILES, LANES), F32),
                            pltpu.SemaphoreType.DMA((2,))],
        ),
        out_shape=jax.ShapeDtypeStruct((n, D_MODEL), F32),
        compiler_params=pltpu.CompilerParams(
            dimension_semantics=("arbitrary",), vmem_limit_bytes=VMEM_LIMIT),
        name="moe_combine",
    )(pos, hs, route, ys_tiles)


def _final_kernel(a_ref, b_ref, g_ref, o_ref):
    tb = a_ref.shape[1]
    o_ref[0, :tb - N_META] = _rms(a_ref[0, N_META:], g_ref[...])
    o_ref[0, tb - N_META:] = _rms(b_ref[0], g_ref[...])


def _final_norm(hs, g, bsz, seq):
    hs3 = hs.reshape(bsz, T_PAD, D_MODEL)
    return pl.pallas_call(
        _final_kernel,
        grid=(bsz, seq // TB_FINAL),
        in_specs=[pl.BlockSpec((1, TB_FINAL, D_MODEL), lambda b, i: (b, i, 0)),
                  pl.BlockSpec((1, N_META, D_MODEL), lambda b, i: (b, (i + 1) * (TB_FINAL // N_META), 0)),
                  _const_spec((1, D_MODEL), lambda b, i: (0, 0))],
        out_specs=pl.BlockSpec((1, TB_FINAL, D_MODEL), lambda b, i: (b, i, 0)),
        out_shape=jax.ShapeDtypeStruct((bsz, seq, D_MODEL), F32),
        compiler_params=pltpu.CompilerParams(
            dimension_semantics=("arbitrary", "arbitrary"), vmem_limit_bytes=VMEM_LIMIT),
        name="final_norm",
    )(hs3, hs3, g)


def _head_blockdiag(w):
    eye = jnp.eye(LRU_HEADS, dtype=w.dtype)
    out = jnp.einsum('lnhk,nm->lnhmk', w, eye)
    return out.reshape(w.shape[0], LRU_WIDTH, LRU_WIDTH)


def kernel(x, meta_tokens, mix_norm, w_in, merge_bias, s5_lambda_re, s5_lambda_im, s5_log_dt, s5_b_re, s5_b_im, s5_c_re, s5_c_im, s5_d, s5_w_glu, s5_b_glu, s5_w_proj, lru_conv_w, lru_conv_b, lru_w_rgate, lru_b_rgate, lru_w_igate, lru_b_igate, lru_lambda, lru_w_proj, w_out, ffn_norm, dense_w_gate, dense_w_up, dense_w_down, router_w, router_b, moe_w_gate, moe_w_up, moe_w_down, final_norm):
    bsz, seq, d = x.shape
    depth = w_in.shape[0]
    assert d == D_MODEL and N_META + seq <= T_PAD
    n = bsz * T_PAD
    assert n % TM == 0 and n % TM_FFN == 0 and seq % TB_FINAL == 0 and TB_FINAL % N_META == 0

    meta = jnp.broadcast_to(meta_tokens[None].astype(x.dtype), (bsz, N_META, d))
    pad = jnp.zeros((bsz, T_PAD - N_META - seq, d), x.dtype)
    hs = jnp.concatenate([meta, x, pad], axis=1).reshape(n, d)

    row3 = lambda a: a[:, None, :]
    w_ri = jnp.concatenate([_head_blockdiag(lru_w_rgate), _head_blockdiag(lru_w_igate)], axis=-1).astype(BF16)
    b_ri = jnp.concatenate([lru_b_rgate, lru_b_igate], axis=-1)
    neg_sp = -LRU_C * jax.nn.softplus(-lru_lambda)
    dense = (dense_w_gate, dense_w_up, dense_w_down)
    n_moe = router_w.shape[0]
    moe = [w.reshape((n_moe * N_EXPERTS,) + w.shape[2:]) for w in (moe_w_gate, moe_w_up, moe_w_down)]
    s5_ops = jax.vmap(_s5_prep)(s5_lambda_re, s5_lambda_im, s5_log_dt, s5_b_re, s5_b_im, s5_c_re, s5_c_im, s5_d)
    rw_pad = jnp.pad(router_w, ((0, 0), (0, 0), (0, LANES - N_EXPERTS)))
    rb_pad = jnp.pad(router_b, ((0, 0), (0, LANES - N_EXPERTS)), constant_values=MASKED_LOGIT)

    for layer in range(depth):
        u_parts, x_lru, g_lru, gates = _in_proj(hs, row3(mix_norm), w_in, row3(merge_bias), layer)
        ys_parts = _s5_scan(u_parts, s5_ops, layer, bsz)
        y_lru = _lru(x_lru, g_lru, lru_conv_w, row3(lru_conv_b), w_ri, row3(b_ri), row3(neg_sp), layer, bsz)
        j = layer // 2
        router = (rw_pad, row3(rb_pad), j) if layer % 2 == 1 else None
        res = _merge(hs, ys_parts, y_lru, gates, s5_w_glu, row3(s5_b_glu), s5_w_proj, lru_w_proj, w_out,
                     row3(ffn_norm), layer, router)
        if layer % 2 == 0:
            hs, hn = res
            hs = _ffn(hn, hs, *dense, layer=j)
        else:
            hs, hn, route, route_t, counts = res
            pos, block_expert, n_used, pad_range = _moe_plan(route_t, counts, n)
            xs = _moe_dispatch(hn, pos, pad_range, n_used, block_expert.shape[0] * MOE_BLOCK)
            ys = _moe_ffn(xs, block_expert, n_used, *moe, first=j * N_EXPERTS)
            hs = _moe_combine(hs, route, ys, pos)

    return _final_norm(hs, final_norm[None, :], bsz, seq)
```

```python
import functools

import jax
import jax.numpy as jnp
from jax import lax
from jax.experimental import pallas as pl
from jax.experimental.pallas import tpu as pltpu

F32 = jnp.float32
BF16 = jnp.bfloat16

D_MODEL = 1024
N_META = 16
S5_WIDTH = 512
S5_GROUP = 16
S5_GROUPS = 32
S5_STATE = 64
LRU_WIDTH = 512
LRU_HEADS = 8
LRU_HEAD_DIM = 64
CONV_WIDTH = 4
LRU_C = 8.0
N_EXPERTS = 8
EPS = 1e-6

FOLD = 8
S5_PARTS = 4
PART_W = S5_WIDTH // S5_PARTS
PART_GROUPS = PART_W // S5_GROUP
PART_STATE = PART_GROUPS * S5_STATE
FOLD_W = FOLD * PART_W

T_PAD = 8256
ROWS = T_PAD // FOLD
TM = 688
TM_FFN = 688
FF_CHUNK = 1024
MOE_BLOCK = 512
GATHER_GROUP = 8
DISPATCH_RING = 3
LRU_CHUNK = 1032
LRU_UNROLL = 3
LRU_SCAN_UNROLL = 43
TB_FINAL = 512
VMEM_LIMIT = 56 * 1024 * 1024
LANES = 128
SUBLANES = 8
S5_SEG = ROWS // SUBLANES
ROW_TILES = D_MODEL // LANES
MASKED_LOGIT = float("-inf")


def _dot(a, b):
    return jnp.dot(a, b, preferred_element_type=F32)


def _const_spec(block_shape, index_map):
    return pl.BlockSpec(block_shape, index_map, pipeline_mode=pl.Buffered(1))


def _rms(x, g):
    ms = jnp.mean(x * x, axis=-1, keepdims=True)
    return x * lax.rsqrt(ms + EPS) * g


def _rows_to_tiles(ref, x):
    rows = x.shape[0]
    for s in range(ROW_TILES):
        ref[pl.ds(s, rows, stride=ROW_TILES), :] = x[:, s * LANES:(s + 1) * LANES]


def _rows_from_tiles(ref, rows):
    return jnp.concatenate([ref[pl.ds(s, rows, stride=ROW_TILES), :] for s in range(ROW_TILES)], axis=-1)


def _in_proj_kernel(hs_ref, g_ref, wf_ref, mb_ref, u_ref, xl_ref, gl_ref, gt_ref, w_ref):
    @pl.when(pl.program_id(0) == 0)
    def _():
        w_ref[0] = wf_ref[0].astype(BF16)

    hn = _rms(hs_ref[...], g_ref[0]).astype(BF16)
    u = _dot(hn, w_ref[0, :, 0:S5_WIDTH])
    for q in range(S5_PARTS):
        u_ref[q] = u[:, q * PART_W:(q + 1) * PART_W]
    o_x = S5_WIDTH
    o_g = o_x + LRU_WIDTH
    o_m = o_g + LRU_WIDTH
    xl_ref[...] = _dot(hn, w_ref[0, :, o_x:o_g]).astype(BF16)
    gl_ref[...] = _dot(hn, w_ref[0, :, o_g:o_m]).astype(BF16)
    z = _dot(hn, w_ref[0, :, o_m:]) + mb_ref[0]
    gt_ref[...] = jax.nn.sigmoid(z).astype(BF16)


def _in_proj(hs, mix_norm, w_in, merge_bias, layer):
    n = hs.shape[0]
    d_in = w_in.shape[-1]
    lay = lambda i: (layer, 0, 0)
    return pl.pallas_call(
        _in_proj_kernel,
        grid=(n // TM,),
        in_specs=[
            pl.BlockSpec((TM, D_MODEL), lambda i: (i, 0)),
            _const_spec((1, 1, D_MODEL), lay),
            _const_spec((1, D_MODEL, d_in), lay),
            _const_spec((1, 1, 2 * D_MODEL), lay),
        ],
        out_specs=[
            pl.BlockSpec((S5_PARTS, TM, PART_W), lambda i: (0, i, 0)),
            pl.BlockSpec((TM, LRU_WIDTH), lambda i: (i, 0)),
            pl.BlockSpec((TM, LRU_WIDTH), lambda i: (i, 0)),
            pl.BlockSpec((TM, 2 * D_MODEL), lambda i: (i, 0)),
        ],
        out_shape=[
            jax.ShapeDtypeStruct((S5_PARTS, n, PART_W), F32),
            jax.ShapeDtypeStruct((n, LRU_WIDTH), BF16),
            jax.ShapeDtypeStruct((n, LRU_WIDTH), BF16),
            jax.ShapeDtypeStruct((n, 2 * D_MODEL), BF16),
        ],
        scratch_shapes=[pltpu.VMEM((1, D_MODEL, d_in), BF16)],
        compiler_params=pltpu.CompilerParams(
            dimension_semantics=("arbitrary",), vmem_limit_bytes=VMEM_LIMIT),
        name="in_proj",
    )(hs, mix_norm, w_in, merge_bias)


def _s5_prep(lam_re, lam_im, log_dt, b_re, b_im, c_re, c_im, d_skip):
    dt = jnp.exp(log_dt)[:, None]
    mag = jnp.exp(lam_re * dt)
    a_re = mag * jnp.cos(lam_im * dt)
    a_im = mag * jnp.sin(lam_im * dt)
    den = lam_re * lam_re + lam_im * lam_im
    num_re = a_re - 1.0
    coef_re = (num_re * lam_re + a_im * lam_im) / den
    coef_im = (a_im * lam_re - num_re * lam_im) / den
    bb_re = coef_re[..., None] * b_re - coef_im[..., None] * b_im
    bb_im = coef_re[..., None] * b_im + coef_im[..., None] * b_re

    def cmul(xr, xi, yr, yi):
        return xr * yr - xi * yi, xr * yi + xi * yr

    def powers(br, bi, n):
        pr, pi = [jnp.ones_like(br)], [jnp.zeros_like(bi)]
        for _ in range(n):
            r, i = cmul(pr[-1], pi[-1], br, bi)
            pr.append(r)
            pi.append(i)
        return jnp.stack(pr), jnp.stack(pi)

    p_re, p_im = powers(a_re, a_im, FOLD)

    def per_part(x):
        lead = x.shape[:-3]
        xp = x.reshape(lead + (S5_PARTS, PART_GROUPS) + x.shape[-2:])
        return jnp.moveaxis(xp, len(lead), 0)

    rev_re = jnp.stack([p_re[FOLD - 1 - j] for j in range(FOLD)])
    rev_im = jnp.stack([p_im[FOLD - 1 - j] for j in range(FOLD)])
    wr, wi = cmul(rev_re[..., None], rev_im[..., None], bb_re[None], bb_im[None])
    w_ri = jnp.swapaxes(jnp.stack([wr, wi], axis=1), -1, -2)
    xq = jnp.transpose(per_part(w_ri), (0, 1, 3, 4, 2, 5)).reshape(S5_PARTS, FOLD_W, 2 * S5_STATE)

    ca_re, ca_im = cmul(c_re[None], c_im[None], p_re[:, :, None, :], p_im[:, :, None, :])
    bt_re = jnp.swapaxes(bb_re, -1, -2)[None, :, :, None, :]
    bt_im = jnp.swapaxes(bb_im, -1, -2)[None, :, :, None, :]
    taps = jnp.sum(ca_re[:FOLD, :, None] * bt_re - ca_im[:FOLD, :, None] * bt_im, axis=-1)
    skip = d_skip.reshape(S5_GROUPS, S5_GROUP)
    taps = taps.at[0].add(skip[:, :, None] * jnp.eye(S5_GROUP, dtype=F32)[None])
    rc = jnp.transpose(per_part(taps), (0, 2, 3, 1, 4)).reshape(S5_PARTS, PART_W, FOLD * S5_GROUP)

    v_ri = jnp.swapaxes(jnp.stack([ca_re[1:], -ca_im[1:]], axis=0), -1, -2)
    vc = jnp.transpose(per_part(v_ri), (0, 1, 3, 4, 2, 5)).reshape(S5_PARTS, 2 * PART_STATE, FOLD * S5_GROUP)

    def part_vec(x):
        lead = x.shape[:-2]
        xp = x.reshape(lead + (S5_PARTS, PART_STATE))
        return jnp.moveaxis(xp, -2, 0)

    row_re, row_im = p_re[FOLD], p_im[FOLD]
    seg_re, seg_im = jnp.ones_like(row_re), jnp.zeros_like(row_im)
    for bit in bin(S5_SEG)[2:]:
        seg_re, seg_im = cmul(seg_re, seg_im, seg_re, seg_im)
        if bit == '1':
            seg_re, seg_im = cmul(seg_re, seg_im, row_re, row_im)
    decay = jnp.stack([jnp.stack([part_vec(row_re), part_vec(row_im)], axis=1),
                       jnp.stack([part_vec(seg_re), part_vec(seg_im)], axis=1)], axis=1)
    return xq, rc, vc, decay[:, :, :, None, :]


def _iota2(shape):
    return (lax.broadcasted_iota(jnp.int32, shape, 0), lax.broadcasted_iota(jnp.int32, shape, 1))


def _s5_expand(xq, rc, vc, w1_s, tv_s):
    ps = PART_STATE
    lg_state, lg_group, lg_part = (v.bit_length() - 1 for v in (S5_STATE, S5_GROUP, PART_W))
    lg_pg = PART_GROUPS.bit_length() - 1
    grp = PART_GROUPS - 1
    one_hot = lambda m: jnp.where(m, 1.0, 0.0).astype(BF16)
    r, c = _iota2((2 * S5_STATE, 2 * ps))
    e1 = one_hot(((r >> lg_state) == (c >> (lg_state + lg_pg))) & ((r & (S5_STATE - 1)) == (c & (S5_STATE - 1))))
    r, c = _iota2((FOLD * S5_GROUP, FOLD_W))
    e2 = one_hot(((r >> lg_group) == (c >> lg_part)) & ((r & (S5_GROUP - 1)) == (c & (S5_GROUP - 1))))
    r, c = _iota2((FOLD_W, 2 * ps))
    m1 = ((r >> lg_group) & grp) == ((c >> lg_state) & grp)
    w1_s[...] = jnp.where(m1, _dot(xq.astype(BF16), e1), 0.0).astype(BF16)
    r, c = _iota2((PART_W, FOLD_W))
    m2 = (r >> lg_group) == ((c >> lg_group) & grp)
    r0 = jnp.where(m2, _dot(rc.astype(BF16), e2), 0.0).astype(BF16)
    for j in range(FOLD):
        if j == 0:
            blk = r0
        else:
            blk = jnp.concatenate([jnp.zeros((PART_W, j * PART_W), BF16), r0[:, :FOLD_W - j * PART_W]], axis=1)
        tv_s[j * PART_W:(j + 1) * PART_W, :] = blk
    r, c = _iota2((2 * ps, FOLD_W))
    m3 = ((r >> lg_state) & grp) == ((c >> lg_group) & grp)
    tv_s[FOLD_W:, :] = jnp.where(m3, _dot(vc.astype(BF16), e2), 0.0).astype(BF16)


def _s5_kernel(u_ref, xq_ref, rc_ref, vc_ref, dec_ref, y_ref, w1_s, tv_s, up_ref, f_ref, hp_ref):
    ps = PART_STATE
    tstride = S5_SEG * FOLD
    cols = lambda j: slice(j * PART_W, (j + 1) * PART_W)

    @pl.when(pl.program_id(1) == 0)
    def _():
        _s5_expand(xq_ref[0, 0], rc_ref[0, 0], vc_ref[0, 0], w1_s, tv_s)

    def fold_body(i, carry):
        r0 = pl.multiple_of(i * SUBLANES, SUBLANES)
        for j in range(FOLD):
            up_ref[pl.ds(r0, SUBLANES), cols(j)] = u_ref[0, 0, pl.ds(i * FOLD + j, SUBLANES, stride=tstride), :]
        return carry

    lax.fori_loop(0, S5_SEG, fold_body, 0)
    u = up_ref[...].astype(BF16)
    f_ref[...] = _dot(u, w1_s[...])
    ar = jnp.broadcast_to(dec_ref[0, 0, 0, 0], (SUBLANES, ps))
    ai = jnp.broadcast_to(dec_ref[0, 0, 0, 1], (SUBLANES, ps))

    def step(i, hr, hi):
        r0 = pl.multiple_of(i * SUBLANES, SUBLANES)
        return (ar * hr - ai * hi + f_ref[pl.ds(r0, SUBLANES), :ps],
                ar * hi + ai * hr + f_ref[pl.ds(r0, SUBLANES), ps:])

    zero = jnp.zeros((SUBLANES, ps), F32)
    er, ei = lax.fori_loop(0, S5_SEG, lambda i, c: step(i, *c), (zero, zero))
    sr = dec_ref[0, 0, 1, 0]
    si = dec_ref[0, 0, 1, 1]
    row = lax.broadcasted_iota(jnp.int32, (SUBLANES, ps), 0)
    nr, ni = zero, zero
    for sgm in range(SUBLANES - 1):
        lr = er + sr * nr - si * ni
        li = ei + sr * ni + si * nr
        nr = nr + jnp.where(row == sgm + 1, pltpu.roll(lr, 1, axis=0), 0.0)
        ni = ni + jnp.where(row == sgm + 1, pltpu.roll(li, 1, axis=0), 0.0)

    def state_body(i, carry):
        hr, hi = carry
        r0 = pl.multiple_of(i * SUBLANES, SUBLANES)
        hp_ref[pl.ds(r0, SUBLANES), :ps] = hr
        hp_ref[pl.ds(r0, SUBLANES), ps:] = hi
        return step(i, hr, hi)

    lax.fori_loop(0, S5_SEG, state_body, (nr, ni))
    hp = hp_ref[...].astype(BF16)
    wide = 2 * PART_W
    for c0 in range(0, FOLD_W, wide):
        y = _dot(u[:, :c0 + wide], tv_s[:c0 + wide, c0:c0 + wide]) + _dot(hp, tv_s[FOLD_W:, c0:c0 + wide])
        f_ref[:, c0:c0 + wide] = jax.nn.gelu(y)

    def unfold_body(i, carry):
        r0 = pl.multiple_of(i * SUBLANES, SUBLANES)
        for j in range(FOLD):
            y_ref[0, 0, pl.ds(i * FOLD + j, SUBLANES, stride=tstride), :] = f_ref[pl.ds(r0, SUBLANES), cols(j)]
        return carry

    lax.fori_loop(0, S5_SEG, unfold_body, 0)


def _s5_scan(u_parts, ops, layer, bsz):
    xq, rc, vc, decay = ops
    n = u_parts.shape[1]
    u4 = u_parts.reshape(S5_PARTS, bsz, T_PAD, PART_W)
    lay4 = lambda q, b: (layer, q, 0, 0)
    y4 = pl.pallas_call(
        _s5_kernel,
        grid=(S5_PARTS, bsz),
        in_specs=[
            pl.BlockSpec((1, 1, T_PAD, PART_W), lambda q, b: (q, b, 0, 0)),
            pl.BlockSpec((1, 1, FOLD_W, 2 * S5_STATE), lay4),
            pl.BlockSpec((1, 1, PART_W, FOLD * S5_GROUP), lay4),
            pl.BlockSpec((1, 1, 2 * PART_STATE, FOLD * S5_GROUP), lay4),
            pl.BlockSpec((1, 1, 2, 2, 1, PART_STATE), lambda q, b: (layer, q, 0, 0, 0, 0)),
        ],
        out_specs=pl.BlockSpec((1, 1, T_PAD, PART_W), lambda q, b: (q, b, 0, 0)),
        out_shape=jax.ShapeDtypeStruct((S5_PARTS, bsz, T_PAD, PART_W), F32),
        scratch_shapes=[
            pltpu.VMEM((FOLD_W, 2 * PART_STATE), BF16),
            pltpu.VMEM((FOLD_W + 2 * PART_STATE, FOLD_W), BF16),
            pltpu.VMEM((ROWS, FOLD_W), F32),
            pltpu.VMEM((ROWS, 2 * PART_STATE), F32),
            pltpu.VMEM((ROWS, 2 * PART_STATE), F32),
        ],
        compiler_params=pltpu.CompilerParams(
            dimension_semantics=("arbitrary", "arbitrary"), vmem_limit_bytes=VMEM_LIMIT),
        name="s5_scan",
    )(u4, xq, rc, vc, decay)
    return y4.reshape(S5_PARTS, n, PART_W)


def _lru_kernel(x_ref, g_ref, cw_ref, cb_ref, wri_ref, bri_ref, nsp_ref, o_ref,
                xs_ref, gs_ref, xc_ref, gp_ref, z_ref, a_ref, b_ref, os_ref, h_ref):
    tc = LRU_CHUNK
    c = LRU_WIDTH
    seg = tc // SUBLANES
    nq = c // LANES
    lanes = lambda q: slice(q * LANES, (q + 1) * LANES)
    halo = SUBLANES

    @pl.when(pl.program_id(1) == 0)
    def _():
        xs_ref[:, 0:halo, :] = jnp.zeros((nq, halo, LANES), F32)
        h_ref[...] = jnp.zeros((1, c), F32)

    x = x_ref[0].astype(F32)
    g = g_ref[0].astype(F32)
    for q in range(nq):
        xs_ref[q, halo:, :] = x[:, lanes(q)]
        gs_ref[q] = g[:, lanes(q)]
    taps = [[cw_ref[0, k:k + 1, lanes(q)] for k in range(CONV_WIDTH)] for q in range(nq)]
    bias = [cb_ref[0, :, lanes(q)] for q in range(nq)]

    def conv_body(i, carry):
        r0 = pl.multiple_of(i * SUBLANES, SUBLANES)
        for q in range(nq):
            acc = bias[q]
            for k in range(CONV_WIDTH):
                first = halo - (CONV_WIDTH - 1) + k + i
                acc = acc + taps[q][k] * xs_ref[q, pl.ds(first, SUBLANES, stride=seg), :]
            xc_ref[pl.ds(r0, SUBLANES), lanes(q)] = acc
            gp_ref[pl.ds(r0, SUBLANES), lanes(q)] = jax.nn.gelu(gs_ref[q, pl.ds(i, SUBLANES, stride=seg), :])
        return carry

    lax.fori_loop(0, seg, conv_body, 0, unroll=LRU_UNROLL)
    for q in range(nq):
        xs_ref[q, 0:halo, :] = xs_ref[q, tc:tc + halo, :]

    z_ref[...] = _dot(xc_ref[...].astype(BF16), wri_ref[0])
    b_r = jnp.broadcast_to(bri_ref[0, :, :c], (SUBLANES, c))
    b_i = jnp.broadcast_to(bri_ref[0, :, c:], (SUBLANES, c))
    nsp = jnp.broadcast_to(nsp_ref[0], (SUBLANES, c))

    def scan_body(i, carry):
        h, p = carry
        r0 = pl.multiple_of(i * SUBLANES, SUBLANES)
        a = jnp.exp(jax.nn.sigmoid(z_ref[pl.ds(r0, SUBLANES), :c] + b_r) * nsp)
        gated = jax.nn.sigmoid(z_ref[pl.ds(r0, SUBLANES), c:] + b_i) * xc_ref[pl.ds(r0, SUBLANES), :]
        h = a * h + jnp.sqrt(1.0 - a * a) * gated
        p = p * a
        b_ref[pl.ds(r0, SUBLANES), :] = h
        a_ref[pl.ds(r0, SUBLANES), :] = p
        return h, p

    h_end, p_end = lax.fori_loop(0, seg, scan_body, (jnp.zeros((SUBLANES, c), F32), jnp.ones((SUBLANES, c), F32)),
                                 unroll=LRU_SCAN_UNROLL)
    row = lax.broadcasted_iota(jnp.int32, (SUBLANES, c), 0)
    enter = jnp.where(row == 0, h_ref[...], 0.0)
    for sgm in range(SUBLANES - 1):
        leave = h_end + p_end * enter
        enter = enter + jnp.where(row == sgm + 1, pltpu.roll(leave, 1, axis=0), 0.0)
    h_ref[...] = (h_end + p_end * enter)[SUBLANES - 1:SUBLANES]

    def out_body(i, carry):
        r0 = pl.multiple_of(i * SUBLANES, SUBLANES)
        h = b_ref[pl.ds(r0, SUBLANES), :] + a_ref[pl.ds(r0, SUBLANES), :] * enter
        y = h * gp_ref[pl.ds(r0, SUBLANES), :]
        for q in range(nq):
            os_ref[q, pl.ds(i, SUBLANES, stride=seg), :] = y[:, lanes(q)]
        return carry

    lax.fori_loop(0, seg, out_body, 0)
    o_ref[0] = jnp.concatenate([os_ref[q] for q in range(nq)], axis=-1).astype(BF16)


def _lru(x_lru, g_lru, conv_w, conv_b, w_ri, b_ri, neg_sp, layer, bsz):
    n = x_lru.shape[0]
    c = LRU_WIDTH
    x3 = x_lru.reshape(bsz, T_PAD, c)
    g3 = g_lru.reshape(bsz, T_PAD, c)
    lay = lambda b, t: (layer, 0, 0)
    out = pl.pallas_call(
        _lru_kernel,
        grid=(bsz, T_PAD // LRU_CHUNK),
        in_specs=[
            pl.BlockSpec((1, LRU_CHUNK, c), lambda b, t: (b, t, 0)),
            pl.BlockSpec((1, LRU_CHUNK, c), lambda b, t: (b, t, 0)),
            _const_spec((1, CONV_WIDTH, c), lay),
            _const_spec((1, 1, c), lay),
            _const_spec((1, c, 2 * c), lay),
            _const_spec((1, 1, 2 * c), lay),
            _const_spec((1, 1, c), lay),
        ],
        out_specs=pl.BlockSpec((1, LRU_CHUNK, c), lambda b, t: (b, t, 0)),
        out_shape=jax.ShapeDtypeStruct((bsz, T_PAD, c), BF16),
        scratch_shapes=[
            pltpu.VMEM((c // LANES, LRU_CHUNK + SUBLANES, LANES), F32),
            pltpu.VMEM((c // LANES, LRU_CHUNK, LANES), F32),
            pltpu.VMEM((LRU_CHUNK, c), F32),
            pltpu.VMEM((LRU_CHUNK, c), F32),
            pltpu.VMEM((LRU_CHUNK, 2 * c), F32),
            pltpu.VMEM((LRU_CHUNK, c), F32),
            pltpu.VMEM((LRU_CHUNK, c), F32),
            pltpu.VMEM((c // LANES, LRU_CHUNK, LANES), F32),
            pltpu.VMEM((1, c), F32),
        ],
        compiler_params=pltpu.CompilerParams(
            dimension_semantics=("arbitrary", "arbitrary"), vmem_limit_bytes=VMEM_LIMIT),
        name="rglru",
    )(x3, g3, conv_w, conv_b, w_ri, b_ri, neg_sp)
    return out.reshape(n, c)


def _merge_kernel(hs_ref, ys_ref, yl_ref, gt_ref, wglu_ref, bglu_ref, wsp_ref, wlp_ref, wout_ref, g_ref,
                  *rest, with_router):
    if with_router:
        rw_ref, rb_ref, hs_out_ref, hn_ref, rt_ref, rtt_ref, cnt_ref, run_ref, tri_ref, *w_bf16 = rest
    else:
        hs_out_ref, hn_ref, *w_bf16 = rest
    wglu_b, wsp_b, wlp_b, wout_b = w_bf16

    @pl.when(pl.program_id(0) == 0)
    def _():
        for w_f32, w_b in zip((wglu_ref, wsp_ref, wlp_ref, wout_ref), w_bf16):
            w_b[...] = w_f32[0].astype(BF16)

    ys = jnp.concatenate([ys_ref[q] for q in range(S5_PARTS)], axis=-1)
    glu = ys * jax.nn.sigmoid(_dot(ys.astype(BF16), wglu_b[...]) + bglu_ref[0])
    y_a = _dot(glu.astype(BF16), wsp_b[...])
    y_b = _dot(yl_ref[...], wlp_b[...])
    y = gt_ref[:, :D_MODEL].astype(F32) * y_a + gt_ref[:, D_MODEL:].astype(F32) * y_b
    hs = hs_ref[...] + _dot(y.astype(BF16), wout_b[...])
    hs_out_ref[...] = hs
    hn = _rms(hs, g_ref[0])
    if not with_router:
        hn_ref[...] = hn.astype(BF16)
    else:
        _rows_to_tiles(hn_ref, hn)
        logits = _dot(hn.astype(BF16), rw_ref[0].astype(BF16)) + rb_ref[0]
        lane = lax.broadcasted_iota(jnp.int32, logits.shape, 1).astype(F32)
        m1 = jnp.max(logits, axis=-1, keepdims=True)
        i1 = jnp.min(jnp.where(logits == m1, lane, float(LANES)), axis=-1, keepdims=True)
        rest_l = jnp.where(lane == i1, MASKED_LOGIT, logits)
        m2 = jnp.max(rest_l, axis=-1, keepdims=True)
        i2 = jnp.min(jnp.where(rest_l == m2, lane, float(LANES)), axis=-1, keepdims=True)
        e2 = jnp.exp(m2 - m1)
        g1 = 1.0 / (1.0 + e2)
        g2 = e2 / (1.0 + e2)
        @pl.when(pl.program_id(0) == 0)
        def _():
            run_ref[...] = jnp.zeros_like(run_ref)
            r, c = _iota2(tri_ref.shape)
            tri_ref[...] = jnp.where(c < r, 1.0, 0.0).astype(BF16)

        first = lane == i1
        second = lane == i2
        picked = jnp.where(first | second, 1.0, 0.0)
        before = _dot(tri_ref[...], picked.astype(BF16)) + run_ref[...]
        rank1 = jnp.sum(jnp.where(first, before, 0.0), axis=-1, keepdims=True)
        rank2 = jnp.sum(jnp.where(second, before, 0.0), axis=-1, keepdims=True)
        run_ref[...] += jnp.sum(picked, axis=0, keepdims=True)
        cnt_ref[...] = jnp.broadcast_to(run_ref[...], cnt_ref.shape)
        rt = (jnp.where(lane == 0.0, i1, 0.0) + jnp.where(lane == 1.0, i2, 0.0)
              + jnp.where(lane == 2.0, g1, 0.0) + jnp.where(lane == 3.0, g2, 0.0)
              + jnp.where(lane == 4.0, rank1, 0.0) + jnp.where(lane == 5.0, rank2, 0.0))
        rt_ref[...] = rt
        r, c = _iota2((SUBLANES, LANES))
        pick = jnp.where(r == c, 1.0, 0.0).astype(BF16)
        hi = rt.astype(BF16)
        mid = (rt - hi.astype(F32)).astype(BF16)
        lo = (rt - hi.astype(F32) - mid.astype(F32)).astype(BF16)
        nt = (((1,), (1,)), ((), ()))
        rtt_ref[0] = (lax.dot_general(pick, hi, nt, preferred_element_type=F32)
                      + lax.dot_general(pick, mid, nt, preferred_element_type=F32)
                      + lax.dot_general(pick, lo, nt, preferred_element_type=F32))


def _merge(hs, ys_parts, y_lru, gates, w_glu, b_glu, w_sp, w_lp, w_out, ffn_norm, layer, router=None):
    n = hs.shape[0]
    tm = TM
    lay = lambda i: (layer, 0, 0)
    in_specs = [
        pl.BlockSpec((tm, D_MODEL), lambda i: (i, 0)),
        pl.BlockSpec((S5_PARTS, tm, PART_W), lambda i: (0, i, 0)),
        pl.BlockSpec((tm, LRU_WIDTH), lambda i: (i, 0)),
        pl.BlockSpec((tm, 2 * D_MODEL), lambda i: (i, 0)),
        _const_spec((1, S5_WIDTH, S5_WIDTH), lay),
        _const_spec((1, 1, S5_WIDTH), lay),
        _const_spec((1, S5_WIDTH, D_MODEL), lay),
        _const_spec((1, LRU_WIDTH, D_MODEL), lay),
        _const_spec((1, D_MODEL, D_MODEL), lay),
        _const_spec((1, 1, D_MODEL), lay),
    ]
    out_specs = [pl.BlockSpec((tm, D_MODEL), lambda i: (i, 0))]
    out_shape = [jax.ShapeDtypeStruct((n, D_MODEL), F32)]
    if router is None:
        out_specs.append(pl.BlockSpec((tm, D_MODEL), lambda i: (i, 0)))
        out_shape.append(jax.ShapeDtypeStruct((n, D_MODEL), BF16))
    else:
        out_specs.append(pl.BlockSpec((tm * ROW_TILES, LANES), lambda i: (i, 0)))
        out_shape.append(jax.ShapeDtypeStruct((n * ROW_TILES, LANES), F32))
    args = [hs, ys_parts, y_lru, gates, w_glu, b_glu, w_sp, w_lp, w_out, ffn_norm]
    if router is not None:
        rw, rb, j = router
        in_specs += [_const_spec((1, D_MODEL, LANES), lambda i: (j, 0, 0)),
                     _const_spec((1, 1, LANES), lambda i: (j, 0, 0))]
        out_specs += [pl.BlockSpec((tm, LANES), lambda i: (i, 0)),
                      pl.BlockSpec((1, SUBLANES, tm), lambda i: (i, 0, 0)),
                      pl.BlockSpec((SUBLANES, LANES), lambda i: (0, 0))]
        out_shape += [jax.ShapeDtypeStruct((n, LANES), F32),
                      jax.ShapeDtypeStruct((n // tm, SUBLANES, tm), F32),
                      jax.ShapeDtypeStruct((SUBLANES, LANES), F32)]
        args += [rw, rb]
    return pl.pallas_call(
        functools.partial(_merge_kernel, with_router=router is not None),
        grid=(n // tm,),
        in_specs=in_specs,
        out_specs=out_specs,
        out_shape=out_shape,
        scratch_shapes=([pltpu.VMEM((1, LANES), F32), pltpu.VMEM((tm, tm), BF16)] if router is not None else [])
        + [pltpu.VMEM(w.shape[1:], BF16) for w in (w_glu, w_sp, w_lp, w_out)],
        compiler_params=pltpu.CompilerParams(
            dimension_semantics=("arbitrary",), vmem_limit_bytes=VMEM_LIMIT),
        name="merge_router" if router is not None else "merge",
    )(*args)


def _ffn_kernel(x_ref, hs_ref, wg_ref, wu_ref, wd_ref, o_ref, h_ref):
    x = x_ref[...]
    for c0 in range(0, h_ref.shape[1], FF_CHUNK):
        g = _dot(x, wg_ref[0, :, c0:c0 + FF_CHUNK])
        u = _dot(x, wu_ref[0, :, c0:c0 + FF_CHUNK])
        h_ref[:, c0:c0 + FF_CHUNK] = (g * jax.nn.sigmoid(g) * u).astype(BF16)
    o_ref[...] = hs_ref[...] + _dot(h_ref[...], wd_ref[0])


def _ffn(hn, hs, w_gate, w_up, w_down, layer):
    n = hn.shape[0]
    ff = w_gate.shape[-1]
    lay = lambda i: (layer, 0, 0)
    return pl.pallas_call(
        _ffn_kernel,
        grid=(n // TM_FFN,),
        in_specs=[
            pl.BlockSpec((TM_FFN, D_MODEL), lambda i: (i, 0)),
            pl.BlockSpec((TM_FFN, D_MODEL), lambda i: (i, 0)),
            _const_spec((1, D_MODEL, ff), lay),
            _const_spec((1, D_MODEL, ff), lay),
            _const_spec((1, ff, D_MODEL), lay),
        ],
        out_specs=pl.BlockSpec((TM_FFN, D_MODEL), lambda i: (i, 0)),
        out_shape=jax.ShapeDtypeStruct((n, D_MODEL), F32),
        scratch_shapes=[pltpu.VMEM((TM_FFN, ff), BF16)],
        compiler_params=pltpu.CompilerParams(
            dimension_semantics=("arbitrary",), vmem_limit_bytes=VMEM_LIMIT),
        name="dense_ffn",
    )(hn, hs, w_gate, w_up, w_down)


def _moe_plan(route_t, counts_f, n):
    n_blocks = -(-2 * n // MOE_BLOCK) + N_EXPERTS
    e = jnp.stack([route_t[:, 0, :], route_t[:, 1, :]]).astype(jnp.int32)
    rank = jnp.stack([route_t[:, 4, :], route_t[:, 5, :]]).astype(jnp.int32)
    counts = counts_f[0, :N_EXPERTS].astype(jnp.int32)
    padded = ((counts + MOE_BLOCK - 1) // MOE_BLOCK) * MOE_BLOCK
    cum_pad = jnp.cumsum(padded)
    pad_start = cum_pad - padded
    pos = rank
    for x in range(N_EXPERTS):
        pos = pos + jnp.where(e == x, pad_start[x], 0)
    block_start = jnp.arange(n_blocks, dtype=jnp.int32) * MOE_BLOCK
    block_expert = jnp.minimum(jnp.sum((block_start[:, None] >= cum_pad[None, :]).astype(jnp.int32), axis=1),
                               N_EXPERTS - 1)
    n_used = (cum_pad[-1] // MOE_BLOCK).astype(jnp.int32).reshape(1)
    pad_range = jnp.stack([pad_start + counts, cum_pad], axis=1).reshape(2 * N_EXPERTS).astype(jnp.int32)
    return pos.reshape(2 * n), block_expert, n_used, pad_range


def _tile(ref, index):
    return ref.at[pl.ds(pl.multiple_of(index * ROW_TILES, ROW_TILES), ROW_TILES)]


def _tile_gather(src_hbm, dst, sem, rows, index_of):
    def body(grp, carry):
        r0 = grp * GATHER_GROUP
        index = [index_of(r0 + j) for j in range(GATHER_GROUP)]
        for j in range(GATHER_GROUP):
            pltpu.make_async_copy(_tile(src_hbm, index[j]), _tile(dst, r0 + j), sem).start(priority=j % 2)
        return carry
    lax.fori_loop(0, rows // GATHER_GROUP, body, 0)


def _tile_gather_wait(src_hbm, dst, sem, rows):
    pltpu.make_async_copy(src_hbm.at[pl.ds(0, rows * ROW_TILES)], dst, sem).wait()


def _moe_dispatch_kernel(pos_ref, pad_ref, nu_ref, x_hbm, xs_hbm, xbuf, zero_ref, sem_in, sem_out, sem_fill):
    t = pl.program_id(0)
    nt = pl.num_programs(0)
    n = nt * TM
    tile_rows = TM * ROW_TILES
    block_rows = MOE_BLOCK * ROW_TILES
    n_blocks = xs_hbm.shape[0] // block_rows

    def read(tile):
        buf = tile % DISPATCH_RING
        return pltpu.make_async_copy(x_hbm.at[pl.ds(pl.multiple_of(tile * tile_rows, tile_rows), tile_rows)],
                                     xbuf.at[buf], sem_in.at[buf])

    def scatter_wait(tile):
        buf = tile % DISPATCH_RING
        for k in range(2):
            pltpu.make_async_copy(xbuf.at[buf], xs_hbm.at[pl.ds(0, tile_rows)], sem_out.at[buf]).wait()

    @pl.when(t == 0)
    def _():
        read(0).start()
        zero_ref[...] = jnp.zeros_like(zero_ref)
        zero_tile = zero_ref.at[pl.ds(0, ROW_TILES)]
        for e in range(N_EXPERTS):
            def fill(slot, carry):
                pltpu.make_async_copy(zero_tile, _tile(xs_hbm, slot), sem_fill.at[0]).start()
                return carry

            def fill_wait(slot, carry):
                pltpu.make_async_copy(zero_tile, _tile(xs_hbm, slot), sem_fill.at[0]).wait()
                return carry
            lax.fori_loop(pad_ref[2 * e], pad_ref[2 * e + 1], fill, 0)
            lax.fori_loop(pad_ref[2 * e], pad_ref[2 * e + 1], fill_wait, 0)

        def block_of(blk):
            return xs_hbm.at[pl.ds(pl.multiple_of(blk * block_rows, block_rows), block_rows)]

        def fill_block(blk, carry):
            pltpu.make_async_copy(zero_ref, block_of(blk), sem_fill.at[0]).start()
            return carry

        def fill_block_wait(blk, carry):
            pltpu.make_async_copy(zero_ref, block_of(blk), sem_fill.at[0]).wait()
            return carry
        lax.fori_loop(nu_ref[0], n_blocks, fill_block, 0)
        lax.fori_loop(nu_ref[0], n_blocks, fill_block_wait, 0)

    @pl.when(t + 1 < nt)
    def _():
        @pl.when(t + 1 >= DISPATCH_RING)
        def _():
            scatter_wait(t + 1 - DISPATCH_RING)
        read(t + 1).start()

    read(t).wait()
    src = xbuf.at[t % DISPATCH_RING]
    for k in range(2):
        def put(grp, carry):
            r0 = grp * GATHER_GROUP
            slot = [pos_ref[k * n + t * TM + r0 + j] for j in range(GATHER_GROUP)]
            for j in range(GATHER_GROUP):
                pltpu.make_async_copy(_tile(src, r0 + j), _tile(xs_hbm, slot[j]),
                                      sem_out.at[t % DISPATCH_RING]).start(priority=j % 2)
            return carry
        lax.fori_loop(0, TM // GATHER_GROUP, put, 0)

    @pl.when(t == nt - 1)
    def _():
        for back in range(DISPATCH_RING - 1, -1, -1):
            @pl.when(t - back >= 0)
            def _():
                scatter_wait(t - back)


def _moe_dispatch(hn_tiles, pos, pad_range, n_used, n_slots):
    n = hn_tiles.shape[0] // ROW_TILES
    return pl.pallas_call(
        _moe_dispatch_kernel,
        grid_spec=pltpu.PrefetchScalarGridSpec(
            num_scalar_prefetch=3,
            grid=(n // TM,),
            in_specs=[pl.BlockSpec(memory_space=pl.ANY)],
            out_specs=pl.BlockSpec(memory_space=pl.ANY),
            scratch_shapes=[pltpu.VMEM((DISPATCH_RING, TM * ROW_TILES, LANES), F32),
                            pltpu.VMEM((MOE_BLOCK * ROW_TILES, LANES), F32),
                            pltpu.SemaphoreType.DMA((DISPATCH_RING,)),
                            pltpu.SemaphoreType.DMA((DISPATCH_RING,)),
                            pltpu.SemaphoreType.DMA((1,))],
        ),
        out_shape=jax.ShapeDtypeStruct((n_slots * ROW_TILES, LANES), F32),
        compiler_params=pltpu.CompilerParams(dimension_semantics=("arbitrary",)),
        name="moe_dispatch",
    )(pos, pad_range, n_used, hn_tiles)


def _moe_ffn_kernel(be_ref, nu_ref, x_ref, wg_ref, wu_ref, wd_ref, y_ref):
    i = pl.program_id(0)

    @pl.when(i < nu_ref[0])
    def _():
        x = _rows_from_tiles(x_ref, MOE_BLOCK).astype(BF16)
        g = _dot(x, wg_ref[0].astype(BF16))
        h = g * jax.nn.sigmoid(g) * _dot(x, wu_ref[0].astype(BF16))
        _rows_to_tiles(y_ref, _dot(h.astype(BF16), wd_ref[0].astype(BF16)))

    @pl.when(i >= nu_ref[0])
    def _():
        y_ref[...] = jnp.zeros_like(y_ref)


def _moe_ffn(xs_tiles, block_expert, n_used, w_gate, w_up, w_down, first):
    n_blocks = block_expert.shape[0]
    ff = w_gate.shape[-1]
    wmap = lambda i, be, nu: (first + be[i], 0, 0)
    return pl.pallas_call(
        _moe_ffn_kernel,
        grid_spec=pltpu.PrefetchScalarGridSpec(
            num_scalar_prefetch=2,
            grid=(n_blocks,),
            in_specs=[
                pl.BlockSpec((MOE_BLOCK * ROW_TILES, LANES), lambda i, be, nu: (jnp.maximum(jnp.minimum(i, nu[0] - 1), 0), 0)),
                pl.BlockSpec((1, D_MODEL, ff), wmap),
                pl.BlockSpec((1, D_MODEL, ff), wmap),
                pl.BlockSpec((1, ff, D_MODEL), wmap),
            ],
            out_specs=pl.BlockSpec((MOE_BLOCK * ROW_TILES, LANES), lambda i, be, nu: (i, 0)),
        ),
        out_shape=jax.ShapeDtypeStruct((n_blocks * MOE_BLOCK * ROW_TILES, LANES), F32),
        compiler_params=pltpu.CompilerParams(
            dimension_semantics=("arbitrary",), vmem_limit_bytes=VMEM_LIMIT),
        name="moe_ffn",
    )(block_expert, n_used, xs_tiles, w_gate, w_up, w_down)


def _moe_combine_kernel(pos_ref, hs_ref, rt_ref, ys_hbm, o_ref, ybuf, sem):
    i = pl.program_id(0)
    nt = pl.num_programs(0)
    slot = i % 2

    def start(t, s):
        for k in range(2):
            _tile_gather(ys_hbm, ybuf.at[s, k], sem.at[s], TM, lambda r: pos_ref[k * (nt * TM) + t * TM + r])

    @pl.when(i == 0)
    def _():
        start(0, 0)

    @pl.when(i + 1 < nt)
    def _():
        start(i + 1, 1 - slot)

    for k in range(2):
        _tile_gather_wait(ys_hbm, ybuf.at[slot, k], sem.at[slot], TM)
    rt = rt_ref[...]
    lane = lax.broadcasted_iota(jnp.int32, rt.shape, 1)
    g1 = jnp.sum(jnp.where(lane == 2, rt, 0.0), axis=-1, keepdims=True)
    g2 = jnp.sum(jnp.where(lane == 3, rt, 0.0), axis=-1, keepdims=True)
    o_ref[...] = (hs_ref[...] + g1 * _rows_from_tiles(ybuf.at[slot, 0], TM)
                  + g2 * _rows_from_tiles(ybuf.at[slot, 1], TM))


def _moe_combine(hs, route, ys_tiles, pos):
    n = hs.shape[0]
    return pl.pallas_call(
        _moe_combine_kernel,
        grid_spec=pltpu.PrefetchScalarGridSpec(
            num_scalar_prefetch=1,
            grid=(n // TM,),
            in_specs=[
                pl.BlockSpec((TM, D_MODEL), lambda i, p: (i, 0)),
                pl.BlockSpec((TM, LANES), lambda i, p: (i, 0)),
                pl.BlockSpec(memory_space=pl.ANY),
            ],
            out_specs=pl.BlockSpec((TM, D_MODEL), lambda i, p: (i, 0)),
            scratch_shapes=[pltpu.VMEM((2, 2, TM * ROW_TILES, LANES), F32),
                            pltpu.SemaphoreType.DMA((2,))],
        ),
        out_shape=jax.ShapeDtypeStruct((n, D_MODEL), F32),
        compiler_params=pltpu.CompilerParams(
            dimension_semantics=("arbitrary",), vmem_limit_bytes=VMEM_LIMIT),
        name="moe_combine",
    )(pos, hs, route, ys_tiles)


def _final_kernel(a_ref, b_ref, g_ref, o_ref):
    tb = a_ref.shape[1]
    o_ref[0, :tb - N_META] = _rms(a_ref[0, N_META:], g_ref[...])
    o_ref[0, tb - N_META:] = _rms(b_ref[0], g_ref[...])


def _final_norm(hs, g, bsz, seq):
    hs3 = hs.reshape(bsz, T_PAD, D_MODEL)
    return pl.pallas_call(
        _final_kernel,
        grid=(bsz, seq // TB_FINAL),
        in_specs=[pl.BlockSpec((1, TB_FINAL, D_MODEL), lambda b, i: (b, i, 0)),
                  pl.BlockSpec((1, N_META, D_MODEL), lambda b, i: (b, (i + 1) * (TB_FINAL // N_META), 0)),
                  _const_spec((1, D_MODEL), lambda b, i: (0, 0))],
        out_specs=pl.BlockSpec((1, TB_FINAL, D_MODEL), lambda b, i: (b, i, 0)),
        out_shape=jax.ShapeDtypeStruct((bsz, seq, D_MODEL), F32),
        compiler_params=pltpu.CompilerParams(
            dimension_semantics=("arbitrary", "arbitrary"), vmem_limit_bytes=VMEM_LIMIT),
        name="final_norm",
    )(hs3, hs3, g)


def _head_blockdiag(w):
    eye = jnp.eye(LRU_HEADS, dtype=w.dtype)
    out = jnp.einsum('lnhk,nm->lnhmk', w, eye)
    return out.reshape(w.shape[0], LRU_WIDTH, LRU_WIDTH)


def kernel(x, meta_tokens, mix_norm, w_in, merge_bias, s5_lambda_re, s5_lambda_im, s5_log_dt, s5_b_re, s5_b_im, s5_c_re, s5_c_im, s5_d, s5_w_glu, s5_b_glu, s5_w_proj, lru_conv_w, lru_conv_b, lru_w_rgate, lru_b_rgate, lru_w_igate, lru_b_igate, lru_lambda, lru_w_proj, w_out, ffn_norm, dense_w_gate, dense_w_up, dense_w_down, router_w, router_b, moe_w_gate, moe_w_up, moe_w_down, final_norm):
    bsz, seq, d = x.shape
    depth = w_in.shape[0]
    assert d == D_MODEL and N_META + seq <= T_PAD
    n = bsz * T_PAD
    assert n % TM == 0 and n % TM_FFN == 0 and seq % TB_FINAL == 0 and TB_FINAL % N_META == 0

    meta = jnp.broadcast_to(meta_tokens[None].astype(x.dtype), (bsz, N_META, d))
    pad = jnp.zeros((bsz, T_PAD - N_META - seq, d), x.dtype)
    hs = jnp.concatenate([meta, x, pad], axis=1).reshape(n, d)

    row3 = lambda a: a[:, None, :]
    w_ri = jnp.concatenate([_head_blockdiag(lru_w_rgate), _head_blockdiag(lru_w_igate)], axis=-1).astype(BF16)
    b_ri = jnp.concatenate([lru_b_rgate, lru_b_igate], axis=-1)
    neg_sp = -LRU_C * jax.nn.softplus(-lru_lambda)
    dense = [w.astype(BF16) for w in (dense_w_gate, dense_w_up, dense_w_down)]
    n_moe = router_w.shape[0]
    moe = [w.reshape((n_moe * N_EXPERTS,) + w.shape[2:]) for w in (moe_w_gate, moe_w_up, moe_w_down)]
    s5_ops = jax.vmap(_s5_prep)(s5_lambda_re, s5_lambda_im, s5_log_dt, s5_b_re, s5_b_im, s5_c_re, s5_c_im, s5_d)
    rw_pad = jnp.pad(router_w, ((0, 0), (0, 0), (0, LANES - N_EXPERTS)))
    rb_pad = jnp.pad(router_b, ((0, 0), (0, LANES - N_EXPERTS)), constant_values=MASKED_LOGIT)

    for layer in range(depth):
        u_parts, x_lru, g_lru, gates = _in_proj(hs, row3(mix_norm), w_in, row3(merge_bias), layer)
        ys_parts = _s5_scan(u_parts, s5_ops, layer, bsz)
        y_lru = _lru(x_lru, g_lru, lru_conv_w, row3(lru_conv_b), w_ri, row3(b_ri), row3(neg_sp), layer, bsz)
        j = layer // 2
        router = (rw_pad, row3(rb_pad), j) if layer % 2 == 1 else None
        res = _merge(hs, ys_parts, y_lru, gates, s5_w_glu, row3(s5_b_glu), s5_w_proj, lru_w_proj, w_out,
                     row3(ffn_norm), layer, router)
        if layer % 2 == 0:
            hs, hn = res
            hs = _ffn(hn, hs, *dense, layer=j)
        else:
            hs, hn, route, route_t, counts = res
            pos, block_expert, n_used, pad_range = _moe_plan(route_t, counts, n)
            xs = _moe_dispatch(hn, pos, pad_range, n_used, block_expert.shape[0] * MOE_BLOCK)
            ys = _moe_ffn(xs, block_expert, n_used, *moe, first=j * N_EXPERTS)
            hs = _moe_combine(hs, route, ys, pos)

    return _final_norm(hs, final_norm[None, :], bsz, seq)
```

```python
import functools

import jax
import jax.numpy as jnp
from jax import lax
from jax.experimental import pallas as pl
from jax.experimental.pallas import tpu as pltpu

F32 = jnp.float32
BF16 = jnp.bfloat16

D_MODEL = 1024
N_META = 16
S5_WIDTH = 512
S5_GROUP = 16
S5_GROUPS = 32
S5_STATE = 64
LRU_WIDTH = 512
LRU_HEADS = 8
LRU_HEAD_DIM = 64
CONV_WIDTH = 4
LRU_C = 8.0
N_EXPERTS = 8
EPS = 1e-6

FOLD = 8
S5_PARTS = 4
PART_W = S5_WIDTH // S5_PARTS
PART_GROUPS = PART_W // S5_GROUP
PART_STATE = PART_GROUPS * S5_STATE
FOLD_W = FOLD * PART_W

T_PAD = 8256
ROWS = T_PAD // FOLD
TM = 688
TM_FFN = 688
FF_CHUNK = 1024
MOE_BLOCK = 512
GATHER_GROUP = 8
DISPATCH_RING = 3
LRU_CHUNK = 1032
LRU_UNROLL = 3
LRU_SCAN_UNROLL = 43
TB_FINAL = 512
VMEM_LIMIT = 56 * 1024 * 1024
LANES = 128
SUBLANES = 8
S5_SEG = ROWS // SUBLANES
ROW_TILES = D_MODEL // LANES
MASKED_LOGIT = float("-inf")


def _dot(a, b):
    return jnp.dot(a, b, preferred_element_type=F32)


def _const_spec(block_shape, index_map):
    return pl.BlockSpec(block_shape, index_map, pipeline_mode=pl.Buffered(1))


def _rms(x, g):
    ms = jnp.mean(x * x, axis=-1, keepdims=True)
    return x * lax.rsqrt(ms + EPS) * g


def _rows_to_tiles(ref, x):
    rows = x.shape[0]
    for s in range(ROW_TILES):
        ref[pl.ds(s, rows, stride=ROW_TILES), :] = x[:, s * LANES:(s + 1) * LANES]


def _rows_from_tiles(ref, rows):
    return jnp.concatenate([ref[pl.ds(s, rows, stride=ROW_TILES), :] for s in range(ROW_TILES)], axis=-1)


def _in_proj_kernel(hs_ref, g_ref, wf_ref, mb_ref, u_ref, xl_ref, gl_ref, gt_ref, w_ref):
    @pl.when(pl.program_id(0) == 0)
    def _():
        w_ref[0] = wf_ref[0].astype(BF16)

    hn = _rms(hs_ref[...], g_ref[0]).astype(BF16)
    u = _dot(hn, w_ref[0, :, 0:S5_WIDTH])
    for q in range(S5_PARTS):
        u_ref[q] = u[:, q * PART_W:(q + 1) * PART_W]
    o_x = S5_WIDTH
    o_g = o_x + LRU_WIDTH
    o_m = o_g + LRU_WIDTH
    xl_ref[...] = _dot(hn, w_ref[0, :, o_x:o_g]).astype(BF16)
    gl_ref[...] = _dot(hn, w_ref[0, :, o_g:o_m]).astype(BF16)
    z = _dot(hn, w_ref[0, :, o_m:]) + mb_ref[0]
    gt_ref[...] = jax.nn.sigmoid(z).astype(BF16)


def _in_proj(hs, mix_norm, w_in, merge_bias, layer):
    n = hs.shape[0]
    d_in = w_in.shape[-1]
    lay = lambda i: (layer, 0, 0)
    return pl.pallas_call(
        _in_proj_kernel,
        grid=(n // TM,),
        in_specs=[
            pl.BlockSpec((TM, D_MODEL), lambda i: (i, 0)),
            _const_spec((1, 1, D_MODEL), lay),
            _const_spec((1, D_MODEL, d_in), lay),
            _const_spec((1, 1, 2 * D_MODEL), lay),
        ],
        out_specs=[
            pl.BlockSpec((S5_PARTS, TM, PART_W), lambda i: (0, i, 0)),
            pl.BlockSpec((TM, LRU_WIDTH), lambda i: (i, 0)),
            pl.BlockSpec((TM, LRU_WIDTH), lambda i: (i, 0)),
            pl.BlockSpec((TM, 2 * D_MODEL), lambda i: (i, 0)),
        ],
        out_shape=[
            jax.ShapeDtypeStruct((S5_PARTS, n, PART_W), F32),
            jax.ShapeDtypeStruct((n, LRU_WIDTH), BF16),
            jax.ShapeDtypeStruct((n, LRU_WIDTH), BF16),
            jax.ShapeDtypeStruct((n, 2 * D_MODEL), BF16),
        ],
        scratch_shapes=[pltpu.VMEM((1, D_MODEL, d_in), BF16)],
        compiler_params=pltpu.CompilerParams(
            dimension_semantics=("arbitrary",), vmem_limit_bytes=VMEM_LIMIT),
        name="in_proj",
    )(hs, mix_norm, w_in, merge_bias)


def _s5_prep(lam_re, lam_im, log_dt, b_re, b_im, c_re, c_im, d_skip):
    dt = jnp.exp(log_dt)[:, None]
    mag = jnp.exp(lam_re * dt)
    a_re = mag * jnp.cos(lam_im * dt)
    a_im = mag * jnp.sin(lam_im * dt)
    den = lam_re * lam_re + lam_im * lam_im
    num_re = a_re - 1.0
    coef_re = (num_re * lam_re + a_im * lam_im) / den
    coef_im = (a_im * lam_re - num_re * lam_im) / den
    bb_re = coef_re[..., None] * b_re - coef_im[..., None] * b_im
    bb_im = coef_re[..., None] * b_im + coef_im[..., None] * b_re

    def cmul(xr, xi, yr, yi):
        return xr * yr - xi * yi, xr * yi + xi * yr

    def powers(br, bi, n):
        pr, pi = [jnp.ones_like(br)], [jnp.zeros_like(bi)]
        for _ in range(n):
            r, i = cmul(pr[-1], pi[-1], br, bi)
            pr.append(r)
            pi.append(i)
        return jnp.stack(pr), jnp.stack(pi)

    p_re, p_im = powers(a_re, a_im, FOLD)

    def per_part(x):
        lead = x.shape[:-3]
        xp = x.reshape(lead + (S5_PARTS, PART_GROUPS) + x.shape[-2:])
        return jnp.moveaxis(xp, len(lead), 0)

    rev_re = jnp.stack([p_re[FOLD - 1 - j] for j in range(FOLD)])
    rev_im = jnp.stack([p_im[FOLD - 1 - j] for j in range(FOLD)])
    wr, wi = cmul(rev_re[..., None], rev_im[..., None], bb_re[None], bb_im[None])
    w_ri = jnp.swapaxes(jnp.stack([wr, wi], axis=1), -1, -2)
    xq = jnp.transpose(per_part(w_ri), (0, 1, 3, 4, 2, 5)).reshape(S5_PARTS, FOLD_W, 2 * S5_STATE)

    ca_re, ca_im = cmul(c_re[None], c_im[None], p_re[:, :, None, :], p_im[:, :, None, :])
    bt_re = jnp.swapaxes(bb_re, -1, -2)[None, :, :, None, :]
    bt_im = jnp.swapaxes(bb_im, -1, -2)[None, :, :, None, :]
    taps = jnp.sum(ca_re[:FOLD, :, None] * bt_re - ca_im[:FOLD, :, None] * bt_im, axis=-1)
    skip = d_skip.reshape(S5_GROUPS, S5_GROUP)
    taps = taps.at[0].add(skip[:, :, None] * jnp.eye(S5_GROUP, dtype=F32)[None])
    rc = jnp.transpose(per_part(taps), (0, 2, 3, 1, 4)).reshape(S5_PARTS, PART_W, FOLD * S5_GROUP)

    v_ri = jnp.swapaxes(jnp.stack([ca_re[1:], -ca_im[1:]], axis=0), -1, -2)
    vc = jnp.transpose(per_part(v_ri), (0, 1, 3, 4, 2, 5)).reshape(S5_PARTS, 2 * PART_STATE, FOLD * S5_GROUP)

    def part_vec(x):
        lead = x.shape[:-2]
        xp = x.reshape(lead + (S5_PARTS, PART_STATE))
        return jnp.moveaxis(xp, -2, 0)

    row_re, row_im = p_re[FOLD], p_im[FOLD]
    seg_re, seg_im = jnp.ones_like(row_re), jnp.zeros_like(row_im)
    for bit in bin(S5_SEG)[2:]:
        seg_re, seg_im = cmul(seg_re, seg_im, seg_re, seg_im)
        if bit == '1':
            seg_re, seg_im = cmul(seg_re, seg_im, row_re, row_im)
    decay = jnp.stack([jnp.stack([part_vec(row_re), part_vec(row_im)], axis=1),
                       jnp.stack([part_vec(seg_re), part_vec(seg_im)], axis=1)], axis=1)
    return xq, rc, vc, decay[:, :, :, None, :]


def _iota2(shape):
    return (lax.broadcasted_iota(jnp.int32, shape, 0), lax.broadcasted_iota(jnp.int32, shape, 1))


def _s5_expand(xq, rc, vc, w1_s, tv_s):
    ps = PART_STATE
    lg_state, lg_group, lg_part = (v.bit_length() - 1 for v in (S5_STATE, S5_GROUP, PART_W))
    lg_pg = PART_GROUPS.bit_length() - 1
    grp = PART_GROUPS - 1
    one_hot = lambda m: jnp.where(m, 1.0, 0.0).astype(BF16)
    r, c = _iota2((2 * S5_STATE, 2 * ps))
    e1 = one_hot(((r >> lg_state) == (c >> (lg_state + lg_pg))) & ((r & (S5_STATE - 1)) == (c & (S5_STATE - 1))))
    r, c = _iota2((FOLD * S5_GROUP, FOLD_W))
    e2 = one_hot(((r >> lg_group) == (c >> lg_part)) & ((r & (S5_GROUP - 1)) == (c & (S5_GROUP - 1))))
    r, c = _iota2((FOLD_W, 2 * ps))
    m1 = ((r >> lg_group) & grp) == ((c >> lg_state) & grp)
    w1_s[...] = jnp.where(m1, _dot(xq.astype(BF16), e1), 0.0).astype(BF16)
    r, c = _iota2((PART_W, FOLD_W))
    m2 = (r >> lg_group) == ((c >> lg_group) & grp)
    r0 = jnp.where(m2, _dot(rc.astype(BF16), e2), 0.0).astype(BF16)
    for j in range(FOLD):
        if j == 0:
            blk = r0
        else:
            blk = jnp.concatenate([jnp.zeros((PART_W, j * PART_W), BF16), r0[:, :FOLD_W - j * PART_W]], axis=1)
        tv_s[j * PART_W:(j + 1) * PART_W, :] = blk
    r, c = _iota2((2 * ps, FOLD_W))
    m3 = ((r >> lg_state) & grp) == ((c >> lg_group) & grp)
    tv_s[FOLD_W:, :] = jnp.where(m3, _dot(vc.astype(BF16), e2), 0.0).astype(BF16)


def _s5_kernel(u_ref, xq_ref, rc_ref, vc_ref, dec_ref, y_ref, w1_s, tv_s, up_ref, f_ref, hp_ref):
    ps = PART_STATE
    tstride = S5_SEG * FOLD
    cols = lambda j: slice(j * PART_W, (j + 1) * PART_W)

    @pl.when(pl.program_id(1) == 0)
    def _():
        _s5_expand(xq_ref[0, 0], rc_ref[0, 0], vc_ref[0, 0], w1_s, tv_s)

    def fold_body(i, carry):
        r0 = pl.multiple_of(i * SUBLANES, SUBLANES)
        for j in range(FOLD):
            up_ref[pl.ds(r0, SUBLANES), cols(j)] = u_ref[0, 0, pl.ds(i * FOLD + j, SUBLANES, stride=tstride), :]
        return carry

    lax.fori_loop(0, S5_SEG, fold_body, 0)
    u = up_ref[...].astype(BF16)
    f_ref[...] = _dot(u, w1_s[...])
    ar = jnp.broadcast_to(dec_ref[0, 0, 0, 0], (SUBLANES, ps))
    ai = jnp.broadcast_to(dec_ref[0, 0, 0, 1], (SUBLANES, ps))

    def step(i, hr, hi):
        r0 = pl.multiple_of(i * SUBLANES, SUBLANES)
        return (ar * hr - ai * hi + f_ref[pl.ds(r0, SUBLANES), :ps],
                ar * hi + ai * hr + f_ref[pl.ds(r0, SUBLANES), ps:])

    zero = jnp.zeros((SUBLANES, ps), F32)
    er, ei = lax.fori_loop(0, S5_SEG, lambda i, c: step(i, *c), (zero, zero))
    sr = dec_ref[0, 0, 1, 0]
    si = dec_ref[0, 0, 1, 1]
    row = lax.broadcasted_iota(jnp.int32, (SUBLANES, ps), 0)
    nr, ni = zero, zero
    for sgm in range(SUBLANES - 1):
        lr = er + sr * nr - si * ni
        li = ei + sr * ni + si * nr
        nr = nr + jnp.where(row == sgm + 1, pltpu.roll(lr, 1, axis=0), 0.0)
        ni = ni + jnp.where(row == sgm + 1, pltpu.roll(li, 1, axis=0), 0.0)

    def state_body(i, carry):
        hr, hi = carry
        r0 = pl.multiple_of(i * SUBLANES, SUBLANES)
        hp_ref[pl.ds(r0, SUBLANES), :ps] = hr
        hp_ref[pl.ds(r0, SUBLANES), ps:] = hi
        return step(i, hr, hi)

    lax.fori_loop(0, S5_SEG, state_body, (nr, ni))
    hp = hp_ref[...].astype(BF16)
    wide = 2 * PART_W
    for c0 in range(0, FOLD_W, wide):
        y = _dot(u[:, :c0 + wide], tv_s[:c0 + wide, c0:c0 + wide]) + _dot(hp, tv_s[FOLD_W:, c0:c0 + wide])
        f_ref[:, c0:c0 + wide] = jax.nn.gelu(y)

    def unfold_body(i, carry):
        r0 = pl.multiple_of(i * SUBLANES, SUBLANES)
        for j in range(FOLD):
            y_ref[0, 0, pl.ds(i * FOLD + j, SUBLANES, stride=tstride), :] = f_ref[pl.ds(r0, SUBLANES), cols(j)]
        return carry

    lax.fori_loop(0, S5_SEG, unfold_body, 0)


def _s5_scan(u_parts, ops, layer, bsz):
    xq, rc, vc, decay = ops
    n = u_parts.shape[1]
    u4 = u_parts.reshape(S5_PARTS, bsz, T_PAD, PART_W)
    lay4 = lambda q, b: (layer, q, 0, 0)
    y4 = pl.pallas_call(
        _s5_kernel,
        grid=(S5_PARTS, bsz),
        in_specs=[
            pl.BlockSpec((1, 1, T_PAD, PART_W), lambda q, b: (q, b, 0, 0)),
            pl.BlockSpec((1, 1, FOLD_W, 2 * S5_STATE), lay4),
            pl.BlockSpec((1, 1, PART_W, FOLD * S5_GROUP), lay4),
            pl.BlockSpec((1, 1, 2 * PART_STATE, FOLD * S5_GROUP), lay4),
            pl.BlockSpec((1, 1, 2, 2, 1, PART_STATE), lambda q, b: (layer, q, 0, 0, 0, 0)),
        ],
        out_specs=pl.BlockSpec((1, 1, T_PAD, PART_W), lambda q, b: (q, b, 0, 0)),
        out_shape=jax.ShapeDtypeStruct((S5_PARTS, bsz, T_PAD, PART_W), F32),
        scratch_shapes=[
            pltpu.VMEM((FOLD_W, 2 * PART_STATE), BF16),
            pltpu.VMEM((FOLD_W + 2 * PART_STATE, FOLD_W), BF16),
            pltpu.VMEM((ROWS, FOLD_W), F32),
            pltpu.VMEM((ROWS, 2 * PART_STATE), F32),
            pltpu.VMEM((ROWS, 2 * PART_STATE), F32),
        ],
        compiler_params=pltpu.CompilerParams(
            dimension_semantics=("arbitrary", "arbitrary"), vmem_limit_bytes=VMEM_LIMIT),
        name="s5_scan",
    )(u4, xq, rc, vc, decay)
    return y4.reshape(S5_PARTS, n, PART_W)


def _lru_kernel(x_ref, g_ref, cw_ref, cb_ref, wri_ref, bri_ref, nsp_ref, o_ref,
                xs_ref, gs_ref, xc_ref, gp_ref, z_ref, a_ref, b_ref, os_ref, h_ref):
    tc = LRU_CHUNK
    c = LRU_WIDTH
    seg = tc // SUBLANES
    nq = c // LANES
    lanes = lambda q: slice(q * LANES, (q + 1) * LANES)
    halo = SUBLANES

    @pl.when(pl.program_id(1) == 0)
    def _():
        xs_ref[:, 0:halo, :] = jnp.zeros((nq, halo, LANES), F32)
        h_ref[...] = jnp.zeros((1, c), F32)

    x = x_ref[0].astype(F32)
    g = g_ref[0].astype(F32)
    for q in range(nq):
        xs_ref[q, halo:, :] = x[:, lanes(q)]
        gs_ref[q] = g[:, lanes(q)]
    taps = [[cw_ref[0, k:k + 1, lanes(q)] for k in range(CONV_WIDTH)] for q in range(nq)]
    bias = [cb_ref[0, :, lanes(q)] for q in range(nq)]

    def conv_body(i, carry):
        r0 = pl.multiple_of(i * SUBLANES, SUBLANES)
        for q in range(nq):
            acc = bias[q]
            for k in range(CONV_WIDTH):
                first = halo - (CONV_WIDTH - 1) + k + i
                acc = acc + taps[q][k] * xs_ref[q, pl.ds(first, SUBLANES, stride=seg), :]
            xc_ref[pl.ds(r0, SUBLANES), lanes(q)] = acc
            gp_ref[pl.ds(r0, SUBLANES), lanes(q)] = jax.nn.gelu(gs_ref[q, pl.ds(i, SUBLANES, stride=seg), :])
        return carry

    lax.fori_loop(0, seg, conv_body, 0, unroll=LRU_UNROLL)
    for q in range(nq):
        xs_ref[q, 0:halo, :] = xs_ref[q, tc:tc + halo, :]

    z_ref[...] = _dot(xc_ref[...].astype(BF16), wri_ref[0])
    b_r = jnp.broadcast_to(bri_ref[0, :, :c], (SUBLANES, c))
    b_i = jnp.broadcast_to(bri_ref[0, :, c:], (SUBLANES, c))
    nsp = jnp.broadcast_to(nsp_ref[0], (SUBLANES, c))

    def scan_body(i, carry):
        h, p = carry
        r0 = pl.multiple_of(i * SUBLANES, SUBLANES)
        a = jnp.exp(jax.nn.sigmoid(z_ref[pl.ds(r0, SUBLANES), :c] + b_r) * nsp)
        gated = jax.nn.sigmoid(z_ref[pl.ds(r0, SUBLANES), c:] + b_i) * xc_ref[pl.ds(r0, SUBLANES), :]
        h = a * h + jnp.sqrt(1.0 - a * a) * gated
        p = p * a
        b_ref[pl.ds(r0, SUBLANES), :] = h
        a_ref[pl.ds(r0, SUBLANES), :] = p
        return h, p

    h_end, p_end = lax.fori_loop(0, seg, scan_body, (jnp.zeros((SUBLANES, c), F32), jnp.ones((SUBLANES, c), F32)),
                                 unroll=LRU_SCAN_UNROLL)
    row = lax.broadcasted_iota(jnp.int32, (SUBLANES, c), 0)
    enter = jnp.where(row == 0, h_ref[...], 0.0)
    for sgm in range(SUBLANES - 1):
        leave = h_end + p_end * enter
        enter = enter + jnp.where(row == sgm + 1, pltpu.roll(leave, 1, axis=0), 0.0)
    h_ref[...] = (h_end + p_end * enter)[SUBLANES - 1:SUBLANES]

    def out_body(i, carry):
        r0 = pl.multiple_of(i * SUBLANES, SUBLANES)
        h = b_ref[pl.ds(r0, SUBLANES), :] + a_ref[pl.ds(r0, SUBLANES), :] * enter
        y = h * gp_ref[pl.ds(r0, SUBLANES), :]
        for q in range(nq):
            os_ref[q, pl.ds(i, SUBLANES, stride=seg), :] = y[:, lanes(q)]
        return carry

    lax.fori_loop(0, seg, out_body, 0)
    o_ref[0] = jnp.concatenate([os_ref[q] for q in range(nq)], axis=-1).astype(BF16)


def _lru(x_lru, g_lru, conv_w, conv_b, w_ri, b_ri, neg_sp, layer, bsz):
    n = x_lru.shape[0]
    c = LRU_WIDTH
    x3 = x_lru.reshape(bsz, T_PAD, c)
    g3 = g_lru.reshape(bsz, T_PAD, c)
    lay = lambda b, t: (layer, 0, 0)
    out = pl.pallas_call(
        _lru_kernel,
        grid=(bsz, T_PAD // LRU_CHUNK),
        in_specs=[
            pl.BlockSpec((1, LRU_CHUNK, c), lambda b, t: (b, t, 0)),
            pl.BlockSpec((1, LRU_CHUNK, c), lambda b, t: (b, t, 0)),
            _const_spec((1, CONV_WIDTH, c), lay),
            _const_spec((1, 1, c), lay),
            _const_spec((1, c, 2 * c), lay),
            _const_spec((1, 1, 2 * c), lay),
            _const_spec((1, 1, c), lay),
        ],
        out_specs=pl.BlockSpec((1, LRU_CHUNK, c), lambda b, t: (b, t, 0)),
        out_shape=jax.ShapeDtypeStruct((bsz, T_PAD, c), BF16),
        scratch_shapes=[
            pltpu.VMEM((c // LANES, LRU_CHUNK + SUBLANES, LANES), F32),
            pltpu.VMEM((c // LANES, LRU_CHUNK, LANES), F32),
            pltpu.VMEM((LRU_CHUNK, c), F32),
            pltpu.VMEM((LRU_CHUNK, c), F32),
            pltpu.VMEM((LRU_CHUNK, 2 * c), F32),
            pltpu.VMEM((LRU_CHUNK, c), F32),
            pltpu.VMEM((LRU_CHUNK, c), F32),
            pltpu.VMEM((c // LANES, LRU_CHUNK, LANES), F32),
            pltpu.VMEM((1, c), F32),
        ],
        compiler_params=pltpu.CompilerParams(
            dimension_semantics=("arbitrary", "arbitrary"), vmem_limit_bytes=VMEM_LIMIT),
        name="rglru",
    )(x3, g3, conv_w, conv_b, w_ri, b_ri, neg_sp)
    return out.reshape(n, c)


def _merge_kernel(hs_ref, ys_ref, yl_ref, gt_ref, wglu_ref, bglu_ref, wsp_ref, wlp_ref, wout_ref, g_ref,
                  *rest, with_router):
    if with_router:
        rw_ref, rb_ref, hs_out_ref, hn_ref, rt_ref, rtt_ref, cnt_ref, run_ref, tri_ref, *w_bf16 = rest
    else:
        hs_out_ref, hn_ref, *w_bf16 = rest
    wglu_b, wsp_b, wlp_b, wout_b = w_bf16

    @pl.when(pl.program_id(0) == 0)
    def _():
        for w_f32, w_b in zip((wglu_ref, wsp_ref, wlp_ref, wout_ref), w_bf16):
            w_b[...] = w_f32[0].astype(BF16)

    ys = jnp.concatenate([ys_ref[q] for q in range(S5_PARTS)], axis=-1)
    glu = ys * jax.nn.sigmoid(_dot(ys.astype(BF16), wglu_b[...]) + bglu_ref[0])
    y_a = _dot(glu.astype(BF16), wsp_b[...])
    y_b = _dot(yl_ref[...], wlp_b[...])
    y = gt_ref[:, :D_MODEL].astype(F32) * y_a + gt_ref[:, D_MODEL:].astype(F32) * y_b
    hs = hs_ref[...] + _dot(y.astype(BF16), wout_b[...])
    hs_out_ref[...] = hs
    hn = _rms(hs, g_ref[0])
    if not with_router:
        hn_ref[...] = hn.astype(BF16)
    else:
        _rows_to_tiles(hn_ref, hn)
        logits = _dot(hn.astype(BF16), rw_ref[0].astype(BF16)) + rb_ref[0]
        lane = lax.broadcasted_iota(jnp.int32, logits.shape, 1).astype(F32)
        m1 = jnp.max(logits, axis=-1, keepdims=True)
        i1 = jnp.min(jnp.where(logits == m1, lane, float(LANES)), axis=-1, keepdims=True)
        rest_l = jnp.where(lane == i1, MASKED_LOGIT, logits)
        m2 = jnp.max(rest_l, axis=-1, keepdims=True)
        i2 = jnp.min(jnp.where(rest_l == m2, lane, float(LANES)), axis=-1, keepdims=True)
        e2 = jnp.exp(m2 - m1)
        g1 = 1.0 / (1.0 + e2)
        g2 = e2 / (1.0 + e2)
        @pl.when(pl.program_id(0) == 0)
        def _():
            run_ref[...] = jnp.zeros_like(run_ref)
            r, c = _iota2(tri_ref.shape)
            tri_ref[...] = jnp.where(c < r, 1.0, 0.0).astype(BF16)

        first = lane == i1
        second = lane == i2
        picked = jnp.where(first | second, 1.0, 0.0)
        before = _dot(tri_ref[...], picked.astype(BF16)) + run_ref[...]
        rank1 = jnp.sum(jnp.where(first, before, 0.0), axis=-1, keepdims=True)
        rank2 = jnp.sum(jnp.where(second, before, 0.0), axis=-1, keepdims=True)
        run_ref[...] += jnp.sum(picked, axis=0, keepdims=True)
        cnt_ref[...] = jnp.broadcast_to(run_ref[...], cnt_ref.shape)
        rt = (jnp.where(lane == 0.0, i1, 0.0) + jnp.where(lane == 1.0, i2, 0.0)
              + jnp.where(lane == 2.0, g1, 0.0) + jnp.where(lane == 3.0, g2, 0.0)
              + jnp.where(lane == 4.0, rank1, 0.0) + jnp.where(lane == 5.0, rank2, 0.0))
        rt_ref[...] = rt
        r, c = _iota2((SUBLANES, LANES))
        pick = jnp.where(r == c, 1.0, 0.0).astype(BF16)
        hi = rt.astype(BF16)
        mid = (rt - hi.astype(F32)).astype(BF16)
        lo = (rt - hi.astype(F32) - mid.astype(F32)).astype(BF16)
        nt = (((1,), (1,)), ((), ()))
        rtt_ref[0] = (lax.dot_general(pick, hi, nt, preferred_element_type=F32)
                      + lax.dot_general(pick, mid, nt, preferred_element_type=F32)
                      + lax.dot_general(pick, lo, nt, preferred_element_type=F32))


def _merge(hs, ys_parts, y_lru, gates, w_glu, b_glu, w_sp, w_lp, w_out, ffn_norm, layer, router=None):
    n = hs.shape[0]
    tm = TM
    lay = lambda i: (layer, 0, 0)
    in_specs = [
        pl.BlockSpec((tm, D_MODEL), lambda i: (i, 0)),
        pl.BlockSpec((S5_PARTS, tm, PART_W), lambda i: (0, i, 0)),
        pl.BlockSpec((tm, LRU_WIDTH), lambda i: (i, 0)),
        pl.BlockSpec((tm, 2 * D_MODEL), lambda i: (i, 0)),
        _const_spec((1, S5_WIDTH, S5_WIDTH), lay),
        _const_spec((1, 1, S5_WIDTH), lay),
        _const_spec((1, S5_WIDTH, D_MODEL), lay),
        _const_spec((1, LRU_WIDTH, D_MODEL), lay),
        _const_spec((1, D_MODEL, D_MODEL), lay),
        _const_spec((1, 1, D_MODEL), lay),
    ]
    out_specs = [pl.BlockSpec((tm, D_MODEL), lambda i: (i, 0))]
    out_shape = [jax.ShapeDtypeStruct((n, D_MODEL), F32)]
    if router is None:
        out_specs.append(pl.BlockSpec((tm, D_MODEL), lambda i: (i, 0)))
        out_shape.append(jax.ShapeDtypeStruct((n, D_MODEL), BF16))
    else:
        out_specs.append(pl.BlockSpec((tm * ROW_TILES, LANES), lambda i: (i, 0)))
        out_shape.append(jax.ShapeDtypeStruct((n * ROW_TILES, LANES), F32))
    args = [hs, ys_parts, y_lru, gates, w_glu, b_glu, w_sp, w_lp, w_out, ffn_norm]
    if router is not None:
        rw, rb, j = router
        in_specs += [_const_spec((1, D_MODEL, LANES), lambda i: (j, 0, 0)),
                     _const_spec((1, 1, LANES), lambda i: (j, 0, 0))]
        out_specs += [pl.BlockSpec((tm, LANES), lambda i: (i, 0)),
                      pl.BlockSpec((1, SUBLANES, tm), lambda i: (i, 0, 0)),
                      pl.BlockSpec((SUBLANES, LANES), lambda i: (0, 0))]
        out_shape += [jax.ShapeDtypeStruct((n, LANES), F32),
                      jax.ShapeDtypeStruct((n // tm, SUBLANES, tm), F32),
                      jax.ShapeDtypeStruct((SUBLANES, LANES), F32)]
        args += [rw, rb]
    return pl.pallas_call(
        functools.partial(_merge_kernel, with_router=router is not None),
        grid=(n // tm,),
        in_specs=in_specs,
        out_specs=out_specs,
        out_shape=out_shape,
        scratch_shapes=([pltpu.VMEM((1, LANES), F32), pltpu.VMEM((tm, tm), BF16)] if router is not None else [])
        + [pltpu.VMEM(w.shape[1:], BF16) for w in (w_glu, w_sp, w_lp, w_out)],
        compiler_params=pltpu.CompilerParams(
            dimension_semantics=("arbitrary",), vmem_limit_bytes=VMEM_LIMIT),
        name="merge_router" if router is not None else "merge",
    )(*args)


def _ffn_kernel(x_ref, hs_ref, wg_ref, wu_ref, wd_ref, o_ref, h_ref):
    x = x_ref[...]
    for c0 in range(0, h_ref.shape[1], FF_CHUNK):
        g = _dot(x, wg_ref[0, :, c0:c0 + FF_CHUNK])
        u = _dot(x, wu_ref[0, :, c0:c0 + FF_CHUNK])
        h_ref[:, c0:c0 + FF_CHUNK] = (g * jax.nn.sigmoid(g) * u).astype(BF16)
    o_ref[...] = hs_ref[...] + _dot(h_ref[...], wd_ref[0])


def _ffn(hn, hs, w_gate, w_up, w_down, layer):
    n = hn.shape[0]
    ff = w_gate.shape[-1]
    lay = lambda i: (layer, 0, 0)
    return pl.pallas_call(
        _ffn_kernel,
        grid=(n // TM_FFN,),
        in_specs=[
            pl.BlockSpec((TM_FFN, D_MODEL), lambda i: (i, 0)),
            pl.BlockSpec((TM_FFN, D_MODEL), lambda i: (i, 0)),
            _const_spec((1, D_MODEL, ff), lay),
            _const_spec((1, D_MODEL, ff), lay),
            _const_spec((1, ff, D_MODEL), lay),
        ],
        out_specs=pl.BlockSpec((TM_FFN, D_MODEL), lambda i: (i, 0)),
        out_shape=jax.ShapeDtypeStruct((n, D_MODEL), F32),
        scratch_shapes=[pltpu.VMEM((TM_FFN, ff), BF16)],
        compiler_params=pltpu.CompilerParams(
            dimension_semantics=("arbitrary",), vmem_limit_bytes=VMEM_LIMIT),
        name="dense_ffn",
    )(hn, hs, w_gate, w_up, w_down)


def _moe_plan(route_t, counts_f, n):
    n_blocks = -(-2 * n // MOE_BLOCK) + N_EXPERTS
    e = jnp.stack([route_t[:, 0, :], route_t[:, 1, :]]).astype(jnp.int32)
    rank = jnp.stack([route_t[:, 4, :], route_t[:, 5, :]]).astype(jnp.int32)
    counts = counts_f[0, :N_EXPERTS].astype(jnp.int32)
    padded = ((counts + MOE_BLOCK - 1) // MOE_BLOCK) * MOE_BLOCK
    cum_pad = jnp.cumsum(padded)
    pad_start = cum_pad - padded
    pos = rank
    for x in range(N_EXPERTS):
        pos = pos + jnp.where(e == x, pad_start[x], 0)
    block_start = jnp.arange(n_blocks, dtype=jnp.int32) * MOE_BLOCK
    block_expert = jnp.minimum(jnp.sum((block_start[:, None] >= cum_pad[None, :]).astype(jnp.int32), axis=1),
                               N_EXPERTS - 1)
    n_used = (cum_pad[-1] // MOE_BLOCK).astype(jnp.int32).reshape(1)
    pad_range = jnp.stack([pad_start + counts, cum_pad], axis=1).reshape(2 * N_EXPERTS).astype(jnp.int32)
    ids = jnp.arange(N_EXPERTS, dtype=jnp.int32)
    used = counts > 0
    run_buffer = (jnp.cumsum(used.astype(jnp.int32)) - 1) % 2
    later = jnp.where((ids[None, :] > ids[:, None]) & used[None, :], ids[None, :], N_EXPERTS)
    next_used = jnp.min(later, axis=1)
    next_used = jnp.where(next_used == N_EXPERTS, -1, next_used)
    of_block = lambda table: jnp.sum(jnp.where(block_expert[:, None] == ids[None, :], table[None, :], 0), axis=1)
    weight_plan = jnp.stack([block_expert,
                             (block_start == of_block(pad_start)).astype(jnp.int32),
                             of_block(run_buffer), of_block(next_used)]).astype(jnp.int32)
    return pos.reshape(2 * n), weight_plan, n_used, pad_range


def _tile(ref, index):
    return ref.at[pl.ds(pl.multiple_of(index * ROW_TILES, ROW_TILES), ROW_TILES)]


def _tile_gather(src_hbm, dst, sem, rows, index_of):
    def body(grp, carry):
        r0 = grp * GATHER_GROUP
        index = [index_of(r0 + j) for j in range(GATHER_GROUP)]
        for j in range(GATHER_GROUP):
            pltpu.make_async_copy(_tile(src_hbm, index[j]), _tile(dst, r0 + j), sem).start(priority=j % 2)
        return carry
    lax.fori_loop(0, rows // GATHER_GROUP, body, 0)


def _tile_gather_wait(src_hbm, dst, sem, rows):
    pltpu.make_async_copy(src_hbm.at[pl.ds(0, rows * ROW_TILES)], dst, sem).wait()


def _moe_dispatch_kernel(pos_ref, pad_ref, nu_ref, x_hbm, xs_hbm, xbuf, zero_ref, sem_in, sem_out, sem_fill):
    t = pl.program_id(0)
    nt = pl.num_programs(0)
    n = nt * TM
    tile_rows = TM * ROW_TILES
    block_rows = MOE_BLOCK * ROW_TILES
    n_blocks = xs_hbm.shape[0] // block_rows

    def read(tile):
        buf = tile % DISPATCH_RING
        return pltpu.make_async_copy(x_hbm.at[pl.ds(pl.multiple_of(tile * tile_rows, tile_rows), tile_rows)],
                                     xbuf.at[buf], sem_in.at[buf])

    def scatter_wait(tile):
        buf = tile % DISPATCH_RING
        for k in range(2):
            pltpu.make_async_copy(xbuf.at[buf], xs_hbm.at[pl.ds(0, tile_rows)], sem_out.at[buf]).wait()

    @pl.when(t == 0)
    def _():
        read(0).start()
        zero_ref[...] = jnp.zeros_like(zero_ref)
        zero_tile = zero_ref.at[pl.ds(0, ROW_TILES)]
        for e in range(N_EXPERTS):
            def fill(slot, carry):
                pltpu.make_async_copy(zero_tile, _tile(xs_hbm, slot), sem_fill.at[0]).start()
                return carry

            def fill_wait(slot, carry):
                pltpu.make_async_copy(zero_tile, _tile(xs_hbm, slot), sem_fill.at[0]).wait()
                return carry
            lax.fori_loop(pad_ref[2 * e], pad_ref[2 * e + 1], fill, 0)
            lax.fori_loop(pad_ref[2 * e], pad_ref[2 * e + 1], fill_wait, 0)

        def block_of(blk):
            return xs_hbm.at[pl.ds(pl.multiple_of(blk * block_rows, block_rows), block_rows)]

        def fill_block(blk, carry):
            pltpu.make_async_copy(zero_ref, block_of(blk), sem_fill.at[0]).start()
            return carry

        def fill_block_wait(blk, carry):
            pltpu.make_async_copy(zero_ref, block_of(blk), sem_fill.at[0]).wait()
            return carry
        lax.fori_loop(nu_ref[0], n_blocks, fill_block, 0)
        lax.fori_loop(nu_ref[0], n_blocks, fill_block_wait, 0)

    @pl.when(t + 1 < nt)
    def _():
        @pl.when(t + 1 >= DISPATCH_RING)
        def _():
            scatter_wait(t + 1 - DISPATCH_RING)
        read(t + 1).start()

    read(t).wait()
    src = xbuf.at[t % DISPATCH_RING]
    for k in range(2):
        def put(grp, carry):
            r0 = grp * GATHER_GROUP
            slot = [pos_ref[k * n + t * TM + r0 + j] for j in range(GATHER_GROUP)]
            for j in range(GATHER_GROUP):
                pltpu.make_async_copy(_tile(src, r0 + j), _tile(xs_hbm, slot[j]),
                                      sem_out.at[t % DISPATCH_RING]).start(priority=j % 2)
            return carry
        lax.fori_loop(0, TM // GATHER_GROUP, put, 0)

    @pl.when(t == nt - 1)
    def _():
        for back in range(DISPATCH_RING - 1, -1, -1):
            @pl.when(t - back >= 0)
            def _():
                scatter_wait(t - back)


def _moe_dispatch(hn_tiles, pos, pad_range, n_used, n_slots):
    n = hn_tiles.shape[0] // ROW_TILES
    return pl.pallas_call(
        _moe_dispatch_kernel,
        grid_spec=pltpu.PrefetchScalarGridSpec(
            num_scalar_prefetch=3,
            grid=(n // TM,),
            in_specs=[pl.BlockSpec(memory_space=pl.ANY)],
            out_specs=pl.BlockSpec(memory_space=pl.ANY),
            scratch_shapes=[pltpu.VMEM((DISPATCH_RING, TM * ROW_TILES, LANES), F32),
                            pltpu.VMEM((MOE_BLOCK * ROW_TILES, LANES), F32),
                            pltpu.SemaphoreType.DMA((DISPATCH_RING,)),
                            pltpu.SemaphoreType.DMA((DISPATCH_RING,)),
                            pltpu.SemaphoreType.DMA((1,))],
        ),
        out_shape=jax.ShapeDtypeStruct((n_slots * ROW_TILES, LANES), F32),
        compiler_params=pltpu.CompilerParams(dimension_semantics=("arbitrary",)),
        name="moe_dispatch",
    )(pos, pad_range, n_used, hn_tiles)


def _moe_ffn_kernel(be_ref, first_ref, buf_ref, next_ref, nu_ref, x_ref, wg_hbm, wu_hbm, wd_hbm, y_ref,
                    wg_buf, wu_buf, wd_buf, sem, *, first_expert):
    i = pl.program_id(0)
    sources = (wg_hbm, wu_hbm, wd_hbm)
    buffers = (wg_buf, wu_buf, wd_buf)

    def copies(expert, b):
        return [pltpu.make_async_copy(src.at[first_expert + expert], dst.at[b], sem.at[b, k])
                for k, (src, dst) in enumerate(zip(sources, buffers))]

    @pl.when(i < nu_ref[0])
    def _():
        b = buf_ref[i]

        @pl.when(first_ref[i] == 1)
        def _():
            @pl.when(i == 0)
            def _():
                for c in copies(be_ref[i], b):
                    c.start()
            for c in copies(be_ref[i], b):
                c.wait()

            @pl.when(next_ref[i] >= 0)
            def _():
                for c in copies(next_ref[i], 1 - b):
                    c.start()

        x = _rows_from_tiles(x_ref, MOE_BLOCK).astype(BF16)
        g = _dot(x, wg_buf[b].astype(BF16))
        h = g * jax.nn.sigmoid(g) * _dot(x, wu_buf[b].astype(BF16))
        _rows_to_tiles(y_ref, _dot(h.astype(BF16), wd_buf[b].astype(BF16)))

    @pl.when(i >= nu_ref[0])
    def _():
        y_ref[...] = jnp.zeros_like(y_ref)


def _moe_ffn(xs_tiles, weight_plan, n_used, w_gate, w_up, w_down, first):
    n_blocks = weight_plan.shape[1]
    ff = w_gate.shape[-1]
    return pl.pallas_call(
        functools.partial(_moe_ffn_kernel, first_expert=first),
        grid_spec=pltpu.PrefetchScalarGridSpec(
            num_scalar_prefetch=5,
            grid=(n_blocks,),
            in_specs=[
                pl.BlockSpec((MOE_BLOCK * ROW_TILES, LANES),
                             lambda i, be, fi, bu, nx, nu: (jnp.maximum(jnp.minimum(i, nu[0] - 1), 0), 0)),
                pl.BlockSpec(memory_space=pl.ANY),
                pl.BlockSpec(memory_space=pl.ANY),
                pl.BlockSpec(memory_space=pl.ANY),
            ],
            out_specs=pl.BlockSpec((MOE_BLOCK * ROW_TILES, LANES), lambda i, be, fi, bu, nx, nu: (i, 0)),
            scratch_shapes=[pltpu.VMEM((2, D_MODEL, ff), F32),
                            pltpu.VMEM((2, D_MODEL, ff), F32),
                            pltpu.VMEM((2, ff, D_MODEL), F32),
                            pltpu.SemaphoreType.DMA((2, 3))],
        ),
        out_shape=jax.ShapeDtypeStruct((n_blocks * MOE_BLOCK * ROW_TILES, LANES), F32),
        compiler_params=pltpu.CompilerParams(
            dimension_semantics=("arbitrary",), vmem_limit_bytes=VMEM_LIMIT),
        name="moe_ffn",
    )(weight_plan[0], weight_plan[1], weight_plan[2], weight_plan[3], n_used, xs_tiles, w_gate, w_up, w_down)


def _moe_combine_kernel(pos_ref, hs_ref, rt_ref, ys_hbm, o_ref, ybuf, sem):
    i = pl.program_id(0)
    nt = pl.num_programs(0)
    slot = i % 2

    def start(t, s):
        for k in range(2):
            _tile_gather(ys_hbm, ybuf.at[s, k], sem.at[s], TM, lambda r: pos_ref[k * (nt * TM) + t * TM + r])

    @pl.when(i == 0)
    def _():
        start(0, 0)

    @pl.when(i + 1 < nt)
    def _():
        start(i + 1, 1 - slot)

    for k in range(2):
        _tile_gather_wait(ys_hbm, ybuf.at[slot, k], sem.at[slot], TM)
    rt = rt_ref[...]
    lane = lax.broadcasted_iota(jnp.int32, rt.shape, 1)
    g1 = jnp.sum(jnp.where(lane == 2, rt, 0.0), axis=-1, keepdims=True)
    g2 = jnp.sum(jnp.where(lane == 3, rt, 0.0), axis=-1, keepdims=True)
    o_ref[...] = (hs_ref[...] + g1 * _rows_from_tiles(ybuf.at[slot, 0], TM)
                  + g2 * _rows_from_tiles(ybuf.at[slot, 1], TM))


def _moe_combine(hs, route, ys_tiles, pos):
    n = hs.shape[0]
    return pl.pallas_call(
        _moe_combine_kernel,
        grid_spec=pltpu.PrefetchScalarGridSpec(
            num_scalar_prefetch=1,
            grid=(n // TM,),
            in_specs=[
                pl.BlockSpec((TM, D_MODEL), lambda i, p: (i, 0)),
                pl.BlockSpec((TM, LANES), lambda i, p: (i, 0)),
                pl.BlockSpec(memory_space=pl.ANY),
            ],
            out_specs=pl.BlockSpec((TM, D_MODEL), lambda i, p: (i, 0)),
            scratch_shapes=[pltpu.VMEM((2, 2, TM * ROW_TILES, LANES), F32),
                            pltpu.SemaphoreType.DMA((2,))],
        ),
        out_shape=jax.ShapeDtypeStruct((n, D_MODEL), F32),
        compiler_params=pltpu.CompilerParams(
            dimension_semantics=("arbitrary",), vmem_limit_bytes=VMEM_LIMIT),
        name="moe_combine",
    )(pos, hs, route, ys_tiles)


def _final_kernel(a_ref, b_ref, g_ref, o_ref):
    tb = a_ref.shape[1]
    o_ref[0, :tb - N_META] = _rms(a_ref[0, N_META:], g_ref[...])
    o_ref[0, tb - N_META:] = _rms(b_ref[0], g_ref[...])


def _final_norm(hs, g, bsz, seq):
    hs3 = hs.reshape(bsz, T_PAD, D_MODEL)
    return pl.pallas_call(
        _final_kernel,
        grid=(bsz, seq // TB_FINAL),
        in_specs=[pl.BlockSpec((1, TB_FINAL, D_MODEL), lambda b, i: (b, i, 0)),
                  pl.BlockSpec((1, N_META, D_MODEL), lambda b, i: (b, (i + 1) * (TB_FINAL // N_META), 0)),
                  _const_spec((1, D_MODEL), lambda b, i: (0, 0))],
        out_specs=pl.BlockSpec((1, TB_FINAL, D_MODEL), lambda b, i: (b, i, 0)),
        out_shape=jax.ShapeDtypeStruct((bsz, seq, D_MODEL), F32),
        compiler_params=pltpu.CompilerParams(
            dimension_semantics=("arbitrary", "arbitrary"), vmem_limit_bytes=VMEM_LIMIT),
        name="final_norm",
    )(hs3, hs3, g)


def _head_blockdiag(w):
    eye = jnp.eye(LRU_HEADS, dtype=w.dtype)
    out = jnp.einsum('lnhk,nm->lnhmk', w, eye)
    return out.reshape(w.shape[0], LRU_WIDTH, LRU_WIDTH)


def kernel(x, meta_tokens, mix_norm, w_in, merge_bias, s5_lambda_re, s5_lambda_im, s5_log_dt, s5_b_re, s5_b_im, s5_c_re, s5_c_im, s5_d, s5_w_glu, s5_b_glu, s5_w_proj, lru_conv_w, lru_conv_b, lru_w_rgate, lru_b_rgate, lru_w_igate, lru_b_igate, lru_lambda, lru_w_proj, w_out, ffn_norm, dense_w_gate, dense_w_up, dense_w_down, router_w, router_b, moe_w_gate, moe_w_up, moe_w_down, final_norm):
    bsz, seq, d = x.shape
    depth = w_in.shape[0]
    assert d == D_MODEL and N_META + seq <= T_PAD
    n = bsz * T_PAD
    assert n % TM == 0 and n % TM_FFN == 0 and seq % TB_FINAL == 0 and TB_FINAL % N_META == 0

    meta = jnp.broadcast_to(meta_tokens[None].astype(x.dtype), (bsz, N_META, d))
    pad = jnp.zeros((bsz, T_PAD - N_META - seq, d), x.dtype)
    hs = jnp.concatenate([meta, x, pad], axis=1).reshape(n, d)

    row3 = lambda a: a[:, None, :]
    w_ri = jnp.concatenate([_head_blockdiag(lru_w_rgate), _head_blockdiag(lru_w_igate)], axis=-1).astype(BF16)
    b_ri = jnp.concatenate([lru_b_rgate, lru_b_igate], axis=-1)
    neg_sp = -LRU_C * jax.nn.softplus(-lru_lambda)
    dense = [w.astype(BF16) for w in (dense_w_gate, dense_w_up, dense_w_down)]
    n_moe = router_w.shape[0]
    moe = [w.reshape((n_moe * N_EXPERTS,) + w.shape[2:]) for w in (moe_w_gate, moe_w_up, moe_w_down)]
    s5_ops = jax.vmap(_s5_prep)(s5_lambda_re, s5_lambda_im, s5_log_dt, s5_b_re, s5_b_im, s5_c_re, s5_c_im, s5_d)
    rw_pad = jnp.pad(router_w, ((0, 0), (0, 0), (0, LANES - N_EXPERTS)))
    rb_pad = jnp.pad(router_b, ((0, 0), (0, LANES - N_EXPERTS)), constant_values=MASKED_LOGIT)

    for layer in range(depth):
        u_parts, x_lru, g_lru, gates = _in_proj(hs, row3(mix_norm), w_in, row3(merge_bias), layer)
        ys_parts = _s5_scan(u_parts, s5_ops, layer, bsz)
        y_lru = _lru(x_lru, g_lru, lru_conv_w, row3(lru_conv_b), w_ri, row3(b_ri), row3(neg_sp), layer, bsz)
        j = layer // 2
        router = (rw_pad, row3(rb_pad), j) if layer % 2 == 1 else None
        res = _merge(hs, ys_parts, y_lru, gates, s5_w_glu, row3(s5_b_glu), s5_w_proj, lru_w_proj, w_out,
                     row3(ffn_norm), layer, router)
        if layer % 2 == 0:
            hs, hn = res
            hs = _ffn(hn, hs, *dense, layer=j)
        else:
            hs, hn, route, route_t, counts = res
            pos, weight_plan, n_used, pad_range = _moe_plan(route_t, counts, n)
            xs = _moe_dispatch(hn, pos, pad_range, n_used, weight_plan.shape[1] * MOE_BLOCK)
            ys = _moe_ffn(xs, weight_plan, n_used, *moe, first=j * N_EXPERTS)
            hs = _moe_combine(hs, route, ys, pos)

    return _final_norm(hs, final_norm[None, :], bsz, seq)
```

```python
import functools

import jax
import jax.numpy as jnp
from jax import lax
from jax.experimental import pallas as pl
from jax.experimental.pallas import tpu as pltpu

F32 = jnp.float32
BF16 = jnp.bfloat16

D_MODEL = 1024
N_META = 16
S5_WIDTH = 512
S5_GROUP = 16
S5_GROUPS = 32
S5_STATE = 64
LRU_WIDTH = 512
LRU_HEADS = 8
CONV_WIDTH = 4
LRU_C = 8.0
N_EXPERTS = 8
EPS = 1e-6

FOLD = 8
S5_PARTS = 4
PART_W = S5_WIDTH // S5_PARTS
PART_GROUPS = PART_W // S5_GROUP
PART_STATE = PART_GROUPS * S5_STATE
FOLD_W = FOLD * PART_W

T_PAD = 8256
ROWS = T_PAD // FOLD
TM = 688
TM_FFN = 688
FF_CHUNK = 1024
MOE_BLOCK = 512
GATHER_GROUP = 8
DISPATCH_RING = 3
LRU_CHUNK = 1032
LRU_UNROLL = 3
LRU_SCAN_UNROLL = 43
TB_FINAL = 2048
VMEM_LIMIT = 56 * 1024 * 1024
LANES = 128
SUBLANES = 8
S5_SEG = ROWS // SUBLANES
ROW_TILES = D_MODEL // LANES
MASKED_LOGIT = float("-inf")


def _dot(a, b):
    return jnp.dot(a, b, preferred_element_type=F32)


def _const_spec(block_shape, index_map):
    return pl.BlockSpec(block_shape, index_map, pipeline_mode=pl.Buffered(1))


def _rms(x, g):
    ms = jnp.mean(x * x, axis=-1, keepdims=True)
    return x * lax.rsqrt(ms + EPS) * g


def _rows_to_tiles(ref, x):
    rows = x.shape[0]
    for s in range(ROW_TILES):
        ref[pl.ds(s, rows, stride=ROW_TILES), :] = x[:, s * LANES:(s + 1) * LANES]


def _rows_from_tiles(ref, rows):
    return jnp.concatenate([ref[pl.ds(s, rows, stride=ROW_TILES), :] for s in range(ROW_TILES)], axis=-1)


def _in_proj_kernel(hs_ref, g_ref, wf_ref, mb_ref, u_ref, xl_ref, gl_ref, gt_ref, w_ref):
    @pl.when(pl.program_id(0) == 0)
    def _():
        w_ref[0] = wf_ref[0].astype(BF16)

    hn = _rms(hs_ref[...], g_ref[0]).astype(BF16)
    u = _dot(hn, w_ref[0, :, 0:S5_WIDTH])
    for q in range(S5_PARTS):
        u_ref[q] = u[:, q * PART_W:(q + 1) * PART_W]
    o_x = S5_WIDTH
    o_g = o_x + LRU_WIDTH
    o_m = o_g + LRU_WIDTH
    xl_ref[...] = _dot(hn, w_ref[0, :, o_x:o_g]).astype(BF16)
    gl_ref[...] = _dot(hn, w_ref[0, :, o_g:o_m]).astype(BF16)
    z = _dot(hn, w_ref[0, :, o_m:]) + mb_ref[0]
    gt_ref[...] = jax.nn.sigmoid(z).astype(BF16)


def _in_proj(hs, mix_norm, w_in, merge_bias, layer):
    n = hs.shape[0]
    d_in = w_in.shape[-1]
    lay = lambda i: (layer, 0, 0)
    return pl.pallas_call(
        _in_proj_kernel,
        grid=(n // TM,),
        in_specs=[
            pl.BlockSpec((TM, D_MODEL), lambda i: (i, 0)),
            _const_spec((1, 1, D_MODEL), lay),
            _const_spec((1, D_MODEL, d_in), lay),
            _const_spec((1, 1, 2 * D_MODEL), lay),
        ],
        out_specs=[
            pl.BlockSpec((S5_PARTS, TM, PART_W), lambda i: (0, i, 0)),
            pl.BlockSpec((TM, LRU_WIDTH), lambda i: (i, 0)),
            pl.BlockSpec((TM, LRU_WIDTH), lambda i: (i, 0)),
            pl.BlockSpec((TM, 2 * D_MODEL), lambda i: (i, 0)),
        ],
        out_shape=[
            jax.ShapeDtypeStruct((S5_PARTS, n, PART_W), F32),
            jax.ShapeDtypeStruct((n, LRU_WIDTH), BF16),
            jax.ShapeDtypeStruct((n, LRU_WIDTH), BF16),
            jax.ShapeDtypeStruct((n, 2 * D_MODEL), BF16),
        ],
        scratch_shapes=[pltpu.VMEM((1, D_MODEL, d_in), BF16)],
        compiler_params=pltpu.CompilerParams(
            dimension_semantics=("arbitrary",), vmem_limit_bytes=VMEM_LIMIT),
        name="in_proj",
    )(hs, mix_norm, w_in, merge_bias)


def _s5_prep(lam_re, lam_im, log_dt, b_re, b_im, c_re, c_im, d_skip):
    dt = jnp.exp(log_dt)[:, None]
    mag = jnp.exp(lam_re * dt)
    a_re = mag * jnp.cos(lam_im * dt)
    a_im = mag * jnp.sin(lam_im * dt)
    den = lam_re * lam_re + lam_im * lam_im
    num_re = a_re - 1.0
    coef_re = (num_re * lam_re + a_im * lam_im) / den
    coef_im = (a_im * lam_re - num_re * lam_im) / den
    bb_re = coef_re[..., None] * b_re - coef_im[..., None] * b_im
    bb_im = coef_re[..., None] * b_im + coef_im[..., None] * b_re

    def cmul(xr, xi, yr, yi):
        return xr * yr - xi * yi, xr * yi + xi * yr

    def powers(br, bi, n):
        pr, pi = [jnp.ones_like(br)], [jnp.zeros_like(bi)]
        for _ in range(n):
            r, i = cmul(pr[-1], pi[-1], br, bi)
            pr.append(r)
            pi.append(i)
        return jnp.stack(pr), jnp.stack(pi)

    p_re, p_im = powers(a_re, a_im, FOLD)

    def per_part(x):
        lead = x.shape[:-3]
        xp = x.reshape(lead + (S5_PARTS, PART_GROUPS) + x.shape[-2:])
        return jnp.moveaxis(xp, len(lead), 0)

    rev_re = jnp.stack([p_re[FOLD - 1 - j] for j in range(FOLD)])
    rev_im = jnp.stack([p_im[FOLD - 1 - j] for j in range(FOLD)])
    wr, wi = cmul(rev_re[..., None], rev_im[..., None], bb_re[None], bb_im[None])
    w_ri = jnp.swapaxes(jnp.stack([wr, wi], axis=1), -1, -2)
    xq = jnp.transpose(per_part(w_ri), (0, 2, 1, 3, 4, 5)).reshape(S5_PARTS, 2, FOLD_W, S5_STATE)

    ca_re, ca_im = cmul(c_re[None], c_im[None], p_re[:, :, None, :], p_im[:, :, None, :])
    bt_re = jnp.swapaxes(bb_re, -1, -2)[None, :, :, None, :]
    bt_im = jnp.swapaxes(bb_im, -1, -2)[None, :, :, None, :]
    taps = jnp.sum(ca_re[:FOLD, :, None] * bt_re - ca_im[:FOLD, :, None] * bt_im, axis=-1)
    skip = d_skip.reshape(S5_GROUPS, S5_GROUP)
    taps = taps.at[0].add(skip[:, :, None] * jnp.eye(S5_GROUP, dtype=F32)[None])
    rc = jnp.transpose(per_part(taps), (0, 2, 3, 1, 4)).reshape(S5_PARTS, PART_W, FOLD * S5_GROUP)

    v_ri = jnp.swapaxes(jnp.stack([ca_re[1:], -ca_im[1:]], axis=0), -1, -2)
    vc = jnp.transpose(per_part(v_ri), (0, 1, 3, 4, 2, 5)).reshape(S5_PARTS, 2 * PART_STATE, FOLD * S5_GROUP)

    def part_vec(x):
        lead = x.shape[:-2]
        xp = x.reshape(lead + (S5_PARTS, PART_STATE))
        return jnp.moveaxis(xp, -2, 0)

    row_re, row_im = p_re[FOLD], p_im[FOLD]
    seg_re, seg_im = jnp.ones_like(row_re), jnp.zeros_like(row_im)
    for bit in bin(S5_SEG)[2:]:
        seg_re, seg_im = cmul(seg_re, seg_im, seg_re, seg_im)
        if bit == '1':
            seg_re, seg_im = cmul(seg_re, seg_im, row_re, row_im)
    decay = jnp.stack([jnp.stack([part_vec(row_re), part_vec(row_im)], axis=1),
                       jnp.stack([part_vec(seg_re), part_vec(seg_im)], axis=1)], axis=1)
    return xq, rc, vc, decay[:, :, :, None, :]


def _iota2(shape):
    return (lax.broadcasted_iota(jnp.int32, shape, 0), lax.broadcasted_iota(jnp.int32, shape, 1))


def _s5_expand(xq, rc, vc, w1_s, tv_s):
    ps = PART_STATE
    lg_state, lg_group, lg_part = (v.bit_length() - 1 for v in (S5_STATE, S5_GROUP, PART_W))
    grp = PART_GROUPS - 1
    one_hot = lambda m: jnp.where(m, 1.0, 0.0).astype(BF16)
    r, c = _iota2((S5_STATE, ps))
    e1 = one_hot(r == (c & (S5_STATE - 1)))
    r, c = _iota2((FOLD * S5_GROUP, FOLD_W))
    e2 = one_hot(((r >> lg_group) == (c >> lg_part)) & ((r & (S5_GROUP - 1)) == (c & (S5_GROUP - 1))))
    r, c = _iota2((FOLD_W, ps))
    m1 = ((r >> lg_group) & grp) == (c >> lg_state)
    for ri in range(2):
        w1_s[:, ri * ps:(ri + 1) * ps] = jnp.where(m1, _dot(xq[ri].astype(BF16), e1), 0.0).astype(BF16)
    r, c = _iota2((PART_W, FOLD_W))
    m2 = (r >> lg_group) == ((c >> lg_group) & grp)
    r0 = jnp.where(m2, _dot(rc.astype(BF16), e2), 0.0).astype(BF16)
    for j in range(FOLD):
        if j == 0:
            blk = r0
        else:
            blk = jnp.concatenate([jnp.zeros((PART_W, j * PART_W), BF16), r0[:, :FOLD_W - j * PART_W]], axis=1)
        tv_s[j * PART_W:(j + 1) * PART_W, :] = blk
    r, c = _iota2((2 * ps, FOLD_W))
    m3 = ((r >> lg_state) & grp) == ((c >> lg_group) & grp)
    tv_s[FOLD_W:, :] = jnp.where(m3, _dot(vc.astype(BF16), e2), 0.0).astype(BF16)


def _s5_kernel(u_ref, xq_ref, rc_ref, vc_ref, dec_ref, y_ref, w1_s, tv_s, up_ref, f_ref, hp_ref):
    ps = PART_STATE
    tstride = S5_SEG * FOLD
    cols = lambda j: slice(j * PART_W, (j + 1) * PART_W)

    @pl.when(pl.program_id(1) == 0)
    def _():
        _s5_expand(xq_ref[0, 0], rc_ref[0, 0], vc_ref[0, 0], w1_s, tv_s)

    def fold_body(i, carry):
        r0 = pl.multiple_of(i * SUBLANES, SUBLANES)
        for j in range(FOLD):
            up_ref[pl.ds(r0, SUBLANES), cols(j)] = u_ref[0, 0, pl.ds(i * FOLD + j, SUBLANES, stride=tstride), :]
        return carry

    lax.fori_loop(0, S5_SEG, fold_body, 0)
    u = up_ref[...].astype(BF16)
    f_ref[...] = _dot(u, w1_s[...])
    ar = jnp.broadcast_to(dec_ref[0, 0, 0, 0], (SUBLANES, ps))
    ai = jnp.broadcast_to(dec_ref[0, 0, 0, 1], (SUBLANES, ps))

    def step(i, hr, hi):
        r0 = pl.multiple_of(i * SUBLANES, SUBLANES)
        return (ar * hr - ai * hi + f_ref[pl.ds(r0, SUBLANES), :ps],
                ar * hi + ai * hr + f_ref[pl.ds(r0, SUBLANES), ps:])

    zero = jnp.zeros((SUBLANES, ps), F32)
    er, ei = lax.fori_loop(0, S5_SEG, lambda i, c: step(i, *c), (zero, zero))
    sr = dec_ref[0, 0, 1, 0]
    si = dec_ref[0, 0, 1, 1]
    row = lax.broadcasted_iota(jnp.int32, (SUBLANES, ps), 0)
    nr, ni = zero, zero
    for sgm in range(SUBLANES - 1):
        lr = er + sr * nr - si * ni
        li = ei + sr * ni + si * nr
        nr = nr + jnp.where(row == sgm + 1, pltpu.roll(lr, 1, axis=0), 0.0)
        ni = ni + jnp.where(row == sgm + 1, pltpu.roll(li, 1, axis=0), 0.0)

    def state_body(i, carry):
        hr, hi = carry
        r0 = pl.multiple_of(i * SUBLANES, SUBLANES)
        hp_ref[pl.ds(r0, SUBLANES), :ps] = hr
        hp_ref[pl.ds(r0, SUBLANES), ps:] = hi
        return step(i, hr, hi)

    lax.fori_loop(0, S5_SEG, state_body, (nr, ni))
    hp = hp_ref[...].astype(BF16)
    wide = 2 * PART_W
    for c0 in range(0, FOLD_W, wide):
        y = _dot(u[:, :c0 + wide], tv_s[:c0 + wide, c0:c0 + wide]) + _dot(hp, tv_s[FOLD_W:, c0:c0 + wide])
        f_ref[:, c0:c0 + wide] = jax.nn.gelu(y)

    def unfold_body(i, carry):
        r0 = pl.multiple_of(i * SUBLANES, SUBLANES)
        for j in range(FOLD):
            y_ref[0, 0, pl.ds(i * FOLD + j, SUBLANES, stride=tstride), :] = f_ref[pl.ds(r0, SUBLANES), cols(j)]
        return carry

    lax.fori_loop(0, S5_SEG, unfold_body, 0)


def _s5_scan(u_parts, ops, layer, bsz):
    xq, rc, vc, decay = ops
    n = u_parts.shape[1]
    u4 = u_parts.reshape(S5_PARTS, bsz, T_PAD, PART_W)
    lay4 = lambda q, b: (layer, q, 0, 0)
    y4 = pl.pallas_call(
        _s5_kernel,
        grid=(S5_PARTS, bsz),
        in_specs=[
            pl.BlockSpec((1, 1, T_PAD, PART_W), lambda q, b: (q, b, 0, 0)),
            pl.BlockSpec((1, 1, 2, FOLD_W, S5_STATE), lambda q, b: (layer, q, 0, 0, 0)),
            pl.BlockSpec((1, 1, PART_W, FOLD * S5_GROUP), lay4),
            pl.BlockSpec((1, 1, 2 * PART_STATE, FOLD * S5_GROUP), lay4),
            pl.BlockSpec((1, 1, 2, 2, 1, PART_STATE), lambda q, b: (layer, q, 0, 0, 0, 0)),
        ],
        out_specs=pl.BlockSpec((1, 1, T_PAD, PART_W), lambda q, b: (q, b, 0, 0)),
        out_shape=jax.ShapeDtypeStruct((S5_PARTS, bsz, T_PAD, PART_W), F32),
        scratch_shapes=[
            pltpu.VMEM((FOLD_W, 2 * PART_STATE), BF16),
            pltpu.VMEM((FOLD_W + 2 * PART_STATE, FOLD_W), BF16),
            pltpu.VMEM((ROWS, FOLD_W), F32),
            pltpu.VMEM((ROWS, 2 * PART_STATE), F32),
            pltpu.VMEM((ROWS, 2 * PART_STATE), F32),
        ],
        compiler_params=pltpu.CompilerParams(
            dimension_semantics=("arbitrary", "arbitrary"), vmem_limit_bytes=VMEM_LIMIT),
        name="s5_scan",
    )(u4, xq, rc, vc, decay)
    return y4.reshape(S5_PARTS, n, PART_W)


def _lru_kernel(x_ref, g_ref, cw_ref, cb_ref, wri_ref, bri_ref, nsp_ref, o_ref,
                xs_ref, gs_ref, xc_ref, gp_ref, z_ref, a_ref, b_ref, os_ref, h_ref):
    tc = LRU_CHUNK
    c = LRU_WIDTH
    seg = tc // SUBLANES
    nq = c // LANES
    lanes = lambda q: slice(q * LANES, (q + 1) * LANES)
    halo = SUBLANES

    @pl.when(pl.program_id(1) == 0)
    def _():
        xs_ref[:, 0:halo, :] = jnp.zeros((nq, halo, LANES), F32)
        h_ref[...] = jnp.zeros((1, c), F32)

    x = x_ref[0].astype(F32)
    g = g_ref[0].astype(F32)
    for q in range(nq):
        xs_ref[q, halo:, :] = x[:, lanes(q)]
        gs_ref[q] = g[:, lanes(q)]
    taps = [[cw_ref[0, k:k + 1, lanes(q)] for k in range(CONV_WIDTH)] for q in range(nq)]
    bias = [cb_ref[0, :, lanes(q)] for q in range(nq)]

    def conv_body(i, carry):
        r0 = pl.multiple_of(i * SUBLANES, SUBLANES)
        for q in range(nq):
            acc = bias[q]
            for k in range(CONV_WIDTH):
                first = halo - (CONV_WIDTH - 1) + k + i
                acc = acc + taps[q][k] * xs_ref[q, pl.ds(first, SUBLANES, stride=seg), :]
            xc_ref[pl.ds(r0, SUBLANES), lanes(q)] = acc
            gp_ref[pl.ds(r0, SUBLANES), lanes(q)] = jax.nn.gelu(gs_ref[q, pl.ds(i, SUBLANES, stride=seg), :])
        return carry

    lax.fori_loop(0, seg, conv_body, 0, unroll=LRU_UNROLL)
    for q in range(nq):
        xs_ref[q, 0:halo, :] = xs_ref[q, tc:tc + halo, :]

    z_ref[...] = _dot(xc_ref[...].astype(BF16), wri_ref[0])
    b_r = jnp.broadcast_to(bri_ref[0, :, :c], (SUBLANES, c))
    b_i = jnp.broadcast_to(bri_ref[0, :, c:], (SUBLANES, c))
    nsp = jnp.broadcast_to(nsp_ref[0], (SUBLANES, c))

    def scan_body(i, carry):
        h, p = carry
        r0 = pl.multiple_of(i * SUBLANES, SUBLANES)
        a = jnp.exp(jax.nn.sigmoid(z_ref[pl.ds(r0, SUBLANES), :c] + b_r) * nsp)
        gated = jax.nn.sigmoid(z_ref[pl.ds(r0, SUBLANES), c:] + b_i) * xc_ref[pl.ds(r0, SUBLANES), :]
        h = a * h + jnp.sqrt(1.0 - a * a) * gated
        p = p * a
        b_ref[pl.ds(r0, SUBLANES), :] = h
        a_ref[pl.ds(r0, SUBLANES), :] = p
        return h, p

    h_end, p_end = lax.fori_loop(0, seg, scan_body, (jnp.zeros((SUBLANES, c), F32), jnp.ones((SUBLANES, c), F32)),
                                 unroll=LRU_SCAN_UNROLL)
    row = lax.broadcasted_iota(jnp.int32, (SUBLANES, c), 0)
    enter = jnp.where(row == 0, h_ref[...], 0.0)
    for sgm in range(SUBLANES - 1):
        leave = h_end + p_end * enter
        enter = enter + jnp.where(row == sgm + 1, pltpu.roll(leave, 1, axis=0), 0.0)
    h_ref[...] = (h_end + p_end * enter)[SUBLANES - 1:SUBLANES]

    def out_body(i, carry):
        r0 = pl.multiple_of(i * SUBLANES, SUBLANES)
        h = b_ref[pl.ds(r0, SUBLANES), :] + a_ref[pl.ds(r0, SUBLANES), :] * enter
        y = h * gp_ref[pl.ds(r0, SUBLANES), :]
        for q in range(nq):
            os_ref[q, pl.ds(i, SUBLANES, stride=seg), :] = y[:, lanes(q)]
        return carry

    lax.fori_loop(0, seg, out_body, 0)
    o_ref[0] = jnp.concatenate([os_ref[q] for q in range(nq)], axis=-1).astype(BF16)


def _lru(x_lru, g_lru, conv_w, conv_b, w_ri, b_ri, neg_sp, layer, bsz):
    n = x_lru.shape[0]
    c = LRU_WIDTH
    x3 = x_lru.reshape(bsz, T_PAD, c)
    g3 = g_lru.reshape(bsz, T_PAD, c)
    lay = lambda b, t: (layer, 0, 0)
    out = pl.pallas_call(
        _lru_kernel,
        grid=(bsz, T_PAD // LRU_CHUNK),
        in_specs=[
            pl.BlockSpec((1, LRU_CHUNK, c), lambda b, t: (b, t, 0)),
            pl.BlockSpec((1, LRU_CHUNK, c), lambda b, t: (b, t, 0)),
            _const_spec((1, CONV_WIDTH, c), lay),
            _const_spec((1, 1, c), lay),
            _const_spec((1, c, 2 * c), lay),
            _const_spec((1, 1, 2 * c), lay),
            _const_spec((1, 1, c), lay),
        ],
        out_specs=pl.BlockSpec((1, LRU_CHUNK, c), lambda b, t: (b, t, 0)),
        out_shape=jax.ShapeDtypeStruct((bsz, T_PAD, c), BF16),
        scratch_shapes=[
            pltpu.VMEM((c // LANES, LRU_CHUNK + SUBLANES, LANES), F32),
            pltpu.VMEM((c // LANES, LRU_CHUNK, LANES), F32),
            pltpu.VMEM((LRU_CHUNK, c), F32),
            pltpu.VMEM((LRU_CHUNK, c), F32),
            pltpu.VMEM((LRU_CHUNK, 2 * c), F32),
            pltpu.VMEM((LRU_CHUNK, c), F32),
            pltpu.VMEM((LRU_CHUNK, c), F32),
            pltpu.VMEM((c // LANES, LRU_CHUNK, LANES), F32),
            pltpu.VMEM((1, c), F32),
        ],
        compiler_params=pltpu.CompilerParams(
            dimension_semantics=("arbitrary", "arbitrary"), vmem_limit_bytes=VMEM_LIMIT),
        name="rglru",
    )(x3, g3, conv_w, conv_b, w_ri, b_ri, neg_sp)
    return out.reshape(n, c)


def _merge_kernel(hs_ref, ys_ref, yl_ref, gt_ref, wglu_ref, bglu_ref, wsp_ref, wlp_ref, wout_ref, g_ref,
                  *rest, with_router):
    if with_router:
        rw_ref, rb_ref, hs_out_ref, hn_ref, rt_ref, rtt_ref, cnt_ref, run_ref, tri_ref, *w_bf16 = rest
    else:
        hs_out_ref, hn_ref, *w_bf16 = rest
    wglu_b, wsp_b, wlp_b, wout_b = w_bf16

    @pl.when(pl.program_id(0) == 0)
    def _():
        for w_f32, w_b in zip((wglu_ref, wsp_ref, wlp_ref, wout_ref), w_bf16):
            w_b[...] = w_f32[0].astype(BF16)

    ys = jnp.concatenate([ys_ref[q] for q in range(S5_PARTS)], axis=-1)
    glu = ys * jax.nn.sigmoid(_dot(ys.astype(BF16), wglu_b[...]) + bglu_ref[0])
    y_a = _dot(glu.astype(BF16), wsp_b[...])
    y_b = _dot(yl_ref[...], wlp_b[...])
    y = gt_ref[:, :D_MODEL].astype(F32) * y_a + gt_ref[:, D_MODEL:].astype(F32) * y_b
    hs = hs_ref[...] + _dot(y.astype(BF16), wout_b[...])
    hs_out_ref[...] = hs
    hn = _rms(hs, g_ref[0])
    if not with_router:
        hn_ref[...] = hn.astype(BF16)
    else:
        _rows_to_tiles(hn_ref, hn)
        logits = _dot(hn.astype(BF16), rw_ref[0].astype(BF16)) + rb_ref[0]
        lane = lax.broadcasted_iota(jnp.int32, logits.shape, 1).astype(F32)
        m1 = jnp.max(logits, axis=-1, keepdims=True)
        i1 = jnp.min(jnp.where(logits == m1, lane, float(LANES)), axis=-1, keepdims=True)
        rest_l = jnp.where(lane == i1, MASKED_LOGIT, logits)
        m2 = jnp.max(rest_l, axis=-1, keepdims=True)
        i2 = jnp.min(jnp.where(rest_l == m2, lane, float(LANES)), axis=-1, keepdims=True)
        e2 = jnp.exp(m2 - m1)
        g1 = 1.0 / (1.0 + e2)
        g2 = e2 / (1.0 + e2)
        @pl.when(pl.program_id(0) == 0)
        def _():
            run_ref[...] = jnp.zeros_like(run_ref)
            r, c = _iota2(tri_ref.shape)
            tri_ref[...] = jnp.where(c < r, 1.0, 0.0).astype(BF16)

        first = lane == i1
        second = lane == i2
        picked = jnp.where(first | second, 1.0, 0.0)
        before = _dot(tri_ref[...], picked.astype(BF16)) + run_ref[...]
        rank1 = jnp.sum(jnp.where(first, before, 0.0), axis=-1, keepdims=True)
        rank2 = jnp.sum(jnp.where(second, before, 0.0), axis=-1, keepdims=True)
        run_ref[...] += jnp.sum(picked, axis=0, keepdims=True)
        cnt_ref[...] = jnp.broadcast_to(run_ref[...], cnt_ref.shape)
        rt = (jnp.where(lane == 0.0, i1, 0.0) + jnp.where(lane == 1.0, i2, 0.0)
              + jnp.where(lane == 2.0, g1, 0.0) + jnp.where(lane == 3.0, g2, 0.0)
              + jnp.where(lane == 4.0, rank1, 0.0) + jnp.where(lane == 5.0, rank2, 0.0))
        rt_ref[...] = rt
        r, c = _iota2((SUBLANES, LANES))
        pick = jnp.where(r == c, 1.0, 0.0).astype(BF16)
        hi = rt.astype(BF16)
        mid = (rt - hi.astype(F32)).astype(BF16)
        lo = (rt - hi.astype(F32) - mid.astype(F32)).astype(BF16)
        nt = (((1,), (1,)), ((), ()))
        rtt_ref[0] = (lax.dot_general(pick, hi, nt, preferred_element_type=F32)
                      + lax.dot_general(pick, mid, nt, preferred_element_type=F32)
                      + lax.dot_general(pick, lo, nt, preferred_element_type=F32))


def _merge(hs, ys_parts, y_lru, gates, w_glu, b_glu, w_sp, w_lp, w_out, ffn_norm, layer, router=None):
    n = hs.shape[0]
    tm = TM
    lay = lambda i: (layer, 0, 0)
    in_specs = [
        pl.BlockSpec((tm, D_MODEL), lambda i: (i, 0)),
        pl.BlockSpec((S5_PARTS, tm, PART_W), lambda i: (0, i, 0)),
        pl.BlockSpec((tm, LRU_WIDTH), lambda i: (i, 0)),
        pl.BlockSpec((tm, 2 * D_MODEL), lambda i: (i, 0)),
        _const_spec((1, S5_WIDTH, S5_WIDTH), lay),
        _const_spec((1, 1, S5_WIDTH), lay),
        _const_spec((1, S5_WIDTH, D_MODEL), lay),
        _const_spec((1, LRU_WIDTH, D_MODEL), lay),
        _const_spec((1, D_MODEL, D_MODEL), lay),
        _const_spec((1, 1, D_MODEL), lay),
    ]
    out_specs = [pl.BlockSpec((tm, D_MODEL), lambda i: (i, 0))]
    out_shape = [jax.ShapeDtypeStruct((n, D_MODEL), F32)]
    if router is None:
        out_specs.append(pl.BlockSpec((tm, D_MODEL), lambda i: (i, 0)))
        out_shape.append(jax.ShapeDtypeStruct((n, D_MODEL), BF16))
    else:
        out_specs.append(pl.BlockSpec((tm * ROW_TILES, LANES), lambda i: (i, 0)))
        out_shape.append(jax.ShapeDtypeStruct((n * ROW_TILES, LANES), F32))
    args = [hs, ys_parts, y_lru, gates, w_glu, b_glu, w_sp, w_lp, w_out, ffn_norm]
    if router is not None:
        rw, rb, j = router
        in_specs += [_const_spec((1, D_MODEL, LANES), lambda i: (j, 0, 0)),
                     _const_spec((1, 1, LANES), lambda i: (j, 0, 0))]
        out_specs += [pl.BlockSpec((tm, LANES), lambda i: (i, 0)),
                      pl.BlockSpec((1, SUBLANES, tm), lambda i: (i, 0, 0)),
                      pl.BlockSpec((SUBLANES, LANES), lambda i: (0, 0))]
        out_shape += [jax.ShapeDtypeStruct((n, LANES), F32),
                      jax.ShapeDtypeStruct((n // tm, SUBLANES, tm), F32),
                      jax.ShapeDtypeStruct((SUBLANES, LANES), F32)]
        args += [rw, rb]
    return pl.pallas_call(
        functools.partial(_merge_kernel, with_router=router is not None),
        grid=(n // tm,),
        in_specs=in_specs,
        out_specs=out_specs,
        out_shape=out_shape,
        scratch_shapes=([pltpu.VMEM((1, LANES), F32), pltpu.VMEM((tm, tm), BF16)] if router is not None else [])
        + [pltpu.VMEM(w.shape[1:], BF16) for w in (w_glu, w_sp, w_lp, w_out)],
        compiler_params=pltpu.CompilerParams(
            dimension_semantics=("arbitrary",), vmem_limit_bytes=VMEM_LIMIT),
        name="merge_router" if router is not None else "merge",
    )(*args)


def _ffn_kernel(x_ref, hs_ref, wg_ref, wu_ref, wd_ref, o_ref, h_ref):
    x = x_ref[...]
    for c0 in range(0, h_ref.shape[1], FF_CHUNK):
        g = _dot(x, wg_ref[0, :, c0:c0 + FF_CHUNK])
        u = _dot(x, wu_ref[0, :, c0:c0 + FF_CHUNK])
        h_ref[:, c0:c0 + FF_CHUNK] = (g * jax.nn.sigmoid(g) * u).astype(BF16)
    o_ref[...] = hs_ref[...] + _dot(h_ref[...], wd_ref[0])


def _ffn(hn, hs, w_gate, w_up, w_down, layer):
    n = hn.shape[0]
    ff = w_gate.shape[-1]
    lay = lambda i: (layer, 0, 0)
    return pl.pallas_call(
        _ffn_kernel,
        grid=(n // TM_FFN,),
        in_specs=[
            pl.BlockSpec((TM_FFN, D_MODEL), lambda i: (i, 0)),
            pl.BlockSpec((TM_FFN, D_MODEL), lambda i: (i, 0)),
            _const_spec((1, D_MODEL, ff), lay),
            _const_spec((1, D_MODEL, ff), lay),
            _const_spec((1, ff, D_MODEL), lay),
        ],
        out_specs=pl.BlockSpec((TM_FFN, D_MODEL), lambda i: (i, 0)),
        out_shape=jax.ShapeDtypeStruct((n, D_MODEL), F32),
        scratch_shapes=[pltpu.VMEM((TM_FFN, ff), BF16)],
        compiler_params=pltpu.CompilerParams(
            dimension_semantics=("arbitrary",), vmem_limit_bytes=VMEM_LIMIT),
        name="dense_ffn",
    )(hn, hs, w_gate, w_up, w_down)


def _moe_plan(route_t, counts_f, n):
    n_blocks = -(-2 * n // MOE_BLOCK) + N_EXPERTS
    e = jnp.stack([route_t[:, 0, :], route_t[:, 1, :]]).astype(jnp.int32)
    rank = jnp.stack([route_t[:, 4, :], route_t[:, 5, :]]).astype(jnp.int32)
    counts = counts_f[0, :N_EXPERTS].astype(jnp.int32)
    padded = ((counts + MOE_BLOCK - 1) // MOE_BLOCK) * MOE_BLOCK
    cum_pad = jnp.cumsum(padded)
    pad_start = cum_pad - padded
    pos = rank
    for x in range(N_EXPERTS):
        pos = pos + jnp.where(e == x, pad_start[x], 0)
    block_start = jnp.arange(n_blocks, dtype=jnp.int32) * MOE_BLOCK
    block_expert = jnp.minimum(jnp.sum((block_start[:, None] >= cum_pad[None, :]).astype(jnp.int32), axis=1),
                               N_EXPERTS - 1)
    n_used = (cum_pad[-1] // MOE_BLOCK).astype(jnp.int32).reshape(1)
    pad_range = jnp.stack([pad_start + counts, cum_pad], axis=1).reshape(2 * N_EXPERTS).astype(jnp.int32)
    ids = jnp.arange(N_EXPERTS, dtype=jnp.int32)
    used = counts > 0
    run_buffer = (jnp.cumsum(used.astype(jnp.int32)) - 1) % 2
    later = jnp.where((ids[None, :] > ids[:, None]) & used[None, :], ids[None, :], N_EXPERTS)
    next_used = jnp.min(later, axis=1)
    next_used = jnp.where(next_used == N_EXPERTS, -1, next_used)
    of_block = lambda table: jnp.sum(jnp.where(block_expert[:, None] == ids[None, :], table[None, :], 0), axis=1)
    weight_plan = jnp.stack([block_expert,
                             (block_start == of_block(pad_start)).astype(jnp.int32),
                             of_block(run_buffer), of_block(next_used)]).astype(jnp.int32)
    return pos.reshape(2 * n), weight_plan, n_used, pad_range


def _tile(ref, index):
    return ref.at[pl.ds(pl.multiple_of(index * ROW_TILES, ROW_TILES), ROW_TILES)]


def _tile_gather(src_hbm, dst, sem, rows, index_of):
    def body(grp, carry):
        r0 = grp * GATHER_GROUP
        index = [index_of(r0 + j) for j in range(GATHER_GROUP)]
        for j in range(GATHER_GROUP):
            pltpu.make_async_copy(_tile(src_hbm, index[j]), _tile(dst, r0 + j), sem).start(priority=j % 2)
        return carry
    lax.fori_loop(0, rows // GATHER_GROUP, body, 0)


def _tile_gather_wait(src_hbm, dst, sem, rows):
    pltpu.make_async_copy(src_hbm.at[pl.ds(0, rows * ROW_TILES)], dst, sem).wait()


def _moe_dispatch_kernel(pos_ref, pad_ref, nu_ref, x_hbm, xs_hbm, xbuf, zero_ref, sem_in, sem_out, sem_fill):
    t = pl.program_id(0)
    nt = pl.num_programs(0)
    n = nt * TM
    tile_rows = TM * ROW_TILES
    block_rows = MOE_BLOCK * ROW_TILES
    n_blocks = xs_hbm.shape[0] // block_rows

    def read(tile):
        buf = tile % DISPATCH_RING
        return pltpu.make_async_copy(x_hbm.at[pl.ds(pl.multiple_of(tile * tile_rows, tile_rows), tile_rows)],
                                     xbuf.at[buf], sem_in.at[buf])

    def scatter_wait(tile):
        buf = tile % DISPATCH_RING
        for k in range(2):
            pltpu.make_async_copy(xbuf.at[buf], xs_hbm.at[pl.ds(0, tile_rows)], sem_out.at[buf]).wait()

    @pl.when(t == 0)
    def _():
        read(0).start()
        zero_ref[...] = jnp.zeros_like(zero_ref)
        zero_tile = zero_ref.at[pl.ds(0, ROW_TILES)]
        for e in range(N_EXPERTS):
            def fill(slot, carry):
                pltpu.make_async_copy(zero_tile, _tile(xs_hbm, slot), sem_fill.at[0]).start()
                return carry

            def fill_wait(slot, carry):
                pltpu.make_async_copy(zero_tile, _tile(xs_hbm, slot), sem_fill.at[0]).wait()
                return carry
            lax.fori_loop(pad_ref[2 * e], pad_ref[2 * e + 1], fill, 0)
            lax.fori_loop(pad_ref[2 * e], pad_ref[2 * e + 1], fill_wait, 0)

        def block_of(blk):
            return xs_hbm.at[pl.ds(pl.multiple_of(blk * block_rows, block_rows), block_rows)]

        def fill_block(blk, carry):
            pltpu.make_async_copy(zero_ref, block_of(blk), sem_fill.at[0]).start()
            return carry

        def fill_block_wait(blk, carry):
            pltpu.make_async_copy(zero_ref, block_of(blk), sem_fill.at[0]).wait()
            return carry
        lax.fori_loop(nu_ref[0], n_blocks, fill_block, 0)
        lax.fori_loop(nu_ref[0], n_blocks, fill_block_wait, 0)

    @pl.when(t + 1 < nt)
    def _():
        @pl.when(t + 1 >= DISPATCH_RING)
        def _():
            scatter_wait(t + 1 - DISPATCH_RING)
        read(t + 1).start()

    read(t).wait()
    src = xbuf.at[t % DISPATCH_RING]
    for k in range(2):
        def put(grp, carry):
            r0 = grp * GATHER_GROUP
            slot = [pos_ref[k * n + t * TM + r0 + j] for j in range(GATHER_GROUP)]
            for j in range(GATHER_GROUP):
                pltpu.make_async_copy(_tile(src, r0 + j), _tile(xs_hbm, slot[j]),
                                      sem_out.at[t % DISPATCH_RING]).start(priority=j % 2)
            return carry
        lax.fori_loop(0, TM // GATHER_GROUP, put, 0)

    @pl.when(t == nt - 1)
    def _():
        for back in range(DISPATCH_RING - 1, -1, -1):
            @pl.when(t - back >= 0)
            def _():
                scatter_wait(t - back)


def _moe_dispatch(hn_tiles, pos, pad_range, n_used, n_slots):
    n = hn_tiles.shape[0] // ROW_TILES
    return pl.pallas_call(
        _moe_dispatch_kernel,
        grid_spec=pltpu.PrefetchScalarGridSpec(
            num_scalar_prefetch=3,
            grid=(n // TM,),
            in_specs=[pl.BlockSpec(memory_space=pl.ANY)],
            out_specs=pl.BlockSpec(memory_space=pl.ANY),
            scratch_shapes=[pltpu.VMEM((DISPATCH_RING, TM * ROW_TILES, LANES), F32),
                            pltpu.VMEM((MOE_BLOCK * ROW_TILES, LANES), F32),
                            pltpu.SemaphoreType.DMA((DISPATCH_RING,)),
                            pltpu.SemaphoreType.DMA((DISPATCH_RING,)),
                            pltpu.SemaphoreType.DMA((1,))],
        ),
        out_shape=jax.ShapeDtypeStruct((n_slots * ROW_TILES, LANES), F32),
        compiler_params=pltpu.CompilerParams(dimension_semantics=("arbitrary",)),
        name="moe_dispatch",
    )(pos, pad_range, n_used, hn_tiles)


def _moe_ffn_kernel(be_ref, first_ref, buf_ref, next_ref, nu_ref, x_ref, wg_hbm, wu_hbm, wd_hbm, y_ref,
                    wg_buf, wu_buf, wd_buf, sem, *, first_expert):
    i = pl.program_id(0)
    sources = (wg_hbm, wu_hbm, wd_hbm)
    buffers = (wg_buf, wu_buf, wd_buf)

    def copies(expert, b):
        return [pltpu.make_async_copy(src.at[first_expert + expert], dst.at[b], sem.at[b, k])
                for k, (src, dst) in enumerate(zip(sources, buffers))]

    @pl.when(i < nu_ref[0])
    def _():
        b = buf_ref[i]

        @pl.when(first_ref[i] == 1)
        def _():
            @pl.when(i == 0)
            def _():
                for c in copies(be_ref[i], b):
                    c.start()
            for c in copies(be_ref[i], b):
                c.wait()

            @pl.when(next_ref[i] >= 0)
            def _():
                for c in copies(next_ref[i], 1 - b):
                    c.start()

        x = _rows_from_tiles(x_ref, MOE_BLOCK).astype(BF16)
        g = _dot(x, wg_buf[b].astype(BF16))
        h = g * jax.nn.sigmoid(g) * _dot(x, wu_buf[b].astype(BF16))
        _rows_to_tiles(y_ref, _dot(h.astype(BF16), wd_buf[b].astype(BF16)))

    @pl.when(i >= nu_ref[0])
    def _():
        y_ref[...] = jnp.zeros_like(y_ref)


def _moe_ffn(xs_tiles, weight_plan, n_used, w_gate, w_up, w_down, first):
    n_blocks = weight_plan.shape[1]
    ff = w_gate.shape[-1]
    return pl.pallas_call(
        functools.partial(_moe_ffn_kernel, first_expert=first),
        grid_spec=pltpu.PrefetchScalarGridSpec(
            num_scalar_prefetch=5,
            grid=(n_blocks,),
            in_specs=[
                pl.BlockSpec((MOE_BLOCK * ROW_TILES, LANES),
                             lambda i, be, fi, bu, nx, nu: (jnp.maximum(jnp.minimum(i, nu[0] - 1), 0), 0)),
                pl.BlockSpec(memory_space=pl.ANY),
                pl.BlockSpec(memory_space=pl.ANY),
                pl.BlockSpec(memory_space=pl.ANY),
            ],
            out_specs=pl.BlockSpec((MOE_BLOCK * ROW_TILES, LANES), lambda i, be, fi, bu, nx, nu: (i, 0)),
            scratch_shapes=[pltpu.VMEM((2, D_MODEL, ff), F32),
                            pltpu.VMEM((2, D_MODEL, ff), F32),
                            pltpu.VMEM((2, ff, D_MODEL), F32),
                            pltpu.SemaphoreType.DMA((2, 3))],
        ),
        out_shape=jax.ShapeDtypeStruct((n_blocks * MOE_BLOCK * ROW_TILES, LANES), F32),
        compiler_params=pltpu.CompilerParams(
            dimension_semantics=("arbitrary",), vmem_limit_bytes=VMEM_LIMIT),
        name="moe_ffn",
    )(weight_plan[0], weight_plan[1], weight_plan[2], weight_plan[3], n_used, xs_tiles, w_gate, w_up, w_down)


def _moe_combine_kernel(pos_ref, hs_ref, rt_ref, ys_hbm, o_ref, ybuf, sem):
    i = pl.program_id(0)
    nt = pl.num_programs(0)
    slot = i % 2

    def start(t, s):
        for k in range(2):
            _tile_gather(ys_hbm, ybuf.at[s, k], sem.at[s], TM, lambda r: pos_ref[k * (nt * TM) + t * TM + r])

    @pl.when(i == 0)
    def _():
        start(0, 0)

    @pl.when(i + 1 < nt)
    def _():
        start(i + 1, 1 - slot)

    for k in range(2):
        _tile_gather_wait(ys_hbm, ybuf.at[slot, k], sem.at[slot], TM)
    rt = rt_ref[...]
    lane = lax.broadcasted_iota(jnp.int32, rt.shape, 1)
    g1 = jnp.sum(jnp.where(lane == 2, rt, 0.0), axis=-1, keepdims=True)
    g2 = jnp.sum(jnp.where(lane == 3, rt, 0.0), axis=-1, keepdims=True)
    o_ref[...] = (hs_ref[...] + g1 * _rows_from_tiles(ybuf.at[slot, 0], TM)
                  + g2 * _rows_from_tiles(ybuf.at[slot, 1], TM))


def _moe_combine(hs, route, ys_tiles, pos):
    n = hs.shape[0]
    return pl.pallas_call(
        _moe_combine_kernel,
        grid_spec=pltpu.PrefetchScalarGridSpec(
            num_scalar_prefetch=1,
            grid=(n // TM,),
            in_specs=[
                pl.BlockSpec((TM, D_MODEL), lambda i, p: (i, 0)),
                pl.BlockSpec((TM, LANES), lambda i, p: (i, 0)),
                pl.BlockSpec(memory_space=pl.ANY),
            ],
            out_specs=pl.BlockSpec((TM, D_MODEL), lambda i, p: (i, 0)),
            scratch_shapes=[pltpu.VMEM((2, 2, TM * ROW_TILES, LANES), F32),
                            pltpu.SemaphoreType.DMA((2,))],
        ),
        out_shape=jax.ShapeDtypeStruct((n, D_MODEL), F32),
        compiler_params=pltpu.CompilerParams(
            dimension_semantics=("arbitrary",), vmem_limit_bytes=VMEM_LIMIT),
        name="moe_combine",
    )(pos, hs, route, ys_tiles)


def _final_kernel(a_ref, b_ref, g_ref, o_ref):
    tb = a_ref.shape[1]
    o_ref[0, :tb - N_META] = _rms(a_ref[0, N_META:], g_ref[...])
    o_ref[0, tb - N_META:] = _rms(b_ref[0], g_ref[...])


def _final_norm(hs, g, bsz, seq):
    hs3 = hs.reshape(bsz, T_PAD, D_MODEL)
    return pl.pallas_call(
        _final_kernel,
        grid=(bsz, seq // TB_FINAL),
        in_specs=[pl.BlockSpec((1, TB_FINAL, D_MODEL), lambda b, i: (b, i, 0)),
                  pl.BlockSpec((1, N_META, D_MODEL), lambda b, i: (b, (i + 1) * (TB_FINAL // N_META), 0)),
                  _const_spec((1, D_MODEL), lambda b, i: (0, 0))],
        out_specs=pl.BlockSpec((1, TB_FINAL, D_MODEL), lambda b, i: (b, i, 0)),
        out_shape=jax.ShapeDtypeStruct((bsz, seq, D_MODEL), F32),
        compiler_params=pltpu.CompilerParams(
            dimension_semantics=("arbitrary", "arbitrary"), vmem_limit_bytes=VMEM_LIMIT),
        name="final_norm",
    )(hs3, hs3, g)


def _head_blockdiag(w):
    eye = jnp.eye(LRU_HEADS, dtype=w.dtype)
    out = jnp.einsum('lnhk,nm->lnhmk', w, eye)
    return out.reshape(w.shape[0], LRU_WIDTH, LRU_WIDTH)


def kernel(x, meta_tokens, mix_norm, w_in, merge_bias, s5_lambda_re, s5_lambda_im, s5_log_dt, s5_b_re, s5_b_im, s5_c_re, s5_c_im, s5_d, s5_w_glu, s5_b_glu, s5_w_proj, lru_conv_w, lru_conv_b, lru_w_rgate, lru_b_rgate, lru_w_igate, lru_b_igate, lru_lambda, lru_w_proj, w_out, ffn_norm, dense_w_gate, dense_w_up, dense_w_down, router_w, router_b, moe_w_gate, moe_w_up, moe_w_down, final_norm):
    bsz, seq, d = x.shape
    depth = w_in.shape[0]
    assert d == D_MODEL and N_META + seq <= T_PAD
    n = bsz * T_PAD
    assert n % TM == 0 and n % TM_FFN == 0 and seq % TB_FINAL == 0 and TB_FINAL % N_META == 0

    meta = jnp.broadcast_to(meta_tokens[None].astype(x.dtype), (bsz, N_META, d))
    pad = jnp.zeros((bsz, T_PAD - N_META - seq, d), x.dtype)
    hs = jnp.concatenate([meta, x, pad], axis=1).reshape(n, d)

    row3 = lambda a: a[:, None, :]
    w_ri = jnp.concatenate([_head_blockdiag(lru_w_rgate), _head_blockdiag(lru_w_igate)], axis=-1).astype(BF16)
    b_ri = jnp.concatenate([lru_b_rgate, lru_b_igate], axis=-1)
    neg_sp = -LRU_C * jax.nn.softplus(-lru_lambda)
    dense = [w.astype(BF16) for w in (dense_w_gate, dense_w_up, dense_w_down)]
    n_moe = router_w.shape[0]
    moe = [w.reshape((n_moe * N_EXPERTS,) + w.shape[2:]) for w in (moe_w_gate, moe_w_up, moe_w_down)]
    s5_ops = jax.vmap(_s5_prep)(s5_lambda_re, s5_lambda_im, s5_log_dt, s5_b_re, s5_b_im, s5_c_re, s5_c_im, s5_d)
    rw_pad = jnp.pad(router_w, ((0, 0), (0, 0), (0, LANES - N_EXPERTS)))
    rb_pad = jnp.pad(router_b, ((0, 0), (0, LANES - N_EXPERTS)), constant_values=MASKED_LOGIT)

    for layer in range(depth):
        u_parts, x_lru, g_lru, gates = _in_proj(hs, row3(mix_norm), w_in, row3(merge_bias), layer)
        ys_parts = _s5_scan(u_parts, s5_ops, layer, bsz)
        y_lru = _lru(x_lru, g_lru, lru_conv_w, row3(lru_conv_b), w_ri, row3(b_ri), row3(neg_sp), layer, bsz)
        j = layer // 2
        router = (rw_pad, row3(rb_pad), j) if layer % 2 == 1 else None
        res = _merge(hs, ys_parts, y_lru, gates, s5_w_glu, row3(s5_b_glu), s5_w_proj, lru_w_proj, w_out,
                     row3(ffn_norm), layer, router)
        if layer % 2 == 0:
            hs, hn = res
            hs = _ffn(hn, hs, *dense, layer=j)
        else:
            hs, hn, route, route_t, counts = res
            pos, weight_plan, n_used, pad_range = _moe_plan(route_t, counts, n)
            xs = _moe_dispatch(hn, pos, pad_range, n_used, weight_plan.shape[1] * MOE_BLOCK)
            ys = _moe_ffn(xs, weight_plan, n_used, *moe, first=j * N_EXPERTS)
            hs = _moe_combine(hs, route, ys, pos)

    return _final_norm(hs, final_norm[None, :], bsz, seq)
```

```python
import functools

import jax
import jax.numpy as jnp
from jax import lax
from jax.experimental import pallas as pl
from jax.experimental.pallas import tpu as pltpu

F32 = jnp.float32
BF16 = jnp.bfloat16

D_MODEL = 1024
N_META = 16
S5_WIDTH = 512
S5_GROUP = 16
S5_GROUPS = 32
S5_STATE = 64
LRU_WIDTH = 512
LRU_HEADS = 8
CONV_WIDTH = 4
LRU_C = 8.0
N_EXPERTS = 8
EPS = 1e-6

FOLD = 8
S5_PARTS = 4
PART_W = S5_WIDTH // S5_PARTS
PART_GROUPS = PART_W // S5_GROUP
PART_STATE = PART_GROUPS * S5_STATE
FOLD_W = FOLD * PART_W

T_PAD = 8256
ROWS = T_PAD // FOLD
TM = 688
TM_FFN = 688
FF_CHUNK = 1024
MOE_BLOCK = 512
GATHER_GROUP = 8
DISPATCH_RING = 3
LRU_CHUNK = 1032
LRU_UNROLL = 3
LRU_SCAN_UNROLL = 43
TB_FINAL = 2048
VMEM_LIMIT = 56 * 1024 * 1024
LANES = 128
SUBLANES = 8
S5_SEG = ROWS // SUBLANES
S5_UNROLL = 3
ROW_TILES = D_MODEL // LANES
MASKED_LOGIT = float("-inf")


def _dot(a, b):
    return jnp.dot(a, b, preferred_element_type=F32)


def _const_spec(block_shape, index_map):
    return pl.BlockSpec(block_shape, index_map, pipeline_mode=pl.Buffered(1))


def _rms(x, g):
    ms = jnp.mean(x * x, axis=-1, keepdims=True)
    return x * lax.rsqrt(ms + EPS) * g


def _rows_to_tiles(ref, x):
    rows = x.shape[0]
    for s in range(ROW_TILES):
        ref[pl.ds(s, rows, stride=ROW_TILES), :] = x[:, s * LANES:(s + 1) * LANES]


def _rows_from_tiles(ref, rows):
    return jnp.concatenate([ref[pl.ds(s, rows, stride=ROW_TILES), :] for s in range(ROW_TILES)], axis=-1)


def _in_proj_kernel(hs_ref, g_ref, wf_ref, mb_ref, u_ref, xl_ref, gl_ref, gt_ref, w_ref):
    @pl.when(pl.program_id(0) == 0)
    def _():
        w_ref[0] = wf_ref[0].astype(BF16)

    hn = _rms(hs_ref[...], g_ref[0]).astype(BF16)
    u = _dot(hn, w_ref[0, :, 0:S5_WIDTH])
    for q in range(S5_PARTS):
        u_ref[q] = u[:, q * PART_W:(q + 1) * PART_W]
    o_x = S5_WIDTH
    o_g = o_x + LRU_WIDTH
    o_m = o_g + LRU_WIDTH
    xl_ref[...] = _dot(hn, w_ref[0, :, o_x:o_g]).astype(BF16)
    gl_ref[...] = _dot(hn, w_ref[0, :, o_g:o_m]).astype(BF16)
    z = _dot(hn, w_ref[0, :, o_m:]) + mb_ref[0]
    gt_ref[...] = jax.nn.sigmoid(z).astype(BF16)


def _in_proj(hs, mix_norm, w_in, merge_bias, layer):
    n = hs.shape[0]
    d_in = w_in.shape[-1]
    lay = lambda i: (layer, 0, 0)
    return pl.pallas_call(
        _in_proj_kernel,
        grid=(n // TM,),
        in_specs=[
            pl.BlockSpec((TM, D_MODEL), lambda i: (i, 0)),
            _const_spec((1, 1, D_MODEL), lay),
            _const_spec((1, D_MODEL, d_in), lay),
            _const_spec((1, 1, 2 * D_MODEL), lay),
        ],
        out_specs=[
            pl.BlockSpec((S5_PARTS, TM, PART_W), lambda i: (0, i, 0)),
            pl.BlockSpec((TM, LRU_WIDTH), lambda i: (i, 0)),
            pl.BlockSpec((TM, LRU_WIDTH), lambda i: (i, 0)),
            pl.BlockSpec((TM, 2 * D_MODEL), lambda i: (i, 0)),
        ],
        out_shape=[
            jax.ShapeDtypeStruct((S5_PARTS, n, PART_W), F32),
            jax.ShapeDtypeStruct((n, LRU_WIDTH), BF16),
            jax.ShapeDtypeStruct((n, LRU_WIDTH), BF16),
            jax.ShapeDtypeStruct((n, 2 * D_MODEL), BF16),
        ],
        scratch_shapes=[pltpu.VMEM((1, D_MODEL, d_in), BF16)],
        compiler_params=pltpu.CompilerParams(
            dimension_semantics=("arbitrary",), vmem_limit_bytes=VMEM_LIMIT),
        name="in_proj",
    )(hs, mix_norm, w_in, merge_bias)


def _s5_prep(lam_re, lam_im, log_dt, b_re, b_im, c_re, c_im, d_skip):
    dt = jnp.exp(log_dt)[:, None]
    mag = jnp.exp(lam_re * dt)
    a_re = mag * jnp.cos(lam_im * dt)
    a_im = mag * jnp.sin(lam_im * dt)
    den = lam_re * lam_re + lam_im * lam_im
    num_re = a_re - 1.0
    coef_re = (num_re * lam_re + a_im * lam_im) / den
    coef_im = (a_im * lam_re - num_re * lam_im) / den
    bb_re = coef_re[..., None] * b_re - coef_im[..., None] * b_im
    bb_im = coef_re[..., None] * b_im + coef_im[..., None] * b_re

    def cmul(xr, xi, yr, yi):
        return xr * yr - xi * yi, xr * yi + xi * yr

    def powers(br, bi, n):
        pr, pi = [jnp.ones_like(br)], [jnp.zeros_like(bi)]
        for _ in range(n):
            r, i = cmul(pr[-1], pi[-1], br, bi)
            pr.append(r)
            pi.append(i)
        return jnp.stack(pr), jnp.stack(pi)

    p_re, p_im = powers(a_re, a_im, FOLD)

    def per_part(x):
        lead = x.shape[:-3]
        xp = x.reshape(lead + (S5_PARTS, PART_GROUPS) + x.shape[-2:])
        return jnp.moveaxis(xp, len(lead), 0)

    rev_re = jnp.stack([p_re[FOLD - 1 - j] for j in range(FOLD)])
    rev_im = jnp.stack([p_im[FOLD - 1 - j] for j in range(FOLD)])
    wr, wi = cmul(rev_re[..., None], rev_im[..., None], bb_re[None], bb_im[None])
    w_ri = jnp.swapaxes(jnp.stack([wr, wi], axis=1), -1, -2)
    xq = jnp.transpose(per_part(w_ri), (0, 2, 1, 3, 4, 5)).reshape(S5_PARTS, 2, FOLD_W, S5_STATE)

    ca_re, ca_im = cmul(c_re[None], c_im[None], p_re[:, :, None, :], p_im[:, :, None, :])
    bt_re = jnp.swapaxes(bb_re, -1, -2)[None, :, :, None, :]
    bt_im = jnp.swapaxes(bb_im, -1, -2)[None, :, :, None, :]
    taps = jnp.sum(ca_re[:FOLD, :, None] * bt_re - ca_im[:FOLD, :, None] * bt_im, axis=-1)
    skip = d_skip.reshape(S5_GROUPS, S5_GROUP)
    taps = taps.at[0].add(skip[:, :, None] * jnp.eye(S5_GROUP, dtype=F32)[None])
    rc = jnp.transpose(per_part(taps), (0, 2, 3, 1, 4)).reshape(S5_PARTS, PART_W, FOLD * S5_GROUP)

    v_ri = jnp.swapaxes(jnp.stack([ca_re[1:], -ca_im[1:]], axis=0), -1, -2)
    vc = jnp.transpose(per_part(v_ri), (0, 1, 3, 4, 2, 5)).reshape(S5_PARTS, 2 * PART_STATE, FOLD * S5_GROUP)

    def part_vec(x):
        lead = x.shape[:-2]
        xp = x.reshape(lead + (S5_PARTS, PART_STATE))
        return jnp.moveaxis(xp, -2, 0)

    row_re, row_im = p_re[FOLD], p_im[FOLD]
    seg_re, seg_im = jnp.ones_like(row_re), jnp.zeros_like(row_im)
    for bit in bin(S5_SEG)[2:]:
        seg_re, seg_im = cmul(seg_re, seg_im, seg_re, seg_im)
        if bit == '1':
            seg_re, seg_im = cmul(seg_re, seg_im, row_re, row_im)
    decay = jnp.stack([jnp.stack([part_vec(row_re), part_vec(row_im)], axis=1),
                       jnp.stack([part_vec(seg_re), part_vec(seg_im)], axis=1)], axis=1)
    return xq, rc, vc, decay[:, :, :, None, :]


def _iota2(shape):
    return (lax.broadcasted_iota(jnp.int32, shape, 0), lax.broadcasted_iota(jnp.int32, shape, 1))


def _s5_expand(xq, rc, vc, w1_s, tv_s):
    ps = PART_STATE
    lg_state, lg_group, lg_part = (v.bit_length() - 1 for v in (S5_STATE, S5_GROUP, PART_W))
    grp = PART_GROUPS - 1
    one_hot = lambda m: jnp.where(m, 1.0, 0.0).astype(BF16)
    r, c = _iota2((S5_STATE, ps))
    e1 = one_hot(r == (c & (S5_STATE - 1)))
    r, c = _iota2((FOLD * S5_GROUP, FOLD_W))
    e2 = one_hot(((r >> lg_group) == (c >> lg_part)) & ((r & (S5_GROUP - 1)) == (c & (S5_GROUP - 1))))
    r, c = _iota2((FOLD_W, ps))
    m1 = ((r >> lg_group) & grp) == (c >> lg_state)
    for ri in range(2):
        w1_s[:, ri * ps:(ri + 1) * ps] = jnp.where(m1, _dot(xq[ri].astype(BF16), e1), 0.0).astype(BF16)
    r, c = _iota2((PART_W, FOLD_W))
    m2 = (r >> lg_group) == ((c >> lg_group) & grp)
    r0 = jnp.where(m2, _dot(rc.astype(BF16), e2), 0.0).astype(BF16)
    for j in range(FOLD):
        if j == 0:
            blk = r0
        else:
            blk = jnp.concatenate([jnp.zeros((PART_W, j * PART_W), BF16), r0[:, :FOLD_W - j * PART_W]], axis=1)
        tv_s[j * PART_W:(j + 1) * PART_W, :] = blk
    r, c = _iota2((2 * ps, FOLD_W))
    m3 = ((r >> lg_state) & grp) == ((c >> lg_group) & grp)
    tv_s[FOLD_W:, :] = jnp.where(m3, _dot(vc.astype(BF16), e2), 0.0).astype(BF16)


def _s5_kernel(u_ref, xq_ref, rc_ref, vc_ref, dec_ref, y_ref, w1_s, tv_s, up_ref, f_ref, hp_ref):
    ps = PART_STATE
    tstride = S5_SEG * FOLD
    cols = lambda j: slice(j * PART_W, (j + 1) * PART_W)

    @pl.when(pl.program_id(1) == 0)
    def _():
        _s5_expand(xq_ref[0, 0], rc_ref[0, 0], vc_ref[0, 0], w1_s, tv_s)

    def fold_body(i, carry):
        r0 = pl.multiple_of(i * SUBLANES, SUBLANES)
        for j in range(FOLD):
            up_ref[pl.ds(r0, SUBLANES), cols(j)] = u_ref[0, 0, pl.ds(i * FOLD + j, SUBLANES, stride=tstride), :]
        return carry

    lax.fori_loop(0, S5_SEG, fold_body, 0, unroll=S5_UNROLL)
    u = up_ref[...].astype(BF16)
    f_ref[...] = _dot(u, w1_s[...])
    ar = jnp.broadcast_to(dec_ref[0, 0, 0, 0], (SUBLANES, ps))
    ai = jnp.broadcast_to(dec_ref[0, 0, 0, 1], (SUBLANES, ps))

    def step(i, hr, hi):
        r0 = pl.multiple_of(i * SUBLANES, SUBLANES)
        return (ar * hr - ai * hi + f_ref[pl.ds(r0, SUBLANES), :ps],
                ar * hi + ai * hr + f_ref[pl.ds(r0, SUBLANES), ps:])

    zero = jnp.zeros((SUBLANES, ps), F32)
    er, ei = lax.fori_loop(0, S5_SEG, lambda i, c: step(i, *c), (zero, zero))
    sr = dec_ref[0, 0, 1, 0]
    si = dec_ref[0, 0, 1, 1]
    row = lax.broadcasted_iota(jnp.int32, (SUBLANES, ps), 0)
    nr, ni = zero, zero
    for sgm in range(SUBLANES - 1):
        lr = er + sr * nr - si * ni
        li = ei + sr * ni + si * nr
        nr = nr + jnp.where(row == sgm + 1, pltpu.roll(lr, 1, axis=0), 0.0)
        ni = ni + jnp.where(row == sgm + 1, pltpu.roll(li, 1, axis=0), 0.0)

    def state_body(i, carry):
        hr, hi = carry
        r0 = pl.multiple_of(i * SUBLANES, SUBLANES)
        hp_ref[pl.ds(r0, SUBLANES), :ps] = hr
        hp_ref[pl.ds(r0, SUBLANES), ps:] = hi
        return step(i, hr, hi)

    lax.fori_loop(0, S5_SEG, state_body, (nr, ni))
    hp = hp_ref[...].astype(BF16)
    wide = 2 * PART_W
    for c0 in range(0, FOLD_W, wide):
        y = _dot(u[:, :c0 + wide], tv_s[:c0 + wide, c0:c0 + wide]) + _dot(hp, tv_s[FOLD_W:, c0:c0 + wide])
        f_ref[:, c0:c0 + wide] = jax.nn.gelu(y)

    def unfold_body(i, carry):
        r0 = pl.multiple_of(i * SUBLANES, SUBLANES)
        for j in range(FOLD):
            y_ref[0, 0, pl.ds(i * FOLD + j, SUBLANES, stride=tstride), :] = f_ref[pl.ds(r0, SUBLANES), cols(j)]
        return carry

    lax.fori_loop(0, S5_SEG, unfold_body, 0, unroll=S5_UNROLL)


def _s5_scan(u_parts, ops, layer, bsz):
    xq, rc, vc, decay = ops
    n = u_parts.shape[1]
    u4 = u_parts.reshape(S5_PARTS, bsz, T_PAD, PART_W)
    lay4 = lambda q, b: (layer, q, 0, 0)
    y4 = pl.pallas_call(
        _s5_kernel,
        grid=(S5_PARTS, bsz),
        in_specs=[
            pl.BlockSpec((1, 1, T_PAD, PART_W), lambda q, b: (q, b, 0, 0)),
            pl.BlockSpec((1, 1, 2, FOLD_W, S5_STATE), lambda q, b: (layer, q, 0, 0, 0)),
            pl.BlockSpec((1, 1, PART_W, FOLD * S5_GROUP), lay4),
            pl.BlockSpec((1, 1, 2 * PART_STATE, FOLD * S5_GROUP), lay4),
            pl.BlockSpec((1, 1, 2, 2, 1, PART_STATE), lambda q, b: (layer, q, 0, 0, 0, 0)),
        ],
        out_specs=pl.BlockSpec((1, 1, T_PAD, PART_W), lambda q, b: (q, b, 0, 0)),
        out_shape=jax.ShapeDtypeStruct((S5_PARTS, bsz, T_PAD, PART_W), F32),
        scratch_shapes=[
            pltpu.VMEM((FOLD_W, 2 * PART_STATE), BF16),
            pltpu.VMEM((FOLD_W + 2 * PART_STATE, FOLD_W), BF16),
            pltpu.VMEM((ROWS, FOLD_W), F32),
            pltpu.VMEM((ROWS, 2 * PART_STATE), F32),
            pltpu.VMEM((ROWS, 2 * PART_STATE), F32),
        ],
        compiler_params=pltpu.CompilerParams(
            dimension_semantics=("arbitrary", "arbitrary"), vmem_limit_bytes=VMEM_LIMIT),
        name="s5_scan",
    )(u4, xq, rc, vc, decay)
    return y4.reshape(S5_PARTS, n, PART_W)


def _lru_kernel(x_ref, g_ref, cw_ref, cb_ref, wri_ref, bri_ref, nsp_ref, o_ref,
                xs_ref, gs_ref, xc_ref, gp_ref, z_ref, a_ref, b_ref, os_ref, h_ref):
    tc = LRU_CHUNK
    c = LRU_WIDTH
    seg = tc // SUBLANES
    nq = c // LANES
    lanes = lambda q: slice(q * LANES, (q + 1) * LANES)
    halo = SUBLANES

    @pl.when(pl.program_id(1) == 0)
    def _():
        xs_ref[:, 0:halo, :] = jnp.zeros((nq, halo, LANES), F32)
        h_ref[...] = jnp.zeros((1, c), F32)

    x = x_ref[0].astype(F32)
    g = g_ref[0].astype(F32)
    for q in range(nq):
        xs_ref[q, halo:, :] = x[:, lanes(q)]
        gs_ref[q] = g[:, lanes(q)]
    taps = [[cw_ref[0, k:k + 1, lanes(q)] for k in range(CONV_WIDTH)] for q in range(nq)]
    bias = [cb_ref[0, :, lanes(q)] for q in range(nq)]

    def conv_body(i, carry):
        r0 = pl.multiple_of(i * SUBLANES, SUBLANES)
        for q in range(nq):
            acc = bias[q]
            for k in range(CONV_WIDTH):
                first = halo - (CONV_WIDTH - 1) + k + i
                acc = acc + taps[q][k] * xs_ref[q, pl.ds(first, SUBLANES, stride=seg), :]
            xc_ref[pl.ds(r0, SUBLANES), lanes(q)] = acc
            gp_ref[pl.ds(r0, SUBLANES), lanes(q)] = jax.nn.gelu(gs_ref[q, pl.ds(i, SUBLANES, stride=seg), :])
        return carry

    lax.fori_loop(0, seg, conv_body, 0, unroll=LRU_UNROLL)
    for q in range(nq):
        xs_ref[q, 0:halo, :] = xs_ref[q, tc:tc + halo, :]

    z_ref[...] = _dot(xc_ref[...].astype(BF16), wri_ref[0])
    b_r = jnp.broadcast_to(bri_ref[0, :, :c], (SUBLANES, c))
    b_i = jnp.broadcast_to(bri_ref[0, :, c:], (SUBLANES, c))
    nsp = jnp.broadcast_to(nsp_ref[0], (SUBLANES, c))

    def scan_body(i, carry):
        h, p = carry
        r0 = pl.multiple_of(i * SUBLANES, SUBLANES)
        a = jnp.exp(jax.nn.sigmoid(z_ref[pl.ds(r0, SUBLANES), :c] + b_r) * nsp)
        gated = jax.nn.sigmoid(z_ref[pl.ds(r0, SUBLANES), c:] + b_i) * xc_ref[pl.ds(r0, SUBLANES), :]
        h = a * h + jnp.sqrt(1.0 - a * a) * gated
        p = p * a
        b_ref[pl.ds(r0, SUBLANES), :] = h
        a_ref[pl.ds(r0, SUBLANES), :] = p
        return h, p

    h_end, p_end = lax.fori_loop(0, seg, scan_body, (jnp.zeros((SUBLANES, c), F32), jnp.ones((SUBLANES, c), F32)),
                                 unroll=LRU_SCAN_UNROLL)
    row = lax.broadcasted_iota(jnp.int32, (SUBLANES, c), 0)
    enter = jnp.where(row == 0, h_ref[...], 0.0)
    for sgm in range(SUBLANES - 1):
        leave = h_end + p_end * enter
        enter = enter + jnp.where(row == sgm + 1, pltpu.roll(leave, 1, axis=0), 0.0)
    h_ref[...] = (h_end + p_end * enter)[SUBLANES - 1:SUBLANES]

    def out_body(i, carry):
        r0 = pl.multiple_of(i * SUBLANES, SUBLANES)
        h = b_ref[pl.ds(r0, SUBLANES), :] + a_ref[pl.ds(r0, SUBLANES), :] * enter
        y = h * gp_ref[pl.ds(r0, SUBLANES), :]
        for q in range(nq):
            os_ref[q, pl.ds(i, SUBLANES, stride=seg), :] = y[:, lanes(q)]
        return carry

    lax.fori_loop(0, seg, out_body, 0, unroll=LRU_UNROLL)
    o_ref[0] = jnp.concatenate([os_ref[q] for q in range(nq)], axis=-1).astype(BF16)


def _lru(x_lru, g_lru, conv_w, conv_b, w_ri, b_ri, neg_sp, layer, bsz):
    n = x_lru.shape[0]
    c = LRU_WIDTH
    x3 = x_lru.reshape(bsz, T_PAD, c)
    g3 = g_lru.reshape(bsz, T_PAD, c)
    lay = lambda b, t: (layer, 0, 0)
    out = pl.pallas_call(
        _lru_kernel,
        grid=(bsz, T_PAD // LRU_CHUNK),
        in_specs=[
            pl.BlockSpec((1, LRU_CHUNK, c), lambda b, t: (b, t, 0)),
            pl.BlockSpec((1, LRU_CHUNK, c), lambda b, t: (b, t, 0)),
            _const_spec((1, CONV_WIDTH, c), lay),
            _const_spec((1, 1, c), lay),
            _const_spec((1, c, 2 * c), lay),
            _const_spec((1, 1, 2 * c), lay),
            _const_spec((1, 1, c), lay),
        ],
        out_specs=pl.BlockSpec((1, LRU_CHUNK, c), lambda b, t: (b, t, 0)),
        out_shape=jax.ShapeDtypeStruct((bsz, T_PAD, c), BF16),
        scratch_shapes=[
            pltpu.VMEM((c // LANES, LRU_CHUNK + SUBLANES, LANES), F32),
            pltpu.VMEM((c // LANES, LRU_CHUNK, LANES), F32),
            pltpu.VMEM((LRU_CHUNK, c), F32),
            pltpu.VMEM((LRU_CHUNK, c), F32),
            pltpu.VMEM((LRU_CHUNK, 2 * c), F32),
            pltpu.VMEM((LRU_CHUNK, c), F32),
            pltpu.VMEM((LRU_CHUNK, c), F32),
            pltpu.VMEM((c // LANES, LRU_CHUNK, LANES), F32),
            pltpu.VMEM((1, c), F32),
        ],
        compiler_params=pltpu.CompilerParams(
            dimension_semantics=("arbitrary", "arbitrary"), vmem_limit_bytes=VMEM_LIMIT),
        name="rglru",
    )(x3, g3, conv_w, conv_b, w_ri, b_ri, neg_sp)
    return out.reshape(n, c)


def _merge_kernel(hs_ref, ys_ref, yl_ref, gt_ref, wglu_ref, bglu_ref, wsp_ref, wlp_ref, wout_ref, g_ref,
                  *rest, with_router):
    if with_router:
        rw_ref, rb_ref, hs_out_ref, hn_ref, rt_ref, rtt_ref, cnt_ref, run_ref, tri_ref, *w_bf16 = rest
    else:
        hs_out_ref, hn_ref, *w_bf16 = rest
    wglu_b, wsp_b, wlp_b, wout_b = w_bf16

    @pl.when(pl.program_id(0) == 0)
    def _():
        for w_f32, w_b in zip((wglu_ref, wsp_ref, wlp_ref, wout_ref), w_bf16):
            w_b[...] = w_f32[0].astype(BF16)

    ys = jnp.concatenate([ys_ref[q] for q in range(S5_PARTS)], axis=-1)
    glu = ys * jax.nn.sigmoid(_dot(ys.astype(BF16), wglu_b[...]) + bglu_ref[0])
    y_a = _dot(glu.astype(BF16), wsp_b[...])
    y_b = _dot(yl_ref[...], wlp_b[...])
    y = gt_ref[:, :D_MODEL].astype(F32) * y_a + gt_ref[:, D_MODEL:].astype(F32) * y_b
    hs = hs_ref[...] + _dot(y.astype(BF16), wout_b[...])
    hs_out_ref[...] = hs
    hn = _rms(hs, g_ref[0])
    if not with_router:
        hn_ref[...] = hn.astype(BF16)
    else:
        _rows_to_tiles(hn_ref, hn)
        logits = _dot(hn.astype(BF16), rw_ref[0].astype(BF16)) + rb_ref[0]
        lane = lax.broadcasted_iota(jnp.int32, logits.shape, 1).astype(F32)
        m1 = jnp.max(logits, axis=-1, keepdims=True)
        i1 = jnp.min(jnp.where(logits == m1, lane, float(LANES)), axis=-1, keepdims=True)
        rest_l = jnp.where(lane == i1, MASKED_LOGIT, logits)
        m2 = jnp.max(rest_l, axis=-1, keepdims=True)
        i2 = jnp.min(jnp.where(rest_l == m2, lane, float(LANES)), axis=-1, keepdims=True)
        e2 = jnp.exp(m2 - m1)
        g1 = 1.0 / (1.0 + e2)
        g2 = e2 / (1.0 + e2)
        @pl.when(pl.program_id(0) == 0)
        def _():
            run_ref[...] = jnp.zeros_like(run_ref)
            r, c = _iota2(tri_ref.shape)
            tri_ref[...] = jnp.where(c < r, 1.0, 0.0).astype(BF16)

        first = lane == i1
        second = lane == i2
        picked = jnp.where(first | second, 1.0, 0.0)
        before = _dot(tri_ref[...], picked.astype(BF16)) + run_ref[...]
        rank1 = jnp.sum(jnp.where(first, before, 0.0), axis=-1, keepdims=True)
        rank2 = jnp.sum(jnp.where(second, before, 0.0), axis=-1, keepdims=True)
        run_ref[...] += jnp.sum(picked, axis=0, keepdims=True)
        cnt_ref[...] = jnp.broadcast_to(run_ref[...], cnt_ref.shape)
        rt = (jnp.where(lane == 0.0, i1, 0.0) + jnp.where(lane == 1.0, i2, 0.0)
              + jnp.where(lane == 2.0, g1, 0.0) + jnp.where(lane == 3.0, g2, 0.0)
              + jnp.where(lane == 4.0, rank1, 0.0) + jnp.where(lane == 5.0, rank2, 0.0))
        rt_ref[...] = rt
        r, c = _iota2((SUBLANES, LANES))
        pick = jnp.where(r == c, 1.0, 0.0).astype(BF16)
        hi = rt.astype(BF16)
        mid = (rt - hi.astype(F32)).astype(BF16)
        lo = (rt - hi.astype(F32) - mid.astype(F32)).astype(BF16)
        nt = (((1,), (1,)), ((), ()))
        rtt_ref[0] = (lax.dot_general(pick, hi, nt, preferred_element_type=F32)
                      + lax.dot_general(pick, mid, nt, preferred_element_type=F32)
                      + lax.dot_general(pick, lo, nt, preferred_element_type=F32))


def _merge(hs, ys_parts, y_lru, gates, w_glu, b_glu, w_sp, w_lp, w_out, ffn_norm, layer, router=None):
    n = hs.shape[0]
    tm = TM
    lay = lambda i: (layer, 0, 0)
    in_specs = [
        pl.BlockSpec((tm, D_MODEL), lambda i: (i, 0)),
        pl.BlockSpec((S5_PARTS, tm, PART_W), lambda i: (0, i, 0)),
        pl.BlockSpec((tm, LRU_WIDTH), lambda i: (i, 0)),
        pl.BlockSpec((tm, 2 * D_MODEL), lambda i: (i, 0)),
        _const_spec((1, S5_WIDTH, S5_WIDTH), lay),
        _const_spec((1, 1, S5_WIDTH), lay),
        _const_spec((1, S5_WIDTH, D_MODEL), lay),
        _const_spec((1, LRU_WIDTH, D_MODEL), lay),
        _const_spec((1, D_MODEL, D_MODEL), lay),
        _const_spec((1, 1, D_MODEL), lay),
    ]
    out_specs = [pl.BlockSpec((tm, D_MODEL), lambda i: (i, 0))]
    out_shape = [jax.ShapeDtypeStruct((n, D_MODEL), F32)]
    if router is None:
        out_specs.append(pl.BlockSpec((tm, D_MODEL), lambda i: (i, 0)))
        out_shape.append(jax.ShapeDtypeStruct((n, D_MODEL), BF16))
    else:
        out_specs.append(pl.BlockSpec((tm * ROW_TILES, LANES), lambda i: (i, 0)))
        out_shape.append(jax.ShapeDtypeStruct((n * ROW_TILES, LANES), F32))
    args = [hs, ys_parts, y_lru, gates, w_glu, b_glu, w_sp, w_lp, w_out, ffn_norm]
    if router is not None:
        rw, rb, j = router
        in_specs += [_const_spec((1, D_MODEL, LANES), lambda i: (j, 0, 0)),
                     _const_spec((1, 1, LANES), lambda i: (j, 0, 0))]
        out_specs += [pl.BlockSpec((tm, LANES), lambda i: (i, 0)),
                      pl.BlockSpec((1, SUBLANES, tm), lambda i: (i, 0, 0)),
                      pl.BlockSpec((SUBLANES, LANES), lambda i: (0, 0))]
        out_shape += [jax.ShapeDtypeStruct((n, LANES), F32),
                      jax.ShapeDtypeStruct((n // tm, SUBLANES, tm), F32),
                      jax.ShapeDtypeStruct((SUBLANES, LANES), F32)]
        args += [rw, rb]
    return pl.pallas_call(
        functools.partial(_merge_kernel, with_router=router is not None),
        grid=(n // tm,),
        in_specs=in_specs,
        out_specs=out_specs,
        out_shape=out_shape,
        scratch_shapes=([pltpu.VMEM((1, LANES), F32), pltpu.VMEM((tm, tm), BF16)] if router is not None else [])
        + [pltpu.VMEM(w.shape[1:], BF16) for w in (w_glu, w_sp, w_lp, w_out)],
        compiler_params=pltpu.CompilerParams(
            dimension_semantics=("arbitrary",), vmem_limit_bytes=VMEM_LIMIT),
        name="merge_router" if router is not None else "merge",
    )(*args)


def _ffn_kernel(x_ref, hs_ref, wg_ref, wu_ref, wd_ref, o_ref, h_ref):
    x = x_ref[...]
    for c0 in range(0, h_ref.shape[1], FF_CHUNK):
        g = _dot(x, wg_ref[0, :, c0:c0 + FF_CHUNK])
        u = _dot(x, wu_ref[0, :, c0:c0 + FF_CHUNK])
        h_ref[:, c0:c0 + FF_CHUNK] = (g * jax.nn.sigmoid(g) * u).astype(BF16)
    o_ref[...] = hs_ref[...] + _dot(h_ref[...], wd_ref[0])


def _ffn(hn, hs, w_gate, w_up, w_down, layer):
    n = hn.shape[0]
    ff = w_gate.shape[-1]
    lay = lambda i: (layer, 0, 0)
    return pl.pallas_call(
        _ffn_kernel,
        grid=(n // TM_FFN,),
        in_specs=[
            pl.BlockSpec((TM_FFN, D_MODEL), lambda i: (i, 0)),
            pl.BlockSpec((TM_FFN, D_MODEL), lambda i: (i, 0)),
            _const_spec((1, D_MODEL, ff), lay),
            _const_spec((1, D_MODEL, ff), lay),
            _const_spec((1, ff, D_MODEL), lay),
        ],
        out_specs=pl.BlockSpec((TM_FFN, D_MODEL), lambda i: (i, 0)),
        out_shape=jax.ShapeDtypeStruct((n, D_MODEL), F32),
        scratch_shapes=[pltpu.VMEM((TM_FFN, ff), BF16)],
        compiler_params=pltpu.CompilerParams(
            dimension_semantics=("arbitrary",), vmem_limit_bytes=VMEM_LIMIT),
        name="dense_ffn",
    )(hn, hs, w_gate, w_up, w_down)


def _moe_plan(route_t, counts_f, n):
    n_blocks = -(-2 * n // MOE_BLOCK) + N_EXPERTS
    e = jnp.stack([route_t[:, 0, :], route_t[:, 1, :]]).astype(jnp.int32)
    rank = jnp.stack([route_t[:, 4, :], route_t[:, 5, :]]).astype(jnp.int32)
    counts = counts_f[0, :N_EXPERTS].astype(jnp.int32)
    padded = ((counts + MOE_BLOCK - 1) // MOE_BLOCK) * MOE_BLOCK
    cum_pad = jnp.cumsum(padded)
    pad_start = cum_pad - padded
    pos = rank
    for x in range(N_EXPERTS):
        pos = pos + jnp.where(e == x, pad_start[x], 0)
    block_start = jnp.arange(n_blocks, dtype=jnp.int32) * MOE_BLOCK
    block_expert = jnp.minimum(jnp.sum((block_start[:, None] >= cum_pad[None, :]).astype(jnp.int32), axis=1),
                               N_EXPERTS - 1)
    n_used = (cum_pad[-1] // MOE_BLOCK).astype(jnp.int32).reshape(1)
    pad_range = jnp.stack([pad_start + counts, cum_pad], axis=1).reshape(2 * N_EXPERTS).astype(jnp.int32)
    ids = jnp.arange(N_EXPERTS, dtype=jnp.int32)
    used = counts > 0
    run_buffer = (jnp.cumsum(used.astype(jnp.int32)) - 1) % 2
    later = jnp.where((ids[None, :] > ids[:, None]) & used[None, :], ids[None, :], N_EXPERTS)
    next_used = jnp.min(later, axis=1)
    next_used = jnp.where(next_used == N_EXPERTS, -1, next_used)
    of_block = lambda table: jnp.sum(jnp.where(block_expert[:, None] == ids[None, :], table[None, :], 0), axis=1)
    weight_plan = jnp.stack([block_expert,
                             (block_start == of_block(pad_start)).astype(jnp.int32),
                             of_block(run_buffer), of_block(next_used)]).astype(jnp.int32)
    return pos.reshape(2 * n), weight_plan, n_used, pad_range


def _tile(ref, index):
    return ref.at[pl.ds(pl.multiple_of(index * ROW_TILES, ROW_TILES), ROW_TILES)]


def _tile_gather(src_hbm, dst, sem, rows, index_of):
    def body(grp, carry):
        r0 = grp * GATHER_GROUP
        index = [index_of(r0 + j) for j in range(GATHER_GROUP)]
        for j in range(GATHER_GROUP):
            pltpu.make_async_copy(_tile(src_hbm, index[j]), _tile(dst, r0 + j), sem).start(priority=j % 2)
        return carry
    lax.fori_loop(0, rows // GATHER_GROUP, body, 0)


def _tile_gather_wait(src_hbm, dst, sem, rows):
    pltpu.make_async_copy(src_hbm.at[pl.ds(0, rows * ROW_TILES)], dst, sem).wait()


def _moe_dispatch_kernel(pos_ref, pad_ref, nu_ref, x_hbm, xs_hbm, xbuf, zero_ref, sem_in, sem_out, sem_fill):
    t = pl.program_id(0)
    nt = pl.num_programs(0)
    n = nt * TM
    tile_rows = TM * ROW_TILES
    block_rows = MOE_BLOCK * ROW_TILES
    n_blocks = xs_hbm.shape[0] // block_rows

    def read(tile):
        buf = tile % DISPATCH_RING
        return pltpu.make_async_copy(x_hbm.at[pl.ds(pl.multiple_of(tile * tile_rows, tile_rows), tile_rows)],
                                     xbuf.at[buf], sem_in.at[buf])

    def scatter_wait(tile):
        buf = tile % DISPATCH_RING
        for k in range(2):
            pltpu.make_async_copy(xbuf.at[buf], xs_hbm.at[pl.ds(0, tile_rows)], sem_out.at[buf]).wait()

    @pl.when(t == 0)
    def _():
        read(0).start()
        zero_ref[...] = jnp.zeros_like(zero_ref)
        zero_tile = zero_ref.at[pl.ds(0, ROW_TILES)]
        for e in range(N_EXPERTS):
            def fill(slot, carry):
                pltpu.make_async_copy(zero_tile, _tile(xs_hbm, slot), sem_fill.at[0]).start()
                return carry

            def fill_wait(slot, carry):
                pltpu.make_async_copy(zero_tile, _tile(xs_hbm, slot), sem_fill.at[0]).wait()
                return carry
            lax.fori_loop(pad_ref[2 * e], pad_ref[2 * e + 1], fill, 0)
            lax.fori_loop(pad_ref[2 * e], pad_ref[2 * e + 1], fill_wait, 0)

        def block_of(blk):
            return xs_hbm.at[pl.ds(pl.multiple_of(blk * block_rows, block_rows), block_rows)]

        def fill_block(blk, carry):
            pltpu.make_async_copy(zero_ref, block_of(blk), sem_fill.at[0]).start()
            return carry

        def fill_block_wait(blk, carry):
            pltpu.make_async_copy(zero_ref, block_of(blk), sem_fill.at[0]).wait()
            return carry
        lax.fori_loop(nu_ref[0], n_blocks, fill_block, 0)
        lax.fori_loop(nu_ref[0], n_blocks, fill_block_wait, 0)

    @pl.when(t + 1 < nt)
    def _():
        @pl.when(t + 1 >= DISPATCH_RING)
        def _():
            scatter_wait(t + 1 - DISPATCH_RING)
        read(t + 1).start()

    read(t).wait()
    src = xbuf.at[t % DISPATCH_RING]
    for k in range(2):
        def put(grp, carry):
            r0 = grp * GATHER_GROUP
            slot = [pos_ref[k * n + t * TM + r0 + j] for j in range(GATHER_GROUP)]
            for j in range(GATHER_GROUP):
                pltpu.make_async_copy(_tile(src, r0 + j), _tile(xs_hbm, slot[j]),
                                      sem_out.at[t % DISPATCH_RING]).start(priority=j % 2)
            return carry
        lax.fori_loop(0, TM // GATHER_GROUP, put, 0)

    @pl.when(t == nt - 1)
    def _():
        for back in range(DISPATCH_RING - 1, -1, -1):
            @pl.when(t - back >= 0)
            def _():
                scatter_wait(t - back)


def _moe_dispatch(hn_tiles, pos, pad_range, n_used, n_slots):
    n = hn_tiles.shape[0] // ROW_TILES
    return pl.pallas_call(
        _moe_dispatch_kernel,
        grid_spec=pltpu.PrefetchScalarGridSpec(
            num_scalar_prefetch=3,
            grid=(n // TM,),
            in_specs=[pl.BlockSpec(memory_space=pl.ANY)],
            out_specs=pl.BlockSpec(memory_space=pl.ANY),
            scratch_shapes=[pltpu.VMEM((DISPATCH_RING, TM * ROW_TILES, LANES), F32),
                            pltpu.VMEM((MOE_BLOCK * ROW_TILES, LANES), F32),
                            pltpu.SemaphoreType.DMA((DISPATCH_RING,)),
                            pltpu.SemaphoreType.DMA((DISPATCH_RING,)),
                            pltpu.SemaphoreType.DMA((1,))],
        ),
        out_shape=jax.ShapeDtypeStruct((n_slots * ROW_TILES, LANES), F32),
        compiler_params=pltpu.CompilerParams(dimension_semantics=("arbitrary",)),
        name="moe_dispatch",
    )(pos, pad_range, n_used, hn_tiles)


def _moe_ffn_kernel(be_ref, first_ref, buf_ref, next_ref, nu_ref, x_ref, wg_hbm, wu_hbm, wd_hbm, y_ref,
                    wg_buf, wu_buf, wd_buf, sem, *, first_expert):
    i = pl.program_id(0)
    sources = (wg_hbm, wu_hbm, wd_hbm)
    buffers = (wg_buf, wu_buf, wd_buf)

    def copies(expert, b):
        return [pltpu.make_async_copy(src.at[first_expert + expert], dst.at[b], sem.at[b, k])
                for k, (src, dst) in enumerate(zip(sources, buffers))]

    @pl.when(i < nu_ref[0])
    def _():
        b = buf_ref[i]

        @pl.when(first_ref[i] == 1)
        def _():
            @pl.when(i == 0)
            def _():
                for c in copies(be_ref[i], b):
                    c.start()
            for c in copies(be_ref[i], b):
                c.wait()

            @pl.when(next_ref[i] >= 0)
            def _():
                for c in copies(next_ref[i], 1 - b):
                    c.start()

        x = _rows_from_tiles(x_ref, MOE_BLOCK).astype(BF16)
        g = _dot(x, wg_buf[b].astype(BF16))
        h = g * jax.nn.sigmoid(g) * _dot(x, wu_buf[b].astype(BF16))
        _rows_to_tiles(y_ref, _dot(h.astype(BF16), wd_buf[b].astype(BF16)))

    @pl.when(i >= nu_ref[0])
    def _():
        y_ref[...] = jnp.zeros_like(y_ref)


def _moe_ffn(xs_tiles, weight_plan, n_used, w_gate, w_up, w_down, first):
    n_blocks = weight_plan.shape[1]
    ff = w_gate.shape[-1]
    return pl.pallas_call(
        functools.partial(_moe_ffn_kernel, first_expert=first),
        grid_spec=pltpu.PrefetchScalarGridSpec(
            num_scalar_prefetch=5,
            grid=(n_blocks,),
            in_specs=[
                pl.BlockSpec((MOE_BLOCK * ROW_TILES, LANES),
                             lambda i, be, fi, bu, nx, nu: (jnp.maximum(jnp.minimum(i, nu[0] - 1), 0), 0)),
                pl.BlockSpec(memory_space=pl.ANY),
                pl.BlockSpec(memory_space=pl.ANY),
                pl.BlockSpec(memory_space=pl.ANY),
            ],
            out_specs=pl.BlockSpec((MOE_BLOCK * ROW_TILES, LANES), lambda i, be, fi, bu, nx, nu: (i, 0)),
            scratch_shapes=[pltpu.VMEM((2, D_MODEL, ff), F32),
                            pltpu.VMEM((2, D_MODEL, ff), F32),
                            pltpu.VMEM((2, ff, D_MODEL), F32),
                            pltpu.SemaphoreType.DMA((2, 3))],
        ),
        out_shape=jax.ShapeDtypeStruct((n_blocks * MOE_BLOCK * ROW_TILES, LANES), F32),
        compiler_params=pltpu.CompilerParams(
            dimension_semantics=("arbitrary",), vmem_limit_bytes=VMEM_LIMIT),
        name="moe_ffn",
    )(weight_plan[0], weight_plan[1], weight_plan[2], weight_plan[3], n_used, xs_tiles, w_gate, w_up, w_down)


def _moe_combine_kernel(pos_ref, hs_ref, rt_ref, ys_hbm, o_ref, ybuf, sem):
    i = pl.program_id(0)
    nt = pl.num_programs(0)
    slot = i % 2

    def start(t, s):
        for k in range(2):
            _tile_gather(ys_hbm, ybuf.at[s, k], sem.at[s], TM, lambda r: pos_ref[k * (nt * TM) + t * TM + r])

    @pl.when(i == 0)
    def _():
        start(0, 0)

    @pl.when(i + 1 < nt)
    def _():
        start(i + 1, 1 - slot)

    for k in range(2):
        _tile_gather_wait(ys_hbm, ybuf.at[slot, k], sem.at[slot], TM)
    rt = rt_ref[...]
    lane = lax.broadcasted_iota(jnp.int32, rt.shape, 1)
    g1 = jnp.sum(jnp.where(lane == 2, rt, 0.0), axis=-1, keepdims=True)
    g2 = jnp.sum(jnp.where(lane == 3, rt, 0.0), axis=-1, keepdims=True)
    o_ref[...] = (hs_ref[...] + g1 * _rows_from_tiles(ybuf.at[slot, 0], TM)
                  + g2 * _rows_from_tiles(ybuf.at[slot, 1], TM))


def _moe_combine(hs, route, ys_tiles, pos):
    n = hs.shape[0]
    return pl.pallas_call(
        _moe_combine_kernel,
        grid_spec=pltpu.PrefetchScalarGridSpec(
            num_scalar_prefetch=1,
            grid=(n // TM,),
            in_specs=[
                pl.BlockSpec((TM, D_MODEL), lambda i, p: (i, 0)),
                pl.BlockSpec((TM, LANES), lambda i, p: (i, 0)),
                pl.BlockSpec(memory_space=pl.ANY),
            ],
            out_specs=pl.BlockSpec((TM, D_MODEL), lambda i, p: (i, 0)),
            scratch_shapes=[pltpu.VMEM((2, 2, TM * ROW_TILES, LANES), F32),
                            pltpu.SemaphoreType.DMA((2,))],
        ),
        out_shape=jax.ShapeDtypeStruct((n, D_MODEL), F32),
        compiler_params=pltpu.CompilerParams(
            dimension_semantics=("arbitrary",), vmem_limit_bytes=VMEM_LIMIT),
        name="moe_combine",
    )(pos, hs, route, ys_tiles)


def _final_kernel(a_ref, b_ref, g_ref, o_ref):
    tb = a_ref.shape[1]
    o_ref[0, :tb - N_META] = _rms(a_ref[0, N_META:], g_ref[...])
    o_ref[0, tb - N_META:] = _rms(b_ref[0], g_ref[...])


def _final_norm(hs, g, bsz, seq):
    hs3 = hs.reshape(bsz, T_PAD, D_MODEL)
    return pl.pallas_call(
        _final_kernel,
        grid=(bsz, seq // TB_FINAL),
        in_specs=[pl.BlockSpec((1, TB_FINAL, D_MODEL), lambda b, i: (b, i, 0)),
                  pl.BlockSpec((1, N_META, D_MODEL), lambda b, i: (b, (i + 1) * (TB_FINAL // N_META), 0)),
                  _const_spec((1, D_MODEL), lambda b, i: (0, 0))],
        out_specs=pl.BlockSpec((1, TB_FINAL, D_MODEL), lambda b, i: (b, i, 0)),
        out_shape=jax.ShapeDtypeStruct((bsz, seq, D_MODEL), F32),
        compiler_params=pltpu.CompilerParams(
            dimension_semantics=("arbitrary", "arbitrary"), vmem_limit_bytes=VMEM_LIMIT),
        name="final_norm",
    )(hs3, hs3, g)


def _head_blockdiag(w):
    eye = jnp.eye(LRU_HEADS, dtype=w.dtype)
    out = jnp.einsum('lnhk,nm->lnhmk', w, eye)
    return out.reshape(w.shape[0], LRU_WIDTH, LRU_WIDTH)


def kernel(x, meta_tokens, mix_norm, w_in, merge_bias, s5_lambda_re, s5_lambda_im, s5_log_dt, s5_b_re, s5_b_im, s5_c_re, s5_c_im, s5_d, s5_w_glu, s5_b_glu, s5_w_proj, lru_conv_w, lru_conv_b, lru_w_rgate, lru_b_rgate, lru_w_igate, lru_b_igate, lru_lambda, lru_w_proj, w_out, ffn_norm, dense_w_gate, dense_w_up, dense_w_down, router_w, router_b, moe_w_gate, moe_w_up, moe_w_down, final_norm):
    bsz, seq, d = x.shape
    depth = w_in.shape[0]
    assert d == D_MODEL and N_META + seq <= T_PAD
    n = bsz * T_PAD
    assert n % TM == 0 and n % TM_FFN == 0 and seq % TB_FINAL == 0 and TB_FINAL % N_META == 0

    meta = jnp.broadcast_to(meta_tokens[None].astype(x.dtype), (bsz, N_META, d))
    pad = jnp.zeros((bsz, T_PAD - N_META - seq, d), x.dtype)
    hs = jnp.concatenate([meta, x, pad], axis=1).reshape(n, d)

    row3 = lambda a: a[:, None, :]
    w_ri = jnp.concatenate([_head_blockdiag(lru_w_rgate), _head_blockdiag(lru_w_igate)], axis=-1).astype(BF16)
    b_ri = jnp.concatenate([lru_b_rgate, lru_b_igate], axis=-1)
    neg_sp = -LRU_C * jax.nn.softplus(-lru_lambda)
    dense = [w.astype(BF16) for w in (dense_w_gate, dense_w_up, dense_w_down)]
    n_moe = router_w.shape[0]
    moe = [w.reshape((n_moe * N_EXPERTS,) + w.shape[2:]) for w in (moe_w_gate, moe_w_up, moe_w_down)]
    s5_ops = jax.vmap(_s5_prep)(s5_lambda_re, s5_lambda_im, s5_log_dt, s5_b_re, s5_b_im, s5_c_re, s5_c_im, s5_d)
    rw_pad = jnp.pad(router_w, ((0, 0), (0, 0), (0, LANES - N_EXPERTS)))
    rb_pad = jnp.pad(router_b, ((0, 0), (0, LANES - N_EXPERTS)), constant_values=MASKED_LOGIT)

    for layer in range(depth):
        u_parts, x_lru, g_lru, gates = _in_proj(hs, row3(mix_norm), w_in, row3(merge_bias), layer)
        ys_parts = _s5_scan(u_parts, s5_ops, layer, bsz)
        y_lru = _lru(x_lru, g_lru, lru_conv_w, row3(lru_conv_b), w_ri, row3(b_ri), row3(neg_sp), layer, bsz)
        j = layer // 2
        router = (rw_pad, row3(rb_pad), j) if layer % 2 == 1 else None
        res = _merge(hs, ys_parts, y_lru, gates, s5_w_glu, row3(s5_b_glu), s5_w_proj, lru_w_proj, w_out,
                     row3(ffn_norm), layer, router)
        if layer % 2 == 0:
            hs, hn = res
            hs = _ffn(hn, hs, *dense, layer=j)
        else:
            hs, hn, route, route_t, counts = res
            pos, weight_plan, n_used, pad_range = _moe_plan(route_t, counts, n)
            xs = _moe_dispatch(hn, pos, pad_range, n_used, weight_plan.shape[1] * MOE_BLOCK)
            ys = _moe_ffn(xs, weight_plan, n_used, *moe, first=j * N_EXPERTS)
            hs = _moe_combine(hs, route, ys, pos)

    return _final_norm(hs, final_norm[None, :], bsz, seq)
```

```python
import functools

import jax
import jax.numpy as jnp
from jax import lax
from jax.experimental import pallas as pl
from jax.experimental.pallas import tpu as pltpu

F32 = jnp.float32
BF16 = jnp.bfloat16

D_MODEL = 1024
N_META = 16
S5_WIDTH = 512
S5_GROUP = 16
S5_GROUPS = 32
S5_STATE = 64
LRU_WIDTH = 512
LRU_HEADS = 8
CONV_WIDTH = 4
LRU_C = 8.0
N_EXPERTS = 8
EPS = 1e-6
LOG2_E = 1.4426950408889634

FOLD = 8
S5_PARTS = 4
PART_W = S5_WIDTH // S5_PARTS
PART_GROUPS = PART_W // S5_GROUP
PART_STATE = PART_GROUPS * S5_STATE
FOLD_W = FOLD * PART_W

T_PAD = 8256
ROWS = T_PAD // FOLD
TM = 688
TM_FFN = 688
FF_CHUNK = 1024
MOE_BLOCK = 512
GATHER_GROUP = 8
DISPATCH_RING = 3
LRU_CHUNK = 1032
LRU_UNROLL = 3
LRU_SCAN_UNROLL = 43
TB_FINAL = 2048
VMEM_LIMIT = 56 * 1024 * 1024
LANES = 128
SUBLANES = 8
S5_SEG = ROWS // SUBLANES
S5_UNROLL = 3
ROW_TILES = D_MODEL // LANES
MASKED_LOGIT = float("-inf")


def _dot(a, b):
    return jnp.dot(a, b, preferred_element_type=F32)


def _const_spec(block_shape, index_map):
    return pl.BlockSpec(block_shape, index_map, pipeline_mode=pl.Buffered(1))


def _rms(x, g):
    ms = jnp.mean(x * x, axis=-1, keepdims=True)
    return x * lax.rsqrt(ms + EPS) * g


def _rows_to_tiles(ref, x):
    rows = x.shape[0]
    for s in range(ROW_TILES):
        ref[pl.ds(s, rows, stride=ROW_TILES), :] = x[:, s * LANES:(s + 1) * LANES]


def _rows_from_tiles(ref, rows):
    return jnp.concatenate([ref[pl.ds(s, rows, stride=ROW_TILES), :] for s in range(ROW_TILES)], axis=-1)


def _in_proj_kernel(hs_ref, g_ref, wf_ref, mb_ref, u_ref, xl_ref, gl_ref, gt_ref, w_ref):
    @pl.when(pl.program_id(0) == 0)
    def _():
        w_ref[0] = wf_ref[0].astype(BF16)

    hn = _rms(hs_ref[...], g_ref[0]).astype(BF16)
    u = _dot(hn, w_ref[0, :, 0:S5_WIDTH])
    for q in range(S5_PARTS):
        u_ref[q] = u[:, q * PART_W:(q + 1) * PART_W]
    o_x = S5_WIDTH
    o_g = o_x + LRU_WIDTH
    o_m = o_g + LRU_WIDTH
    xl_ref[...] = _dot(hn, w_ref[0, :, o_x:o_g]).astype(BF16)
    gl_ref[...] = _dot(hn, w_ref[0, :, o_g:o_m]).astype(BF16)
    z = _dot(hn, w_ref[0, :, o_m:]) + mb_ref[0]
    gt_ref[...] = jax.nn.sigmoid(z).astype(BF16)


def _in_proj(hs, mix_norm, w_in, merge_bias, layer):
    n = hs.shape[0]
    d_in = w_in.shape[-1]
    lay = lambda i: (layer, 0, 0)
    return pl.pallas_call(
        _in_proj_kernel,
        grid=(n // TM,),
        in_specs=[
            pl.BlockSpec((TM, D_MODEL), lambda i: (i, 0)),
            _const_spec((1, 1, D_MODEL), lay),
            _const_spec((1, D_MODEL, d_in), lay),
            _const_spec((1, 1, 2 * D_MODEL), lay),
        ],
        out_specs=[
            pl.BlockSpec((S5_PARTS, TM, PART_W), lambda i: (0, i, 0)),
            pl.BlockSpec((TM, LRU_WIDTH), lambda i: (i, 0)),
            pl.BlockSpec((TM, LRU_WIDTH), lambda i: (i, 0)),
            pl.BlockSpec((TM, 2 * D_MODEL), lambda i: (i, 0)),
        ],
        out_shape=[
            jax.ShapeDtypeStruct((S5_PARTS, n, PART_W), F32),
            jax.ShapeDtypeStruct((n, LRU_WIDTH), BF16),
            jax.ShapeDtypeStruct((n, LRU_WIDTH), BF16),
            jax.ShapeDtypeStruct((n, 2 * D_MODEL), BF16),
        ],
        scratch_shapes=[pltpu.VMEM((1, D_MODEL, d_in), BF16)],
        compiler_params=pltpu.CompilerParams(
            dimension_semantics=("arbitrary",), vmem_limit_bytes=VMEM_LIMIT),
        name="in_proj",
    )(hs, mix_norm, w_in, merge_bias)


def _s5_prep(lam_re, lam_im, log_dt, b_re, b_im, c_re, c_im, d_skip):
    dt = jnp.exp(log_dt)[:, None]
    mag = jnp.exp(lam_re * dt)
    a_re = mag * jnp.cos(lam_im * dt)
    a_im = mag * jnp.sin(lam_im * dt)
    den = lam_re * lam_re + lam_im * lam_im
    num_re = a_re - 1.0
    coef_re = (num_re * lam_re + a_im * lam_im) / den
    coef_im = (a_im * lam_re - num_re * lam_im) / den
    bb_re = coef_re[..., None] * b_re - coef_im[..., None] * b_im
    bb_im = coef_re[..., None] * b_im + coef_im[..., None] * b_re

    def cmul(xr, xi, yr, yi):
        return xr * yr - xi * yi, xr * yi + xi * yr

    def powers(br, bi, n):
        pr, pi = [jnp.ones_like(br)], [jnp.zeros_like(bi)]
        for _ in range(n):
            r, i = cmul(pr[-1], pi[-1], br, bi)
            pr.append(r)
            pi.append(i)
        return jnp.stack(pr), jnp.stack(pi)

    p_re, p_im = powers(a_re, a_im, FOLD)

    def per_part(x):
        lead = x.shape[:-3]
        xp = x.reshape(lead + (S5_PARTS, PART_GROUPS) + x.shape[-2:])
        return jnp.moveaxis(xp, len(lead), 0)

    rev_re = jnp.stack([p_re[FOLD - 1 - j] for j in range(FOLD)])
    rev_im = jnp.stack([p_im[FOLD - 1 - j] for j in range(FOLD)])
    wr, wi = cmul(rev_re[..., None], rev_im[..., None], bb_re[None], bb_im[None])
    w_ri = jnp.swapaxes(jnp.stack([wr, wi], axis=1), -1, -2)
    xq = jnp.transpose(per_part(w_ri), (0, 2, 1, 3, 4, 5)).reshape(S5_PARTS, 2, FOLD_W, S5_STATE)

    ca_re, ca_im = cmul(c_re[None], c_im[None], p_re[:, :, None, :], p_im[:, :, None, :])
    bt_re = jnp.swapaxes(bb_re, -1, -2)[None, :, :, None, :]
    bt_im = jnp.swapaxes(bb_im, -1, -2)[None, :, :, None, :]
    taps = jnp.sum(ca_re[:FOLD, :, None] * bt_re - ca_im[:FOLD, :, None] * bt_im, axis=-1)
    skip = d_skip.reshape(S5_GROUPS, S5_GROUP)
    taps = taps.at[0].add(skip[:, :, None] * jnp.eye(S5_GROUP, dtype=F32)[None])
    rc = jnp.transpose(per_part(taps), (0, 2, 3, 1, 4)).reshape(S5_PARTS, PART_W, FOLD * S5_GROUP)

    v_ri = jnp.swapaxes(jnp.stack([ca_re[1:], -ca_im[1:]], axis=0), -1, -2)
    vc = jnp.transpose(per_part(v_ri), (0, 1, 3, 4, 2, 5)).reshape(S5_PARTS, 2 * PART_STATE, FOLD * S5_GROUP)

    def part_vec(x):
        lead = x.shape[:-2]
        xp = x.reshape(lead + (S5_PARTS, PART_STATE))
        return jnp.moveaxis(xp, -2, 0)

    row_re, row_im = p_re[FOLD], p_im[FOLD]
    seg_re, seg_im = jnp.ones_like(row_re), jnp.zeros_like(row_im)
    for bit in bin(S5_SEG)[2:]:
        seg_re, seg_im = cmul(seg_re, seg_im, seg_re, seg_im)
        if bit == '1':
            seg_re, seg_im = cmul(seg_re, seg_im, row_re, row_im)
    decay = jnp.stack([jnp.stack([part_vec(row_re), part_vec(row_im)], axis=1),
                       jnp.stack([part_vec(seg_re), part_vec(seg_im)], axis=1)], axis=1)
    return xq, rc, vc, decay[:, :, :, None, :]


def _iota2(shape):
    return (lax.broadcasted_iota(jnp.int32, shape, 0), lax.broadcasted_iota(jnp.int32, shape, 1))


def _s5_expand(xq, rc, vc, w1_s, tv_s):
    ps = PART_STATE
    lg_state, lg_group, lg_part = (v.bit_length() - 1 for v in (S5_STATE, S5_GROUP, PART_W))
    grp = PART_GROUPS - 1
    one_hot = lambda m: jnp.where(m, 1.0, 0.0).astype(BF16)
    r, c = _iota2((S5_STATE, ps))
    e1 = one_hot(r == (c & (S5_STATE - 1)))
    r, c = _iota2((FOLD * S5_GROUP, FOLD_W))
    e2 = one_hot(((r >> lg_group) == (c >> lg_part)) & ((r & (S5_GROUP - 1)) == (c & (S5_GROUP - 1))))
    r, c = _iota2((FOLD_W, ps))
    m1 = ((r >> lg_group) & grp) == (c >> lg_state)
    for ri in range(2):
        w1_s[:, ri * ps:(ri + 1) * ps] = jnp.where(m1, _dot(xq[ri].astype(BF16), e1), 0.0).astype(BF16)
    r, c = _iota2((PART_W, FOLD_W))
    m2 = (r >> lg_group) == ((c >> lg_group) & grp)
    r0 = jnp.where(m2, _dot(rc.astype(BF16), e2), 0.0).astype(BF16)
    for j in range(FOLD):
        if j == 0:
            blk = r0
        else:
            blk = jnp.concatenate([jnp.zeros((PART_W, j * PART_W), BF16), r0[:, :FOLD_W - j * PART_W]], axis=1)
        tv_s[j * PART_W:(j + 1) * PART_W, :] = blk
    r, c = _iota2((2 * ps, FOLD_W))
    m3 = ((r >> lg_state) & grp) == ((c >> lg_group) & grp)
    tv_s[FOLD_W:, :] = jnp.where(m3, _dot(vc.astype(BF16), e2), 0.0).astype(BF16)


def _s5_kernel(u_ref, xq_ref, rc_ref, vc_ref, dec_ref, y_ref, w1_s, tv_s, up_ref, f_ref, hp_ref):
    ps = PART_STATE
    tstride = S5_SEG * FOLD
    cols = lambda j: slice(j * PART_W, (j + 1) * PART_W)

    @pl.when(pl.program_id(1) == 0)
    def _():
        _s5_expand(xq_ref[0, 0], rc_ref[0, 0], vc_ref[0, 0], w1_s, tv_s)

    def fold_body(i, carry):
        r0 = pl.multiple_of(i * SUBLANES, SUBLANES)
        for j in range(FOLD):
            up_ref[pl.ds(r0, SUBLANES), cols(j)] = u_ref[0, 0, pl.ds(i * FOLD + j, SUBLANES, stride=tstride), :]
        return carry

    lax.fori_loop(0, S5_SEG, fold_body, 0, unroll=S5_UNROLL)
    u = up_ref[...].astype(BF16)
    f_ref[...] = _dot(u, w1_s[...])
    ar = jnp.broadcast_to(dec_ref[0, 0, 0, 0], (SUBLANES, ps))
    ai = jnp.broadcast_to(dec_ref[0, 0, 0, 1], (SUBLANES, ps))

    def step(i, hr, hi):
        r0 = pl.multiple_of(i * SUBLANES, SUBLANES)
        return (ar * hr - ai * hi + f_ref[pl.ds(r0, SUBLANES), :ps],
                ar * hi + ai * hr + f_ref[pl.ds(r0, SUBLANES), ps:])

    zero = jnp.zeros((SUBLANES, ps), F32)
    er, ei = lax.fori_loop(0, S5_SEG, lambda i, c: step(i, *c), (zero, zero))
    sr = dec_ref[0, 0, 1, 0]
    si = dec_ref[0, 0, 1, 1]
    row = lax.broadcasted_iota(jnp.int32, (SUBLANES, ps), 0)
    nr, ni = zero, zero
    for sgm in range(SUBLANES - 1):
        lr = er + sr * nr - si * ni
        li = ei + sr * ni + si * nr
        nr = nr + jnp.where(row == sgm + 1, pltpu.roll(lr, 1, axis=0), 0.0)
        ni = ni + jnp.where(row == sgm + 1, pltpu.roll(li, 1, axis=0), 0.0)

    def state_body(i, carry):
        hr, hi = carry
        r0 = pl.multiple_of(i * SUBLANES, SUBLANES)
        hp_ref[pl.ds(r0, SUBLANES), :ps] = hr
        hp_ref[pl.ds(r0, SUBLANES), ps:] = hi
        return step(i, hr, hi)

    lax.fori_loop(0, S5_SEG, state_body, (nr, ni))
    hp = hp_ref[...].astype(BF16)
    wide = 2 * PART_W
    for c0 in range(0, FOLD_W, wide):
        y = _dot(u[:, :c0 + wide], tv_s[:c0 + wide, c0:c0 + wide]) + _dot(hp, tv_s[FOLD_W:, c0:c0 + wide])
        f_ref[:, c0:c0 + wide] = jax.nn.gelu(y)

    def unfold_body(i, carry):
        r0 = pl.multiple_of(i * SUBLANES, SUBLANES)
        for j in range(FOLD):
            y_ref[0, 0, pl.ds(i * FOLD + j, SUBLANES, stride=tstride), :] = f_ref[pl.ds(r0, SUBLANES), cols(j)]
        return carry

    lax.fori_loop(0, S5_SEG, unfold_body, 0, unroll=S5_UNROLL)


def _s5_scan(u_parts, ops, layer, bsz):
    xq, rc, vc, decay = ops
    n = u_parts.shape[1]
    u4 = u_parts.reshape(S5_PARTS, bsz, T_PAD, PART_W)
    lay4 = lambda q, b: (layer, q, 0, 0)
    y4 = pl.pallas_call(
        _s5_kernel,
        grid=(S5_PARTS, bsz),
        in_specs=[
            pl.BlockSpec((1, 1, T_PAD, PART_W), lambda q, b: (q, b, 0, 0)),
            pl.BlockSpec((1, 1, 2, FOLD_W, S5_STATE), lambda q, b: (layer, q, 0, 0, 0)),
            pl.BlockSpec((1, 1, PART_W, FOLD * S5_GROUP), lay4),
            pl.BlockSpec((1, 1, 2 * PART_STATE, FOLD * S5_GROUP), lay4),
            pl.BlockSpec((1, 1, 2, 2, 1, PART_STATE), lambda q, b: (layer, q, 0, 0, 0, 0)),
        ],
        out_specs=pl.BlockSpec((1, 1, T_PAD, PART_W), lambda q, b: (q, b, 0, 0)),
        out_shape=jax.ShapeDtypeStruct((S5_PARTS, bsz, T_PAD, PART_W), F32),
        scratch_shapes=[
            pltpu.VMEM((FOLD_W, 2 * PART_STATE), BF16),
            pltpu.VMEM((FOLD_W + 2 * PART_STATE, FOLD_W), BF16),
            pltpu.VMEM((ROWS, FOLD_W), F32),
            pltpu.VMEM((ROWS, 2 * PART_STATE), F32),
            pltpu.VMEM((ROWS, 2 * PART_STATE), F32),
        ],
        compiler_params=pltpu.CompilerParams(
            dimension_semantics=("arbitrary", "arbitrary"), vmem_limit_bytes=VMEM_LIMIT),
        name="s5_scan",
    )(u4, xq, rc, vc, decay)
    return y4.reshape(S5_PARTS, n, PART_W)


def _lru_kernel(x_ref, g_ref, cw_ref, cb_ref, wri_ref, bri_ref, nsp_ref, o_ref,
                xs_ref, gs_ref, xc_ref, gp_ref, z_ref, a_ref, b_ref, os_ref, h_ref):
    tc = LRU_CHUNK
    c = LRU_WIDTH
    seg = tc // SUBLANES
    nq = c // LANES
    lanes = lambda q: slice(q * LANES, (q + 1) * LANES)
    halo = SUBLANES

    @pl.when(pl.program_id(1) == 0)
    def _():
        xs_ref[:, 0:halo, :] = jnp.zeros((nq, halo, LANES), F32)
        h_ref[...] = jnp.zeros((1, c), F32)

    x = x_ref[0].astype(F32)
    g = g_ref[0].astype(F32)
    for q in range(nq):
        xs_ref[q, halo:, :] = x[:, lanes(q)]
        gs_ref[q] = g[:, lanes(q)]
    taps = [[cw_ref[0, k:k + 1, lanes(q)] for k in range(CONV_WIDTH)] for q in range(nq)]
    bias = [cb_ref[0, :, lanes(q)] for q in range(nq)]

    def conv_body(i, carry):
        r0 = pl.multiple_of(i * SUBLANES, SUBLANES)
        for q in range(nq):
            acc = bias[q]
            for k in range(CONV_WIDTH):
                first = halo - (CONV_WIDTH - 1) + k + i
                acc = acc + taps[q][k] * xs_ref[q, pl.ds(first, SUBLANES, stride=seg), :]
            xc_ref[pl.ds(r0, SUBLANES), lanes(q)] = acc
            gp_ref[pl.ds(r0, SUBLANES), lanes(q)] = jax.nn.gelu(gs_ref[q, pl.ds(i, SUBLANES, stride=seg), :])
        return carry

    lax.fori_loop(0, seg, conv_body, 0, unroll=LRU_UNROLL)
    for q in range(nq):
        xs_ref[q, 0:halo, :] = xs_ref[q, tc:tc + halo, :]

    z_ref[...] = _dot(xc_ref[...].astype(BF16), wri_ref[0])
    b_r = jnp.broadcast_to(bri_ref[0, :, :c], (SUBLANES, c))
    b_i = jnp.broadcast_to(bri_ref[0, :, c:], (SUBLANES, c))
    nsp = jnp.broadcast_to(nsp_ref[0], (SUBLANES, c))

    def scan_body(i, carry):
        h, p = carry
        r0 = pl.multiple_of(i * SUBLANES, SUBLANES)
        a = jnp.exp2(jax.nn.sigmoid(z_ref[pl.ds(r0, SUBLANES), :c] + b_r) * nsp)
        gated = jax.nn.sigmoid(z_ref[pl.ds(r0, SUBLANES), c:] + b_i) * xc_ref[pl.ds(r0, SUBLANES), :]
        w = 1.0 - a * a
        h = a * h + jnp.where(w > 0.0, w * lax.rsqrt(w), 0.0) * gated
        p = p * a
        b_ref[pl.ds(r0, SUBLANES), :] = h
        a_ref[pl.ds(r0, SUBLANES), :] = p
        return h, p

    h_end, p_end = lax.fori_loop(0, seg, scan_body, (jnp.zeros((SUBLANES, c), F32), jnp.ones((SUBLANES, c), F32)),
                                 unroll=LRU_SCAN_UNROLL)
    row = lax.broadcasted_iota(jnp.int32, (SUBLANES, c), 0)
    enter = jnp.where(row == 0, h_ref[...], 0.0)
    for sgm in range(SUBLANES - 1):
        leave = h_end + p_end * enter
        enter = enter + jnp.where(row == sgm + 1, pltpu.roll(leave, 1, axis=0), 0.0)
    h_ref[...] = (h_end + p_end * enter)[SUBLANES - 1:SUBLANES]

    def out_body(i, carry):
        r0 = pl.multiple_of(i * SUBLANES, SUBLANES)
        h = b_ref[pl.ds(r0, SUBLANES), :] + a_ref[pl.ds(r0, SUBLANES), :] * enter
        y = h * gp_ref[pl.ds(r0, SUBLANES), :]
        for q in range(nq):
            os_ref[q, pl.ds(i, SUBLANES, stride=seg), :] = y[:, lanes(q)]
        return carry

    lax.fori_loop(0, seg, out_body, 0, unroll=LRU_UNROLL)
    o_ref[0] = jnp.concatenate([os_ref[q] for q in range(nq)], axis=-1).astype(BF16)


def _lru(x_lru, g_lru, conv_w, conv_b, w_ri, b_ri, neg_sp, layer, bsz):
    n = x_lru.shape[0]
    c = LRU_WIDTH
    x3 = x_lru.reshape(bsz, T_PAD, c)
    g3 = g_lru.reshape(bsz, T_PAD, c)
    lay = lambda b, t: (layer, 0, 0)
    out = pl.pallas_call(
        _lru_kernel,
        grid=(bsz, T_PAD // LRU_CHUNK),
        in_specs=[
            pl.BlockSpec((1, LRU_CHUNK, c), lambda b, t: (b, t, 0)),
            pl.BlockSpec((1, LRU_CHUNK, c), lambda b, t: (b, t, 0)),
            _const_spec((1, CONV_WIDTH, c), lay),
            _const_spec((1, 1, c), lay),
            _const_spec((1, c, 2 * c), lay),
            _const_spec((1, 1, 2 * c), lay),
            _const_spec((1, 1, c), lay),
        ],
        out_specs=pl.BlockSpec((1, LRU_CHUNK, c), lambda b, t: (b, t, 0)),
        out_shape=jax.ShapeDtypeStruct((bsz, T_PAD, c), BF16),
        scratch_shapes=[
            pltpu.VMEM((c // LANES, LRU_CHUNK + SUBLANES, LANES), F32),
            pltpu.VMEM((c // LANES, LRU_CHUNK, LANES), F32),
            pltpu.VMEM((LRU_CHUNK, c), F32),
            pltpu.VMEM((LRU_CHUNK, c), F32),
            pltpu.VMEM((LRU_CHUNK, 2 * c), F32),
            pltpu.VMEM((LRU_CHUNK, c), F32),
            pltpu.VMEM((LRU_CHUNK, c), F32),
            pltpu.VMEM((c // LANES, LRU_CHUNK, LANES), F32),
            pltpu.VMEM((1, c), F32),
        ],
        compiler_params=pltpu.CompilerParams(
            dimension_semantics=("arbitrary", "arbitrary"), vmem_limit_bytes=VMEM_LIMIT),
        name="rglru",
    )(x3, g3, conv_w, conv_b, w_ri, b_ri, neg_sp)
    return out.reshape(n, c)


def _merge_kernel(hs_ref, ys_ref, yl_ref, gt_ref, wglu_ref, bglu_ref, wsp_ref, wlp_ref, wout_ref, g_ref,
                  *rest, with_router):
    if with_router:
        rw_ref, rb_ref, hs_out_ref, hn_ref, rt_ref, rtt_ref, cnt_ref, run_ref, tri_ref, *w_bf16 = rest
    else:
        hs_out_ref, hn_ref, *w_bf16 = rest
    wglu_b, wsp_b, wlp_b, wout_b = w_bf16

    @pl.when(pl.program_id(0) == 0)
    def _():
        for w_f32, w_b in zip((wglu_ref, wsp_ref, wlp_ref, wout_ref), w_bf16):
            w_b[...] = w_f32[0].astype(BF16)

    ys = jnp.concatenate([ys_ref[q] for q in range(S5_PARTS)], axis=-1)
    glu = ys * jax.nn.sigmoid(_dot(ys.astype(BF16), wglu_b[...]) + bglu_ref[0])
    y_a = _dot(glu.astype(BF16), wsp_b[...])
    y_b = _dot(yl_ref[...], wlp_b[...])
    y = gt_ref[:, :D_MODEL].astype(F32) * y_a + gt_ref[:, D_MODEL:].astype(F32) * y_b
    hs = hs_ref[...] + _dot(y.astype(BF16), wout_b[...])
    hs_out_ref[...] = hs
    hn = _rms(hs, g_ref[0])
    if not with_router:
        hn_ref[...] = hn.astype(BF16)
    else:
        _rows_to_tiles(hn_ref, hn)
        logits = _dot(hn.astype(BF16), rw_ref[0].astype(BF16)) + rb_ref[0]
        lane = lax.broadcasted_iota(jnp.int32, logits.shape, 1).astype(F32)
        m1 = jnp.max(logits, axis=-1, keepdims=True)
        i1 = jnp.min(jnp.where(logits == m1, lane, float(LANES)), axis=-1, keepdims=True)
        rest_l = jnp.where(lane == i1, MASKED_LOGIT, logits)
        m2 = jnp.max(rest_l, axis=-1, keepdims=True)
        i2 = jnp.min(jnp.where(rest_l == m2, lane, float(LANES)), axis=-1, keepdims=True)
        e2 = jnp.exp(m2 - m1)
        g1 = 1.0 / (1.0 + e2)
        g2 = e2 / (1.0 + e2)
        @pl.when(pl.program_id(0) == 0)
        def _():
            run_ref[...] = jnp.zeros_like(run_ref)
            r, c = _iota2(tri_ref.shape)
            tri_ref[...] = jnp.where(c < r, 1.0, 0.0).astype(BF16)

        first = lane == i1
        second = lane == i2
        picked = jnp.where(first | second, 1.0, 0.0)
        before = _dot(tri_ref[...], picked.astype(BF16)) + run_ref[...]
        rank1 = jnp.sum(jnp.where(first, before, 0.0), axis=-1, keepdims=True)
        rank2 = jnp.sum(jnp.where(second, before, 0.0), axis=-1, keepdims=True)
        run_ref[...] += jnp.sum(picked, axis=0, keepdims=True)
        cnt_ref[...] = jnp.broadcast_to(run_ref[...], cnt_ref.shape)
        rt = (jnp.where(lane == 0.0, i1, 0.0) + jnp.where(lane == 1.0, i2, 0.0)
              + jnp.where(lane == 2.0, g1, 0.0) + jnp.where(lane == 3.0, g2, 0.0)
              + jnp.where(lane == 4.0, rank1, 0.0) + jnp.where(lane == 5.0, rank2, 0.0))
        rt_ref[...] = rt
        r, c = _iota2((SUBLANES, LANES))
        pick = jnp.where(r == c, 1.0, 0.0).astype(BF16)
        hi = rt.astype(BF16)
        mid = (rt - hi.astype(F32)).astype(BF16)
        lo = (rt - hi.astype(F32) - mid.astype(F32)).astype(BF16)
        nt = (((1,), (1,)), ((), ()))
        rtt_ref[0] = (lax.dot_general(pick, hi, nt, preferred_element_type=F32)
                      + lax.dot_general(pick, mid, nt, preferred_element_type=F32)
                      + lax.dot_general(pick, lo, nt, preferred_element_type=F32))


def _merge(hs, ys_parts, y_lru, gates, w_glu, b_glu, w_sp, w_lp, w_out, ffn_norm, layer, router=None):
    n = hs.shape[0]
    tm = TM
    lay = lambda i: (layer, 0, 0)
    in_specs = [
        pl.BlockSpec((tm, D_MODEL), lambda i: (i, 0)),
        pl.BlockSpec((S5_PARTS, tm, PART_W), lambda i: (0, i, 0)),
        pl.BlockSpec((tm, LRU_WIDTH), lambda i: (i, 0)),
        pl.BlockSpec((tm, 2 * D_MODEL), lambda i: (i, 0)),
        _const_spec((1, S5_WIDTH, S5_WIDTH), lay),
        _const_spec((1, 1, S5_WIDTH), lay),
        _const_spec((1, S5_WIDTH, D_MODEL), lay),
        _const_spec((1, LRU_WIDTH, D_MODEL), lay),
        _const_spec((1, D_MODEL, D_MODEL), lay),
        _const_spec((1, 1, D_MODEL), lay),
    ]
    out_specs = [pl.BlockSpec((tm, D_MODEL), lambda i: (i, 0))]
    out_shape = [jax.ShapeDtypeStruct((n, D_MODEL), F32)]
    if router is None:
        out_specs.append(pl.BlockSpec((tm, D_MODEL), lambda i: (i, 0)))
        out_shape.append(jax.ShapeDtypeStruct((n, D_MODEL), BF16))
    else:
        out_specs.append(pl.BlockSpec((tm * ROW_TILES, LANES), lambda i: (i, 0)))
        out_shape.append(jax.ShapeDtypeStruct((n * ROW_TILES, LANES), F32))
    args = [hs, ys_parts, y_lru, gates, w_glu, b_glu, w_sp, w_lp, w_out, ffn_norm]
    if router is not None:
        rw, rb, j = router
        in_specs += [_const_spec((1, D_MODEL, LANES), lambda i: (j, 0, 0)),
                     _const_spec((1, 1, LANES), lambda i: (j, 0, 0))]
        out_specs += [pl.BlockSpec((tm, LANES), lambda i: (i, 0)),
                      pl.BlockSpec((1, SUBLANES, tm), lambda i: (i, 0, 0)),
                      pl.BlockSpec((SUBLANES, LANES), lambda i: (0, 0))]
        out_shape += [jax.ShapeDtypeStruct((n, LANES), F32),
                      jax.ShapeDtypeStruct((n // tm, SUBLANES, tm), F32),
                      jax.ShapeDtypeStruct((SUBLANES, LANES), F32)]
        args += [rw, rb]
    return pl.pallas_call(
        functools.partial(_merge_kernel, with_router=router is not None),
        grid=(n // tm,),
        in_specs=in_specs,
        out_specs=out_specs,
        out_shape=out_shape,
        scratch_shapes=([pltpu.VMEM((1, LANES), F32), pltpu.VMEM((tm, tm), BF16)] if router is not None else [])
        + [pltpu.VMEM(w.shape[1:], BF16) for w in (w_glu, w_sp, w_lp, w_out)],
        compiler_params=pltpu.CompilerParams(
            dimension_semantics=("arbitrary",), vmem_limit_bytes=VMEM_LIMIT),
        name="merge_router" if router is not None else "merge",
    )(*args)


def _ffn_kernel(x_ref, hs_ref, wg_ref, wu_ref, wd_ref, o_ref, h_ref):
    x = x_ref[...]
    for c0 in range(0, h_ref.shape[1], FF_CHUNK):
        g = _dot(x, wg_ref[0, :, c0:c0 + FF_CHUNK])
        u = _dot(x, wu_ref[0, :, c0:c0 + FF_CHUNK])
        h_ref[:, c0:c0 + FF_CHUNK] = (g * jax.nn.sigmoid(g) * u).astype(BF16)
    o_ref[...] = hs_ref[...] + _dot(h_ref[...], wd_ref[0])


def _ffn(hn, hs, w_gate, w_up, w_down, layer):
    n = hn.shape[0]
    ff = w_gate.shape[-1]
    lay = lambda i: (layer, 0, 0)
    return pl.pallas_call(
        _ffn_kernel,
        grid=(n // TM_FFN,),
        in_specs=[
            pl.BlockSpec((TM_FFN, D_MODEL), lambda i: (i, 0)),
            pl.BlockSpec((TM_FFN, D_MODEL), lambda i: (i, 0)),
            _const_spec((1, D_MODEL, ff), lay),
            _const_spec((1, D_MODEL, ff), lay),
            _const_spec((1, ff, D_MODEL), lay),
        ],
        out_specs=pl.BlockSpec((TM_FFN, D_MODEL), lambda i: (i, 0)),
        out_shape=jax.ShapeDtypeStruct((n, D_MODEL), F32),
        scratch_shapes=[pltpu.VMEM((TM_FFN, ff), BF16)],
        compiler_params=pltpu.CompilerParams(
            dimension_semantics=("arbitrary",), vmem_limit_bytes=VMEM_LIMIT),
        name="dense_ffn",
    )(hn, hs, w_gate, w_up, w_down)


def _moe_plan(route_t, counts_f, n):
    n_blocks = -(-2 * n // MOE_BLOCK) + N_EXPERTS
    e = jnp.stack([route_t[:, 0, :], route_t[:, 1, :]]).astype(jnp.int32)
    rank = jnp.stack([route_t[:, 4, :], route_t[:, 5, :]]).astype(jnp.int32)
    counts = counts_f[0, :N_EXPERTS].astype(jnp.int32)
    padded = ((counts + MOE_BLOCK - 1) // MOE_BLOCK) * MOE_BLOCK
    cum_pad = jnp.cumsum(padded)
    pad_start = cum_pad - padded
    pos = rank
    for x in range(N_EXPERTS):
        pos = pos + jnp.where(e == x, pad_start[x], 0)
    block_start = jnp.arange(n_blocks, dtype=jnp.int32) * MOE_BLOCK
    block_expert = jnp.minimum(jnp.sum((block_start[:, None] >= cum_pad[None, :]).astype(jnp.int32), axis=1),
                               N_EXPERTS - 1)
    n_used = (cum_pad[-1] // MOE_BLOCK).astype(jnp.int32).reshape(1)
    pad_range = jnp.stack([pad_start + counts, cum_pad], axis=1).reshape(2 * N_EXPERTS).astype(jnp.int32)
    ids = jnp.arange(N_EXPERTS, dtype=jnp.int32)
    used = counts > 0
    run_buffer = (jnp.cumsum(used.astype(jnp.int32)) - 1) % 2
    later = jnp.where((ids[None, :] > ids[:, None]) & used[None, :], ids[None, :], N_EXPERTS)
    next_used = jnp.min(later, axis=1)
    next_used = jnp.where(next_used == N_EXPERTS, -1, next_used)
    of_block = lambda table: jnp.sum(jnp.where(block_expert[:, None] == ids[None, :], table[None, :], 0), axis=1)
    weight_plan = jnp.stack([block_expert,
                             (block_start == of_block(pad_start)).astype(jnp.int32),
                             of_block(run_buffer), of_block(next_used)]).astype(jnp.int32)
    return pos.reshape(2 * n), weight_plan, n_used, pad_range


def _tile(ref, index):
    return ref.at[pl.ds(pl.multiple_of(index * ROW_TILES, ROW_TILES), ROW_TILES)]


def _tile_gather(src_hbm, dst, sem, rows, index_of):
    def body(grp, carry):
        r0 = grp * GATHER_GROUP
        index = [index_of(r0 + j) for j in range(GATHER_GROUP)]
        for j in range(GATHER_GROUP):
            pltpu.make_async_copy(_tile(src_hbm, index[j]), _tile(dst, r0 + j), sem).start(priority=j % 2)
        return carry
    lax.fori_loop(0, rows // GATHER_GROUP, body, 0)


def _tile_gather_wait(src_hbm, dst, sem, rows):
    pltpu.make_async_copy(src_hbm.at[pl.ds(0, rows * ROW_TILES)], dst, sem).wait()


def _moe_dispatch_kernel(pos_ref, pad_ref, nu_ref, x_hbm, xs_hbm, xbuf, zero_ref, sem_in, sem_out, sem_fill):
    t = pl.program_id(0)
    nt = pl.num_programs(0)
    n = nt * TM
    tile_rows = TM * ROW_TILES
    block_rows = MOE_BLOCK * ROW_TILES
    n_blocks = xs_hbm.shape[0] // block_rows

    def read(tile):
        buf = tile % DISPATCH_RING
        return pltpu.make_async_copy(x_hbm.at[pl.ds(pl.multiple_of(tile * tile_rows, tile_rows), tile_rows)],
                                     xbuf.at[buf], sem_in.at[buf])

    def scatter_wait(tile):
        buf = tile % DISPATCH_RING
        for k in range(2):
            pltpu.make_async_copy(xbuf.at[buf], xs_hbm.at[pl.ds(0, tile_rows)], sem_out.at[buf]).wait()

    @pl.when(t == 0)
    def _():
        read(0).start()
        zero_ref[...] = jnp.zeros_like(zero_ref)
        zero_tile = zero_ref.at[pl.ds(0, ROW_TILES)]
        for e in range(N_EXPERTS):
            def fill(slot, carry):
                pltpu.make_async_copy(zero_tile, _tile(xs_hbm, slot), sem_fill.at[0]).start()
                return carry

            def fill_wait(slot, carry):
                pltpu.make_async_copy(zero_tile, _tile(xs_hbm, slot), sem_fill.at[0]).wait()
                return carry
            lax.fori_loop(pad_ref[2 * e], pad_ref[2 * e + 1], fill, 0)
            lax.fori_loop(pad_ref[2 * e], pad_ref[2 * e + 1], fill_wait, 0)

        def block_of(blk):
            return xs_hbm.at[pl.ds(pl.multiple_of(blk * block_rows, block_rows), block_rows)]

        def fill_block(blk, carry):
            pltpu.make_async_copy(zero_ref, block_of(blk), sem_fill.at[0]).start()
            return carry

        def fill_block_wait(blk, carry):
            pltpu.make_async_copy(zero_ref, block_of(blk), sem_fill.at[0]).wait()
            return carry
        lax.fori_loop(nu_ref[0], n_blocks, fill_block, 0)
        lax.fori_loop(nu_ref[0], n_blocks, fill_block_wait, 0)

    @pl.when(t + 1 < nt)
    def _():
        @pl.when(t + 1 >= DISPATCH_RING)
        def _():
            scatter_wait(t + 1 - DISPATCH_RING)
        read(t + 1).start()

    read(t).wait()
    src = xbuf.at[t % DISPATCH_RING]
    for k in range(2):
        def put(grp, carry):
            r0 = grp * GATHER_GROUP
            slot = [pos_ref[k * n + t * TM + r0 + j] for j in range(GATHER_GROUP)]
            for j in range(GATHER_GROUP):
                pltpu.make_async_copy(_tile(src, r0 + j), _tile(xs_hbm, slot[j]),
                                      sem_out.at[t % DISPATCH_RING]).start(priority=j % 2)
            return carry
        lax.fori_loop(0, TM // GATHER_GROUP, put, 0)

    @pl.when(t == nt - 1)
    def _():
        for back in range(DISPATCH_RING - 1, -1, -1):
            @pl.when(t - back >= 0)
            def _():
                scatter_wait(t - back)


def _moe_dispatch(hn_tiles, pos, pad_range, n_used, n_slots):
    n = hn_tiles.shape[0] // ROW_TILES
    return pl.pallas_call(
        _moe_dispatch_kernel,
        grid_spec=pltpu.PrefetchScalarGridSpec(
            num_scalar_prefetch=3,
            grid=(n // TM,),
            in_specs=[pl.BlockSpec(memory_space=pl.ANY)],
            out_specs=pl.BlockSpec(memory_space=pl.ANY),
            scratch_shapes=[pltpu.VMEM((DISPATCH_RING, TM * ROW_TILES, LANES), F32),
                            pltpu.VMEM((MOE_BLOCK * ROW_TILES, LANES), F32),
                            pltpu.SemaphoreType.DMA((DISPATCH_RING,)),
                            pltpu.SemaphoreType.DMA((DISPATCH_RING,)),
                            pltpu.SemaphoreType.DMA((1,))],
        ),
        out_shape=jax.ShapeDtypeStruct((n_slots * ROW_TILES, LANES), F32),
        compiler_params=pltpu.CompilerParams(dimension_semantics=("arbitrary",)),
        name="moe_dispatch",
    )(pos, pad_range, n_used, hn_tiles)


def _moe_ffn_kernel(be_ref, first_ref, buf_ref, next_ref, nu_ref, x_ref, wg_hbm, wu_hbm, wd_hbm, y_ref,
                    wg_buf, wu_buf, wd_buf, sem, *, first_expert):
    i = pl.program_id(0)
    sources = (wg_hbm, wu_hbm, wd_hbm)
    buffers = (wg_buf, wu_buf, wd_buf)

    def copies(expert, b):
        return [pltpu.make_async_copy(src.at[first_expert + expert], dst.at[b], sem.at[b, k])
                for k, (src, dst) in enumerate(zip(sources, buffers))]

    @pl.when(i < nu_ref[0])
    def _():
        b = buf_ref[i]

        @pl.when(first_ref[i] == 1)
        def _():
            @pl.when(i == 0)
            def _():
                for c in copies(be_ref[i], b):
                    c.start()
            for c in copies(be_ref[i], b):
                c.wait()

            @pl.when(next_ref[i] >= 0)
            def _():
                for c in copies(next_ref[i], 1 - b):
                    c.start()

        x = _rows_from_tiles(x_ref, MOE_BLOCK).astype(BF16)
        g = _dot(x, wg_buf[b].astype(BF16))
        h = g * jax.nn.sigmoid(g) * _dot(x, wu_buf[b].astype(BF16))
        _rows_to_tiles(y_ref, _dot(h.astype(BF16), wd_buf[b].astype(BF16)))

    @pl.when(i >= nu_ref[0])
    def _():
        y_ref[...] = jnp.zeros_like(y_ref)


def _moe_ffn(xs_tiles, weight_plan, n_used, w_gate, w_up, w_down, first):
    n_blocks = weight_plan.shape[1]
    ff = w_gate.shape[-1]
    return pl.pallas_call(
        functools.partial(_moe_ffn_kernel, first_expert=first),
        grid_spec=pltpu.PrefetchScalarGridSpec(
            num_scalar_prefetch=5,
            grid=(n_blocks,),
            in_specs=[
                pl.BlockSpec((MOE_BLOCK * ROW_TILES, LANES),
                             lambda i, be, fi, bu, nx, nu: (jnp.maximum(jnp.minimum(i, nu[0] - 1), 0), 0)),
                pl.BlockSpec(memory_space=pl.ANY),
                pl.BlockSpec(memory_space=pl.ANY),
                pl.BlockSpec(memory_space=pl.ANY),
            ],
            out_specs=pl.BlockSpec((MOE_BLOCK * ROW_TILES, LANES), lambda i, be, fi, bu, nx, nu: (i, 0)),
            scratch_shapes=[pltpu.VMEM((2, D_MODEL, ff), F32),
                            pltpu.VMEM((2, D_MODEL, ff), F32),
                            pltpu.VMEM((2, ff, D_MODEL), F32),
                            pltpu.SemaphoreType.DMA((2, 3))],
        ),
        out_shape=jax.ShapeDtypeStruct((n_blocks * MOE_BLOCK * ROW_TILES, LANES), F32),
        compiler_params=pltpu.CompilerParams(
            dimension_semantics=("arbitrary",), vmem_limit_bytes=VMEM_LIMIT),
        name="moe_ffn",
    )(weight_plan[0], weight_plan[1], weight_plan[2], weight_plan[3], n_used, xs_tiles, w_gate, w_up, w_down)


def _moe_combine_kernel(pos_ref, hs_ref, rt_ref, ys_hbm, o_ref, ybuf, sem):
    i = pl.program_id(0)
    nt = pl.num_programs(0)
    slot = i % 2

    def start(t, s):
        for k in range(2):
            _tile_gather(ys_hbm, ybuf.at[s, k], sem.at[s], TM, lambda r: pos_ref[k * (nt * TM) + t * TM + r])

    @pl.when(i == 0)
    def _():
        start(0, 0)

    @pl.when(i + 1 < nt)
    def _():
        start(i + 1, 1 - slot)

    for k in range(2):
        _tile_gather_wait(ys_hbm, ybuf.at[slot, k], sem.at[slot], TM)
    rt = rt_ref[...]
    lane = lax.broadcasted_iota(jnp.int32, rt.shape, 1)
    g1 = jnp.sum(jnp.where(lane == 2, rt, 0.0), axis=-1, keepdims=True)
    g2 = jnp.sum(jnp.where(lane == 3, rt, 0.0), axis=-1, keepdims=True)
    o_ref[...] = (hs_ref[...] + g1 * _rows_from_tiles(ybuf.at[slot, 0], TM)
                  + g2 * _rows_from_tiles(ybuf.at[slot, 1], TM))


def _moe_combine(hs, route, ys_tiles, pos):
    n = hs.shape[0]
    return pl.pallas_call(
        _moe_combine_kernel,
        grid_spec=pltpu.PrefetchScalarGridSpec(
            num_scalar_prefetch=1,
            grid=(n // TM,),
            in_specs=[
                pl.BlockSpec((TM, D_MODEL), lambda i, p: (i, 0)),
                pl.BlockSpec((TM, LANES), lambda i, p: (i, 0)),
                pl.BlockSpec(memory_space=pl.ANY),
            ],
            out_specs=pl.BlockSpec((TM, D_MODEL), lambda i, p: (i, 0)),
            scratch_shapes=[pltpu.VMEM((2, 2, TM * ROW_TILES, LANES), F32),
                            pltpu.SemaphoreType.DMA((2,))],
        ),
        out_shape=jax.ShapeDtypeStruct((n, D_MODEL), F32),
        compiler_params=pltpu.CompilerParams(
            dimension_semantics=("arbitrary",), vmem_limit_bytes=VMEM_LIMIT),
        name="moe_combine",
    )(pos, hs, route, ys_tiles)


def _final_kernel(a_ref, b_ref, g_ref, o_ref):
    tb = a_ref.shape[1]
    o_ref[0, :tb - N_META] = _rms(a_ref[0, N_META:], g_ref[...])
    o_ref[0, tb - N_META:] = _rms(b_ref[0], g_ref[...])


def _final_norm(hs, g, bsz, seq):
    hs3 = hs.reshape(bsz, T_PAD, D_MODEL)
    return pl.pallas_call(
        _final_kernel,
        grid=(bsz, seq // TB_FINAL),
        in_specs=[pl.BlockSpec((1, TB_FINAL, D_MODEL), lambda b, i: (b, i, 0)),
                  pl.BlockSpec((1, N_META, D_MODEL), lambda b, i: (b, (i + 1) * (TB_FINAL // N_META), 0)),
                  _const_spec((1, D_MODEL), lambda b, i: (0, 0))],
        out_specs=pl.BlockSpec((1, TB_FINAL, D_MODEL), lambda b, i: (b, i, 0)),
        out_shape=jax.ShapeDtypeStruct((bsz, seq, D_MODEL), F32),
        compiler_params=pltpu.CompilerParams(
            dimension_semantics=("arbitrary", "arbitrary"), vmem_limit_bytes=VMEM_LIMIT),
        name="final_norm",
    )(hs3, hs3, g)


def _head_blockdiag(w):
    eye = jnp.eye(LRU_HEADS, dtype=w.dtype)
    out = jnp.einsum('lnhk,nm->lnhmk', w, eye)
    return out.reshape(w.shape[0], LRU_WIDTH, LRU_WIDTH)


def kernel(x, meta_tokens, mix_norm, w_in, merge_bias, s5_lambda_re, s5_lambda_im, s5_log_dt, s5_b_re, s5_b_im, s5_c_re, s5_c_im, s5_d, s5_w_glu, s5_b_glu, s5_w_proj, lru_conv_w, lru_conv_b, lru_w_rgate, lru_b_rgate, lru_w_igate, lru_b_igate, lru_lambda, lru_w_proj, w_out, ffn_norm, dense_w_gate, dense_w_up, dense_w_down, router_w, router_b, moe_w_gate, moe_w_up, moe_w_down, final_norm):
    bsz, seq, d = x.shape
    depth = w_in.shape[0]
    assert d == D_MODEL and N_META + seq <= T_PAD
    n = bsz * T_PAD
    assert n % TM == 0 and n % TM_FFN == 0 and seq % TB_FINAL == 0 and TB_FINAL % N_META == 0

    meta = jnp.broadcast_to(meta_tokens[None].astype(x.dtype), (bsz, N_META, d))
    pad = jnp.zeros((bsz, T_PAD - N_META - seq, d), x.dtype)
    hs = jnp.concatenate([meta, x, pad], axis=1).reshape(n, d)

    row3 = lambda a: a[:, None, :]
    w_ri = jnp.concatenate([_head_blockdiag(lru_w_rgate), _head_blockdiag(lru_w_igate)], axis=-1).astype(BF16)
    b_ri = jnp.concatenate([lru_b_rgate, lru_b_igate], axis=-1)
    neg_sp = -LRU_C * jax.nn.softplus(-lru_lambda) * LOG2_E
    dense = [w.astype(BF16) for w in (dense_w_gate, dense_w_up, dense_w_down)]
    n_moe = router_w.shape[0]
    moe = [w.reshape((n_moe * N_EXPERTS,) + w.shape[2:]) for w in (moe_w_gate, moe_w_up, moe_w_down)]
    s5_ops = jax.vmap(_s5_prep)(s5_lambda_re, s5_lambda_im, s5_log_dt, s5_b_re, s5_b_im, s5_c_re, s5_c_im, s5_d)
    rw_pad = jnp.pad(router_w, ((0, 0), (0, 0), (0, LANES - N_EXPERTS)))
    rb_pad = jnp.pad(router_b, ((0, 0), (0, LANES - N_EXPERTS)), constant_values=MASKED_LOGIT)

    for layer in range(depth):
        u_parts, x_lru, g_lru, gates = _in_proj(hs, row3(mix_norm), w_in, row3(merge_bias), layer)
        ys_parts = _s5_scan(u_parts, s5_ops, layer, bsz)
        y_lru = _lru(x_lru, g_lru, lru_conv_w, row3(lru_conv_b), w_ri, row3(b_ri), row3(neg_sp), layer, bsz)
        j = layer // 2
        router = (rw_pad, row3(rb_pad), j) if layer % 2 == 1 else None
        res = _merge(hs, ys_parts, y_lru, gates, s5_w_glu, row3(s5_b_glu), s5_w_proj, lru_w_proj, w_out,
                     row3(ffn_norm), layer, router)
        if layer % 2 == 0:
            hs, hn = res
            hs = _ffn(hn, hs, *dense, layer=j)
        else:
            hs, hn, route, route_t, counts = res
            pos, weight_plan, n_used, pad_range = _moe_plan(route_t, counts, n)
            xs = _moe_dispatch(hn, pos, pad_range, n_used, weight_plan.shape[1] * MOE_BLOCK)
            ys = _moe_ffn(xs, weight_plan, n_used, *moe, first=j * N_EXPERTS)
            hs = _moe_combine(hs, route, ys, pos)

    return _final_norm(hs, final_norm[None, :], bsz, seq)
```

```python
import functools

import jax
import jax.numpy as jnp
from jax import lax
from jax.experimental import pallas as pl
from jax.experimental.pallas import tpu as pltpu

F32 = jnp.float32
BF16 = jnp.bfloat16

D_MODEL = 1024
N_META = 16
S5_WIDTH = 512
S5_GROUP = 16
S5_GROUPS = 32
S5_STATE = 64
LRU_WIDTH = 512
LRU_HEADS = 8
CONV_WIDTH = 4
LRU_C = 8.0
N_EXPERTS = 8
EPS = 1e-6
LOG2_E = 1.4426950408889634

FOLD = 8
S5_PARTS = 4
PART_W = S5_WIDTH // S5_PARTS
PART_GROUPS = PART_W // S5_GROUP
PART_STATE = PART_GROUPS * S5_STATE
FOLD_W = FOLD * PART_W

T_PAD = 8256
ROWS = T_PAD // FOLD
TM = 688
TM_FFN = 688
FF_CHUNK = 1024
MOE_BLOCK = 512
GATHER_GROUP = 8
DISPATCH_RING = 3
LRU_CHUNK = 1032
LRU_UNROLL = 3
LRU_SCAN_UNROLL = 43
TB_FINAL = 2048
VMEM_LIMIT = 56 * 1024 * 1024
LANES = 128
SUBLANES = 8
S5_SEG = ROWS // SUBLANES
S5_UNROLL = 3
ROW_TILES = D_MODEL // LANES
MASKED_LOGIT = float("-inf")


def _dot(a, b):
    return jnp.dot(a, b, preferred_element_type=F32)


def _const_spec(block_shape, index_map):
    return pl.BlockSpec(block_shape, index_map, pipeline_mode=pl.Buffered(1))


def _sigmoid(x):
    return 0.5 * jnp.tanh(0.5 * x) + 0.5


def _rms(x, g):
    ms = jnp.mean(x * x, axis=-1, keepdims=True)
    return x * lax.rsqrt(ms + EPS) * g


def _rows_to_tiles(ref, x):
    rows = x.shape[0]
    for s in range(ROW_TILES):
        ref[pl.ds(s, rows, stride=ROW_TILES), :] = x[:, s * LANES:(s + 1) * LANES]


def _rows_from_tiles(ref, rows):
    return jnp.concatenate([ref[pl.ds(s, rows, stride=ROW_TILES), :] for s in range(ROW_TILES)], axis=-1)


def _in_proj_kernel(hs_ref, g_ref, wf_ref, mb_ref, u_ref, xl_ref, gl_ref, gt_ref, w_ref):
    @pl.when(pl.program_id(0) == 0)
    def _():
        w_ref[0] = wf_ref[0].astype(BF16)

    hn = _rms(hs_ref[...], g_ref[0]).astype(BF16)
    u = _dot(hn, w_ref[0, :, 0:S5_WIDTH])
    for q in range(S5_PARTS):
        u_ref[q] = u[:, q * PART_W:(q + 1) * PART_W]
    o_x = S5_WIDTH
    o_g = o_x + LRU_WIDTH
    o_m = o_g + LRU_WIDTH
    xl_ref[...] = _dot(hn, w_ref[0, :, o_x:o_g]).astype(BF16)
    gl_ref[...] = _dot(hn, w_ref[0, :, o_g:o_m]).astype(BF16)
    z = _dot(hn, w_ref[0, :, o_m:]) + mb_ref[0]
    gt_ref[...] = jax.nn.sigmoid(z).astype(BF16)


def _in_proj(hs, mix_norm, w_in, merge_bias, layer):
    n = hs.shape[0]
    d_in = w_in.shape[-1]
    lay = lambda i: (layer, 0, 0)
    return pl.pallas_call(
        _in_proj_kernel,
        grid=(n // TM,),
        in_specs=[
            pl.BlockSpec((TM, D_MODEL), lambda i: (i, 0)),
            _const_spec((1, 1, D_MODEL), lay),
            _const_spec((1, D_MODEL, d_in), lay),
            _const_spec((1, 1, 2 * D_MODEL), lay),
        ],
        out_specs=[
            pl.BlockSpec((S5_PARTS, TM, PART_W), lambda i: (0, i, 0)),
            pl.BlockSpec((TM, LRU_WIDTH), lambda i: (i, 0)),
            pl.BlockSpec((TM, LRU_WIDTH), lambda i: (i, 0)),
            pl.BlockSpec((TM, 2 * D_MODEL), lambda i: (i, 0)),
        ],
        out_shape=[
            jax.ShapeDtypeStruct((S5_PARTS, n, PART_W), F32),
            jax.ShapeDtypeStruct((n, LRU_WIDTH), BF16),
            jax.ShapeDtypeStruct((n, LRU_WIDTH), BF16),
            jax.ShapeDtypeStruct((n, 2 * D_MODEL), BF16),
        ],
        scratch_shapes=[pltpu.VMEM((1, D_MODEL, d_in), BF16)],
        compiler_params=pltpu.CompilerParams(
            dimension_semantics=("arbitrary",), vmem_limit_bytes=VMEM_LIMIT),
        name="in_proj",
    )(hs, mix_norm, w_in, merge_bias)


def _s5_prep(lam_re, lam_im, log_dt, b_re, b_im, c_re, c_im, d_skip):
    dt = jnp.exp(log_dt)[:, None]
    mag = jnp.exp(lam_re * dt)
    a_re = mag * jnp.cos(lam_im * dt)
    a_im = mag * jnp.sin(lam_im * dt)
    den = lam_re * lam_re + lam_im * lam_im
    num_re = a_re - 1.0
    coef_re = (num_re * lam_re + a_im * lam_im) / den
    coef_im = (a_im * lam_re - num_re * lam_im) / den
    bb_re = coef_re[..., None] * b_re - coef_im[..., None] * b_im
    bb_im = coef_re[..., None] * b_im + coef_im[..., None] * b_re

    def cmul(xr, xi, yr, yi):
        return xr * yr - xi * yi, xr * yi + xi * yr

    def powers(br, bi, n):
        pr, pi = [jnp.ones_like(br)], [jnp.zeros_like(bi)]
        for _ in range(n):
            r, i = cmul(pr[-1], pi[-1], br, bi)
            pr.append(r)
            pi.append(i)
        return jnp.stack(pr), jnp.stack(pi)

    p_re, p_im = powers(a_re, a_im, FOLD)

    def per_part(x):
        lead = x.shape[:-3]
        xp = x.reshape(lead + (S5_PARTS, PART_GROUPS) + x.shape[-2:])
        return jnp.moveaxis(xp, len(lead), 0)

    rev_re = jnp.stack([p_re[FOLD - 1 - j] for j in range(FOLD)])
    rev_im = jnp.stack([p_im[FOLD - 1 - j] for j in range(FOLD)])
    wr, wi = cmul(rev_re[..., None], rev_im[..., None], bb_re[None], bb_im[None])
    w_ri = jnp.swapaxes(jnp.stack([wr, wi], axis=1), -1, -2)
    xq = jnp.transpose(per_part(w_ri), (0, 2, 1, 3, 4, 5)).reshape(S5_PARTS, 2, FOLD_W, S5_STATE)

    ca_re, ca_im = cmul(c_re[None], c_im[None], p_re[:, :, None, :], p_im[:, :, None, :])
    bt_re = jnp.swapaxes(bb_re, -1, -2)[None, :, :, None, :]
    bt_im = jnp.swapaxes(bb_im, -1, -2)[None, :, :, None, :]
    taps = jnp.sum(ca_re[:FOLD, :, None] * bt_re - ca_im[:FOLD, :, None] * bt_im, axis=-1)
    skip = d_skip.reshape(S5_GROUPS, S5_GROUP)
    taps = taps.at[0].add(skip[:, :, None] * jnp.eye(S5_GROUP, dtype=F32)[None])
    rc = jnp.transpose(per_part(taps), (0, 2, 3, 1, 4)).reshape(S5_PARTS, PART_W, FOLD * S5_GROUP)

    v_ri = jnp.swapaxes(jnp.stack([ca_re[1:], -ca_im[1:]], axis=0), -1, -2)
    vc = jnp.transpose(per_part(v_ri), (0, 1, 3, 4, 2, 5)).reshape(S5_PARTS, 2 * PART_STATE, FOLD * S5_GROUP)

    def part_vec(x):
        lead = x.shape[:-2]
        xp = x.reshape(lead + (S5_PARTS, PART_STATE))
        return jnp.moveaxis(xp, -2, 0)

    row_re, row_im = p_re[FOLD], p_im[FOLD]
    seg_re, seg_im = jnp.ones_like(row_re), jnp.zeros_like(row_im)
    for bit in bin(S5_SEG)[2:]:
        seg_re, seg_im = cmul(seg_re, seg_im, seg_re, seg_im)
        if bit == '1':
            seg_re, seg_im = cmul(seg_re, seg_im, row_re, row_im)
    decay = jnp.stack([jnp.stack([part_vec(row_re), part_vec(row_im)], axis=1),
                       jnp.stack([part_vec(seg_re), part_vec(seg_im)], axis=1)], axis=1)
    return xq, rc, vc, decay[:, :, :, None, :]


def _iota2(shape):
    return (lax.broadcasted_iota(jnp.int32, shape, 0), lax.broadcasted_iota(jnp.int32, shape, 1))


def _s5_expand(xq, rc, vc, w1_s, tv_s):
    ps = PART_STATE
    lg_state, lg_group, lg_part = (v.bit_length() - 1 for v in (S5_STATE, S5_GROUP, PART_W))
    grp = PART_GROUPS - 1
    one_hot = lambda m: jnp.where(m, 1.0, 0.0).astype(BF16)
    r, c = _iota2((S5_STATE, ps))
    e1 = one_hot(r == (c & (S5_STATE - 1)))
    r, c = _iota2((FOLD * S5_GROUP, FOLD_W))
    e2 = one_hot(((r >> lg_group) == (c >> lg_part)) & ((r & (S5_GROUP - 1)) == (c & (S5_GROUP - 1))))
    r, c = _iota2((FOLD_W, ps))
    m1 = ((r >> lg_group) & grp) == (c >> lg_state)
    for ri in range(2):
        w1_s[:, ri * ps:(ri + 1) * ps] = jnp.where(m1, _dot(xq[ri].astype(BF16), e1), 0.0).astype(BF16)
    r, c = _iota2((PART_W, FOLD_W))
    m2 = (r >> lg_group) == ((c >> lg_group) & grp)
    r0 = jnp.where(m2, _dot(rc.astype(BF16), e2), 0.0).astype(BF16)
    for j in range(FOLD):
        if j == 0:
            blk = r0
        else:
            blk = jnp.concatenate([jnp.zeros((PART_W, j * PART_W), BF16), r0[:, :FOLD_W - j * PART_W]], axis=1)
        tv_s[j * PART_W:(j + 1) * PART_W, :] = blk
    r, c = _iota2((2 * ps, FOLD_W))
    m3 = ((r >> lg_state) & grp) == ((c >> lg_group) & grp)
    tv_s[FOLD_W:, :] = jnp.where(m3, _dot(vc.astype(BF16), e2), 0.0).astype(BF16)


def _s5_kernel(u_ref, xq_ref, rc_ref, vc_ref, dec_ref, y_ref, w1_s, tv_s, up_ref, f_ref, hp_ref):
    ps = PART_STATE
    tstride = S5_SEG * FOLD
    cols = lambda j: slice(j * PART_W, (j + 1) * PART_W)

    @pl.when(pl.program_id(1) == 0)
    def _():
        _s5_expand(xq_ref[0, 0], rc_ref[0, 0], vc_ref[0, 0], w1_s, tv_s)

    def fold_body(i, carry):
        r0 = pl.multiple_of(i * SUBLANES, SUBLANES)
        for j in range(FOLD):
            up_ref[pl.ds(r0, SUBLANES), cols(j)] = u_ref[0, 0, pl.ds(i * FOLD + j, SUBLANES, stride=tstride), :]
        return carry

    lax.fori_loop(0, S5_SEG, fold_body, 0, unroll=S5_UNROLL)
    u = up_ref[...].astype(BF16)
    f_ref[...] = _dot(u, w1_s[...])
    ar = jnp.broadcast_to(dec_ref[0, 0, 0, 0], (SUBLANES, ps))
    ai = jnp.broadcast_to(dec_ref[0, 0, 0, 1], (SUBLANES, ps))

    def step(i, hr, hi):
        r0 = pl.multiple_of(i * SUBLANES, SUBLANES)
        return (ar * hr - ai * hi + f_ref[pl.ds(r0, SUBLANES), :ps],
                ar * hi + ai * hr + f_ref[pl.ds(r0, SUBLANES), ps:])

    zero = jnp.zeros((SUBLANES, ps), F32)
    er, ei = lax.fori_loop(0, S5_SEG, lambda i, c: step(i, *c), (zero, zero))
    sr = dec_ref[0, 0, 1, 0]
    si = dec_ref[0, 0, 1, 1]
    row = lax.broadcasted_iota(jnp.int32, (SUBLANES, ps), 0)
    nr, ni = zero, zero
    for sgm in range(SUBLANES - 1):
        lr = er + sr * nr - si * ni
        li = ei + sr * ni + si * nr
        nr = nr + jnp.where(row == sgm + 1, pltpu.roll(lr, 1, axis=0), 0.0)
        ni = ni + jnp.where(row == sgm + 1, pltpu.roll(li, 1, axis=0), 0.0)

    def state_body(i, carry):
        hr, hi = carry
        r0 = pl.multiple_of(i * SUBLANES, SUBLANES)
        hp_ref[pl.ds(r0, SUBLANES), :ps] = hr
        hp_ref[pl.ds(r0, SUBLANES), ps:] = hi
        return step(i, hr, hi)

    lax.fori_loop(0, S5_SEG, state_body, (nr, ni))
    hp = hp_ref[...].astype(BF16)
    wide = 2 * PART_W
    for c0 in range(0, FOLD_W, wide):
        y = _dot(u[:, :c0 + wide], tv_s[:c0 + wide, c0:c0 + wide]) + _dot(hp, tv_s[FOLD_W:, c0:c0 + wide])
        f_ref[:, c0:c0 + wide] = jax.nn.gelu(y)

    def unfold_body(i, carry):
        r0 = pl.multiple_of(i * SUBLANES, SUBLANES)
        for j in range(FOLD):
            y_ref[0, 0, pl.ds(i * FOLD + j, SUBLANES, stride=tstride), :] = f_ref[pl.ds(r0, SUBLANES), cols(j)]
        return carry

    lax.fori_loop(0, S5_SEG, unfold_body, 0, unroll=S5_UNROLL)


def _s5_scan(u_parts, ops, layer, bsz):
    xq, rc, vc, decay = ops
    n = u_parts.shape[1]
    u4 = u_parts.reshape(S5_PARTS, bsz, T_PAD, PART_W)
    lay4 = lambda q, b: (layer, q, 0, 0)
    y4 = pl.pallas_call(
        _s5_kernel,
        grid=(S5_PARTS, bsz),
        in_specs=[
            pl.BlockSpec((1, 1, T_PAD, PART_W), lambda q, b: (q, b, 0, 0)),
            pl.BlockSpec((1, 1, 2, FOLD_W, S5_STATE), lambda q, b: (layer, q, 0, 0, 0)),
            pl.BlockSpec((1, 1, PART_W, FOLD * S5_GROUP), lay4),
            pl.BlockSpec((1, 1, 2 * PART_STATE, FOLD * S5_GROUP), lay4),
            pl.BlockSpec((1, 1, 2, 2, 1, PART_STATE), lambda q, b: (layer, q, 0, 0, 0, 0)),
        ],
        out_specs=pl.BlockSpec((1, 1, T_PAD, PART_W), lambda q, b: (q, b, 0, 0)),
        out_shape=jax.ShapeDtypeStruct((S5_PARTS, bsz, T_PAD, PART_W), F32),
        scratch_shapes=[
            pltpu.VMEM((FOLD_W, 2 * PART_STATE), BF16),
            pltpu.VMEM((FOLD_W + 2 * PART_STATE, FOLD_W), BF16),
            pltpu.VMEM((ROWS, FOLD_W), F32),
            pltpu.VMEM((ROWS, 2 * PART_STATE), F32),
            pltpu.VMEM((ROWS, 2 * PART_STATE), F32),
        ],
        compiler_params=pltpu.CompilerParams(
            dimension_semantics=("arbitrary", "arbitrary"), vmem_limit_bytes=VMEM_LIMIT),
        name="s5_scan",
    )(u4, xq, rc, vc, decay)
    return y4.reshape(S5_PARTS, n, PART_W)


def _lru_kernel(x_ref, g_ref, cw_ref, cb_ref, wri_ref, bri_ref, nsp_ref, o_ref,
                xs_ref, gs_ref, xc_ref, gp_ref, z_ref, a_ref, b_ref, os_ref, h_ref):
    tc = LRU_CHUNK
    c = LRU_WIDTH
    seg = tc // SUBLANES
    nq = c // LANES
    lanes = lambda q: slice(q * LANES, (q + 1) * LANES)
    halo = SUBLANES

    @pl.when(pl.program_id(1) == 0)
    def _():
        xs_ref[:, 0:halo, :] = jnp.zeros((nq, halo, LANES), F32)
        h_ref[...] = jnp.zeros((1, c), F32)

    x = x_ref[0].astype(F32)
    g = g_ref[0].astype(F32)
    for q in range(nq):
        xs_ref[q, halo:, :] = x[:, lanes(q)]
        gs_ref[q] = g[:, lanes(q)]
    taps = [[cw_ref[0, k:k + 1, lanes(q)] for k in range(CONV_WIDTH)] for q in range(nq)]
    bias = [cb_ref[0, :, lanes(q)] for q in range(nq)]

    def conv_body(i, carry):
        r0 = pl.multiple_of(i * SUBLANES, SUBLANES)
        for q in range(nq):
            acc = bias[q]
            for k in range(CONV_WIDTH):
                first = halo - (CONV_WIDTH - 1) + k + i
                acc = acc + taps[q][k] * xs_ref[q, pl.ds(first, SUBLANES, stride=seg), :]
            xc_ref[pl.ds(r0, SUBLANES), lanes(q)] = acc
            gp_ref[pl.ds(r0, SUBLANES), lanes(q)] = jax.nn.gelu(gs_ref[q, pl.ds(i, SUBLANES, stride=seg), :])
        return carry

    lax.fori_loop(0, seg, conv_body, 0, unroll=LRU_UNROLL)
    for q in range(nq):
        xs_ref[q, 0:halo, :] = xs_ref[q, tc:tc + halo, :]

    z_ref[...] = _dot(xc_ref[...].astype(BF16), wri_ref[0])
    b_r = jnp.broadcast_to(bri_ref[0, :, :c], (SUBLANES, c))
    b_i = jnp.broadcast_to(bri_ref[0, :, c:], (SUBLANES, c))
    nsp = jnp.broadcast_to(nsp_ref[0], (SUBLANES, c))

    def scan_body(i, carry):
        h, p = carry
        r0 = pl.multiple_of(i * SUBLANES, SUBLANES)
        a = jnp.exp2(_sigmoid(z_ref[pl.ds(r0, SUBLANES), :c] + b_r) * nsp)
        gated = _sigmoid(z_ref[pl.ds(r0, SUBLANES), c:] + b_i) * xc_ref[pl.ds(r0, SUBLANES), :]
        w = 1.0 - a * a
        h = a * h + jnp.where(w > 0.0, w * lax.rsqrt(w), 0.0) * gated
        p = p * a
        b_ref[pl.ds(r0, SUBLANES), :] = h
        a_ref[pl.ds(r0, SUBLANES), :] = p
        return h, p

    h_end, p_end = lax.fori_loop(0, seg, scan_body, (jnp.zeros((SUBLANES, c), F32), jnp.ones((SUBLANES, c), F32)),
                                 unroll=LRU_SCAN_UNROLL)
    row = lax.broadcasted_iota(jnp.int32, (SUBLANES, c), 0)
    enter = jnp.where(row == 0, h_ref[...], 0.0)
    for sgm in range(SUBLANES - 1):
        leave = h_end + p_end * enter
        enter = enter + jnp.where(row == sgm + 1, pltpu.roll(leave, 1, axis=0), 0.0)
    h_ref[...] = (h_end + p_end * enter)[SUBLANES - 1:SUBLANES]

    def out_body(i, carry):
        r0 = pl.multiple_of(i * SUBLANES, SUBLANES)
        h = b_ref[pl.ds(r0, SUBLANES), :] + a_ref[pl.ds(r0, SUBLANES), :] * enter
        y = h * gp_ref[pl.ds(r0, SUBLANES), :]
        for q in range(nq):
            os_ref[q, pl.ds(i, SUBLANES, stride=seg), :] = y[:, lanes(q)]
        return carry

    lax.fori_loop(0, seg, out_body, 0, unroll=LRU_UNROLL)
    o_ref[0] = jnp.concatenate([os_ref[q] for q in range(nq)], axis=-1).astype(BF16)


def _lru(x_lru, g_lru, conv_w, conv_b, w_ri, b_ri, neg_sp, layer, bsz):
    n = x_lru.shape[0]
    c = LRU_WIDTH
    x3 = x_lru.reshape(bsz, T_PAD, c)
    g3 = g_lru.reshape(bsz, T_PAD, c)
    lay = lambda b, t: (layer, 0, 0)
    out = pl.pallas_call(
        _lru_kernel,
        grid=(bsz, T_PAD // LRU_CHUNK),
        in_specs=[
            pl.BlockSpec((1, LRU_CHUNK, c), lambda b, t: (b, t, 0)),
            pl.BlockSpec((1, LRU_CHUNK, c), lambda b, t: (b, t, 0)),
            _const_spec((1, CONV_WIDTH, c), lay),
            _const_spec((1, 1, c), lay),
            _const_spec((1, c, 2 * c), lay),
            _const_spec((1, 1, 2 * c), lay),
            _const_spec((1, 1, c), lay),
        ],
        out_specs=pl.BlockSpec((1, LRU_CHUNK, c), lambda b, t: (b, t, 0)),
        out_shape=jax.ShapeDtypeStruct((bsz, T_PAD, c), BF16),
        scratch_shapes=[
            pltpu.VMEM((c // LANES, LRU_CHUNK + SUBLANES, LANES), F32),
            pltpu.VMEM((c // LANES, LRU_CHUNK, LANES), F32),
            pltpu.VMEM((LRU_CHUNK, c), F32),
            pltpu.VMEM((LRU_CHUNK, c), F32),
            pltpu.VMEM((LRU_CHUNK, 2 * c), F32),
            pltpu.VMEM((LRU_CHUNK, c), F32),
            pltpu.VMEM((LRU_CHUNK, c), F32),
            pltpu.VMEM((c // LANES, LRU_CHUNK, LANES), F32),
            pltpu.VMEM((1, c), F32),
        ],
        compiler_params=pltpu.CompilerParams(
            dimension_semantics=("arbitrary", "arbitrary"), vmem_limit_bytes=VMEM_LIMIT),
        name="rglru",
    )(x3, g3, conv_w, conv_b, w_ri, b_ri, neg_sp)
    return out.reshape(n, c)


def _merge_kernel(hs_ref, ys_ref, yl_ref, gt_ref, wglu_ref, bglu_ref, wsp_ref, wlp_ref, wout_ref, g_ref,
                  *rest, with_router):
    if with_router:
        rw_ref, rb_ref, hs_out_ref, hn_ref, rt_ref, rtt_ref, cnt_ref, run_ref, tri_ref, *w_bf16 = rest
    else:
        hs_out_ref, hn_ref, *w_bf16 = rest
    wglu_b, wsp_b, wlp_b, wout_b = w_bf16

    @pl.when(pl.program_id(0) == 0)
    def _():
        for w_f32, w_b in zip((wglu_ref, wsp_ref, wlp_ref, wout_ref), w_bf16):
            w_b[...] = w_f32[0].astype(BF16)

    ys = jnp.concatenate([ys_ref[q] for q in range(S5_PARTS)], axis=-1)
    glu = ys * jax.nn.sigmoid(_dot(ys.astype(BF16), wglu_b[...]) + bglu_ref[0])
    y_a = _dot(glu.astype(BF16), wsp_b[...])
    y_b = _dot(yl_ref[...], wlp_b[...])
    y = gt_ref[:, :D_MODEL].astype(F32) * y_a + gt_ref[:, D_MODEL:].astype(F32) * y_b
    hs = hs_ref[...] + _dot(y.astype(BF16), wout_b[...])
    hs_out_ref[...] = hs
    hn = _rms(hs, g_ref[0])
    if not with_router:
        hn_ref[...] = hn.astype(BF16)
    else:
        _rows_to_tiles(hn_ref, hn)
        logits = _dot(hn.astype(BF16), rw_ref[0].astype(BF16)) + rb_ref[0]
        lane = lax.broadcasted_iota(jnp.int32, logits.shape, 1).astype(F32)
        m1 = jnp.max(logits, axis=-1, keepdims=True)
        i1 = jnp.min(jnp.where(logits == m1, lane, float(LANES)), axis=-1, keepdims=True)
        rest_l = jnp.where(lane == i1, MASKED_LOGIT, logits)
        m2 = jnp.max(rest_l, axis=-1, keepdims=True)
        i2 = jnp.min(jnp.where(rest_l == m2, lane, float(LANES)), axis=-1, keepdims=True)
        e2 = jnp.exp(m2 - m1)
        g1 = 1.0 / (1.0 + e2)
        g2 = e2 / (1.0 + e2)
        @pl.when(pl.program_id(0) == 0)
        def _():
            run_ref[...] = jnp.zeros_like(run_ref)
            r, c = _iota2(tri_ref.shape)
            tri_ref[...] = jnp.where(c < r, 1.0, 0.0).astype(BF16)

        first = lane == i1
        second = lane == i2
        picked = jnp.where(first | second, 1.0, 0.0)
        before = _dot(tri_ref[...], picked.astype(BF16)) + run_ref[...]
        rank1 = jnp.sum(jnp.where(first, before, 0.0), axis=-1, keepdims=True)
        rank2 = jnp.sum(jnp.where(second, before, 0.0), axis=-1, keepdims=True)
        run_ref[...] += jnp.sum(picked, axis=0, keepdims=True)
        cnt_ref[...] = jnp.broadcast_to(run_ref[...], cnt_ref.shape)
        rt = (jnp.where(lane == 0.0, i1, 0.0) + jnp.where(lane == 1.0, i2, 0.0)
              + jnp.where(lane == 2.0, g1, 0.0) + jnp.where(lane == 3.0, g2, 0.0)
              + jnp.where(lane == 4.0, rank1, 0.0) + jnp.where(lane == 5.0, rank2, 0.0))
        rt_ref[...] = rt
        r, c = _iota2((SUBLANES, LANES))
        pick = jnp.where(r == c, 1.0, 0.0).astype(BF16)
        hi = rt.astype(BF16)
        mid = (rt - hi.astype(F32)).astype(BF16)
        lo = (rt - hi.astype(F32) - mid.astype(F32)).astype(BF16)
        nt = (((1,), (1,)), ((), ()))
        rtt_ref[0] = (lax.dot_general(pick, hi, nt, preferred_element_type=F32)
                      + lax.dot_general(pick, mid, nt, preferred_element_type=F32)
                      + lax.dot_general(pick, lo, nt, preferred_element_type=F32))


def _merge(hs, ys_parts, y_lru, gates, w_glu, b_glu, w_sp, w_lp, w_out, ffn_norm, layer, router=None):
    n = hs.shape[0]
    tm = TM
    lay = lambda i: (layer, 0, 0)
    in_specs = [
        pl.BlockSpec((tm, D_MODEL), lambda i: (i, 0)),
        pl.BlockSpec((S5_PARTS, tm, PART_W), lambda i: (0, i, 0)),
        pl.BlockSpec((tm, LRU_WIDTH), lambda i: (i, 0)),
        pl.BlockSpec((tm, 2 * D_MODEL), lambda i: (i, 0)),
        _const_spec((1, S5_WIDTH, S5_WIDTH), lay),
        _const_spec((1, 1, S5_WIDTH), lay),
        _const_spec((1, S5_WIDTH, D_MODEL), lay),
        _const_spec((1, LRU_WIDTH, D_MODEL), lay),
        _const_spec((1, D_MODEL, D_MODEL), lay),
        _const_spec((1, 1, D_MODEL), lay),
    ]
    out_specs = [pl.BlockSpec((tm, D_MODEL), lambda i: (i, 0))]
    out_shape = [jax.ShapeDtypeStruct((n, D_MODEL), F32)]
    if router is None:
        out_specs.append(pl.BlockSpec((tm, D_MODEL), lambda i: (i, 0)))
        out_shape.append(jax.ShapeDtypeStruct((n, D_MODEL), BF16))
    else:
        out_specs.append(pl.BlockSpec((tm * ROW_TILES, LANES), lambda i: (i, 0)))
        out_shape.append(jax.ShapeDtypeStruct((n * ROW_TILES, LANES), F32))
    args = [hs, ys_parts, y_lru, gates, w_glu, b_glu, w_sp, w_lp, w_out, ffn_norm]
    if router is not None:
        rw, rb, j = router
        in_specs += [_const_spec((1, D_MODEL, LANES), lambda i: (j, 0, 0)),
                     _const_spec((1, 1, LANES), lambda i: (j, 0, 0))]
        out_specs += [pl.BlockSpec((tm, LANES), lambda i: (i, 0)),
                      pl.BlockSpec((1, SUBLANES, tm), lambda i: (i, 0, 0)),
                      pl.BlockSpec((SUBLANES, LANES), lambda i: (0, 0))]
        out_shape += [jax.ShapeDtypeStruct((n, LANES), F32),
                      jax.ShapeDtypeStruct((n // tm, SUBLANES, tm), F32),
                      jax.ShapeDtypeStruct((SUBLANES, LANES), F32)]
        args += [rw, rb]
    return pl.pallas_call(
        functools.partial(_merge_kernel, with_router=router is not None),
        grid=(n // tm,),
        in_specs=in_specs,
        out_specs=out_specs,
        out_shape=out_shape,
        scratch_shapes=([pltpu.VMEM((1, LANES), F32), pltpu.VMEM((tm, tm), BF16)] if router is not None else [])
        + [pltpu.VMEM(w.shape[1:], BF16) for w in (w_glu, w_sp, w_lp, w_out)],
        compiler_params=pltpu.CompilerParams(
            dimension_semantics=("arbitrary",), vmem_limit_bytes=VMEM_LIMIT),
        name="merge_router" if router is not None else "merge",
    )(*args)


def _ffn_kernel(x_ref, hs_ref, wg_ref, wu_ref, wd_ref, o_ref, h_ref):
    x = x_ref[...]
    for c0 in range(0, h_ref.shape[1], FF_CHUNK):
        g = _dot(x, wg_ref[0, :, c0:c0 + FF_CHUNK])
        u = _dot(x, wu_ref[0, :, c0:c0 + FF_CHUNK])
        h_ref[:, c0:c0 + FF_CHUNK] = (g * jax.nn.sigmoid(g) * u).astype(BF16)
    o_ref[...] = hs_ref[...] + _dot(h_ref[...], wd_ref[0])


def _ffn(hn, hs, w_gate, w_up, w_down, layer):
    n = hn.shape[0]
    ff = w_gate.shape[-1]
    lay = lambda i: (layer, 0, 0)
    return pl.pallas_call(
        _ffn_kernel,
        grid=(n // TM_FFN,),
        in_specs=[
            pl.BlockSpec((TM_FFN, D_MODEL), lambda i: (i, 0)),
            pl.BlockSpec((TM_FFN, D_MODEL), lambda i: (i, 0)),
            _const_spec((1, D_MODEL, ff), lay),
            _const_spec((1, D_MODEL, ff), lay),
            _const_spec((1, ff, D_MODEL), lay),
        ],
        out_specs=pl.BlockSpec((TM_FFN, D_MODEL), lambda i: (i, 0)),
        out_shape=jax.ShapeDtypeStruct((n, D_MODEL), F32),
        scratch_shapes=[pltpu.VMEM((TM_FFN, ff), BF16)],
        compiler_params=pltpu.CompilerParams(
            dimension_semantics=("arbitrary",), vmem_limit_bytes=VMEM_LIMIT),
        name="dense_ffn",
    )(hn, hs, w_gate, w_up, w_down)


def _moe_plan(route_t, counts_f, n):
    n_blocks = -(-2 * n // MOE_BLOCK) + N_EXPERTS
    e = jnp.stack([route_t[:, 0, :], route_t[:, 1, :]]).astype(jnp.int32)
    rank = jnp.stack([route_t[:, 4, :], route_t[:, 5, :]]).astype(jnp.int32)
    counts = counts_f[0, :N_EXPERTS].astype(jnp.int32)
    padded = ((counts + MOE_BLOCK - 1) // MOE_BLOCK) * MOE_BLOCK
    cum_pad = jnp.cumsum(padded)
    pad_start = cum_pad - padded
    pos = rank
    for x in range(N_EXPERTS):
        pos = pos + jnp.where(e == x, pad_start[x], 0)
    block_start = jnp.arange(n_blocks, dtype=jnp.int32) * MOE_BLOCK
    block_expert = jnp.minimum(jnp.sum((block_start[:, None] >= cum_pad[None, :]).astype(jnp.int32), axis=1),
                               N_EXPERTS - 1)
    n_used = (cum_pad[-1] // MOE_BLOCK).astype(jnp.int32).reshape(1)
    pad_range = jnp.stack([pad_start + counts, cum_pad], axis=1).reshape(2 * N_EXPERTS).astype(jnp.int32)
    ids = jnp.arange(N_EXPERTS, dtype=jnp.int32)
    used = counts > 0
    run_buffer = (jnp.cumsum(used.astype(jnp.int32)) - 1) % 2
    later = jnp.where((ids[None, :] > ids[:, None]) & used[None, :], ids[None, :], N_EXPERTS)
    next_used = jnp.min(later, axis=1)
    next_used = jnp.where(next_used == N_EXPERTS, -1, next_used)
    of_block = lambda table: jnp.sum(jnp.where(block_expert[:, None] == ids[None, :], table[None, :], 0), axis=1)
    weight_plan = jnp.stack([block_expert,
                             (block_start == of_block(pad_start)).astype(jnp.int32),
                             of_block(run_buffer), of_block(next_used)]).astype(jnp.int32)
    return pos.reshape(2 * n), weight_plan, n_used, pad_range


def _tile(ref, index):
    return ref.at[pl.ds(pl.multiple_of(index * ROW_TILES, ROW_TILES), ROW_TILES)]


def _tile_gather(src_hbm, dst, sem, rows, index_of):
    def body(grp, carry):
        r0 = grp * GATHER_GROUP
        index = [index_of(r0 + j) for j in range(GATHER_GROUP)]
        for j in range(GATHER_GROUP):
            pltpu.make_async_copy(_tile(src_hbm, index[j]), _tile(dst, r0 + j), sem).start(priority=j % 2)
        return carry
    lax.fori_loop(0, rows // GATHER_GROUP, body, 0)


def _tile_gather_wait(src_hbm, dst, sem, rows):
    pltpu.make_async_copy(src_hbm.at[pl.ds(0, rows * ROW_TILES)], dst, sem).wait()


def _moe_dispatch_kernel(pos_ref, pad_ref, nu_ref, x_hbm, xs_hbm, xbuf, zero_ref, sem_in, sem_out, sem_fill):
    t = pl.program_id(0)
    nt = pl.num_programs(0)
    n = nt * TM
    tile_rows = TM * ROW_TILES
    block_rows = MOE_BLOCK * ROW_TILES
    n_blocks = xs_hbm.shape[0] // block_rows

    def read(tile):
        buf = tile % DISPATCH_RING
        return pltpu.make_async_copy(x_hbm.at[pl.ds(pl.multiple_of(tile * tile_rows, tile_rows), tile_rows)],
                                     xbuf.at[buf], sem_in.at[buf])

    def scatter_wait(tile):
        buf = tile % DISPATCH_RING
        for k in range(2):
            pltpu.make_async_copy(xbuf.at[buf], xs_hbm.at[pl.ds(0, tile_rows)], sem_out.at[buf]).wait()

    @pl.when(t == 0)
    def _():
        read(0).start()
        zero_ref[...] = jnp.zeros_like(zero_ref)
        zero_tile = zero_ref.at[pl.ds(0, ROW_TILES)]
        for e in range(N_EXPERTS):
            def fill(slot, carry):
                pltpu.make_async_copy(zero_tile, _tile(xs_hbm, slot), sem_fill.at[0]).start()
                return carry

            def fill_wait(slot, carry):
                pltpu.make_async_copy(zero_tile, _tile(xs_hbm, slot), sem_fill.at[0]).wait()
                return carry
            lax.fori_loop(pad_ref[2 * e], pad_ref[2 * e + 1], fill, 0)
            lax.fori_loop(pad_ref[2 * e], pad_ref[2 * e + 1], fill_wait, 0)

        def block_of(blk):
            return xs_hbm.at[pl.ds(pl.multiple_of(blk * block_rows, block_rows), block_rows)]

        def fill_block(blk, carry):
            pltpu.make_async_copy(zero_ref, block_of(blk), sem_fill.at[0]).start()
            return carry

        def fill_block_wait(blk, carry):
            pltpu.make_async_copy(zero_ref, block_of(blk), sem_fill.at[0]).wait()
            return carry
        lax.fori_loop(nu_ref[0], n_blocks, fill_block, 0)
        lax.fori_loop(nu_ref[0], n_blocks, fill_block_wait, 0)

    @pl.when(t + 1 < nt)
    def _():
        @pl.when(t + 1 >= DISPATCH_RING)
        def _():
            scatter_wait(t + 1 - DISPATCH_RING)
        read(t + 1).start()

    read(t).wait()
    src = xbuf.at[t % DISPATCH_RING]
    for k in range(2):
        def put(grp, carry):
            r0 = grp * GATHER_GROUP
            slot = [pos_ref[k * n + t * TM + r0 + j] for j in range(GATHER_GROUP)]
            for j in range(GATHER_GROUP):
                pltpu.make_async_copy(_tile(src, r0 + j), _tile(xs_hbm, slot[j]),
                                      sem_out.at[t % DISPATCH_RING]).start(priority=j % 2)
            return carry
        lax.fori_loop(0, TM // GATHER_GROUP, put, 0)

    @pl.when(t == nt - 1)
    def _():
        for back in range(DISPATCH_RING - 1, -1, -1):
            @pl.when(t - back >= 0)
            def _():
                scatter_wait(t - back)


def _moe_dispatch(hn_tiles, pos, pad_range, n_used, n_slots):
    n = hn_tiles.shape[0] // ROW_TILES
    return pl.pallas_call(
        _moe_dispatch_kernel,
        grid_spec=pltpu.PrefetchScalarGridSpec(
            num_scalar_prefetch=3,
            grid=(n // TM,),
            in_specs=[pl.BlockSpec(memory_space=pl.ANY)],
            out_specs=pl.BlockSpec(memory_space=pl.ANY),
            scratch_shapes=[pltpu.VMEM((DISPATCH_RING, TM * ROW_TILES, LANES), F32),
                            pltpu.VMEM((MOE_BLOCK * ROW_TILES, LANES), F32),
                            pltpu.SemaphoreType.DMA((DISPATCH_RING,)),
                            pltpu.SemaphoreType.DMA((DISPATCH_RING,)),
                            pltpu.SemaphoreType.DMA((1,))],
        ),
        out_shape=jax.ShapeDtypeStruct((n_slots * ROW_TILES, LANES), F32),
        compiler_params=pltpu.CompilerParams(dimension_semantics=("arbitrary",)),
        name="moe_dispatch",
    )(pos, pad_range, n_used, hn_tiles)


def _moe_ffn_kernel(be_ref, first_ref, buf_ref, next_ref, nu_ref, x_ref, wg_hbm, wu_hbm, wd_hbm, y_ref,
                    wg_buf, wu_buf, wd_buf, sem, *, first_expert):
    i = pl.program_id(0)
    sources = (wg_hbm, wu_hbm, wd_hbm)
    buffers = (wg_buf, wu_buf, wd_buf)

    def copies(expert, b):
        return [pltpu.make_async_copy(src.at[first_expert + expert], dst.at[b], sem.at[b, k])
                for k, (src, dst) in enumerate(zip(sources, buffers))]

    @pl.when(i < nu_ref[0])
    def _():
        b = buf_ref[i]

        @pl.when(first_ref[i] == 1)
        def _():
            @pl.when(i == 0)
            def _():
                for c in copies(be_ref[i], b):
                    c.start()
            for c in copies(be_ref[i], b):
                c.wait()

            @pl.when(next_ref[i] >= 0)
            def _():
                for c in copies(next_ref[i], 1 - b):
                    c.start()

        x = _rows_from_tiles(x_ref, MOE_BLOCK).astype(BF16)
        g = _dot(x, wg_buf[b].astype(BF16))
        h = g * jax.nn.sigmoid(g) * _dot(x, wu_buf[b].astype(BF16))
        _rows_to_tiles(y_ref, _dot(h.astype(BF16), wd_buf[b].astype(BF16)))

    @pl.when(i >= nu_ref[0])
    def _():
        y_ref[...] = jnp.zeros_like(y_ref)


def _moe_ffn(xs_tiles, weight_plan, n_used, w_gate, w_up, w_down, first):
    n_blocks = weight_plan.shape[1]
    ff = w_gate.shape[-1]
    return pl.pallas_call(
        functools.partial(_moe_ffn_kernel, first_expert=first),
        grid_spec=pltpu.PrefetchScalarGridSpec(
            num_scalar_prefetch=5,
            grid=(n_blocks,),
            in_specs=[
                pl.BlockSpec((MOE_BLOCK * ROW_TILES, LANES),
                             lambda i, be, fi, bu, nx, nu: (jnp.maximum(jnp.minimum(i, nu[0] - 1), 0), 0)),
                pl.BlockSpec(memory_space=pl.ANY),
                pl.BlockSpec(memory_space=pl.ANY),
                pl.BlockSpec(memory_space=pl.ANY),
            ],
            out_specs=pl.BlockSpec((MOE_BLOCK * ROW_TILES, LANES), lambda i, be, fi, bu, nx, nu: (i, 0)),
            scratch_shapes=[pltpu.VMEM((2, D_MODEL, ff), F32),
                            pltpu.VMEM((2, D_MODEL, ff), F32),
                            pltpu.VMEM((2, ff, D_MODEL), F32),
                            pltpu.SemaphoreType.DMA((2, 3))],
        ),
        out_shape=jax.ShapeDtypeStruct((n_blocks * MOE_BLOCK * ROW_TILES, LANES), F32),
        compiler_params=pltpu.CompilerParams(
            dimension_semantics=("arbitrary",), vmem_limit_bytes=VMEM_LIMIT),
        name="moe_ffn",
    )(weight_plan[0], weight_plan[1], weight_plan[2], weight_plan[3], n_used, xs_tiles, w_gate, w_up, w_down)


def _moe_combine_kernel(pos_ref, hs_ref, rt_ref, ys_hbm, o_ref, ybuf, sem):
    i = pl.program_id(0)
    nt = pl.num_programs(0)
    slot = i % 2

    def start(t, s):
        for k in range(2):
            _tile_gather(ys_hbm, ybuf.at[s, k], sem.at[s], TM, lambda r: pos_ref[k * (nt * TM) + t * TM + r])

    @pl.when(i == 0)
    def _():
        start(0, 0)

    @pl.when(i + 1 < nt)
    def _():
        start(i + 1, 1 - slot)

    for k in range(2):
        _tile_gather_wait(ys_hbm, ybuf.at[slot, k], sem.at[slot], TM)
    rt = rt_ref[...]
    lane = lax.broadcasted_iota(jnp.int32, rt.shape, 1)
    g1 = jnp.sum(jnp.where(lane == 2, rt, 0.0), axis=-1, keepdims=True)
    g2 = jnp.sum(jnp.where(lane == 3, rt, 0.0), axis=-1, keepdims=True)
    o_ref[...] = (hs_ref[...] + g1 * _rows_from_tiles(ybuf.at[slot, 0], TM)
                  + g2 * _rows_from_tiles(ybuf.at[slot, 1], TM))


def _moe_combine(hs, route, ys_tiles, pos):
    n = hs.shape[0]
    return pl.pallas_call(
        _moe_combine_kernel,
        grid_spec=pltpu.PrefetchScalarGridSpec(
            num_scalar_prefetch=1,
            grid=(n // TM,),
            in_specs=[
                pl.BlockSpec((TM, D_MODEL), lambda i, p: (i, 0)),
                pl.BlockSpec((TM, LANES), lambda i, p: (i, 0)),
                pl.BlockSpec(memory_space=pl.ANY),
            ],
            out_specs=pl.BlockSpec((TM, D_MODEL), lambda i, p: (i, 0)),
            scratch_shapes=[pltpu.VMEM((2, 2, TM * ROW_TILES, LANES), F32),
                            pltpu.SemaphoreType.DMA((2,))],
        ),
        out_shape=jax.ShapeDtypeStruct((n, D_MODEL), F32),
        compiler_params=pltpu.CompilerParams(
            dimension_semantics=("arbitrary",), vmem_limit_bytes=VMEM_LIMIT),
        name="moe_combine",
    )(pos, hs, route, ys_tiles)


def _final_kernel(a_ref, b_ref, g_ref, o_ref):
    tb = a_ref.shape[1]
    o_ref[0, :tb - N_META] = _rms(a_ref[0, N_META:], g_ref[...])
    o_ref[0, tb - N_META:] = _rms(b_ref[0], g_ref[...])


def _final_norm(hs, g, bsz, seq):
    hs3 = hs.reshape(bsz, T_PAD, D_MODEL)
    return pl.pallas_call(
        _final_kernel,
        grid=(bsz, seq // TB_FINAL),
        in_specs=[pl.BlockSpec((1, TB_FINAL, D_MODEL), lambda b, i: (b, i, 0)),
                  pl.BlockSpec((1, N_META, D_MODEL), lambda b, i: (b, (i + 1) * (TB_FINAL // N_META), 0)),
                  _const_spec((1, D_MODEL), lambda b, i: (0, 0))],
        out_specs=pl.BlockSpec((1, TB_FINAL, D_MODEL), lambda b, i: (b, i, 0)),
        out_shape=jax.ShapeDtypeStruct((bsz, seq, D_MODEL), F32),
        compiler_params=pltpu.CompilerParams(
            dimension_semantics=("arbitrary", "arbitrary"), vmem_limit_bytes=VMEM_LIMIT),
        name="final_norm",
    )(hs3, hs3, g)


def _head_blockdiag(w):
    eye = jnp.eye(LRU_HEADS, dtype=w.dtype)
    out = jnp.einsum('lnhk,nm->lnhmk', w, eye)
    return out.reshape(w.shape[0], LRU_WIDTH, LRU_WIDTH)


def kernel(x, meta_tokens, mix_norm, w_in, merge_bias, s5_lambda_re, s5_lambda_im, s5_log_dt, s5_b_re, s5_b_im, s5_c_re, s5_c_im, s5_d, s5_w_glu, s5_b_glu, s5_w_proj, lru_conv_w, lru_conv_b, lru_w_rgate, lru_b_rgate, lru_w_igate, lru_b_igate, lru_lambda, lru_w_proj, w_out, ffn_norm, dense_w_gate, dense_w_up, dense_w_down, router_w, router_b, moe_w_gate, moe_w_up, moe_w_down, final_norm):
    bsz, seq, d = x.shape
    depth = w_in.shape[0]
    assert d == D_MODEL and N_META + seq <= T_PAD
    n = bsz * T_PAD
    assert n % TM == 0 and n % TM_FFN == 0 and seq % TB_FINAL == 0 and TB_FINAL % N_META == 0

    meta = jnp.broadcast_to(meta_tokens[None].astype(x.dtype), (bsz, N_META, d))
    pad = jnp.zeros((bsz, T_PAD - N_META - seq, d), x.dtype)
    hs = jnp.concatenate([meta, x, pad], axis=1).reshape(n, d)

    row3 = lambda a: a[:, None, :]
    w_ri = jnp.concatenate([_head_blockdiag(lru_w_rgate), _head_blockdiag(lru_w_igate)], axis=-1).astype(BF16)
    b_ri = jnp.concatenate([lru_b_rgate, lru_b_igate], axis=-1)
    neg_sp = -LRU_C * jax.nn.softplus(-lru_lambda) * LOG2_E
    dense = [w.astype(BF16) for w in (dense_w_gate, dense_w_up, dense_w_down)]
    n_moe = router_w.shape[0]
    moe = [w.reshape((n_moe * N_EXPERTS,) + w.shape[2:]) for w in (moe_w_gate, moe_w_up, moe_w_down)]
    s5_ops = jax.vmap(_s5_prep)(s5_lambda_re, s5_lambda_im, s5_log_dt, s5_b_re, s5_b_im, s5_c_re, s5_c_im, s5_d)
    rw_pad = jnp.pad(router_w, ((0, 0), (0, 0), (0, LANES - N_EXPERTS)))
    rb_pad = jnp.pad(router_b, ((0, 0), (0, LANES - N_EXPERTS)), constant_values=MASKED_LOGIT)

    for layer in range(depth):
        u_parts, x_lru, g_lru, gates = _in_proj(hs, row3(mix_norm), w_in, row3(merge_bias), layer)
        ys_parts = _s5_scan(u_parts, s5_ops, layer, bsz)
        y_lru = _lru(x_lru, g_lru, lru_conv_w, row3(lru_conv_b), w_ri, row3(b_ri), row3(neg_sp), layer, bsz)
        j = layer // 2
        router = (rw_pad, row3(rb_pad), j) if layer % 2 == 1 else None
        res = _merge(hs, ys_parts, y_lru, gates, s5_w_glu, row3(s5_b_glu), s5_w_proj, lru_w_proj, w_out,
                     row3(ffn_norm), layer, router)
        if layer % 2 == 0:
            hs, hn = res
            hs = _ffn(hn, hs, *dense, layer=j)
        else:
            hs, hn, route, route_t, counts = res
            pos, weight_plan, n_used, pad_range = _moe_plan(route_t, counts, n)
            xs = _moe_dispatch(hn, pos, pad_range, n_used, weight_plan.shape[1] * MOE_BLOCK)
            ys = _moe_ffn(xs, weight_plan, n_used, *moe, first=j * N_EXPERTS)
            hs = _moe_combine(hs, route, ys, pos)

    return _final_norm(hs, final_norm[None, :], bsz, seq)
```
